```python
import jax, jax.numpy as jnp
from jax import lax
import numpy as np

D_MODEL = 2048
BATCH = 1
SEQ = 8192
DEPTH = 1

N_HEADS = 8
HEAD_DIM = 128
KV_LATENT = 256
IDX_HEADS = 16
IDX_DIM = 64
TOPK_MAX = 256
Q_BLOCK = 128
ATTN_WIDTH = N_HEADS * HEAD_DIM
GMLP_GROUPS = 8
GMLP_GROUP_DIM = 128
GMLP_WIDTH = GMLP_GROUPS * GMLP_GROUP_DIM
CHUNK = 128
SPLIT_SIZES = (
    ATTN_WIDTH,
    KV_LATENT,
    IDX_HEADS * IDX_DIM,
    IDX_DIM,
    IDX_HEADS,
    2 * GMLP_WIDTH,
    2 * D_MODEL,
)
IN_COLS = int(sum(SPLIT_SIZES))
SPLIT_POINTS = tuple(int(v) for v in np.cumsum(SPLIT_SIZES)[:-1])
N_GROUPS = 8
EXPERTS_PER_GROUP = 8
N_EXPERTS = N_GROUPS * EXPERTS_PER_GROUP
TOPK_IN_GROUP = 2
EXPERT_FF = D_MODEL // 4
EPS = 1e-6

kernel_name = "hybrid_dsa_gmlp_hiermoe_block"


def rms_norm(x, g):
    xf = x.astype(jnp.float32)
    y = xf * lax.rsqrt(jnp.mean(xf * xf, axis=-1, keepdims=True) + EPS)
    return (y * g.astype(jnp.float32)).astype(x.dtype)


def layer_norm(x, g, b):
    xf = x.astype(jnp.float32)
    mu = jnp.mean(xf, axis=-1, keepdims=True)
    var = jnp.mean(jnp.square(xf - mu), axis=-1, keepdims=True)
    y = (xf - mu) * lax.rsqrt(var + EPS)
    return (y * g.astype(jnp.float32) + b.astype(jnp.float32)).astype(x.dtype)


def sparse_mla_attention(q, c_kv, q_idx, k_idx, w_idx, w_uk, w_uv):
    B, S = q.shape[0], q.shape[1]
    topk = min(TOPK_MAX, S // 4)
    nb = S // Q_BLOCK
    scale = HEAD_DIM ** -0.5
    q_abs = jnp.einsum('bthd,hdc->bthc', q, w_uk)

    def to_blocks(a):
        return a.reshape((B, nb, Q_BLOCK) + a.shape[2:]).swapaxes(0, 1)

    s_pos = jnp.arange(S)

    def block(args):
        i, qa, qi, wi = args
        t_pos = i * Q_BLOCK + jnp.arange(Q_BLOCK)
        dots = jnp.einsum('bthd,bsd->bths', qi, k_idx).astype(jnp.float32)
        score = jnp.einsum('bth,bths->bts', wi.astype(jnp.float32), jax.nn.relu(dots))
        causal = s_pos[None, :] <= t_pos[:, None]
        score = jnp.where(causal[None], score, -jnp.inf)
        _, idx = lax.top_k(score, topk)
        valid = idx <= t_pos[None, :, None]
        c_sel = jax.vmap(lambda c, ix: c[ix])(c_kv, idx)
        logits = jnp.einsum('bthc,btkc->bthk', qa, c_sel).astype(jnp.float32) * scale
        logits = jnp.where(valid[:, :, None, :], logits, -jnp.inf)
        p = jax.nn.softmax(logits, axis=-1).astype(c_sel.dtype)
        return jnp.einsum('bthk,btkc->bthc', p, c_sel)

    o = lax.map(block, (jnp.arange(nb), to_blocks(q_abs), to_blocks(q_idx), to_blocks(w_idx)))
    o = o.swapaxes(0, 1).reshape(B, S, N_HEADS, KV_LATENT)
    o = jnp.einsum('bthc,hcd->bthd', o, w_uv)
    return o.reshape(B, S, ATTN_WIDTH)


def chunked_spatial_gating(uv, w_s, b_s, ln_g, ln_b):
    B, S = uv.shape[0], uv.shape[1]
    z = jax.nn.gelu(uv)
    u, v = jnp.split(z, 2, axis=-1)
    v = layer_norm(v, ln_g, ln_b)
    v = v.reshape(B, S // CHUNK, CHUNK, GMLP_GROUPS, GMLP_GROUP_DIM)
    causal = jnp.tril(jnp.ones((CHUNK, CHUNK), dtype=w_s.dtype))
    y = jnp.einsum('gts,bnsgc->bntgc', w_s * causal[None], v) + b_s.T[None, None, :, :, None]
    return u * y.reshape(B, S, GMLP_WIDTH)


def hybrid_mixer(h, w_in, kv_norm_g, w_uk, w_uv, gmlp_ws, gmlp_bs, ln_v_g, ln_v_b,
                 w_br_attn, w_br_gmlp, w_out):
    B, S, _ = h.shape
    proj = jnp.einsum('bsd,de->bse', h, w_in)
    q, c_kv, q_idx, k_idx, w_idx, uv, gates = jnp.split(proj, SPLIT_POINTS, axis=-1)
    q = q.reshape(B, S, N_HEADS, HEAD_DIM)
    c_kv = rms_norm(c_kv, kv_norm_g)
    q_idx = q_idx.reshape(B, S, IDX_HEADS, IDX_DIM) * (IDX_DIM ** -0.5)
    w_idx = w_idx * (IDX_HEADS ** -0.5)
    a = sparse_mla_attention(q, c_kv, q_idx, k_idx, w_idx, w_uk, w_uv)
    m = chunked_spatial_gating(uv, gmlp_ws, gmlp_bs, ln_v_g, ln_v_b)
    g_a, g_b = jnp.split(gates, 2, axis=-1)
    merged = (jax.nn.sigmoid(g_a) * jnp.einsum('bsa,ad->bsd', a, w_br_attn)
              + jax.nn.sigmoid(g_b) * jnp.einsum('bsm,md->bsd', m, w_br_gmlp))
    return jnp.einsum('bsd,de->bse', merged, w_out)


def hierarchical_moe(h, w_group, b_group, w_router, b_router, w_e_gate, w_e_up, w_e_down):
    B, S, D = h.shape
    xt = h.reshape(B * S, D)
    n_tok = xt.shape[0]
    g_logits = jnp.einsum('nd,dg->ng', xt, w_group).astype(jnp.float32) + b_group.astype(jnp.float32)
    g_prob = jax.nn.softmax(g_logits, axis=-1)
    g_val, g_idx = lax.top_k(g_prob, 1)
    g_onehot = jax.nn.one_hot(g_idx[:, 0], N_GROUPS, dtype=jnp.float32)
    e_logits = (jnp.einsum('nd,de->ne', xt, w_router).astype(jnp.float32)
                + b_router.astype(jnp.float32)).reshape(n_tok, N_GROUPS, EXPERTS_PER_GROUP)
    sel = jnp.einsum('ng,nge->ne', g_onehot, e_logits)
    top_v, top_i = lax.top_k(sel, TOPK_IN_GROUP)
    q = jax.nn.softmax(top_v, axis=-1)
    within = jnp.sum(jax.nn.one_hot(top_i, EXPERTS_PER_GROUP, dtype=jnp.float32) * q[..., None], axis=1)
    combine = (g_onehot[:, :, None] * g_val[:, :, None] * within[:, None, :])
    combine = combine.reshape(n_tok, N_EXPERTS).astype(h.dtype)
    out = jnp.zeros_like(xt)
    for g in range(N_GROUPS):
        sl = slice(g * EXPERTS_PER_GROUP, (g + 1) * EXPERTS_PER_GROUP)
        hid = (jax.nn.silu(jnp.einsum('nd,edf->nef', xt, w_e_gate[sl]))
               * jnp.einsum('nd,edf->nef', xt, w_e_up[sl]) * combine[:, sl, None])
        out = out + jnp.einsum('nef,efd->nd', hid, w_e_down[sl])
    return out.reshape(B, S, D)


def setup_inputs(seed: int = 0) -> dict:
    key = jax.random.key(seed)
    ks = jax.random.split(key, 24)
    f32 = jnp.float32

    def nrm(k, shape, scale):
        return jax.random.normal(k, shape, f32) * scale

    return {
        "x": nrm(ks[0], (BATCH, SEQ, D_MODEL), 1.0),
        "norm1_g": 1.0 + nrm(ks[1], (D_MODEL,), 0.02),
        "w_in": nrm(ks[2], (D_MODEL, IN_COLS), D_MODEL ** -0.5),
        "kv_norm_g": 1.0 + nrm(ks[3], (KV_LATENT,), 0.02),
        "w_uk": nrm(ks[4], (N_HEADS, HEAD_DIM, KV_LATENT), KV_LATENT ** -0.5),
        "w_uv": nrm(ks[5], (N_HEADS, KV_LATENT, HEAD_DIM), KV_LATENT ** -0.5),
        "gmlp_ws": nrm(ks[6], (GMLP_GROUPS, CHUNK, CHUNK), CHUNK ** -0.5),
        "gmlp_bs": 1.0 + nrm(ks[7], (GMLP_GROUPS, CHUNK), 0.1),
        "ln_v_g": 1.0 + nrm(ks[8], (GMLP_WIDTH,), 0.02),
        "ln_v_b": nrm(ks[9], (GMLP_WIDTH,), 0.02),
        "w_br_attn": nrm(ks[10], (ATTN_WIDTH, D_MODEL), ATTN_WIDTH ** -0.5),
        "w_br_gmlp": nrm(ks[11], (GMLP_WIDTH, D_MODEL), GMLP_WIDTH ** -0.5),
        "w_out": nrm(ks[12], (D_MODEL, D_MODEL), D_MODEL ** -0.5),
        "norm2_g": 1.0 + nrm(ks[13], (D_MODEL,), 0.02),
        "w_group": nrm(ks[14], (D_MODEL, N_GROUPS), D_MODEL ** -0.5),
        "b_group": nrm(ks[15], (N_GROUPS,), 0.01),
        "w_router": nrm(ks[16], (D_MODEL, N_EXPERTS), D_MODEL ** -0.5),
        "b_router": nrm(ks[17], (N_EXPERTS,), 0.01),
        "w_e_gate": nrm(ks[18], (N_EXPERTS, D_MODEL, EXPERT_FF), D_MODEL ** -0.5),
        "w_e_up": nrm(ks[19], (N_EXPERTS, D_MODEL, EXPERT_FF), D_MODEL ** -0.5),
        "w_e_down": nrm(ks[20], (N_EXPERTS, EXPERT_FF, D_MODEL), EXPERT_FF ** -0.5),
        "norm_f_g": 1.0 + nrm(ks[21], (D_MODEL,), 0.02),
    }


def reference(x, norm1_g, w_in, kv_norm_g, w_uk, w_uv, gmlp_ws, gmlp_bs, ln_v_g, ln_v_b,
              w_br_attn, w_br_gmlp, w_out, norm2_g, w_group, b_group, w_router, b_router,
              w_e_gate, w_e_up, w_e_down, norm_f_g):
    h = x
    for _ in range(DEPTH):
        h = h + hybrid_mixer(rms_norm(h, norm1_g), w_in, kv_norm_g, w_uk, w_uv, gmlp_ws, gmlp_bs,
                             ln_v_g, ln_v_b, w_br_attn, w_br_gmlp, w_out)
        h = h + hierarchical_moe(rms_norm(h, norm2_g), w_group, b_group, w_router, b_router,
                                 w_e_gate, w_e_up, w_e_down)
    return rms_norm(h, norm_f_g)
```

```python
import functools

import jax
import jax.numpy as jnp
import numpy as np
from jax import lax
from jax.experimental import pallas as pl
from jax.experimental.pallas import tpu as pltpu

F32 = jnp.float32
BF16 = jnp.bfloat16

D_MODEL = 2048
SEQ = 8192
N_HEADS = 8
HEAD_DIM = 128
KV_LATENT = 256
IDX_HEADS = 16
IDX_DIM = 64
TOPK = 256
ATTN_WIDTH = N_HEADS * HEAD_DIM
GMLP_GROUPS = 8
GMLP_WIDTH = 1024
CHUNK = 128
N_GROUPS = 8
EXPERTS_PER_GROUP = 8
N_EXPERTS = 64
EXPERT_FF = 512
EPS = 1e-6

LANES = 128
VMEM_LIMIT = 60 * 1024 * 1024
MASK_VALUE = -0.7 * float(np.finfo(np.float32).max)

ROW_TILE = 256
TQ = 256
KB = 256
NKB = SEQ // KB
RG = 128
MAX_SEARCH_ITERS = 64
EXPERT_TILE = 256
N_EXPERT_TILES = (2 * SEQ) // EXPERT_TILE + N_EXPERTS


def _dot(a, b):
    return jnp.dot(a, b, preferred_element_type=F32)


def _resident(shape):
    zeros = (0,) * len(shape)
    return pl.BlockSpec(shape, lambda *_: zeros, pipeline_mode=pl.Buffered(1))


def _params(n_axes=1):
    return pltpu.CompilerParams(
        dimension_semantics=("arbitrary",) * n_axes, vmem_limit_bytes=VMEM_LIMIT)


def _rep(x, n):
    return x if n == 1 else jnp.concatenate([x] * n, axis=1)


def _proj_kernel(x_ref, g1_ref, wq_ref, wc_ref, wqi_ref, wkk_ref, ww_ref, kvg_ref,
                 n1_ref, q_ref, ckv_ref, ckvT_ref, qi_ref, kkT_ref, wi_ref):
    x = x_ref[...]
    ms = jnp.mean(x * x, axis=-1, keepdims=True)
    n1 = (x * lax.rsqrt(ms + EPS) * g1_ref[...]).astype(BF16)
    n1_ref[...] = n1
    q_ref[...] = _dot(n1, wq_ref[...]).astype(BF16)
    c = _dot(n1, wc_ref[...])
    c = c * lax.rsqrt(jnp.mean(c * c, axis=-1, keepdims=True) + EPS) * kvg_ref[...]
    ckv_ref[...] = c.astype(BF16)
    ckvT_ref[0] = c.T.astype(BF16)
    qi_ref[...] = (_dot(n1, wqi_ref[...]) * (IDX_DIM ** -0.5)).astype(BF16)
    kkT_ref[0] = _dot(n1, wkk_ref[...]).T.astype(BF16)
    wi_ref[...] = _dot(n1, ww_ref[...]) * (IDX_HEADS ** -0.5)


def _proj(x2, g1, wq, wc, wqi, wkk, ww, kvg):
    tm = KB
    row = lambda w: pl.BlockSpec((tm, w), lambda i: (i, 0))
    blk = pl.BlockSpec((1, 2 * LANES, tm), lambda i: (i, 0, 0))
    return pl.pallas_call(
        _proj_kernel,
        grid=(SEQ // tm,),
        in_specs=[row(D_MODEL), _resident(g1.shape), _resident(wq.shape), _resident(wc.shape),
                  _resident(wqi.shape), _resident(wkk.shape), _resident(ww.shape),
                  _resident(kvg.shape)],
        out_specs=[row(D_MODEL), row(ATTN_WIDTH), row(KV_LATENT), blk, row(IDX_HEADS * IDX_DIM),
                   blk, row(LANES)],
        out_shape=[
            jax.ShapeDtypeStruct((SEQ, D_MODEL), BF16),
            jax.ShapeDtypeStruct((SEQ, ATTN_WIDTH), BF16),
            jax.ShapeDtypeStruct((SEQ, KV_LATENT), BF16),
            jax.ShapeDtypeStruct((NKB, KV_LATENT, KB), BF16),
            jax.ShapeDtypeStruct((SEQ, IDX_HEADS * IDX_DIM), BF16),
            jax.ShapeDtypeStruct((NKB, 2 * LANES, KB), BF16),
            jax.ShapeDtypeStruct((SEQ, LANES), F32),
        ],
        compiler_params=_params(),
        name="proj",
    )(x2, g1, wq, wc, wqi, wkk, ww, kvg)


def _gmlp_kernel(n1_ref, wuv_ref, ws_ref, bsT_ref, lng_ref, lnb_ref, m_ref):
    uv = _dot(n1_ref[...], wuv_ref[...])
    z = jax.nn.gelu(uv)
    u = z[:, :GMLP_WIDTH]
    v = z[:, GMLP_WIDTH:]
    mu = jnp.mean(v, axis=-1, keepdims=True)
    var = jnp.mean(jnp.square(v - mu), axis=-1, keepdims=True)
    vn = ((v - mu) * lax.rsqrt(var + EPS) * lng_ref[...] + lnb_ref[...]).astype(BF16)
    t_pos = lax.broadcasted_iota(jnp.int32, (CHUNK, CHUNK), 0)
    s_pos = lax.broadcasted_iota(jnp.int32, (CHUNK, CHUNK), 1)
    causal = s_pos <= t_pos
    for g in range(GMLP_GROUPS):
        wm = jnp.where(causal, ws_ref[g], 0.0).astype(BF16)
        bias = bsT_ref[:, g:g + 1]
        cols = slice(g * LANES, (g + 1) * LANES)
        for c in range(ROW_TILE // CHUNK):
            rows = slice(c * CHUNK, (c + 1) * CHUNK)
            y = _dot(wm, vn[rows, cols]) + bias
            m_ref[rows, cols] = (u[rows, cols] * y).astype(BF16)


def _gmlp(n1, wuv, ws, bsT, lng, lnb):
    tm = ROW_TILE
    return pl.pallas_call(
        _gmlp_kernel,
        grid=(SEQ // tm,),
        in_specs=[pl.BlockSpec((tm, D_MODEL), lambda i: (i, 0)), _resident(wuv.shape),
                  _resident(ws.shape), _resident(bsT.shape), _resident(lng.shape),
                  _resident(lnb.shape)],
        out_specs=pl.BlockSpec((tm, GMLP_WIDTH), lambda i: (i, 0)),
        out_shape=jax.ShapeDtypeStruct((SEQ, GMLP_WIDTH), BF16),
        compiler_params=_params(),
        name="gmlp",
    )(n1, wuv, ws, bsT, lng, lnb)


def _attn_kernel(q_ref, qi_ref, wi_ref, kkT_ref, ckv_ref, ckvT_ref, wuk_ref, wuv_ref,
                 a_ref,
                 sc_ref, qabs_ref, wb_ref, mx_ref, mn_ref, thr_ref, m_ref, l_ref, acc_ref):
    i = pl.program_id(0)
    nkb = i + 1
    nrep = KB // LANES

    for h in range(N_HEADS):
        qa = _dot(q_ref[:, h * HEAD_DIM:(h + 1) * HEAD_DIM], wuk_ref[h]) * (HEAD_DIM ** -0.5)
        qabs_ref[h] = qa.astype(BF16)
    wi = wi_ref[...]
    for h in range(IDX_HEADS):
        wb_ref[h] = jnp.broadcast_to(wi[:, h:h + 1], (TQ, KB))

    mx_ref[...] = jnp.full((TQ, LANES), -jnp.inf, F32)
    mn_ref[...] = jnp.full((TQ, LANES), jnp.inf, F32)
    q_pos = i * TQ + lax.broadcasted_iota(jnp.int32, (TQ, KB), 0)
    k_off = lax.broadcasted_iota(jnp.int32, (TQ, KB), 1)

    def score_body(kb, carry):
        k_even = kkT_ref[kb, 0:LANES, :]
        k_odd = kkT_ref[kb, LANES:2 * LANES, :]
        acc = jnp.zeros((TQ, KB), F32)
        for j in range(IDX_HEADS // 2):
            qp = qi_ref[:, j * LANES:(j + 1) * LANES]
            acc = acc + jnp.maximum(_dot(qp, k_even), 0.0) * wb_ref[2 * j]
            acc = acc + jnp.maximum(_dot(qp, k_odd), 0.0) * wb_ref[2 * j + 1]
        causal = (kb * KB + k_off) <= q_pos
        sc_ref[kb] = jnp.where(causal, acc, -jnp.inf)
        hi = jnp.where(causal, acc, -jnp.inf)
        lo = jnp.where(causal, acc, jnp.inf)
        bmax = hi[:, :LANES]
        bmin = lo[:, :LANES]
        for c in range(1, nrep):
            bmax = jnp.maximum(bmax, hi[:, c * LANES:(c + 1) * LANES])
            bmin = jnp.minimum(bmin, lo[:, c * LANES:(c + 1) * LANES])
        mx_ref[...] = jnp.maximum(mx_ref[...], bmax)
        mn_ref[...] = jnp.minimum(mn_ref[...], bmin)
        return carry

    lax.fori_loop(0, nkb, score_body, 0)

    for rg in range(TQ // RG):
        rows = slice(rg * RG, (rg + 1) * RG)
        row_max = jnp.max(mx_ref[rows, :], axis=-1, keepdims=True)
        row_min = jnp.min(mn_ref[rows, :], axis=-1, keepdims=True)
        n_causal = (i * TQ + rg * RG + 1 + lax.broadcasted_iota(jnp.int32, (RG, LANES), 0)).astype(F32)
        want = jnp.minimum(n_causal, float(TOPK))
        lo0 = jnp.broadcast_to(row_min, (RG, LANES))
        hi0 = jnp.broadcast_to(row_max, (RG, LANES))
        active0 = (n_causal != want).astype(F32)

        def search_cond(state):
            it, _, _, _, flag = state
            return jnp.logical_and(it < MAX_SEARCH_ITERS, flag > 0.0)

        def search_body(state):
            it, lo, hi, active, _ = state
            mid = 0.5 * lo + 0.5 * hi

            def count_body(kb, cnt):
                s = sc_ref[kb, rows, :]
                for c in range(nrep):
                    cnt = cnt + jnp.where(s[:, c * LANES:(c + 1) * LANES] >= mid, 1.0, 0.0)
                return cnt

            cnt = lax.fori_loop(0, nkb, count_body, jnp.zeros((RG, LANES), F32))
            c = jnp.broadcast_to(jnp.sum(cnt, axis=-1, keepdims=True), (RG, LANES))
            live = active > 0.0
            progress = jnp.logical_and(mid > lo, mid < hi)
            ge = c >= want
            lo = jnp.where(jnp.logical_and(live, jnp.logical_and(progress, ge)), mid, lo)
            hi = jnp.where(jnp.logical_and(live, jnp.logical_and(progress, jnp.logical_not(ge))), mid, hi)
            found = jnp.logical_and(ge, c == want)
            still = jnp.logical_and(live, jnp.logical_and(progress, jnp.logical_not(found)))
            active = still.astype(F32)
            return it + 1, lo, hi, active, jnp.max(active)

        _, lo_f, _, _, _ = lax.while_loop(
            search_cond, search_body, (jnp.int32(0), lo0, hi0, active0, jnp.max(active0)))
        thr_ref[rows, :] = lo_f

    m_ref[...] = jnp.full(m_ref.shape, MASK_VALUE, F32)
    l_ref[...] = jnp.zeros(l_ref.shape, F32)
    acc_ref[...] = jnp.zeros(acc_ref.shape, F32)

    def att_body(kb, carry):
        bias = jnp.where(sc_ref[kb] >= _rep(thr_ref[...], nrep), 0.0, MASK_VALUE)
        c_t = ckvT_ref[kb]
        c_n = ckv_ref[pl.ds(pl.multiple_of(kb * KB, KB), KB), :]
        for h in range(N_HEADS):
            lg = _dot(qabs_ref[h], c_t) + bias
            m_old = m_ref[h]
            m_new = jnp.maximum(m_old, jnp.max(lg, axis=-1, keepdims=True))
            alpha = jnp.exp(m_old - m_new)
            p = jnp.exp(lg - _rep(m_new, nrep))
            l_ref[h] = alpha * l_ref[h] + jnp.sum(p, axis=-1, keepdims=True)
            acc_ref[h] = acc_ref[h] * _rep(alpha, KV_LATENT // LANES) + _dot(p.astype(BF16), c_n)
            m_ref[h] = m_new
        return carry

    lax.fori_loop(0, nkb, att_body, 0)

    for h in range(N_HEADS):
        o = acc_ref[h] / _rep(l_ref[h], KV_LATENT // LANES)
        a_ref[:, h * HEAD_DIM:(h + 1) * HEAD_DIM] = _dot(o.astype(BF16), wuv_ref[h]).astype(BF16)


def _attn(q, qi, wi, kkT, ckv, ckvT, wuk, wuv):
    row = lambda w: pl.BlockSpec((TQ, w), lambda i: (i, 0))
    return pl.pallas_call(
        _attn_kernel,
        grid=(SEQ // TQ,),
        in_specs=[row(ATTN_WIDTH), row(IDX_HEADS * IDX_DIM), row(LANES), _resident(kkT.shape),
                  _resident(ckv.shape), _resident(ckvT.shape), _resident(wuk.shape),
                  _resident(wuv.shape)],
        out_specs=row(ATTN_WIDTH),
        out_shape=jax.ShapeDtypeStruct((SEQ, ATTN_WIDTH), BF16),
        scratch_shapes=[
            pltpu.VMEM((NKB, TQ, KB), F32),
            pltpu.VMEM((N_HEADS, TQ, KV_LATENT), BF16),
            pltpu.VMEM((IDX_HEADS, TQ, KB), F32),
            pltpu.VMEM((TQ, LANES), F32),
            pltpu.VMEM((TQ, LANES), F32),
            pltpu.VMEM((TQ, LANES), F32),
            pltpu.VMEM((N_HEADS, TQ, LANES), F32),
            pltpu.VMEM((N_HEADS, TQ, LANES), F32),
            pltpu.VMEM((N_HEADS, TQ, KV_LATENT), F32),
        ],
        compiler_params=_params(),
        name="attn",
    )(q, qi, wi, kkT, ckv, ckvT, wuk, wuv)


def _merge_kernel(n1_ref, a_ref, m_ref, wga_ref, wgb_ref, wba_ref, wbg_ref, o_ref):
    n1 = n1_ref[...]
    br_a = jax.nn.sigmoid(_dot(n1, wga_ref[...])) * _dot(a_ref[...], wba_ref[...])
    br_b = jax.nn.sigmoid(_dot(n1, wgb_ref[...])) * _dot(m_ref[...], wbg_ref[...])
    o_ref[...] = (br_a + br_b).astype(BF16)


def _merge(n1, a, m, wga, wgb, wba, wbg):
    tm = ROW_TILE
    row = lambda w: pl.BlockSpec((tm, w), lambda i: (i, 0))
    return pl.pallas_call(
        _merge_kernel,
        grid=(SEQ // tm,),
        in_specs=[row(D_MODEL), row(ATTN_WIDTH), row(GMLP_WIDTH), _resident(wga.shape),
                  _resident(wgb.shape), _resident(wba.shape), _resident(wbg.shape)],
        out_specs=row(D_MODEL),
        out_shape=jax.ShapeDtypeStruct((SEQ, D_MODEL), BF16),
        compiler_params=_params(),
        name="merge",
    )(n1, a, m, wga, wgb, wba, wbg)


def _outproj_kernel(mg_ref, x_ref, wo_ref, g2_ref, wr_ref, br_ref, h_ref, n2_ref, route_ref):
    h = x_ref[...] + _dot(mg_ref[...], wo_ref[...])
    h_ref[...] = h
    n2 = h * lax.rsqrt(jnp.mean(h * h, axis=-1, keepdims=True) + EPS) * g2_ref[...]
    n2_ref[...] = n2
    logits = jnp.dot(n2, wr_ref[...], preferred_element_type=F32,
                     precision=lax.Precision.HIGHEST) + br_ref[...]
    tm = logits.shape[0]
    lane = lax.broadcasted_iota(jnp.int32, (tm, LANES), 1).astype(F32)
    is_group = lane < N_GROUPS
    gl = jnp.where(is_group, logits, -jnp.inf)
    gmax = jnp.max(gl, axis=-1, keepdims=True)
    gsum = jnp.sum(jnp.where(is_group, jnp.exp(logits - gmax), 0.0), axis=-1, keepdims=True)
    g_val = 1.0 / gsum
    g_idx = jnp.min(jnp.where(gl == gmax, lane, float(LANES)), axis=-1, keepdims=True)
    e_id = lane - N_GROUPS
    in_group = jnp.logical_and(
        jnp.logical_and(e_id >= 0, e_id < N_EXPERTS),
        jnp.floor(e_id * (1.0 / EXPERTS_PER_GROUP)) == g_idx)
    sel = jnp.where(in_group, logits, -jnp.inf)
    v1 = jnp.max(sel, axis=-1, keepdims=True)
    i1 = jnp.min(jnp.where(sel == v1, lane, float(LANES)), axis=-1, keepdims=True)
    sel2 = jnp.where(lane == i1, -jnp.inf, sel)
    v2 = jnp.max(sel2, axis=-1, keepdims=True)
    i2 = jnp.min(jnp.where(sel2 == v2, lane, float(LANES)), axis=-1, keepdims=True)
    e2 = jnp.exp(v2 - v1)
    den = 1.0 + e2
    w1 = g_val * (1.0 / den)
    w2 = g_val * (e2 / den)
    route = jnp.where(lane == 0, i1 - N_GROUPS, 0.0)
    route = jnp.where(lane == 1, i2 - N_GROUPS, route)
    route = jnp.where(lane == 2, w1, route)
    route = jnp.where(lane == 3, w2, route)
    route_ref[...] = route


def _outproj(mg, x2, wo, g2, wr, br):
    tm = ROW_TILE
    row = lambda w: pl.BlockSpec((tm, w), lambda i: (i, 0))
    return pl.pallas_call(
        _outproj_kernel,
        grid=(SEQ // tm,),
        in_specs=[row(D_MODEL), row(D_MODEL), _resident(wo.shape), _resident(g2.shape),
                  _resident(wr.shape), _resident(br.shape)],
        out_specs=[row(D_MODEL), row(D_MODEL), row(LANES)],
        out_shape=[jax.ShapeDtypeStruct((SEQ, D_MODEL), F32),
                   jax.ShapeDtypeStruct((SEQ, D_MODEL), F32),
                   jax.ShapeDtypeStruct((SEQ, LANES), F32)],
        compiler_params=_params(),
        name="outproj",
    )(mg, x2, wo, g2, wr, br)


def _row_copy(src_hbm, row, dst, slot, r, sem):
    return pltpu.make_async_copy(src_hbm.at[pl.ds(row, 1), :], dst.at[slot, pl.ds(r, 1), :],
                                 sem.at[slot])


def _expert_kernel(te_ref, nt_ref, tok_ref, tokn_ref, x_hbm, wg_ref, wu_ref, wd_ref,
                   y_ref, xbuf, sem, wgb, wub, wdb):
    i = pl.program_id(0)
    nt = nt_ref[0]

    def issue(slot, ids_ref):
        def body(r, carry):
            _row_copy(x_hbm, ids_ref[0, 0, r], xbuf, slot, r, sem).start()
            return carry
        lax.fori_loop(0, EXPERT_TILE, body, 0)

    @pl.when(i == 0)
    def _():
        issue(0, tok_ref)

    @pl.when(i + 1 < nt)
    def _():
        issue((i + 1) % 2, tokn_ref)

    @pl.when(i < nt)
    def _():
        slot = i % 2

        def wait_body(r, carry):
            _row_copy(x_hbm, 0, xbuf, slot, r, sem).wait()
            return carry
        lax.fori_loop(0, EXPERT_TILE, wait_body, 0)

        new_expert = jnp.logical_or(i == 0, te_ref[i] != te_ref[jnp.maximum(i - 1, 0)])

        @pl.when(new_expert)
        def _():
            wgb[...] = wg_ref[0].astype(BF16)
            wub[...] = wu_ref[0].astype(BF16)
            wdb[...] = wd_ref[0].astype(BF16)

        xt = xbuf[slot].astype(BF16)
        hid = jax.nn.silu(_dot(xt, wgb[...])) * _dot(xt, wub[...])
        y_ref[...] = _dot(hid.astype(BF16), wdb[...])


def _experts(tile_expert, n_tiles, sorted_tok, n2, w_gate, w_up, w_down):
    last = lambda i, nt: jnp.minimum(i, nt[0] - 1)
    wspec = lambda s: pl.BlockSpec((1,) + s, lambda i, te, nt: (te[last(i, nt)], 0, 0))
    smem_blk = lambda f: pl.BlockSpec((1, 1, EXPERT_TILE), f, memory_space=pltpu.SMEM)
    grid_spec = pltpu.PrefetchScalarGridSpec(
        num_scalar_prefetch=2,
        grid=(N_EXPERT_TILES,),
        in_specs=[
            smem_blk(lambda i, te, nt: (last(i, nt), 0, 0)),
            smem_blk(lambda i, te, nt: (last(i + 1, nt), 0, 0)),
            pl.BlockSpec(memory_space=pl.ANY),
            wspec((D_MODEL, EXPERT_FF)), wspec((D_MODEL, EXPERT_FF)), wspec((EXPERT_FF, D_MODEL)),
        ],
        out_specs=pl.BlockSpec((EXPERT_TILE, D_MODEL), lambda i, te, nt: (last(i, nt), 0)),
        scratch_shapes=[
            pltpu.VMEM((2, EXPERT_TILE, D_MODEL), F32),
            pltpu.SemaphoreType.DMA((2,)),
            pltpu.VMEM((D_MODEL, EXPERT_FF), BF16),
            pltpu.VMEM((D_MODEL, EXPERT_FF), BF16),
            pltpu.VMEM((EXPERT_FF, D_MODEL), BF16),
        ],
    )
    return pl.pallas_call(
        _expert_kernel,
        grid_spec=grid_spec,
        out_shape=jax.ShapeDtypeStruct((N_EXPERT_TILES * EXPERT_TILE, D_MODEL), F32),
        compiler_params=_params(),
        name="experts",
    )(tile_expert, n_tiles, sorted_tok, sorted_tok, n2, w_gate, w_up, w_down)


def _final_kernel(pos_ref, posn_ref, h_ref, route_ref, gf_ref, y_hbm, o_ref, ybuf, sem):
    i = pl.program_id(0)
    n = pl.num_programs(0)
    tm = ROW_TILE

    def issue(slot, ids_ref):
        def body(r, carry):
            _row_copy(y_hbm, ids_ref[0, 0, r], ybuf, slot, r, sem).start()
            return carry
        lax.fori_loop(0, 2 * tm, body, 0)

    @pl.when(i == 0)
    def _():
        issue(0, pos_ref)

    @pl.when(i + 1 < n)
    def _():
        issue((i + 1) % 2, posn_ref)

    slot = i % 2

    def wait_body(r, carry):
        _row_copy(y_hbm, 0, ybuf, slot, r, sem).wait()
        return carry
    lax.fori_loop(0, 2 * tm, wait_body, 0)

    route = route_ref[...]
    w1 = route[:, 2:3]
    w2 = route[:, 3:4]
    h2 = h_ref[...] + (w1 * ybuf[slot, 0:tm, :] + w2 * ybuf[slot, tm:2 * tm, :])
    o_ref[...] = h2 * lax.rsqrt(jnp.mean(h2 * h2, axis=-1, keepdims=True) + EPS) * gf_ref[...]


def _final(pos, h, route, gf, y_sorted):
    tm = ROW_TILE
    n = SEQ // tm
    row = lambda w: pl.BlockSpec((tm, w), lambda i: (i, 0))
    smem_blk = lambda f: pl.BlockSpec((1, 1, 2 * tm), f, memory_space=pltpu.SMEM)
    return pl.pallas_call(
        _final_kernel,
        grid=(n,),
        in_specs=[smem_blk(lambda i: (i, 0, 0)),
                  smem_blk(lambda i: (jnp.minimum(i + 1, n - 1), 0, 0)),
                  row(D_MODEL), row(LANES), _resident(gf.shape),
                  pl.BlockSpec(memory_space=pl.ANY)],
        out_specs=row(D_MODEL),
        out_shape=jax.ShapeDtypeStruct((SEQ, D_MODEL), F32),
        scratch_shapes=[pltpu.VMEM((2, 2 * tm, D_MODEL), F32), pltpu.SemaphoreType.DMA((2,))],
        compiler_params=_params(),
        name="final",
    )(pos, pos, h, route, gf, y_sorted)


def _dispatch(route):
    e_flat = route[:, 0:2].astype(jnp.int32).reshape(-1)
    onehot = (e_flat[:, None] == jnp.arange(N_EXPERTS, dtype=jnp.int32)[None, :]).astype(jnp.int32)
    csum = jnp.cumsum(onehot, axis=0)
    rank = jnp.take_along_axis(csum, e_flat[:, None], axis=1)[:, 0] - 1
    counts = csum[-1]
    tiles_per = (counts + EXPERT_TILE - 1) // EXPERT_TILE
    tile_end = jnp.cumsum(tiles_per)
    tile_start = tile_end - tiles_per
    pos = tile_start[e_flat] * EXPERT_TILE + rank
    tok = jnp.arange(2 * SEQ, dtype=jnp.int32) // 2
    sorted_tok = jnp.zeros((N_EXPERT_TILES * EXPERT_TILE,), jnp.int32).at[pos].set(tok)
    tile_ids = jnp.arange(N_EXPERT_TILES, dtype=jnp.int32)
    tile_expert = jnp.sum((tile_end[None, :] <= tile_ids[:, None]).astype(jnp.int32), axis=1)
    tile_expert = jnp.minimum(tile_expert, N_EXPERTS - 1)
    n_tiles = tile_end[-1:].astype(jnp.int32)
    return pos, sorted_tok, tile_expert, n_tiles


def kernel(x, norm1_g, w_in, kv_norm_g, w_uk, w_uv, gmlp_ws, gmlp_bs, ln_v_g, ln_v_b, w_br_attn,
           w_br_gmlp, w_out, norm2_g, w_group, b_group, w_router, b_router, w_e_gate, w_e_up,
           w_e_down, norm_f_g):
    assert x.shape == (1, SEQ, D_MODEL)
    x2 = x.reshape(SEQ, D_MODEL)
    row_vec = lambda v: v.reshape(1, -1).astype(F32)

    c_q = ATTN_WIDTH
    c_kv = c_q + KV_LATENT
    c_qi = c_kv + IDX_HEADS * IDX_DIM
    c_k = c_qi + IDX_DIM
    c_w = c_k + IDX_HEADS
    c_uv = c_w + 2 * GMLP_WIDTH
    w_bf = w_in.astype(BF16)
    wq, wc, wqi = w_bf[:, :c_q], w_bf[:, c_q:c_kv], w_bf[:, c_kv:c_qi]
    wk, ww = w_bf[:, c_qi:c_k], w_bf[:, c_k:c_w]
    wuv_in, wga, wgb = w_bf[:, c_w:c_uv], w_bf[:, c_uv:c_uv + D_MODEL], w_bf[:, c_uv + D_MODEL:]
    zk = jnp.zeros_like(wk)
    wkk = jnp.concatenate([wk, zk, zk, wk], axis=1)
    wwp = jnp.pad(ww, ((0, 0), (0, LANES - IDX_HEADS)))

    n1, q, ckv, ckvT, qi, kkT, wi = _proj(x2, row_vec(norm1_g), wq, wc, wqi, wkk, wwp,
                                          row_vec(kv_norm_g))
    m = _gmlp(n1, wuv_in, gmlp_ws, jnp.pad(gmlp_bs.T, ((0, 0), (0, LANES - GMLP_GROUPS))),
              row_vec(ln_v_g), row_vec(ln_v_b))
    a = _attn(q, qi, wi, kkT, ckv, ckvT, w_uk.astype(BF16), w_uv.astype(BF16))
    mg = _merge(n1, a, m, wga, wgb, w_br_attn.astype(BF16), w_br_gmlp.astype(BF16))

    w_route = jnp.pad(jnp.concatenate([w_group, w_router], axis=1),
                      ((0, 0), (0, LANES - N_GROUPS - N_EXPERTS)))
    b_route = jnp.pad(jnp.concatenate([b_group, b_router]), (0, LANES - N_GROUPS - N_EXPERTS))
    h, n2, route = _outproj(mg, x2, w_out.astype(BF16), row_vec(norm2_g), w_route,
                            row_vec(b_route))

    pos, sorted_tok, tile_expert, n_tiles = _dispatch(route)
    y_sorted = _experts(tile_expert, n_tiles,
                        sorted_tok.reshape(N_EXPERT_TILES, 1, EXPERT_TILE), n2,
                        w_e_gate, w_e_up, w_e_down)
    pos_tiles = pos.reshape(SEQ // ROW_TILE, ROW_TILE, 2).transpose(0, 2, 1).reshape(
        SEQ // ROW_TILE, 1, 2 * ROW_TILE)
    out = _final(pos_tiles, h, route, row_vec(norm_f_g), y_sorted)
    return out.reshape(1, SEQ, D_MODEL)
```

```python
import jax
import jax.numpy as jnp
import numpy as np
from jax import lax
from jax.experimental import pallas as pl
from jax.experimental.pallas import tpu as pltpu

F32 = jnp.float32
BF16 = jnp.bfloat16

D_MODEL = 2048
SEQ = 8192
N_HEADS = 8
HEAD_DIM = 128
KV_LATENT = 256
IDX_HEADS = 16
IDX_DIM = 64
TOPK = 256
ATTN_WIDTH = N_HEADS * HEAD_DIM
GMLP_GROUPS = 8
GMLP_WIDTH = 1024
CHUNK = 128
N_GROUPS = 8
EXPERTS_PER_GROUP = 8
N_EXPERTS = 64
EXPERT_FF = 512
EPS = 1e-6

LANES = 128
SUBLANES = 8
VMEM_LIMIT = 60 * 1024 * 1024
MASK_VALUE = -0.7 * float(np.finfo(np.float32).max)
LOG2_E = float(np.log2(np.e))

ROW_TILE = 256
TQ = 256
KB = 256
NKB = SEQ // KB
COUNT_ROWS = 32
MAX_SEARCH_ITERS = 64
EXPERT_TILE = 256
N_EXPERT_TILES = (2 * SEQ) // EXPERT_TILE + N_EXPERTS
DMA_UNROLL = 8


def _dot(a, b):
    return jnp.dot(a, b, preferred_element_type=F32)


def _dot_nt(a, b):
    return lax.dot_general(a, b, (((1,), (1,)), ((), ())), preferred_element_type=F32)


def _resident(shape):
    zeros = (0,) * len(shape)
    return pl.BlockSpec(shape, lambda *_: zeros, pipeline_mode=pl.Buffered(1))


def _params(n_axes=1):
    return pltpu.CompilerParams(
        dimension_semantics=("arbitrary",) * n_axes, vmem_limit_bytes=VMEM_LIMIT)


def _proj_kernel(x_ref, g1_ref, wqT_ref, wc_ref, wqiT_ref, wkk_ref, wwT_ref, kvg_ref,
                 n1_ref, qT_ref, ckv_ref, ckvT_ref, qiT_ref, kk2_ref, wiT_ref):
    x = x_ref[...]
    ms = jnp.mean(x * x, axis=-1, keepdims=True)
    n1 = (x * lax.rsqrt(ms + EPS) * g1_ref[...]).astype(BF16)
    n1_ref[...] = n1
    qT_ref[...] = _dot_nt(wqT_ref[...], n1).astype(BF16)
    qiT_ref[...] = (_dot_nt(wqiT_ref[...], n1) * (IDX_DIM ** -0.5)).astype(BF16)
    wiT_ref[...] = _dot_nt(wwT_ref[...], n1) * (IDX_HEADS ** -0.5)
    c = _dot(n1, wc_ref[...])
    c = c * lax.rsqrt(jnp.mean(c * c, axis=-1, keepdims=True) + EPS) * kvg_ref[...]
    ckv_ref[...] = c.astype(BF16)
    ckvT_ref[0] = c.T.astype(BF16)
    kk = _dot(n1, wkk_ref[...]).astype(BF16)
    kk2_ref[0, 0:KB, :] = kk[:, 0:LANES]
    kk2_ref[0, KB:2 * KB, :] = kk[:, LANES:2 * LANES]


def _proj(x2, g1, wqT, wc, wqiT, wkk, wwT, kvg):
    tm = KB
    row = lambda w: pl.BlockSpec((tm, w), lambda i: (i, 0))
    col = lambda h: pl.BlockSpec((h, tm), lambda i: (0, i))
    return pl.pallas_call(
        _proj_kernel,
        grid=(SEQ // tm,),
        in_specs=[row(D_MODEL), _resident(g1.shape), _resident(wqT.shape), _resident(wc.shape),
                  _resident(wqiT.shape), _resident(wkk.shape), _resident(wwT.shape),
                  _resident(kvg.shape)],
        out_specs=[row(D_MODEL), col(ATTN_WIDTH), row(KV_LATENT),
                   pl.BlockSpec((1, KV_LATENT, tm), lambda i: (i, 0, 0)),
                   col(IDX_HEADS * IDX_DIM),
                   pl.BlockSpec((1, 2 * tm, LANES), lambda i: (i, 0, 0)),
                   col(IDX_HEADS)],
        out_shape=[
            jax.ShapeDtypeStruct((SEQ, D_MODEL), BF16),
            jax.ShapeDtypeStruct((ATTN_WIDTH, SEQ), BF16),
            jax.ShapeDtypeStruct((SEQ, KV_LATENT), BF16),
            jax.ShapeDtypeStruct((NKB, KV_LATENT, KB), BF16),
            jax.ShapeDtypeStruct((IDX_HEADS * IDX_DIM, SEQ), BF16),
            jax.ShapeDtypeStruct((NKB, 2 * KB, LANES), BF16),
            jax.ShapeDtypeStruct((IDX_HEADS, SEQ), F32),
        ],
        compiler_params=_params(),
        name="proj",
    )(x2, g1, wqT, wc, wqiT, wkk, wwT, kvg)


def _gmlp_kernel(n1_ref, wuv_ref, ws_ref, bsT_ref, lng_ref, lnb_ref, m_ref):
    uv = _dot(n1_ref[...], wuv_ref[...])
    z = jax.nn.gelu(uv)
    u = z[:, :GMLP_WIDTH]
    v = z[:, GMLP_WIDTH:]
    mu = jnp.mean(v, axis=-1, keepdims=True)
    var = jnp.mean(jnp.square(v - mu), axis=-1, keepdims=True)
    vn = ((v - mu) * lax.rsqrt(var + EPS) * lng_ref[...] + lnb_ref[...]).astype(BF16)
    t_pos = lax.broadcasted_iota(jnp.int32, (CHUNK, CHUNK), 0)
    s_pos = lax.broadcasted_iota(jnp.int32, (CHUNK, CHUNK), 1)
    causal = s_pos <= t_pos
    for g in range(GMLP_GROUPS):
        wm = jnp.where(causal, ws_ref[g], 0.0).astype(BF16)
        bias = bsT_ref[:, g:g + 1]
        cols = slice(g * LANES, (g + 1) * LANES)
        for c in range(ROW_TILE // CHUNK):
            rows = slice(c * CHUNK, (c + 1) * CHUNK)
            y = _dot(wm, vn[rows, cols]) + bias
            m_ref[rows, cols] = (u[rows, cols] * y).astype(BF16)


def _gmlp(n1, wuv, ws, bsT, lng, lnb):
    tm = ROW_TILE
    return pl.pallas_call(
        _gmlp_kernel,
        grid=(SEQ // tm,),
        in_specs=[pl.BlockSpec((tm, D_MODEL), lambda i: (i, 0)), _resident(wuv.shape),
                  _resident(ws.shape), _resident(bsT.shape), _resident(lng.shape),
                  _resident(lnb.shape)],
        out_specs=pl.BlockSpec((tm, GMLP_WIDTH), lambda i: (i, 0)),
        out_shape=jax.ShapeDtypeStruct((SEQ, GMLP_WIDTH), BF16),
        compiler_params=_params(),
        name="gmlp",
    )(n1, wuv, ws, bsT, lng, lnb)


def _attn_kernel(qT_ref, qiT_ref, wiT_ref, kk2_ref, ckv_ref, ckvT_ref, wukT_ref, wuvT_ref,
                 a_ref,
                 sc_ref, qabs_ref, thr_ref, keep_ref, m_ref, l_ref, acc_ref, lg_ref):
    i = pl.program_id(0)
    nkb = i + 1

    for h in range(N_HEADS):
        qa = _dot(wukT_ref[h], qT_ref[h * HEAD_DIM:(h + 1) * HEAD_DIM, :]) * (
            HEAD_DIM ** -0.5 * LOG2_E)
        qabs_ref[h] = qa.astype(BF16)

    q_pos = i * TQ + lax.broadcasted_iota(jnp.int32, (KB, TQ), 1)
    k_off = lax.broadcasted_iota(jnp.int32, (KB, TQ), 0)

    def score_body(kb, carry):
        smax, smin = carry
        keys = kk2_ref[kb]
        acc = jnp.zeros((KB, TQ), F32)
        for j in range(IDX_HEADS // 2):
            d = _dot(keys, qiT_ref[j * LANES:(j + 1) * LANES, :])
            acc = acc + jnp.maximum(d[0:KB], 0.0) * wiT_ref[2 * j:2 * j + 1, :]
            acc = acc + jnp.maximum(d[KB:2 * KB], 0.0) * wiT_ref[2 * j + 1:2 * j + 2, :]
        causal = (kb * KB + k_off) <= q_pos
        sc_ref[kb] = jnp.where(causal, acc, -jnp.inf)
        smax = jnp.maximum(smax, jnp.max(jnp.where(causal, acc, -jnp.inf), axis=0, keepdims=True))
        smin = jnp.minimum(smin, jnp.min(jnp.where(causal, acc, jnp.inf), axis=0, keepdims=True))
        return smax, smin

    smax, smin = lax.fori_loop(
        0, nkb, score_body,
        (jnp.full((1, TQ), -jnp.inf, F32), jnp.full((1, TQ), jnp.inf, F32)))

    def count_ge(x):
        def body(kb, cnt):
            ge = jnp.where(sc_ref[kb] >= x, 1.0, 0.0)
            return cnt + jnp.sum(ge.reshape(KB // COUNT_ROWS, COUNT_ROWS, TQ), axis=0)
        cnt = lax.fori_loop(0, nkb, body, jnp.zeros((COUNT_ROWS, TQ), F32))
        return jnp.sum(cnt, axis=0, keepdims=True)

    n_causal = (i * TQ + 1 + lax.broadcasted_iota(jnp.int32, (1, TQ), 1)).astype(F32)
    want = jnp.minimum(n_causal, float(TOPK))
    hi0 = smax + jnp.maximum(jnp.abs(smax), 1e-30) * 1e-6
    open0 = (n_causal != want).astype(F32)

    def search_cond(state):
        it, _, _, _, _, flag = state
        return jnp.logical_and(it < MAX_SEARCH_ITERS, flag > 0.0)

    def search_body(state):
        it, lo, hi, c_lo, open_, _ = state
        mid = 0.5 * lo + 0.5 * hi
        c = count_ge(mid)
        live = jnp.logical_and(open_ > 0.0, jnp.logical_and(mid > lo, mid < hi))
        ge = c >= want
        go_lo = jnp.logical_and(live, ge)
        lo = jnp.where(go_lo, mid, lo)
        c_lo = jnp.where(go_lo, c, c_lo)
        hi = jnp.where(jnp.logical_and(live, jnp.logical_not(ge)), mid, hi)
        open_ = jnp.logical_and(live, c != want).astype(F32)
        return it + 1, lo, hi, c_lo, open_, jnp.max(open_)

    _, lo_f, _, c_lo_f, _, _ = lax.while_loop(
        search_cond, search_body,
        (jnp.int32(0), smin, hi0, n_causal, open0, jnp.max(open0)))
    thr_ref[...] = jnp.broadcast_to(lo_f, (SUBLANES, TQ))
    keep_ref[...] = jnp.full((SUBLANES, TQ), float(SEQ), F32)
    unresolved0 = (c_lo_f != want).astype(F32)

    @pl.when(jnp.max(unresolved0) > 0.0)
    def _():
        def next_value(lo, below):
            def body(kb, u):
                s = sc_ref[kb]
                cand = jnp.logical_and(s >= lo, s > below)
                return jnp.minimum(u, jnp.min(jnp.where(cand, s, jnp.inf), axis=0, keepdims=True))
            return lax.fori_loop(0, nkb, body, jnp.full((1, TQ), jnp.inf, F32))

        def count_gt(x):
            def body(kb, cnt):
                gt = jnp.where(sc_ref[kb] > x, 1.0, 0.0)
                return cnt + jnp.sum(gt, axis=0, keepdims=True)
            return lax.fori_loop(0, nkb, body, jnp.zeros((1, TQ), F32))

        def peel_cond(state):
            return state[-1] > 0.0

        def peel_body(state):
            below, unres, thr, keep, _ = state
            u = next_value(lo_f, below)
            c_gt = count_gt(u)
            hit = jnp.logical_and(unres > 0.0, c_gt < want)
            thr = jnp.where(hit, u, thr)
            keep = jnp.where(hit, want - c_gt, keep)
            unres = jnp.logical_and(unres > 0.0, jnp.logical_not(hit)).astype(F32)
            return u, unres, thr, keep, jnp.max(unres)

        _, _, thr_t, keep_t, _ = lax.while_loop(
            peel_cond, peel_body,
            (jnp.full((1, TQ), -jnp.inf, F32), unresolved0, lo_f,
             jnp.full((1, TQ), float(SEQ), F32), jnp.float32(1.0)))
        thr_ref[...] = jnp.broadcast_to(thr_t, (SUBLANES, TQ))
        keep_ref[...] = jnp.broadcast_to(keep_t, (SUBLANES, TQ))

        r_i = lax.broadcasted_iota(jnp.int32, (KB, KB), 0)
        c_i = lax.broadcasted_iota(jnp.int32, (KB, KB), 1)
        before = (c_i < r_i).astype(BF16)

        def drop_body(kb, seen):
            s = sc_ref[kb]
            eq = jnp.logical_and(s == thr_t, unresolved0 > 0.0)
            eq_f = eq.astype(F32)
            rank = seen + _dot(before, eq_f.astype(BF16))
            sc_ref[kb] = jnp.where(jnp.logical_and(eq, rank >= keep_t), -jnp.inf, s)
            return seen + jnp.sum(eq_f, axis=0, keepdims=True)

        lax.fori_loop(0, nkb, drop_body, jnp.zeros((1, TQ), F32))

    m_ref[...] = jnp.full(m_ref.shape, MASK_VALUE, F32)
    l_ref[...] = jnp.zeros(l_ref.shape, F32)
    acc_ref[...] = jnp.zeros(acc_ref.shape, F32)

    @pl.when(nkb % 2 == 1)
    def _():
        sc_ref[nkb] = jnp.full((KB, TQ), -jnp.inf, F32)

    def logits_stage(j, slot):
        thr = thr_ref[0:1, :]
        bias0 = jnp.where(sc_ref[2 * j] >= thr, 0.0, MASK_VALUE)
        bias1 = jnp.where(sc_ref[2 * j + 1] >= thr, 0.0, MASK_VALUE)
        c_n = ckv_ref[pl.ds(pl.multiple_of(j * (2 * KB), 2 * KB), 2 * KB), :]
        for h in range(N_HEADS):
            lg = _dot(c_n, qabs_ref[h])
            lg_ref[slot, h, 0:KB, :] = lg[0:KB] + bias0
            lg_ref[slot, h, KB:2 * KB, :] = lg[KB:2 * KB] + bias1

    def softmax_stage(j, slot):
        c_t0 = ckvT_ref[2 * j]
        c_t1 = ckvT_ref[2 * j + 1]
        for h in range(N_HEADS):
            lg0 = lg_ref[slot, h, 0:KB, :]
            lg1 = lg_ref[slot, h, KB:2 * KB, :]
            m_old = m_ref[h, 0:1, :]
            m_new = jnp.maximum(m_old, jnp.max(jnp.maximum(lg0, lg1), axis=0, keepdims=True))
            alpha = jnp.exp2(m_old - m_new)
            p0 = jnp.exp2(lg0 - m_new)
            p1 = jnp.exp2(lg1 - m_new)
            l_new = alpha * l_ref[h, 0:1, :] + jnp.sum(p0 + p1, axis=0, keepdims=True)
            acc_ref[h] = (acc_ref[h] * alpha + _dot(c_t0, p0.astype(BF16))
                          + _dot(c_t1, p1.astype(BF16)))
            m_ref[h] = jnp.broadcast_to(m_new, (SUBLANES, TQ))
            l_ref[h] = jnp.broadcast_to(l_new, (SUBLANES, TQ))

    n_pairs = (nkb + 1) // 2
    logits_stage(0, 0)

    def att_body(j, carry):
        for slot in range(2):
            @pl.when(j % 2 == slot)
            def _():
                logits_stage(j + 1, 1 - slot)
                softmax_stage(j, slot)
        return carry

    lax.fori_loop(0, n_pairs - 1, att_body, 0)
    for slot in range(2):
        @pl.when((n_pairs - 1) % 2 == slot)
        def _():
            softmax_stage(n_pairs - 1, slot)

    for h in range(N_HEADS):
        o_t = (acc_ref[h] / l_ref[h, 0:1, :]).astype(BF16)
        a_t = _dot(wuvT_ref[h], o_t)
        a_ref[:, h * HEAD_DIM:(h + 1) * HEAD_DIM] = a_t.T.astype(BF16)


def _attn(qT, qiT, wiT, kk2, ckv, ckvT, wukT, wuvT):
    col = lambda h: pl.BlockSpec((h, TQ), lambda i: (0, i))
    return pl.pallas_call(
        _attn_kernel,
        grid=(SEQ // TQ,),
        in_specs=[col(ATTN_WIDTH), col(IDX_HEADS * IDX_DIM), col(IDX_HEADS), _resident(kk2.shape),
                  _resident(ckv.shape), _resident(ckvT.shape), _resident(wukT.shape),
                  _resident(wuvT.shape)],
        out_specs=pl.BlockSpec((TQ, ATTN_WIDTH), lambda i: (i, 0)),
        out_shape=jax.ShapeDtypeStruct((SEQ, ATTN_WIDTH), BF16),
        scratch_shapes=[
            pltpu.VMEM((NKB, KB, TQ), F32),
            pltpu.VMEM((N_HEADS, KV_LATENT, TQ), BF16),
            pltpu.VMEM((SUBLANES, TQ), F32),
            pltpu.VMEM((SUBLANES, TQ), F32),
            pltpu.VMEM((N_HEADS, SUBLANES, TQ), F32),
            pltpu.VMEM((N_HEADS, SUBLANES, TQ), F32),
            pltpu.VMEM((N_HEADS, KV_LATENT, TQ), F32),
            pltpu.VMEM((2, N_HEADS, 2 * KB, TQ), F32),
        ],
        compiler_params=_params(),
        name="attn",
    )(qT, qiT, wiT, kk2, ckv, ckvT, wukT, wuvT)


def _merge_kernel(n1_ref, a_ref, m_ref, wga_ref, wgb_ref, wba_ref, wbg_ref, o_ref):
    n1 = n1_ref[...]
    br_a = jax.nn.sigmoid(_dot(n1, wga_ref[...])) * _dot(a_ref[...], wba_ref[...])
    br_b = jax.nn.sigmoid(_dot(n1, wgb_ref[...])) * _dot(m_ref[...], wbg_ref[...])
    o_ref[...] = (br_a + br_b).astype(BF16)


def _merge(n1, a, m, wga, wgb, wba, wbg):
    tm = ROW_TILE
    row = lambda w: pl.BlockSpec((tm, w), lambda i: (i, 0))
    return pl.pallas_call(
        _merge_kernel,
        grid=(SEQ // tm,),
        in_specs=[row(D_MODEL), row(ATTN_WIDTH), row(GMLP_WIDTH), _resident(wga.shape),
                  _resident(wgb.shape), _resident(wba.shape), _resident(wbg.shape)],
        out_specs=row(D_MODEL),
        out_shape=jax.ShapeDtypeStruct((SEQ, D_MODEL), BF16),
        compiler_params=_params(),
        name="merge",
    )(n1, a, m, wga, wgb, wba, wbg)


def _outproj_kernel(mg_ref, x_ref, wo_ref, g2_ref, wr_ref, br_ref, h_ref, n2_ref, route_ref):
    h = x_ref[...] + _dot(mg_ref[...], wo_ref[...])
    h_ref[...] = h
    n2 = h * lax.rsqrt(jnp.mean(h * h, axis=-1, keepdims=True) + EPS) * g2_ref[...]
    n2_ref[...] = n2
    logits = jnp.dot(n2, wr_ref[...], preferred_element_type=F32,
                     precision=lax.Precision.HIGHEST) + br_ref[...]
    tm = logits.shape[0]
    lane = lax.broadcasted_iota(jnp.int32, (tm, LANES), 1).astype(F32)
    is_group = lane < N_GROUPS
    gl = jnp.where(is_group, logits, -jnp.inf)
    gmax = jnp.max(gl, axis=-1, keepdims=True)
    gsum = jnp.sum(jnp.where(is_group, jnp.exp(logits - gmax), 0.0), axis=-1, keepdims=True)
    g_val = 1.0 / gsum
    g_idx = jnp.min(jnp.where(gl == gmax, lane, float(LANES)), axis=-1, keepdims=True)
    e_id = lane - N_GROUPS
    in_group = jnp.logical_and(
        jnp.logical_and(e_id >= 0, e_id < N_EXPERTS),
        jnp.floor(e_id * (1.0 / EXPERTS_PER_GROUP)) == g_idx)
    sel = jnp.where(in_group, logits, -jnp.inf)
    v1 = jnp.max(sel, axis=-1, keepdims=True)
    i1 = jnp.min(jnp.where(sel == v1, lane, float(LANES)), axis=-1, keepdims=True)
    sel2 = jnp.where(lane == i1, -jnp.inf, sel)
    v2 = jnp.max(sel2, axis=-1, keepdims=True)
    i2 = jnp.min(jnp.where(sel2 == v2, lane, float(LANES)), axis=-1, keepdims=True)
    e2 = jnp.exp(v2 - v1)
    den = 1.0 + e2
    w1 = g_val * (1.0 / den)
    w2 = g_val * (e2 / den)
    route = jnp.where(lane == 0, i1 - N_GROUPS, 0.0)
    route = jnp.where(lane == 1, i2 - N_GROUPS, route)
    route = jnp.where(lane == 2, w1, route)
    route = jnp.where(lane == 3, w2, route)
    route_ref[...] = route


def _outproj(mg, x2, wo, g2, wr, br):
    tm = ROW_TILE
    row = lambda w: pl.BlockSpec((tm, w), lambda i: (i, 0))
    return pl.pallas_call(
        _outproj_kernel,
        grid=(SEQ // tm,),
        in_specs=[row(D_MODEL), row(D_MODEL), _resident(wo.shape), _resident(g2.shape),
                  _resident(wr.shape), _resident(br.shape)],
        out_specs=[row(D_MODEL), row(D_MODEL), row(LANES)],
        out_shape=[jax.ShapeDtypeStruct((SEQ, D_MODEL), F32),
                   jax.ShapeDtypeStruct((SEQ, D_MODEL), F32),
                   jax.ShapeDtypeStruct((SEQ, LANES), F32)],
        compiler_params=_params(),
        name="outproj",
    )(mg, x2, wo, g2, wr, br)


def _start_row_gather(src_hbm, ids_ref, dst, slot, sem, n_rows):
    def body(r, carry):
        pltpu.make_async_copy(src_hbm.at[pl.ds(ids_ref[0, 0, r], 1), :],
                              dst.at[slot, pl.ds(r, 1), :], sem.at[slot]).start()
        return carry
    lax.fori_loop(0, n_rows, body, 0, unroll=DMA_UNROLL)


def _wait_row_gather(src_hbm, dst, slot, sem, n_rows):
    pltpu.make_async_copy(src_hbm.at[pl.ds(0, n_rows), :], dst.at[slot], sem.at[slot]).wait()


def _expert_kernel(te_ref, nt_ref, tok_ref, tokn_ref, x_hbm, wg_ref, wu_ref, wd_ref,
                   y_ref, xbuf, sem, wgb, wub, wdb):
    i = pl.program_id(0)
    nt = nt_ref[0]

    @pl.when(i == 0)
    def _():
        _start_row_gather(x_hbm, tok_ref, xbuf, 0, sem, EXPERT_TILE)

    @pl.when(i + 1 < nt)
    def _():
        _start_row_gather(x_hbm, tokn_ref, xbuf, (i + 1) % 2, sem, EXPERT_TILE)

    @pl.when(i < nt)
    def _():
        slot = i % 2
        _wait_row_gather(x_hbm, xbuf, slot, sem, EXPERT_TILE)
        new_expert = jnp.logical_or(i == 0, te_ref[i] != te_ref[jnp.maximum(i - 1, 0)])

        @pl.when(new_expert)
        def _():
            wgb[...] = wg_ref[0].astype(BF16)
            wub[...] = wu_ref[0].astype(BF16)
            wdb[...] = wd_ref[0].astype(BF16)

        xt = xbuf[slot].astype(BF16)
        hid = jax.nn.silu(_dot(xt, wgb[...])) * _dot(xt, wub[...])
        y_ref[...] = _dot(hid.astype(BF16), wdb[...])


def _experts(tile_expert, n_tiles, sorted_tok, n2, w_gate, w_up, w_down):
    last = lambda i, nt: jnp.minimum(i, nt[0] - 1)
    wspec = lambda s: pl.BlockSpec((1,) + s, lambda i, te, nt: (te[last(i, nt)], 0, 0))
    smem_blk = lambda f: pl.BlockSpec((1, 1, EXPERT_TILE), f, memory_space=pltpu.SMEM)
    grid_spec = pltpu.PrefetchScalarGridSpec(
        num_scalar_prefetch=2,
        grid=(N_EXPERT_TILES,),
        in_specs=[
            smem_blk(lambda i, te, nt: (last(i, nt), 0, 0)),
            smem_blk(lambda i, te, nt: (last(i + 1, nt), 0, 0)),
            pl.BlockSpec(memory_space=pl.ANY),
            wspec((D_MODEL, EXPERT_FF)), wspec((D_MODEL, EXPERT_FF)), wspec((EXPERT_FF, D_MODEL)),
        ],
        out_specs=pl.BlockSpec((EXPERT_TILE, D_MODEL), lambda i, te, nt: (last(i, nt), 0)),
        scratch_shapes=[
            pltpu.VMEM((2, EXPERT_TILE, D_MODEL), F32),
            pltpu.SemaphoreType.DMA((2,)),
            pltpu.VMEM((D_MODEL, EXPERT_FF), BF16),
            pltpu.VMEM((D_MODEL, EXPERT_FF), BF16),
            pltpu.VMEM((EXPERT_FF, D_MODEL), BF16),
        ],
    )
    return pl.pallas_call(
        _expert_kernel,
        grid_spec=grid_spec,
        out_shape=jax.ShapeDtypeStruct((N_EXPERT_TILES * EXPERT_TILE, D_MODEL), F32),
        compiler_params=_params(),
        name="experts",
    )(tile_expert, n_tiles, sorted_tok, sorted_tok, n2, w_gate, w_up, w_down)


def _final_kernel(pos_ref, posn_ref, h_ref, route_ref, gf_ref, y_hbm, o_ref, ybuf, sem):
    i = pl.program_id(0)
    n = pl.num_programs(0)
    tm = ROW_TILE

    @pl.when(i == 0)
    def _():
        _start_row_gather(y_hbm, pos_ref, ybuf, 0, sem, 2 * tm)

    @pl.when(i + 1 < n)
    def _():
        _start_row_gather(y_hbm, posn_ref, ybuf, (i + 1) % 2, sem, 2 * tm)

    slot = i % 2
    _wait_row_gather(y_hbm, ybuf, slot, sem, 2 * tm)
    route = route_ref[...]
    w1 = route[:, 2:3]
    w2 = route[:, 3:4]
    h2 = h_ref[...] + (w1 * ybuf[slot, 0:tm, :] + w2 * ybuf[slot, tm:2 * tm, :])
    o_ref[...] = h2 * lax.rsqrt(jnp.mean(h2 * h2, axis=-1, keepdims=True) + EPS) * gf_ref[...]


def _final(pos, h, route, gf, y_sorted):
    tm = ROW_TILE
    n = SEQ // tm
    row = lambda w: pl.BlockSpec((tm, w), lambda i: (i, 0))
    smem_blk = lambda f: pl.BlockSpec((1, 1, 2 * tm), f, memory_space=pltpu.SMEM)
    return pl.pallas_call(
        _final_kernel,
        grid=(n,),
        in_specs=[smem_blk(lambda i: (i, 0, 0)),
                  smem_blk(lambda i: (jnp.minimum(i + 1, n - 1), 0, 0)),
                  row(D_MODEL), row(LANES), _resident(gf.shape),
                  pl.BlockSpec(memory_space=pl.ANY)],
        out_specs=row(D_MODEL),
        out_shape=jax.ShapeDtypeStruct((SEQ, D_MODEL), F32),
        scratch_shapes=[pltpu.VMEM((2, 2 * tm, D_MODEL), F32), pltpu.SemaphoreType.DMA((2,))],
        compiler_params=_params(),
        name="final",
    )(pos, pos, h, route, gf, y_sorted)


def _dispatch(route):
    e_flat = route[:, 0:2].astype(jnp.int32).reshape(-1)
    onehot = (e_flat[:, None] == jnp.arange(N_EXPERTS, dtype=jnp.int32)[None, :]).astype(jnp.int32)
    csum = jnp.cumsum(onehot, axis=0)
    rank = jnp.take_along_axis(csum, e_flat[:, None], axis=1)[:, 0] - 1
    counts = csum[-1]
    tiles_per = (counts + EXPERT_TILE - 1) // EXPERT_TILE
    tile_end = jnp.cumsum(tiles_per)
    tile_start = tile_end - tiles_per
    pos = tile_start[e_flat] * EXPERT_TILE + rank
    tok = jnp.arange(2 * SEQ, dtype=jnp.int32) // 2
    sorted_tok = jnp.zeros((N_EXPERT_TILES * EXPERT_TILE,), jnp.int32).at[pos].set(tok)
    tile_ids = jnp.arange(N_EXPERT_TILES, dtype=jnp.int32)
    tile_expert = jnp.sum((tile_end[None, :] <= tile_ids[:, None]).astype(jnp.int32), axis=1)
    tile_expert = jnp.minimum(tile_expert, N_EXPERTS - 1)
    n_tiles = tile_end[-1:].astype(jnp.int32)
    return pos, sorted_tok, tile_expert, n_tiles


def kernel(x, norm1_g, w_in, kv_norm_g, w_uk, w_uv, gmlp_ws, gmlp_bs, ln_v_g, ln_v_b, w_br_attn,
           w_br_gmlp, w_out, norm2_g, w_group, b_group, w_router, b_router, w_e_gate, w_e_up,
           w_e_down, norm_f_g):
    assert x.shape == (1, SEQ, D_MODEL)
    x2 = x.reshape(SEQ, D_MODEL)
    row_vec = lambda v: v.reshape(1, -1).astype(F32)

    c_q = ATTN_WIDTH
    c_kv = c_q + KV_LATENT
    c_qi = c_kv + IDX_HEADS * IDX_DIM
    c_k = c_qi + IDX_DIM
    c_w = c_k + IDX_HEADS
    c_uv = c_w + 2 * GMLP_WIDTH
    w_bf = w_in.astype(BF16)
    wqT, wc, wqiT = w_bf[:, :c_q].T, w_bf[:, c_q:c_kv], w_bf[:, c_kv:c_qi].T
    wk, wwT = w_bf[:, c_qi:c_k], w_bf[:, c_k:c_w].T
    wuv_in, wga, wgb = w_bf[:, c_w:c_uv], w_bf[:, c_uv:c_uv + D_MODEL], w_bf[:, c_uv + D_MODEL:]
    zk = jnp.zeros_like(wk)
    wkk = jnp.concatenate([wk, zk, zk, wk], axis=1)

    n1, qT, ckv, ckvT, qiT, kk2, wiT = _proj(x2, row_vec(norm1_g), wqT, wc, wqiT, wkk, wwT,
                                             row_vec(kv_norm_g))
    m = _gmlp(n1, wuv_in, gmlp_ws, jnp.pad(gmlp_bs.T, ((0, 0), (0, LANES - GMLP_GROUPS))),
              row_vec(ln_v_g), row_vec(ln_v_b))
    a = _attn(qT, qiT, wiT, kk2, ckv, ckvT,
              jnp.swapaxes(w_uk, 1, 2).astype(BF16), jnp.swapaxes(w_uv, 1, 2).astype(BF16))
    mg = _merge(n1, a, m, wga, wgb, w_br_attn.astype(BF16), w_br_gmlp.astype(BF16))

    w_route = jnp.pad(jnp.concatenate([w_group, w_router], axis=1),
                      ((0, 0), (0, LANES - N_GROUPS - N_EXPERTS)))
    b_route = jnp.pad(jnp.concatenate([b_group, b_router]), (0, LANES - N_GROUPS - N_EXPERTS))
    h, n2, route = _outproj(mg, x2, w_out.astype(BF16), row_vec(norm2_g), w_route,
                            row_vec(b_route))

    pos, sorted_tok, tile_expert, n_tiles = _dispatch(route)
    y_sorted = _experts(tile_expert, n_tiles,
                        sorted_tok.reshape(N_EXPERT_TILES, 1, EXPERT_TILE), n2,
                        w_e_gate, w_e_up, w_e_down)
    pos_tiles = pos.reshape(SEQ // ROW_TILE, ROW_TILE, 2).transpose(0, 2, 1).reshape(
        SEQ // ROW_TILE, 1, 2 * ROW_TILE)
    out = _final(pos_tiles, h, route, row_vec(norm_f_g), y_sorted)
    return out.reshape(1, SEQ, D_MODEL)
```

```python
import jax
import jax.numpy as jnp
import numpy as np
from jax import lax
from jax.experimental import pallas as pl
from jax.experimental.pallas import tpu as pltpu

F32 = jnp.float32
BF16 = jnp.bfloat16

D_MODEL = 2048
SEQ = 8192
N_HEADS = 8
HEAD_DIM = 128
KV_LATENT = 256
IDX_HEADS = 16
IDX_DIM = 64
TOPK = 256
ATTN_WIDTH = N_HEADS * HEAD_DIM
GMLP_GROUPS = 8
GMLP_WIDTH = 1024
CHUNK = 128
N_GROUPS = 8
EXPERTS_PER_GROUP = 8
N_EXPERTS = 64
EXPERT_FF = 512
EPS = 1e-6

LANES = 128
SUBLANES = 8
VMEM_LIMIT = 60 * 1024 * 1024
MASK_VALUE = -0.7 * float(np.finfo(np.float32).max)
LOG2_E = float(np.log2(np.e))

ROW_TILE = 256
TQ = 256
KB = 256
NKB = SEQ // KB
COUNT_ROWS = 32
MAX_SEARCH_ITERS = 64
EXPERT_TILE = 256
N_EXPERT_TILES = (2 * SEQ) // EXPERT_TILE + N_EXPERTS
DMA_UNROLL = 8


def _dot(a, b):
    return jnp.dot(a, b, preferred_element_type=F32)


def _dot_nt(a, b):
    return lax.dot_general(a, b, (((1,), (1,)), ((), ())), preferred_element_type=F32)


def _resident(shape):
    zeros = (0,) * len(shape)
    return pl.BlockSpec(shape, lambda *_: zeros, pipeline_mode=pl.Buffered(1))


def _params(n_axes=1):
    return pltpu.CompilerParams(
        dimension_semantics=("arbitrary",) * n_axes, vmem_limit_bytes=VMEM_LIMIT)


def _proj_kernel(x_ref, g1_ref, wqT_ref, wc_ref, wqiT_ref, wkk_ref, wwT_ref, kvg_ref,
                 n1_ref, qT_ref, ckv_ref, ckvT_ref, qiT_ref, kk2_ref, wiT_ref):
    x = x_ref[...]
    ms = jnp.mean(x * x, axis=-1, keepdims=True)
    n1 = (x * lax.rsqrt(ms + EPS) * g1_ref[...]).astype(BF16)
    n1_ref[...] = n1
    qT_ref[...] = _dot_nt(wqT_ref[...], n1).astype(BF16)
    qiT_ref[...] = (_dot_nt(wqiT_ref[...], n1) * (IDX_DIM ** -0.5)).astype(BF16)
    wiT_ref[...] = _dot_nt(wwT_ref[...], n1) * (IDX_HEADS ** -0.5)
    c = _dot(n1, wc_ref[...])
    c = c * lax.rsqrt(jnp.mean(c * c, axis=-1, keepdims=True) + EPS) * kvg_ref[...]
    ckv_ref[...] = c.astype(BF16)
    ckvT_ref[0] = c.T.astype(BF16)
    kk = _dot(n1, wkk_ref[...]).astype(BF16)
    kk2_ref[0, 0:KB, :] = kk[:, 0:LANES]
    kk2_ref[0, KB:2 * KB, :] = kk[:, LANES:2 * LANES]


def _proj(x2, g1, wqT, wc, wqiT, wkk, wwT, kvg):
    tm = KB
    row = lambda w: pl.BlockSpec((tm, w), lambda i: (i, 0))
    col = lambda h: pl.BlockSpec((h, tm), lambda i: (0, i))
    return pl.pallas_call(
        _proj_kernel,
        grid=(SEQ // tm,),
        in_specs=[row(D_MODEL), _resident(g1.shape), _resident(wqT.shape), _resident(wc.shape),
                  _resident(wqiT.shape), _resident(wkk.shape), _resident(wwT.shape),
                  _resident(kvg.shape)],
        out_specs=[row(D_MODEL), col(ATTN_WIDTH), row(KV_LATENT),
                   pl.BlockSpec((1, KV_LATENT, tm), lambda i: (i, 0, 0)),
                   col(IDX_HEADS * IDX_DIM),
                   pl.BlockSpec((1, 2 * tm, LANES), lambda i: (i, 0, 0)),
                   col(IDX_HEADS)],
        out_shape=[
            jax.ShapeDtypeStruct((SEQ, D_MODEL), BF16),
            jax.ShapeDtypeStruct((ATTN_WIDTH, SEQ), BF16),
            jax.ShapeDtypeStruct((SEQ, KV_LATENT), BF16),
            jax.ShapeDtypeStruct((NKB, KV_LATENT, KB), BF16),
            jax.ShapeDtypeStruct((IDX_HEADS * IDX_DIM, SEQ), BF16),
            jax.ShapeDtypeStruct((NKB, 2 * KB, LANES), BF16),
            jax.ShapeDtypeStruct((IDX_HEADS, SEQ), F32),
        ],
        compiler_params=_params(),
        name="proj",
    )(x2, g1, wqT, wc, wqiT, wkk, wwT, kvg)


def _gmlp_kernel(n1_ref, wuv_ref, ws_ref, bsT_ref, lng_ref, lnb_ref, m_ref):
    uv = _dot(n1_ref[...], wuv_ref[...])
    z = jax.nn.gelu(uv)
    u = z[:, :GMLP_WIDTH]
    v = z[:, GMLP_WIDTH:]
    mu = jnp.mean(v, axis=-1, keepdims=True)
    var = jnp.mean(jnp.square(v - mu), axis=-1, keepdims=True)
    vn = ((v - mu) * lax.rsqrt(var + EPS) * lng_ref[...] + lnb_ref[...]).astype(BF16)
    t_pos = lax.broadcasted_iota(jnp.int32, (CHUNK, CHUNK), 0)
    s_pos = lax.broadcasted_iota(jnp.int32, (CHUNK, CHUNK), 1)
    causal = s_pos <= t_pos
    for g in range(GMLP_GROUPS):
        wm = jnp.where(causal, ws_ref[g], 0.0).astype(BF16)
        bias = bsT_ref[:, g:g + 1]
        cols = slice(g * LANES, (g + 1) * LANES)
        for c in range(ROW_TILE // CHUNK):
            rows = slice(c * CHUNK, (c + 1) * CHUNK)
            y = _dot(wm, vn[rows, cols]) + bias
            m_ref[rows, cols] = (u[rows, cols] * y).astype(BF16)


def _gmlp(n1, wuv, ws, bsT, lng, lnb):
    tm = ROW_TILE
    return pl.pallas_call(
        _gmlp_kernel,
        grid=(SEQ // tm,),
        in_specs=[pl.BlockSpec((tm, D_MODEL), lambda i: (i, 0)), _resident(wuv.shape),
                  _resident(ws.shape), _resident(bsT.shape), _resident(lng.shape),
                  _resident(lnb.shape)],
        out_specs=pl.BlockSpec((tm, GMLP_WIDTH), lambda i: (i, 0)),
        out_shape=jax.ShapeDtypeStruct((SEQ, GMLP_WIDTH), BF16),
        compiler_params=_params(),
        name="gmlp",
    )(n1, wuv, ws, bsT, lng, lnb)


def _attn_kernel(qT_ref, qiT_ref, wiT_ref, kk2_ref, ckv_ref, ckvT_ref, wukT_ref, wuvT_ref,
                 a_ref,
                 sc_ref, qabs_ref, thr_ref, keep_ref, m_ref, l_ref, acc_ref, lg_ref):
    i = pl.program_id(0)
    nkb = i + 1

    for h in range(N_HEADS):
        qa = _dot(wukT_ref[h], qT_ref[h * HEAD_DIM:(h + 1) * HEAD_DIM, :]) * (
            HEAD_DIM ** -0.5 * LOG2_E)
        qabs_ref[h] = qa.astype(BF16)

    q_pos = i * TQ + lax.broadcasted_iota(jnp.int32, (KB, TQ), 1)
    k_off = lax.broadcasted_iota(jnp.int32, (KB, TQ), 0)

    def score_body(kb, carry):
        smax, smin = carry
        keys = kk2_ref[kb]
        acc = jnp.zeros((KB, TQ), F32)
        for j in range(IDX_HEADS // 2):
            d = _dot(keys, qiT_ref[j * LANES:(j + 1) * LANES, :])
            acc = acc + jnp.maximum(d[0:KB], 0.0) * wiT_ref[2 * j:2 * j + 1, :]
            acc = acc + jnp.maximum(d[KB:2 * KB], 0.0) * wiT_ref[2 * j + 1:2 * j + 2, :]
        causal = (kb * KB + k_off) <= q_pos
        sc_ref[kb] = jnp.where(causal, acc, -jnp.inf)
        smax = jnp.maximum(smax, jnp.max(jnp.where(causal, acc, -jnp.inf), axis=0, keepdims=True))
        smin = jnp.minimum(smin, jnp.min(jnp.where(causal, acc, jnp.inf), axis=0, keepdims=True))
        return smax, smin

    smax, smin = lax.fori_loop(
        0, nkb, score_body,
        (jnp.full((1, TQ), -jnp.inf, F32), jnp.full((1, TQ), jnp.inf, F32)))

    def count_ge(x):
        def body(kb, cnt):
            ge = jnp.where(sc_ref[kb] >= x, 1.0, 0.0)
            return cnt + jnp.sum(ge.reshape(KB // COUNT_ROWS, COUNT_ROWS, TQ), axis=0)
        cnt = lax.fori_loop(0, nkb, body, jnp.zeros((COUNT_ROWS, TQ), F32))
        return jnp.sum(cnt, axis=0, keepdims=True)

    n_causal = (i * TQ + 1 + lax.broadcasted_iota(jnp.int32, (1, TQ), 1)).astype(F32)
    want = jnp.minimum(n_causal, float(TOPK))
    hi0 = smax + jnp.maximum(jnp.abs(smax), 1e-30) * 1e-6
    open0 = (n_causal != want).astype(F32)

    def search_cond(state):
        it, _, _, _, _, flag = state
        return jnp.logical_and(it < MAX_SEARCH_ITERS, flag > 0.0)

    def search_body(state):
        it, lo, hi, c_lo, open_, _ = state
        mid = 0.5 * lo + 0.5 * hi
        c = count_ge(mid)
        live = jnp.logical_and(open_ > 0.0, jnp.logical_and(mid > lo, mid < hi))
        ge = c >= want
        go_lo = jnp.logical_and(live, ge)
        lo = jnp.where(go_lo, mid, lo)
        c_lo = jnp.where(go_lo, c, c_lo)
        hi = jnp.where(jnp.logical_and(live, jnp.logical_not(ge)), mid, hi)
        open_ = jnp.logical_and(live, c != want).astype(F32)
        return it + 1, lo, hi, c_lo, open_, jnp.max(open_)

    _, lo_f, _, c_lo_f, _, _ = lax.while_loop(
        search_cond, search_body,
        (jnp.int32(0), smin, hi0, n_causal, open0, jnp.max(open0)))
    thr_ref[...] = jnp.broadcast_to(lo_f, (SUBLANES, TQ))
    keep_ref[...] = jnp.full((SUBLANES, TQ), float(SEQ), F32)
    unresolved0 = (c_lo_f != want).astype(F32)

    @pl.when(jnp.max(unresolved0) > 0.0)
    def _():
        def next_value(lo, below):
            def body(kb, u):
                s = sc_ref[kb]
                cand = jnp.logical_and(s >= lo, s > below)
                return jnp.minimum(u, jnp.min(jnp.where(cand, s, jnp.inf), axis=0, keepdims=True))
            return lax.fori_loop(0, nkb, body, jnp.full((1, TQ), jnp.inf, F32))

        def count_gt(x):
            def body(kb, cnt):
                gt = jnp.where(sc_ref[kb] > x, 1.0, 0.0)
                return cnt + jnp.sum(gt, axis=0, keepdims=True)
            return lax.fori_loop(0, nkb, body, jnp.zeros((1, TQ), F32))

        def peel_cond(state):
            return state[-1] > 0.0

        def peel_body(state):
            below, unres, thr, keep, _ = state
            u = next_value(lo_f, below)
            c_gt = count_gt(u)
            hit = jnp.logical_and(unres > 0.0, c_gt < want)
            thr = jnp.where(hit, u, thr)
            keep = jnp.where(hit, want - c_gt, keep)
            unres = jnp.logical_and(unres > 0.0, jnp.logical_not(hit)).astype(F32)
            return u, unres, thr, keep, jnp.max(unres)

        _, _, thr_t, keep_t, _ = lax.while_loop(
            peel_cond, peel_body,
            (jnp.full((1, TQ), -jnp.inf, F32), unresolved0, lo_f,
             jnp.full((1, TQ), float(SEQ), F32), jnp.float32(1.0)))
        thr_ref[...] = jnp.broadcast_to(thr_t, (SUBLANES, TQ))
        keep_ref[...] = jnp.broadcast_to(keep_t, (SUBLANES, TQ))

        r_i = lax.broadcasted_iota(jnp.int32, (KB, KB), 0)
        c_i = lax.broadcasted_iota(jnp.int32, (KB, KB), 1)
        before = (c_i < r_i).astype(BF16)

        def drop_body(kb, seen):
            s = sc_ref[kb]
            eq = jnp.logical_and(s == thr_t, unresolved0 > 0.0)
            eq_f = eq.astype(F32)
            rank = seen + _dot(before, eq_f.astype(BF16))
            sc_ref[kb] = jnp.where(jnp.logical_and(eq, rank >= keep_t), -jnp.inf, s)
            return seen + jnp.sum(eq_f, axis=0, keepdims=True)

        lax.fori_loop(0, nkb, drop_body, jnp.zeros((1, TQ), F32))

    m_ref[...] = jnp.full(m_ref.shape, MASK_VALUE, F32)
    l_ref[...] = jnp.zeros(l_ref.shape, F32)
    acc_ref[...] = jnp.zeros(acc_ref.shape, F32)

    @pl.when(nkb % 2 == 1)
    def _():
        sc_ref[nkb] = jnp.full((KB, TQ), -jnp.inf, F32)

    def logits_stage(j, slot):
        thr = thr_ref[0:1, :]
        bias0 = jnp.where(sc_ref[2 * j] >= thr, 0.0, MASK_VALUE)
        bias1 = jnp.where(sc_ref[2 * j + 1] >= thr, 0.0, MASK_VALUE)
        c_n = ckv_ref[pl.ds(pl.multiple_of(j * (2 * KB), 2 * KB), 2 * KB), :]
        for h in range(N_HEADS):
            lg = _dot(c_n, qabs_ref[h])
            lg_ref[slot, h, 0:KB, :] = lg[0:KB] + bias0
            lg_ref[slot, h, KB:2 * KB, :] = lg[KB:2 * KB] + bias1

    def softmax_stage(j, slot):
        c_t0 = ckvT_ref[2 * j]
        c_t1 = ckvT_ref[2 * j + 1]
        for h in range(N_HEADS):
            lg0 = lg_ref[slot, h, 0:KB, :]
            lg1 = lg_ref[slot, h, KB:2 * KB, :]
            m_old = m_ref[h, 0:1, :]
            m_new = jnp.maximum(m_old, jnp.max(jnp.maximum(lg0, lg1), axis=0, keepdims=True))
            alpha = jnp.exp2(m_old - m_new)
            p0 = jnp.exp2(lg0 - m_new)
            p1 = jnp.exp2(lg1 - m_new)
            l_new = alpha * l_ref[h, 0:1, :] + jnp.sum(p0 + p1, axis=0, keepdims=True)
            acc_ref[h] = (acc_ref[h] * alpha + _dot(c_t0, p0.astype(BF16))
                          + _dot(c_t1, p1.astype(BF16)))
            m_ref[h] = jnp.broadcast_to(m_new, (SUBLANES, TQ))
            l_ref[h] = jnp.broadcast_to(l_new, (SUBLANES, TQ))

    n_pairs = (nkb + 1) // 2
    logits_stage(0, 0)

    def att_body(j, carry):
        for slot in range(2):
            @pl.when(j % 2 == slot)
            def _():
                logits_stage(j + 1, 1 - slot)
                softmax_stage(j, slot)
        return carry

    lax.fori_loop(0, n_pairs - 1, att_body, 0)
    for slot in range(2):
        @pl.when((n_pairs - 1) % 2 == slot)
        def _():
            softmax_stage(n_pairs - 1, slot)

    for h in range(N_HEADS):
        o_t = (acc_ref[h] / l_ref[h, 0:1, :]).astype(BF16)
        a_t = _dot(wuvT_ref[h], o_t)
        a_ref[:, h * HEAD_DIM:(h + 1) * HEAD_DIM] = a_t.T.astype(BF16)


def _attn(qT, qiT, wiT, kk2, ckv, ckvT, wukT, wuvT):
    col = lambda h: pl.BlockSpec((h, TQ), lambda i: (0, i))
    return pl.pallas_call(
        _attn_kernel,
        grid=(SEQ // TQ,),
        in_specs=[col(ATTN_WIDTH), col(IDX_HEADS * IDX_DIM), col(IDX_HEADS), _resident(kk2.shape),
                  _resident(ckv.shape), _resident(ckvT.shape), _resident(wukT.shape),
                  _resident(wuvT.shape)],
        out_specs=pl.BlockSpec((TQ, ATTN_WIDTH), lambda i: (i, 0)),
        out_shape=jax.ShapeDtypeStruct((SEQ, ATTN_WIDTH), BF16),
        scratch_shapes=[
            pltpu.VMEM((NKB, KB, TQ), F32),
            pltpu.VMEM((N_HEADS, KV_LATENT, TQ), BF16),
            pltpu.VMEM((SUBLANES, TQ), F32),
            pltpu.VMEM((SUBLANES, TQ), F32),
            pltpu.VMEM((N_HEADS, SUBLANES, TQ), F32),
            pltpu.VMEM((N_HEADS, SUBLANES, TQ), F32),
            pltpu.VMEM((N_HEADS, KV_LATENT, TQ), F32),
            pltpu.VMEM((2, N_HEADS, 2 * KB, TQ), F32),
        ],
        compiler_params=_params(),
        name="attn",
    )(qT, qiT, wiT, kk2, ckv, ckvT, wukT, wuvT)


def _merge_kernel(n1_ref, a_ref, m_ref, wga_ref, wgb_ref, wba_ref, wbg_ref, o_ref):
    n1 = n1_ref[...]
    br_a = jax.nn.sigmoid(_dot(n1, wga_ref[...])) * _dot(a_ref[...], wba_ref[...])
    br_b = jax.nn.sigmoid(_dot(n1, wgb_ref[...])) * _dot(m_ref[...], wbg_ref[...])
    o_ref[...] = (br_a + br_b).astype(BF16)


def _merge(n1, a, m, wga, wgb, wba, wbg):
    tm = ROW_TILE
    row = lambda w: pl.BlockSpec((tm, w), lambda i: (i, 0))
    return pl.pallas_call(
        _merge_kernel,
        grid=(SEQ // tm,),
        in_specs=[row(D_MODEL), row(ATTN_WIDTH), row(GMLP_WIDTH), _resident(wga.shape),
                  _resident(wgb.shape), _resident(wba.shape), _resident(wbg.shape)],
        out_specs=row(D_MODEL),
        out_shape=jax.ShapeDtypeStruct((SEQ, D_MODEL), BF16),
        compiler_params=_params(),
        name="merge",
    )(n1, a, m, wga, wgb, wba, wbg)


ROUTE_ROWS = 8


def _outproj_kernel(mg_ref, x_ref, wo_ref, g2_ref, wrh_ref, wrl_ref, br_ref,
                    h_ref, n2_ref, route_ref, counts_ref, carry_ref):
    i = pl.program_id(0)
    tm = ROW_TILE

    @pl.when(i == 0)
    def _():
        carry_ref[...] = jnp.zeros(carry_ref.shape, F32)

    h = x_ref[...] + _dot(mg_ref[...], wo_ref[...])
    h_ref[...] = h
    n2 = h * lax.rsqrt(jnp.mean(h * h, axis=-1, keepdims=True) + EPS) * g2_ref[...]
    n2_ref[...] = n2
    n2_hi = n2.astype(BF16)
    n2_lo = (n2 - n2_hi.astype(F32)).astype(BF16)
    logits = (_dot_nt(wrh_ref[...], n2_hi) + _dot_nt(wrh_ref[...], n2_lo)
              + _dot_nt(wrl_ref[...], n2_hi)) + br_ref[...]
    row = lax.broadcasted_iota(jnp.int32, (LANES, tm), 0).astype(F32)
    is_group = row < N_GROUPS
    gl = jnp.where(is_group, logits, -jnp.inf)
    gmax = jnp.max(gl, axis=0, keepdims=True)
    gsum = jnp.sum(jnp.where(is_group, jnp.exp(logits - gmax), 0.0), axis=0, keepdims=True)
    g_val = 1.0 / gsum
    g_idx = jnp.min(jnp.where(gl == gmax, row, float(LANES)), axis=0, keepdims=True)
    e_id = row - N_GROUPS
    in_group = jnp.logical_and(
        jnp.logical_and(e_id >= 0, e_id < N_EXPERTS),
        jnp.floor(e_id * (1.0 / EXPERTS_PER_GROUP)) == g_idx)
    sel = jnp.where(in_group, logits, -jnp.inf)
    v1 = jnp.max(sel, axis=0, keepdims=True)
    i1 = jnp.min(jnp.where(sel == v1, row, float(LANES)), axis=0, keepdims=True)
    sel2 = jnp.where(row == i1, -jnp.inf, sel)
    v2 = jnp.max(sel2, axis=0, keepdims=True)
    i2 = jnp.min(jnp.where(sel2 == v2, row, float(LANES)), axis=0, keepdims=True)
    x2 = jnp.exp(v2 - v1)
    den = 1.0 + x2
    w1 = g_val * (1.0 / den)
    w2 = g_val * (x2 / den)
    e1 = i1 - N_GROUPS
    e2 = i2 - N_GROUPS

    e_row = lax.broadcasted_iota(jnp.int32, (N_EXPERTS, tm), 0).astype(F32)
    hit1 = (e_row == e1).astype(F32)
    hit2 = (e_row == e2).astype(F32)
    hits = hit1 + hit2
    t_from = lax.broadcasted_iota(jnp.int32, (tm, tm), 0)
    t_to = lax.broadcasted_iota(jnp.int32, (tm, tm), 1)
    earlier = (t_from < t_to).astype(BF16)
    before = carry_ref[:, 0:1] + _dot(hits.astype(BF16), earlier)
    rank1 = jnp.sum(hit1 * before, axis=0, keepdims=True)
    rank2 = jnp.sum(hit2 * before, axis=0, keepdims=True)
    carry = carry_ref[...] + jnp.sum(hits, axis=1, keepdims=True)
    carry_ref[...] = carry
    counts_ref[...] = carry

    r = lax.broadcasted_iota(jnp.int32, (ROUTE_ROWS, tm), 0)
    route = jnp.where(r == 0, e1, 0.0)
    for k, v in enumerate((e2, w1, w2, rank1, rank2), start=1):
        route = jnp.where(r == k, v, route)
    route_ref[...] = route


def _outproj(mg, x2, wo, g2, wr_hi, wr_lo, br):
    tm = ROW_TILE
    row = lambda w: pl.BlockSpec((tm, w), lambda i: (i, 0))
    return pl.pallas_call(
        _outproj_kernel,
        grid=(SEQ // tm,),
        in_specs=[row(D_MODEL), row(D_MODEL), _resident(wo.shape), _resident(g2.shape),
                  _resident(wr_hi.shape), _resident(wr_lo.shape), _resident(br.shape)],
        out_specs=[row(D_MODEL), row(D_MODEL),
                   pl.BlockSpec((ROUTE_ROWS, tm), lambda i: (0, i)),
                   pl.BlockSpec((N_EXPERTS, LANES), lambda i: (0, 0))],
        out_shape=[jax.ShapeDtypeStruct((SEQ, D_MODEL), F32),
                   jax.ShapeDtypeStruct((SEQ, D_MODEL), F32),
                   jax.ShapeDtypeStruct((ROUTE_ROWS, SEQ), F32),
                   jax.ShapeDtypeStruct((N_EXPERTS, LANES), F32)],
        scratch_shapes=[pltpu.VMEM((N_EXPERTS, LANES), F32)],
        compiler_params=_params(),
        name="outproj",
    )(mg, x2, wo, g2, wr_hi, wr_lo, br)


def _start_row_gather(src_hbm, row_of, dst, slot, sem, rows):
    for r in rows:
        pltpu.make_async_copy(src_hbm.at[pl.ds(row_of(r), 1), :],
                              dst.at[slot, pl.ds(r, 1), :], sem.at[slot]).start()


def _wait_row_gather(src_hbm, dst, slot, sem, n_rows):
    pltpu.make_async_copy(src_hbm.at[pl.ds(0, n_rows), :], dst.at[slot], sem.at[slot]).wait()


def _expert_kernel(ntl_ref, tst_ref, tbase_ref, trows_ref, ntot_ref,
                   posu_ref, x_hbm, wg_ref, wu_ref, wd_ref,
                   y_hbm,
                   order, xbuf, ybuf, gsem, ysem, wgb, wub, wdb):
    e = pl.program_id(0)
    ntot = ntot_ref[0]

    def start_gather(t, slot, rows):
        base = tbase_ref[t]
        last = trows_ref[t] - 1
        token = lambda r: lax.shift_right_logical(order[base + jnp.minimum(r, last)], 1)
        _start_row_gather(x_hbm, token, xbuf, slot, gsem, rows)

    def y_copy(t, slot):
        return pltpu.make_async_copy(
            ybuf.at[slot], y_hbm.at[pl.ds(pl.multiple_of(t * EXPERT_TILE, EXPERT_TILE), EXPERT_TILE), :],
            ysem.at[slot])

    @pl.when(e == 0)
    def _():
        def invert(p, carry):
            order[posu_ref[p]] = p
            return carry
        lax.fori_loop(0, 2 * SEQ, invert, 0, unroll=DMA_UNROLL)
        start_gather(0, 0, range(EXPERT_TILE))

    n_here = ntl_ref[e]

    @pl.when(n_here > 0)
    def _():
        wgb[...] = wg_ref[0].astype(BF16)
        wub[...] = wu_ref[0].astype(BF16)
        wdb[...] = wd_ref[0].astype(BF16)

        def tile_body(k, carry):
            t = tst_ref[e] + k
            slot = t % 2
            _wait_row_gather(x_hbm, xbuf, slot, gsem, EXPERT_TILE)
            nxt = jnp.minimum(t + 1, ntot - 1)
            third = EXPERT_TILE // 3
            xt = xbuf[slot].astype(BF16)
            start_gather(nxt, 1 - slot, range(0, third))
            gate = _dot(xt, wgb[...])
            start_gather(nxt, 1 - slot, range(third, 2 * third))
            up = _dot(xt, wub[...])
            start_gather(nxt, 1 - slot, range(2 * third, EXPERT_TILE))
            y = _dot((jax.nn.silu(gate) * up).astype(BF16), wdb[...])

            @pl.when(t >= 2)
            def _():
                y_copy(t - 2, slot).wait()

            ybuf[slot] = y
            y_copy(t, slot).start()
            return carry
        lax.fori_loop(0, n_here, tile_body, 0)

    @pl.when(e == pl.num_programs(0) - 1)
    def _():
        _wait_row_gather(x_hbm, xbuf, ntot % 2, gsem, EXPERT_TILE)

        @pl.when(ntot >= 2)
        def _():
            y_copy(ntot - 2, (ntot - 2) % 2).wait()
        y_copy(ntot - 1, (ntot - 1) % 2).wait()


def _experts(meta, posu, n2, w_gate, w_up, w_down):
    wspec = lambda s: pl.BlockSpec((1,) + s, lambda e, *_: (e, 0, 0))
    grid_spec = pltpu.PrefetchScalarGridSpec(
        num_scalar_prefetch=len(meta),
        grid=(N_EXPERTS,),
        in_specs=[
            pl.BlockSpec(memory_space=pltpu.SMEM),
            pl.BlockSpec(memory_space=pl.ANY),
            wspec((D_MODEL, EXPERT_FF)), wspec((D_MODEL, EXPERT_FF)), wspec((EXPERT_FF, D_MODEL)),
        ],
        out_specs=pl.BlockSpec(memory_space=pl.ANY),
        scratch_shapes=[
            pltpu.SMEM((2 * SEQ,), jnp.int32),
            pltpu.VMEM((2, EXPERT_TILE, D_MODEL), F32),
            pltpu.VMEM((2, EXPERT_TILE, D_MODEL), F32),
            pltpu.SemaphoreType.DMA((2,)),
            pltpu.SemaphoreType.DMA((2,)),
            pltpu.VMEM((D_MODEL, EXPERT_FF), BF16),
            pltpu.VMEM((D_MODEL, EXPERT_FF), BF16),
            pltpu.VMEM((EXPERT_FF, D_MODEL), BF16),
        ],
    )
    return pl.pallas_call(
        _expert_kernel,
        grid_spec=grid_spec,
        out_shape=jax.ShapeDtypeStruct((N_EXPERT_TILES * EXPERT_TILE, D_MODEL), F32),
        compiler_params=_params(),
        name="experts",
    )(*meta, posu, n2, w_gate, w_up, w_down)


def _final_kernel(pos_ref, posn_ref, h_ref, route_ref, gf_ref, y_hbm, o_ref, ybuf, sem):
    i = pl.program_id(0)
    n = pl.num_programs(0)
    tm = ROW_TILE

    @pl.when(i == 0)
    def _():
        _start_row_gather(y_hbm, lambda r: pos_ref[0, 0, r], ybuf, 0, sem, range(2 * tm))

    @pl.when(i + 1 < n)
    def _():
        _start_row_gather(y_hbm, lambda r: posn_ref[0, 0, r], ybuf, (i + 1) % 2, sem,
                          range(2 * tm))

    slot = i % 2
    _wait_row_gather(y_hbm, ybuf, slot, sem, 2 * tm)
    w1 = route_ref[:, 0:1]
    w2 = route_ref[:, 1:2]
    h2 = h_ref[...] + (w1 * ybuf[slot, 0:tm, :] + w2 * ybuf[slot, tm:2 * tm, :])
    o_ref[...] = h2 * lax.rsqrt(jnp.mean(h2 * h2, axis=-1, keepdims=True) + EPS) * gf_ref[...]


def _final(pos, h, route, gf, y_sorted):
    tm = ROW_TILE
    n = SEQ // tm
    row = lambda w: pl.BlockSpec((tm, w), lambda i: (i, 0))
    smem_blk = lambda f: pl.BlockSpec((1, 1, 2 * tm), f, memory_space=pltpu.SMEM)
    return pl.pallas_call(
        _final_kernel,
        grid=(n,),
        in_specs=[smem_blk(lambda i: (i, 0, 0)),
                  smem_blk(lambda i: (jnp.minimum(i + 1, n - 1), 0, 0)),
                  row(D_MODEL), row(2), _resident(gf.shape),
                  pl.BlockSpec(memory_space=pl.ANY)],
        out_specs=row(D_MODEL),
        out_shape=jax.ShapeDtypeStruct((SEQ, D_MODEL), F32),
        scratch_shapes=[pltpu.VMEM((2, 2 * tm, D_MODEL), F32), pltpu.SemaphoreType.DMA((2,))],
        compiler_params=_params(),
        name="final",
    )(pos, pos, h, route, gf, y_sorted)


def _dispatch(route, counts):
    i32 = jnp.int32
    e_pair = route[0:2].T.astype(i32).reshape(-1)
    rank = route[4:6].T.astype(i32).reshape(-1)
    cnt = counts[:, 0].astype(i32)
    ntl = (cnt + EXPERT_TILE - 1) // EXPERT_TILE
    tend = jnp.cumsum(ntl)
    tst = tend - ntl
    ust = jnp.cumsum(cnt) - cnt
    experts = jnp.arange(N_EXPERTS, dtype=i32)
    pick = lambda onehot, table: jnp.sum(jnp.where(onehot, table[None, :], 0), axis=1)
    of_pair = e_pair[:, None] == experts[None, :]
    posu = pick(of_pair, ust) + rank
    pos = pick(of_pair, tst * EXPERT_TILE) + rank
    tile_ids = jnp.arange(N_EXPERT_TILES, dtype=i32)
    tile_expert = jnp.sum((tend[None, :] <= tile_ids[:, None]).astype(i32), axis=1)
    of_tile = tile_expert[:, None] == experts[None, :]
    k = tile_ids - pick(of_tile, tst)
    tbase = pick(of_tile, ust) + k * EXPERT_TILE
    trows = jnp.clip(pick(of_tile, cnt) - k * EXPERT_TILE, 0, EXPERT_TILE)
    meta = (ntl, tst, tbase, trows, tend[-1:])
    return meta, posu, pos


def kernel(x, norm1_g, w_in, kv_norm_g, w_uk, w_uv, gmlp_ws, gmlp_bs, ln_v_g, ln_v_b, w_br_attn,
           w_br_gmlp, w_out, norm2_g, w_group, b_group, w_router, b_router, w_e_gate, w_e_up,
           w_e_down, norm_f_g):
    assert x.shape == (1, SEQ, D_MODEL)
    x2 = x.reshape(SEQ, D_MODEL)
    row_vec = lambda v: v.reshape(1, -1).astype(F32)

    c_q = ATTN_WIDTH
    c_kv = c_q + KV_LATENT
    c_qi = c_kv + IDX_HEADS * IDX_DIM
    c_k = c_qi + IDX_DIM
    c_w = c_k + IDX_HEADS
    c_uv = c_w + 2 * GMLP_WIDTH
    w_bf = w_in.astype(BF16)
    wqT, wc, wqiT = w_bf[:, :c_q].T, w_bf[:, c_q:c_kv], w_bf[:, c_kv:c_qi].T
    wk, wwT = w_bf[:, c_qi:c_k], w_bf[:, c_k:c_w].T
    wuv_in, wga, wgb = w_bf[:, c_w:c_uv], w_bf[:, c_uv:c_uv + D_MODEL], w_bf[:, c_uv + D_MODEL:]
    zk = jnp.zeros_like(wk)
    wkk = jnp.concatenate([wk, zk, zk, wk], axis=1)

    n1, qT, ckv, ckvT, qiT, kk2, wiT = _proj(x2, row_vec(norm1_g), wqT, wc, wqiT, wkk, wwT,
                                             row_vec(kv_norm_g))
    m = _gmlp(n1, wuv_in, gmlp_ws, jnp.pad(gmlp_bs.T, ((0, 0), (0, LANES - GMLP_GROUPS))),
              row_vec(ln_v_g), row_vec(ln_v_b))
    a = _attn(qT, qiT, wiT, kk2, ckv, ckvT,
              jnp.swapaxes(w_uk, 1, 2).astype(BF16), jnp.swapaxes(w_uv, 1, 2).astype(BF16))
    mg = _merge(n1, a, m, wga, wgb, w_br_attn.astype(BF16), w_br_gmlp.astype(BF16))

    w_route = jnp.pad(jnp.concatenate([w_group, w_router], axis=1),
                      ((0, 0), (0, LANES - N_GROUPS - N_EXPERTS)))
    b_route = jnp.pad(jnp.concatenate([b_group, b_router]), (0, LANES - N_GROUPS - N_EXPERTS))
    w_route_t = w_route.T
    wr_hi = w_route_t.astype(BF16)
    wr_lo = (w_route_t - wr_hi.astype(F32)).astype(BF16)
    h, n2, route, counts = _outproj(mg, x2, w_out.astype(BF16), row_vec(norm2_g), wr_hi, wr_lo,
                                    b_route.reshape(LANES, 1))

    meta, posu, pos = _dispatch(route, counts)
    y_sorted = _experts(meta, posu, n2, w_e_gate, w_e_up, w_e_down)
    pos_tiles = pos.reshape(SEQ // ROW_TILE, ROW_TILE, 2).transpose(0, 2, 1).reshape(
        SEQ // ROW_TILE, 1, 2 * ROW_TILE)
    out = _final(pos_tiles, h, route[2:4].T, row_vec(norm_f_g), y_sorted)
    return out.reshape(1, SEQ, D_MODEL)
```

```python
import jax
import jax.numpy as jnp
import numpy as np
from jax import lax
from jax.experimental import pallas as pl
from jax.experimental.pallas import tpu as pltpu

F32 = jnp.float32
BF16 = jnp.bfloat16

D_MODEL = 2048
SEQ = 8192
N_HEADS = 8
HEAD_DIM = 128
KV_LATENT = 256
IDX_HEADS = 16
IDX_DIM = 64
TOPK = 256
ATTN_WIDTH = N_HEADS * HEAD_DIM
GMLP_GROUPS = 8
GMLP_WIDTH = 1024
CHUNK = 128
N_GROUPS = 8
EXPERTS_PER_GROUP = 8
N_EXPERTS = 64
EXPERT_FF = 512
EPS = 1e-6

LANES = 128
SUBLANES = 8
VMEM_LIMIT = 60 * 1024 * 1024
MASK_VALUE = -0.7 * float(np.finfo(np.float32).max)
LOG2_E = float(np.log2(np.e))

ROW_TILE = 256
TQ = 256
KB = 256
NKB = SEQ // KB
COUNT_ROWS = 32
MAX_SEARCH_ITERS = 64
EXPERT_TILE = 256
N_EXPERT_TILES = (2 * SEQ) // EXPERT_TILE + N_EXPERTS
DMA_UNROLL = 8


def _dot(a, b):
    return jnp.dot(a, b, preferred_element_type=F32)


def _dot_nt(a, b):
    return lax.dot_general(a, b, (((1,), (1,)), ((), ())), preferred_element_type=F32)


def _resident(shape):
    zeros = (0,) * len(shape)
    return pl.BlockSpec(shape, lambda *_: zeros, pipeline_mode=pl.Buffered(1))


def _params(n_axes=1):
    return pltpu.CompilerParams(
        dimension_semantics=("arbitrary",) * n_axes, vmem_limit_bytes=VMEM_LIMIT)


def _proj_kernel(x_ref, g1_ref, wqT_ref, wc_ref, wqiT_ref, wkk_ref, wwT_ref, kvg_ref,
                 n1_ref, qT_ref, ckv_ref, ckvT_ref, qiT_ref, kk2_ref, wiT_ref):
    x = x_ref[...]
    ms = jnp.mean(x * x, axis=-1, keepdims=True)
    n1 = (x * lax.rsqrt(ms + EPS) * g1_ref[...]).astype(BF16)
    n1_ref[...] = n1
    qT_ref[...] = _dot_nt(wqT_ref[...], n1).astype(BF16)
    qiT_ref[...] = (_dot_nt(wqiT_ref[...], n1) * (IDX_DIM ** -0.5)).astype(BF16)
    wiT_ref[...] = _dot_nt(wwT_ref[...], n1) * (IDX_HEADS ** -0.5)
    c = _dot(n1, wc_ref[...])
    c = c * lax.rsqrt(jnp.mean(c * c, axis=-1, keepdims=True) + EPS) * kvg_ref[...]
    ckv_ref[...] = c.astype(BF16)
    ckvT_ref[0] = c.T.astype(BF16)
    kk = _dot(n1, wkk_ref[...]).astype(BF16)
    kk2_ref[0, 0:KB, :] = kk[:, 0:LANES]
    kk2_ref[0, KB:2 * KB, :] = kk[:, LANES:2 * LANES]


def _proj(x2, g1, wqT, wc, wqiT, wkk, wwT, kvg):
    tm = KB
    row = lambda w: pl.BlockSpec((tm, w), lambda i: (i, 0))
    col = lambda h: pl.BlockSpec((h, tm), lambda i: (0, i))
    return pl.pallas_call(
        _proj_kernel,
        grid=(SEQ // tm,),
        in_specs=[row(D_MODEL), _resident(g1.shape), _resident(wqT.shape), _resident(wc.shape),
                  _resident(wqiT.shape), _resident(wkk.shape), _resident(wwT.shape),
                  _resident(kvg.shape)],
        out_specs=[row(D_MODEL), col(ATTN_WIDTH), row(KV_LATENT),
                   pl.BlockSpec((1, KV_LATENT, tm), lambda i: (i, 0, 0)),
                   col(IDX_HEADS * IDX_DIM),
                   pl.BlockSpec((1, 2 * tm, LANES), lambda i: (i, 0, 0)),
                   col(IDX_HEADS)],
        out_shape=[
            jax.ShapeDtypeStruct((SEQ, D_MODEL), BF16),
            jax.ShapeDtypeStruct((ATTN_WIDTH, SEQ), BF16),
            jax.ShapeDtypeStruct((SEQ, KV_LATENT), BF16),
            jax.ShapeDtypeStruct((NKB, KV_LATENT, KB), BF16),
            jax.ShapeDtypeStruct((IDX_HEADS * IDX_DIM, SEQ), BF16),
            jax.ShapeDtypeStruct((NKB, 2 * KB, LANES), BF16),
            jax.ShapeDtypeStruct((IDX_HEADS, SEQ), F32),
        ],
        compiler_params=_params(),
        name="proj",
    )(x2, g1, wqT, wc, wqiT, wkk, wwT, kvg)


def _gmlp_kernel(n1_ref, wuv_ref, ws_ref, bsT_ref, lng_ref, lnb_ref, m_ref):
    uv = _dot(n1_ref[...], wuv_ref[...])
    z = jax.nn.gelu(uv)
    u = z[:, :GMLP_WIDTH]
    v = z[:, GMLP_WIDTH:]
    mu = jnp.mean(v, axis=-1, keepdims=True)
    var = jnp.mean(jnp.square(v - mu), axis=-1, keepdims=True)
    vn = ((v - mu) * lax.rsqrt(var + EPS) * lng_ref[...] + lnb_ref[...]).astype(BF16)
    t_pos = lax.broadcasted_iota(jnp.int32, (CHUNK, CHUNK), 0)
    s_pos = lax.broadcasted_iota(jnp.int32, (CHUNK, CHUNK), 1)
    causal = s_pos <= t_pos
    for g in range(GMLP_GROUPS):
        wm = jnp.where(causal, ws_ref[g], 0.0).astype(BF16)
        bias = bsT_ref[:, g:g + 1]
        cols = slice(g * LANES, (g + 1) * LANES)
        for c in range(ROW_TILE // CHUNK):
            rows = slice(c * CHUNK, (c + 1) * CHUNK)
            y = _dot(wm, vn[rows, cols]) + bias
            m_ref[rows, cols] = (u[rows, cols] * y).astype(BF16)


def _gmlp(n1, wuv, ws, bsT, lng, lnb):
    tm = ROW_TILE
    return pl.pallas_call(
        _gmlp_kernel,
        grid=(SEQ // tm,),
        in_specs=[pl.BlockSpec((tm, D_MODEL), lambda i: (i, 0)), _resident(wuv.shape),
                  _resident(ws.shape), _resident(bsT.shape), _resident(lng.shape),
                  _resident(lnb.shape)],
        out_specs=pl.BlockSpec((tm, GMLP_WIDTH), lambda i: (i, 0)),
        out_shape=jax.ShapeDtypeStruct((SEQ, GMLP_WIDTH), BF16),
        compiler_params=_params(),
        name="gmlp",
    )(n1, wuv, ws, bsT, lng, lnb)


def _attn_kernel(qT_ref, qiT_ref, wiT_ref, kk2_ref, ckv_ref, ckvT_ref, wukT_ref, wuvT_ref,
                 a_ref,
                 sc_ref, qabs_ref, thr_ref, keep_ref, m_ref, l_ref, acc_ref, lg_ref):
    i = pl.program_id(0)
    nkb = i + 1

    for h in range(N_HEADS):
        qa = _dot(wukT_ref[h], qT_ref[h * HEAD_DIM:(h + 1) * HEAD_DIM, :]) * (
            HEAD_DIM ** -0.5 * LOG2_E)
        qabs_ref[h] = qa.astype(BF16)

    q_pos = i * TQ + lax.broadcasted_iota(jnp.int32, (KB, TQ), 1)
    k_off = lax.broadcasted_iota(jnp.int32, (KB, TQ), 0)

    def score_body(kb, carry):
        smax, smin = carry
        keys = kk2_ref[kb]
        acc = jnp.zeros((KB, TQ), F32)
        for j in range(IDX_HEADS // 2):
            d = _dot(keys, qiT_ref[j * LANES:(j + 1) * LANES, :])
            acc = acc + jnp.maximum(d[0:KB], 0.0) * wiT_ref[2 * j:2 * j + 1, :]
            acc = acc + jnp.maximum(d[KB:2 * KB], 0.0) * wiT_ref[2 * j + 1:2 * j + 2, :]
        causal = (kb * KB + k_off) <= q_pos
        sc_ref[kb] = jnp.where(causal, acc, -jnp.inf)
        smax = jnp.maximum(smax, jnp.max(jnp.where(causal, acc, -jnp.inf), axis=0, keepdims=True))
        smin = jnp.minimum(smin, jnp.min(jnp.where(causal, acc, jnp.inf), axis=0, keepdims=True))
        return smax, smin

    n_pairs = (nkb + 1) // 2
    smax, smin = lax.fori_loop(
        0, n_pairs, lambda j, c: score_body(2 * j + 1, score_body(2 * j, c)),
        (jnp.full((1, TQ), -jnp.inf, F32), jnp.full((1, TQ), jnp.inf, F32)))

    def count_ge(x):
        def body(j, cnt):
            for kb in (2 * j, 2 * j + 1):
                ge = jnp.where(sc_ref[kb] >= x, 1.0, 0.0)
                cnt = cnt + jnp.sum(ge.reshape(KB // COUNT_ROWS, COUNT_ROWS, TQ), axis=0)
            return cnt
        cnt = lax.fori_loop(0, n_pairs, body, jnp.zeros((COUNT_ROWS, TQ), F32))
        return jnp.sum(cnt, axis=0, keepdims=True)

    n_causal = (i * TQ + 1 + lax.broadcasted_iota(jnp.int32, (1, TQ), 1)).astype(F32)
    want = jnp.minimum(n_causal, float(TOPK))
    hi0 = smax + jnp.maximum(jnp.abs(smax), 1e-30) * 1e-6
    open0 = (n_causal != want).astype(F32)

    def search_cond(state):
        it, _, _, _, _, flag = state
        return jnp.logical_and(it < MAX_SEARCH_ITERS, flag > 0.0)

    def search_body(state):
        it, lo, hi, c_lo, open_, _ = state
        mid = 0.5 * lo + 0.5 * hi
        c = count_ge(mid)
        live = jnp.logical_and(open_ > 0.0, jnp.logical_and(mid > lo, mid < hi))
        ge = c >= want
        go_lo = jnp.logical_and(live, ge)
        lo = jnp.where(go_lo, mid, lo)
        c_lo = jnp.where(go_lo, c, c_lo)
        hi = jnp.where(jnp.logical_and(live, jnp.logical_not(ge)), mid, hi)
        open_ = jnp.logical_and(live, c != want).astype(F32)
        return it + 1, lo, hi, c_lo, open_, jnp.max(open_)

    _, lo_f, _, c_lo_f, _, _ = lax.while_loop(
        search_cond, search_body,
        (jnp.int32(0), smin, hi0, n_causal, open0, jnp.max(open0)))
    thr_ref[...] = jnp.broadcast_to(lo_f, (SUBLANES, TQ))
    keep_ref[...] = jnp.full((SUBLANES, TQ), float(SEQ), F32)
    unresolved0 = (c_lo_f != want).astype(F32)

    @pl.when(jnp.max(unresolved0) > 0.0)
    def _():
        def next_value(lo, below):
            def body(kb, u):
                s = sc_ref[kb]
                cand = jnp.logical_and(s >= lo, s > below)
                return jnp.minimum(u, jnp.min(jnp.where(cand, s, jnp.inf), axis=0, keepdims=True))
            return lax.fori_loop(0, nkb, body, jnp.full((1, TQ), jnp.inf, F32))

        def count_gt(x):
            def body(kb, cnt):
                gt = jnp.where(sc_ref[kb] > x, 1.0, 0.0)
                return cnt + jnp.sum(gt, axis=0, keepdims=True)
            return lax.fori_loop(0, nkb, body, jnp.zeros((1, TQ), F32))

        def peel_cond(state):
            return state[-1] > 0.0

        def peel_body(state):
            below, unres, thr, keep, _ = state
            u = next_value(lo_f, below)
            c_gt = count_gt(u)
            hit = jnp.logical_and(unres > 0.0, c_gt < want)
            thr = jnp.where(hit, u, thr)
            keep = jnp.where(hit, want - c_gt, keep)
            unres = jnp.logical_and(unres > 0.0, jnp.logical_not(hit)).astype(F32)
            return u, unres, thr, keep, jnp.max(unres)

        _, _, thr_t, keep_t, _ = lax.while_loop(
            peel_cond, peel_body,
            (jnp.full((1, TQ), -jnp.inf, F32), unresolved0, lo_f,
             jnp.full((1, TQ), float(SEQ), F32), jnp.float32(1.0)))
        thr_ref[...] = jnp.broadcast_to(thr_t, (SUBLANES, TQ))
        keep_ref[...] = jnp.broadcast_to(keep_t, (SUBLANES, TQ))

        r_i = lax.broadcasted_iota(jnp.int32, (KB, KB), 0)
        c_i = lax.broadcasted_iota(jnp.int32, (KB, KB), 1)
        before = (c_i < r_i).astype(BF16)

        def drop_body(kb, seen):
            s = sc_ref[kb]
            eq = jnp.logical_and(s == thr_t, unresolved0 > 0.0)
            eq_f = eq.astype(F32)
            rank = seen + _dot(before, eq_f.astype(BF16))
            sc_ref[kb] = jnp.where(jnp.logical_and(eq, rank >= keep_t), -jnp.inf, s)
            return seen + jnp.sum(eq_f, axis=0, keepdims=True)

        lax.fori_loop(0, nkb, drop_body, jnp.zeros((1, TQ), F32))

    m_ref[...] = jnp.full(m_ref.shape, MASK_VALUE, F32)
    l_ref[...] = jnp.zeros(l_ref.shape, F32)
    acc_ref[...] = jnp.zeros(acc_ref.shape, F32)

    def logits_stage(j, slot):
        thr = thr_ref[0:1, :]
        bias0 = jnp.where(sc_ref[2 * j] >= thr, 0.0, MASK_VALUE)
        bias1 = jnp.where(sc_ref[2 * j + 1] >= thr, 0.0, MASK_VALUE)
        c_n = ckv_ref[pl.ds(pl.multiple_of(j * (2 * KB), 2 * KB), 2 * KB), :]
        for h in range(N_HEADS):
            lg = _dot(c_n, qabs_ref[h])
            lg_ref[slot, h, 0:KB, :] = lg[0:KB] + bias0
            lg_ref[slot, h, KB:2 * KB, :] = lg[KB:2 * KB] + bias1

    def softmax_stage(j, slot):
        c_t0 = ckvT_ref[2 * j]
        c_t1 = ckvT_ref[2 * j + 1]
        for h in range(N_HEADS):
            lg0 = lg_ref[slot, h, 0:KB, :]
            lg1 = lg_ref[slot, h, KB:2 * KB, :]
            m_old = m_ref[h, 0:1, :]
            m_new = jnp.maximum(m_old, jnp.max(jnp.maximum(lg0, lg1), axis=0, keepdims=True))
            alpha = jnp.exp2(m_old - m_new)
            p0 = jnp.exp2(lg0 - m_new)
            p1 = jnp.exp2(lg1 - m_new)
            l_new = alpha * l_ref[h, 0:1, :] + jnp.sum(p0 + p1, axis=0, keepdims=True)
            acc_ref[h] = (acc_ref[h] * alpha + _dot(c_t0, p0.astype(BF16))
                          + _dot(c_t1, p1.astype(BF16)))
            m_ref[h] = jnp.broadcast_to(m_new, (SUBLANES, TQ))
            l_ref[h] = jnp.broadcast_to(l_new, (SUBLANES, TQ))

    logits_stage(0, 0)

    def att_body(j, carry):
        for slot in range(2):
            @pl.when(j % 2 == slot)
            def _():
                logits_stage(j + 1, 1 - slot)
                softmax_stage(j, slot)
        return carry

    lax.fori_loop(0, n_pairs - 1, att_body, 0)
    for slot in range(2):
        @pl.when((n_pairs - 1) % 2 == slot)
        def _():
            softmax_stage(n_pairs - 1, slot)

    for h in range(N_HEADS):
        o_t = (acc_ref[h] / l_ref[h, 0:1, :]).astype(BF16)
        a_t = _dot(wuvT_ref[h], o_t)
        a_ref[:, h * HEAD_DIM:(h + 1) * HEAD_DIM] = a_t.T.astype(BF16)


def _attn(qT, qiT, wiT, kk2, ckv, ckvT, wukT, wuvT):
    col = lambda h: pl.BlockSpec((h, TQ), lambda i: (0, i))
    return pl.pallas_call(
        _attn_kernel,
        grid=(SEQ // TQ,),
        in_specs=[col(ATTN_WIDTH), col(IDX_HEADS * IDX_DIM), col(IDX_HEADS), _resident(kk2.shape),
                  _resident(ckv.shape), _resident(ckvT.shape), _resident(wukT.shape),
                  _resident(wuvT.shape)],
        out_specs=pl.BlockSpec((TQ, ATTN_WIDTH), lambda i: (i, 0)),
        out_shape=jax.ShapeDtypeStruct((SEQ, ATTN_WIDTH), BF16),
        scratch_shapes=[
            pltpu.VMEM((NKB, KB, TQ), F32),
            pltpu.VMEM((N_HEADS, KV_LATENT, TQ), BF16),
            pltpu.VMEM((SUBLANES, TQ), F32),
            pltpu.VMEM((SUBLANES, TQ), F32),
            pltpu.VMEM((N_HEADS, SUBLANES, TQ), F32),
            pltpu.VMEM((N_HEADS, SUBLANES, TQ), F32),
            pltpu.VMEM((N_HEADS, KV_LATENT, TQ), F32),
            pltpu.VMEM((2, N_HEADS, 2 * KB, TQ), F32),
        ],
        compiler_params=_params(),
        name="attn",
    )(qT, qiT, wiT, kk2, ckv, ckvT, wukT, wuvT)


def _merge_kernel(n1_ref, a_ref, m_ref, wga_ref, wgb_ref, wba_ref, wbg_ref, o_ref):
    n1 = n1_ref[...]
    br_a = jax.nn.sigmoid(_dot(n1, wga_ref[...])) * _dot(a_ref[...], wba_ref[...])
    br_b = jax.nn.sigmoid(_dot(n1, wgb_ref[...])) * _dot(m_ref[...], wbg_ref[...])
    o_ref[...] = (br_a + br_b).astype(BF16)


def _merge(n1, a, m, wga, wgb, wba, wbg):
    tm = ROW_TILE
    row = lambda w: pl.BlockSpec((tm, w), lambda i: (i, 0))
    return pl.pallas_call(
        _merge_kernel,
        grid=(SEQ // tm,),
        in_specs=[row(D_MODEL), row(ATTN_WIDTH), row(GMLP_WIDTH), _resident(wga.shape),
                  _resident(wgb.shape), _resident(wba.shape), _resident(wbg.shape)],
        out_specs=row(D_MODEL),
        out_shape=jax.ShapeDtypeStruct((SEQ, D_MODEL), BF16),
        compiler_params=_params(),
        name="merge",
    )(n1, a, m, wga, wgb, wba, wbg)


ROUTE_ROWS = 8


def _outproj_kernel(mg_ref, x_ref, wo_ref, g2_ref, wrh_ref, wrl_ref, br_ref,
                    h_ref, n2_ref, route_ref, counts_ref, carry_ref):
    i = pl.program_id(0)
    tm = ROW_TILE

    @pl.when(i == 0)
    def _():
        carry_ref[...] = jnp.zeros(carry_ref.shape, F32)

    h = x_ref[...] + _dot(mg_ref[...], wo_ref[...])
    h_ref[...] = h
    n2 = h * lax.rsqrt(jnp.mean(h * h, axis=-1, keepdims=True) + EPS) * g2_ref[...]
    n2_ref[...] = n2
    n2_hi = n2.astype(BF16)
    n2_lo = (n2 - n2_hi.astype(F32)).astype(BF16)
    logits = (_dot_nt(wrh_ref[...], n2_hi) + _dot_nt(wrh_ref[...], n2_lo)
              + _dot_nt(wrl_ref[...], n2_hi)) + br_ref[...]
    row = lax.broadcasted_iota(jnp.int32, (LANES, tm), 0).astype(F32)
    is_group = row < N_GROUPS
    gl = jnp.where(is_group, logits, -jnp.inf)
    gmax = jnp.max(gl, axis=0, keepdims=True)
    gsum = jnp.sum(jnp.where(is_group, jnp.exp(logits - gmax), 0.0), axis=0, keepdims=True)
    g_val = 1.0 / gsum
    g_idx = jnp.min(jnp.where(gl == gmax, row, float(LANES)), axis=0, keepdims=True)
    e_id = row - N_GROUPS
    in_group = jnp.logical_and(
        jnp.logical_and(e_id >= 0, e_id < N_EXPERTS),
        jnp.floor(e_id * (1.0 / EXPERTS_PER_GROUP)) == g_idx)
    sel = jnp.where(in_group, logits, -jnp.inf)
    v1 = jnp.max(sel, axis=0, keepdims=True)
    i1 = jnp.min(jnp.where(sel == v1, row, float(LANES)), axis=0, keepdims=True)
    sel2 = jnp.where(row == i1, -jnp.inf, sel)
    v2 = jnp.max(sel2, axis=0, keepdims=True)
    i2 = jnp.min(jnp.where(sel2 == v2, row, float(LANES)), axis=0, keepdims=True)
    x2 = jnp.exp(v2 - v1)
    den = 1.0 + x2
    w1 = g_val * (1.0 / den)
    w2 = g_val * (x2 / den)
    e1 = i1 - N_GROUPS
    e2 = i2 - N_GROUPS

    e_row = lax.broadcasted_iota(jnp.int32, (N_EXPERTS, tm), 0).astype(F32)
    hit1 = (e_row == e1).astype(F32)
    hit2 = (e_row == e2).astype(F32)
    hits = hit1 + hit2
    t_from = lax.broadcasted_iota(jnp.int32, (tm, tm), 0)
    t_to = lax.broadcasted_iota(jnp.int32, (tm, tm), 1)
    earlier = (t_from < t_to).astype(BF16)
    before = carry_ref[:, 0:1] + _dot(hits.astype(BF16), earlier)
    rank1 = jnp.sum(hit1 * before, axis=0, keepdims=True)
    rank2 = jnp.sum(hit2 * before, axis=0, keepdims=True)
    carry = carry_ref[...] + jnp.sum(hits, axis=1, keepdims=True)
    carry_ref[...] = carry
    counts_ref[...] = carry

    r = lax.broadcasted_iota(jnp.int32, (ROUTE_ROWS, tm), 0)
    route = jnp.where(r == 0, e1, 0.0)
    for k, v in enumerate((e2, w1, w2, rank1, rank2), start=1):
        route = jnp.where(r == k, v, route)
    route_ref[...] = route


def _outproj(mg, x2, wo, g2, wr_hi, wr_lo, br):
    tm = ROW_TILE
    row = lambda w: pl.BlockSpec((tm, w), lambda i: (i, 0))
    return pl.pallas_call(
        _outproj_kernel,
        grid=(SEQ // tm,),
        in_specs=[row(D_MODEL), row(D_MODEL), _resident(wo.shape), _resident(g2.shape),
                  _resident(wr_hi.shape), _resident(wr_lo.shape), _resident(br.shape)],
        out_specs=[row(D_MODEL), row(D_MODEL),
                   pl.BlockSpec((ROUTE_ROWS, tm), lambda i: (0, i)),
                   pl.BlockSpec((N_EXPERTS, LANES), lambda i: (0, 0))],
        out_shape=[jax.ShapeDtypeStruct((SEQ, D_MODEL), F32),
                   jax.ShapeDtypeStruct((SEQ, D_MODEL), F32),
                   jax.ShapeDtypeStruct((ROUTE_ROWS, SEQ), F32),
                   jax.ShapeDtypeStruct((N_EXPERTS, LANES), F32)],
        scratch_shapes=[pltpu.VMEM((N_EXPERTS, LANES), F32)],
        compiler_params=_params(),
        name="outproj",
    )(mg, x2, wo, g2, wr_hi, wr_lo, br)


def _start_row_gather(src_hbm, row_of, dst, slot, sem, rows):
    for r in rows:
        pltpu.make_async_copy(src_hbm.at[pl.ds(row_of(r), 1), :],
                              dst.at[slot, pl.ds(r, 1), :], sem.at[slot]).start()


def _wait_row_gather(src_hbm, dst, slot, sem, n_rows):
    pltpu.make_async_copy(src_hbm.at[pl.ds(0, n_rows), :], dst.at[slot], sem.at[slot]).wait()


def _expert_kernel(ntl_ref, tst_ref, tbase_ref, trows_ref, ntot_ref,
                   posu_ref, x_hbm, wg_hbm, wu_hbm, wd_hbm,
                   y_hbm,
                   order, xbuf, ybuf, gsem, ysem, wg_st, wu_st, wd_st, wsem, wgb, wub, wdb):
    e = pl.program_id(0)
    ntot = ntot_ref[0]

    def weight_copies(ex, slot):
        return [pltpu.make_async_copy(src.at[ex], dst.at[slot], wsem.at[slot, j])
                for j, (src, dst) in enumerate(((wg_hbm, wg_st), (wu_hbm, wu_st), (wd_hbm, wd_st)))]

    def start_gather(t, slot, rows):
        base = tbase_ref[t]
        last = trows_ref[t] - 1
        token = lambda r: lax.shift_right_logical(order[base + jnp.minimum(r, last)], 1)
        _start_row_gather(x_hbm, token, xbuf, slot, gsem, rows)

    def y_copy(t, slot):
        return pltpu.make_async_copy(
            ybuf.at[slot], y_hbm.at[pl.ds(pl.multiple_of(t * EXPERT_TILE, EXPERT_TILE), EXPERT_TILE), :],
            ysem.at[slot])

    @pl.when(e == 0)
    def _():
        for c in weight_copies(0, 0):
            c.start(priority=1)

    @pl.when(e + 1 < pl.num_programs(0))
    def _():
        for c in weight_copies(e + 1, (e + 1) % 2):
            c.start(priority=1)

    @pl.when(e == 0)
    def _():
        def invert(p, carry):
            order[posu_ref[p]] = p
            return carry
        lax.fori_loop(0, 2 * SEQ, invert, 0, unroll=DMA_UNROLL)
        start_gather(0, 0, range(EXPERT_TILE))

    n_here = ntl_ref[e]
    wslot = e % 2
    for c in weight_copies(e, wslot):
        c.wait()

    @pl.when(n_here > 0)
    def _():
        wgb[...] = wg_st[wslot].astype(BF16)
        wub[...] = wu_st[wslot].astype(BF16)
        wdb[...] = wd_st[wslot].astype(BF16)

        def tile_body(k, carry):
            t = tst_ref[e] + k
            slot = t % 2
            _wait_row_gather(x_hbm, xbuf, slot, gsem, EXPERT_TILE)
            nxt = jnp.minimum(t + 1, ntot - 1)
            third = EXPERT_TILE // 3
            xt = xbuf[slot].astype(BF16)
            start_gather(nxt, 1 - slot, range(0, third))
            gate = _dot(xt, wgb[...])
            start_gather(nxt, 1 - slot, range(third, 2 * third))
            up = _dot(xt, wub[...])
            start_gather(nxt, 1 - slot, range(2 * third, EXPERT_TILE))
            y = _dot((jax.nn.silu(gate) * up).astype(BF16), wdb[...])

            @pl.when(t >= 2)
            def _():
                y_copy(t - 2, slot).wait()

            ybuf[slot] = y
            y_copy(t, slot).start()
            return carry
        lax.fori_loop(0, n_here, tile_body, 0)

    @pl.when(e == pl.num_programs(0) - 1)
    def _():
        _wait_row_gather(x_hbm, xbuf, ntot % 2, gsem, EXPERT_TILE)

        @pl.when(ntot >= 2)
        def _():
            y_copy(ntot - 2, (ntot - 2) % 2).wait()
        y_copy(ntot - 1, (ntot - 1) % 2).wait()


def _experts(meta, posu, n2, w_gate, w_up, w_down):
    hbm = pl.BlockSpec(memory_space=pl.ANY)
    grid_spec = pltpu.PrefetchScalarGridSpec(
        num_scalar_prefetch=len(meta),
        grid=(N_EXPERTS,),
        in_specs=[pl.BlockSpec(memory_space=pltpu.SMEM), hbm, hbm, hbm, hbm],
        out_specs=hbm,
        scratch_shapes=[
            pltpu.SMEM((2 * SEQ,), jnp.int32),
            pltpu.VMEM((2, EXPERT_TILE, D_MODEL), F32),
            pltpu.VMEM((2, EXPERT_TILE, D_MODEL), F32),
            pltpu.SemaphoreType.DMA((2,)),
            pltpu.SemaphoreType.DMA((2,)),
            pltpu.VMEM((2, D_MODEL, EXPERT_FF), F32),
            pltpu.VMEM((2, D_MODEL, EXPERT_FF), F32),
            pltpu.VMEM((2, EXPERT_FF, D_MODEL), F32),
            pltpu.SemaphoreType.DMA((2, 3)),
            pltpu.VMEM((D_MODEL, EXPERT_FF), BF16),
            pltpu.VMEM((D_MODEL, EXPERT_FF), BF16),
            pltpu.VMEM((EXPERT_FF, D_MODEL), BF16),
        ],
    )
    return pl.pallas_call(
        _expert_kernel,
        grid_spec=grid_spec,
        out_shape=jax.ShapeDtypeStruct((N_EXPERT_TILES * EXPERT_TILE, D_MODEL), F32),
        compiler_params=_params(),
        name="experts",
    )(*meta, posu, n2, w_gate, w_up, w_down)


def _final_kernel(pos_ref, posn_ref, h_ref, route_ref, gf_ref, y_hbm, o_ref, ybuf, sem):
    i = pl.program_id(0)
    n = pl.num_programs(0)
    tm = ROW_TILE

    @pl.when(i == 0)
    def _():
        _start_row_gather(y_hbm, lambda r: pos_ref[0, 0, r], ybuf, 0, sem, range(2 * tm))

    @pl.when(i + 1 < n)
    def _():
        _start_row_gather(y_hbm, lambda r: posn_ref[0, 0, r], ybuf, (i + 1) % 2, sem,
                          range(2 * tm))

    slot = i % 2
    _wait_row_gather(y_hbm, ybuf, slot, sem, 2 * tm)
    w1 = route_ref[:, 0:1]
    w2 = route_ref[:, 1:2]
    h2 = h_ref[...] + (w1 * ybuf[slot, 0:tm, :] + w2 * ybuf[slot, tm:2 * tm, :])
    o_ref[...] = h2 * lax.rsqrt(jnp.mean(h2 * h2, axis=-1, keepdims=True) + EPS) * gf_ref[...]


def _final(pos, h, route, gf, y_sorted):
    tm = ROW_TILE
    n = SEQ // tm
    row = lambda w: pl.BlockSpec((tm, w), lambda i: (i, 0))
    smem_blk = lambda f: pl.BlockSpec((1, 1, 2 * tm), f, memory_space=pltpu.SMEM)
    return pl.pallas_call(
        _final_kernel,
        grid=(n,),
        in_specs=[smem_blk(lambda i: (i, 0, 0)),
                  smem_blk(lambda i: (jnp.minimum(i + 1, n - 1), 0, 0)),
                  row(D_MODEL), row(2), _resident(gf.shape),
                  pl.BlockSpec(memory_space=pl.ANY)],
        out_specs=row(D_MODEL),
        out_shape=jax.ShapeDtypeStruct((SEQ, D_MODEL), F32),
        scratch_shapes=[pltpu.VMEM((2, 2 * tm, D_MODEL), F32), pltpu.SemaphoreType.DMA((2,))],
        compiler_params=_params(),
        name="final",
    )(pos, pos, h, route, gf, y_sorted)


def _dispatch(route, counts):
    i32 = jnp.int32
    e_pair = route[0:2].T.astype(i32).reshape(-1)
    rank = route[4:6].T.astype(i32).reshape(-1)
    cnt = counts[:, 0].astype(i32)
    ntl = (cnt + EXPERT_TILE - 1) // EXPERT_TILE
    tend = jnp.cumsum(ntl)
    tst = tend - ntl
    ust = jnp.cumsum(cnt) - cnt
    experts = jnp.arange(N_EXPERTS, dtype=i32)
    pick = lambda onehot, table: jnp.sum(jnp.where(onehot, table[None, :], 0), axis=1)
    of_pair = e_pair[:, None] == experts[None, :]
    posu = pick(of_pair, ust) + rank
    pos = pick(of_pair, tst * EXPERT_TILE) + rank
    tile_ids = jnp.arange(N_EXPERT_TILES, dtype=i32)
    tile_expert = jnp.sum((tend[None, :] <= tile_ids[:, None]).astype(i32), axis=1)
    of_tile = tile_expert[:, None] == experts[None, :]
    k = tile_ids - pick(of_tile, tst)
    tbase = pick(of_tile, ust) + k * EXPERT_TILE
    trows = jnp.clip(pick(of_tile, cnt) - k * EXPERT_TILE, 0, EXPERT_TILE)
    meta = (ntl, tst, tbase, trows, tend[-1:])
    return meta, posu, pos


def kernel(x, norm1_g, w_in, kv_norm_g, w_uk, w_uv, gmlp_ws, gmlp_bs, ln_v_g, ln_v_b, w_br_attn,
           w_br_gmlp, w_out, norm2_g, w_group, b_group, w_router, b_router, w_e_gate, w_e_up,
           w_e_down, norm_f_g):
    assert x.shape == (1, SEQ, D_MODEL)
    x2 = x.reshape(SEQ, D_MODEL)
    row_vec = lambda v: v.reshape(1, -1).astype(F32)

    c_q = ATTN_WIDTH
    c_kv = c_q + KV_LATENT
    c_qi = c_kv + IDX_HEADS * IDX_DIM
    c_k = c_qi + IDX_DIM
    c_w = c_k + IDX_HEADS
    c_uv = c_w + 2 * GMLP_WIDTH
    w_bf = w_in.astype(BF16)
    wqT, wc, wqiT = w_bf[:, :c_q].T, w_bf[:, c_q:c_kv], w_bf[:, c_kv:c_qi].T
    wk, wwT = w_bf[:, c_qi:c_k], w_bf[:, c_k:c_w].T
    wuv_in, wga, wgb = w_bf[:, c_w:c_uv], w_bf[:, c_uv:c_uv + D_MODEL], w_bf[:, c_uv + D_MODEL:]
    zk = jnp.zeros_like(wk)
    wkk = jnp.concatenate([wk, zk, zk, wk], axis=1)

    n1, qT, ckv, ckvT, qiT, kk2, wiT = _proj(x2, row_vec(norm1_g), wqT, wc, wqiT, wkk, wwT,
                                             row_vec(kv_norm_g))
    m = _gmlp(n1, wuv_in, gmlp_ws, jnp.pad(gmlp_bs.T, ((0, 0), (0, LANES - GMLP_GROUPS))),
              row_vec(ln_v_g), row_vec(ln_v_b))
    a = _attn(qT, qiT, wiT, kk2, ckv, ckvT,
              jnp.swapaxes(w_uk, 1, 2).astype(BF16), jnp.swapaxes(w_uv, 1, 2).astype(BF16))
    mg = _merge(n1, a, m, wga, wgb, w_br_attn.astype(BF16), w_br_gmlp.astype(BF16))

    w_route = jnp.pad(jnp.concatenate([w_group, w_router], axis=1),
                      ((0, 0), (0, LANES - N_GROUPS - N_EXPERTS)))
    b_route = jnp.pad(jnp.concatenate([b_group, b_router]), (0, LANES - N_GROUPS - N_EXPERTS))
    w_route_t = w_route.T
    wr_hi = w_route_t.astype(BF16)
    wr_lo = (w_route_t - wr_hi.astype(F32)).astype(BF16)
    h, n2, route, counts = _outproj(mg, x2, w_out.astype(BF16), row_vec(norm2_g), wr_hi, wr_lo,
                                    b_route.reshape(LANES, 1))

    meta, posu, pos = _dispatch(route, counts)
    y_sorted = _experts(meta, posu, n2, w_e_gate, w_e_up, w_e_down)
    pos_tiles = pos.reshape(SEQ // ROW_TILE, ROW_TILE, 2).transpose(0, 2, 1).reshape(
        SEQ // ROW_TILE, 1, 2 * ROW_TILE)
    out = _final(pos_tiles, h, route[2:4].T, row_vec(norm_f_g), y_sorted)
    return out.reshape(1, SEQ, D_MODEL)
```

```python
import jax
import jax.numpy as jnp
import numpy as np
from jax import lax
from jax.experimental import pallas as pl
from jax.experimental.pallas import tpu as pltpu

F32 = jnp.float32
BF16 = jnp.bfloat16

D_MODEL = 2048
SEQ = 8192
N_HEADS = 8
HEAD_DIM = 128
KV_LATENT = 256
IDX_HEADS = 16
IDX_DIM = 64
TOPK = 256
ATTN_WIDTH = N_HEADS * HEAD_DIM
GMLP_GROUPS = 8
GMLP_WIDTH = 1024
CHUNK = 128
N_GROUPS = 8
EXPERTS_PER_GROUP = 8
N_EXPERTS = 64
EXPERT_FF = 512
EPS = 1e-6

LANES = 128
SUBLANES = 8
VMEM_LIMIT = 60 * 1024 * 1024
MASK_VALUE = -0.7 * float(np.finfo(np.float32).max)
LOG2_E = float(np.log2(np.e))

ROW_TILE = 256
TQ = 256
KB = 256
NKB = SEQ // KB
COUNT_ROWS = 32
MAX_SEARCH_ITERS = 64
EXPERT_TILE = 256
N_EXPERT_TILES = (2 * SEQ) // EXPERT_TILE + N_EXPERTS
DMA_UNROLL = 8


def _dot(a, b):
    return jnp.dot(a, b, preferred_element_type=F32)


def _dot_nt(a, b):
    return lax.dot_general(a, b, (((1,), (1,)), ((), ())), preferred_element_type=F32)


def _resident(shape):
    zeros = (0,) * len(shape)
    return pl.BlockSpec(shape, lambda *_: zeros, pipeline_mode=pl.Buffered(1))


def _params(n_axes=1):
    return pltpu.CompilerParams(
        dimension_semantics=("arbitrary",) * n_axes, vmem_limit_bytes=VMEM_LIMIT)


def _proj_kernel(x_ref, g1_ref, wqT_ref, wc_ref, wqiT_ref, wkk_ref, wwT_ref, kvg_ref,
                 n1_ref, qT_ref, ckv_ref, ckvT_ref, qiT_ref, kk2_ref, wiT_ref):
    x = x_ref[...]
    ms = jnp.mean(x * x, axis=-1, keepdims=True)
    n1 = (x * lax.rsqrt(ms + EPS) * g1_ref[...]).astype(BF16)
    n1_ref[...] = n1
    qT_ref[...] = _dot_nt(wqT_ref[...], n1).astype(BF16)
    qiT_ref[...] = (_dot_nt(wqiT_ref[...], n1) * (IDX_DIM ** -0.5)).astype(BF16)
    wiT_ref[...] = _dot_nt(wwT_ref[...], n1) * (IDX_HEADS ** -0.5)
    c = _dot(n1, wc_ref[...])
    c = c * lax.rsqrt(jnp.mean(c * c, axis=-1, keepdims=True) + EPS) * kvg_ref[...]
    ckv_ref[...] = c.astype(BF16)
    ckvT_ref[0] = c.T.astype(BF16)
    kk = _dot(n1, wkk_ref[...]).astype(BF16)
    kk2_ref[0, 0:KB, :] = kk[:, 0:LANES]
    kk2_ref[0, KB:2 * KB, :] = kk[:, LANES:2 * LANES]


def _proj(x2, g1, wqT, wc, wqiT, wkk, wwT, kvg):
    tm = KB
    row = lambda w: pl.BlockSpec((tm, w), lambda i: (i, 0))
    col = lambda h: pl.BlockSpec((h, tm), lambda i: (0, i))
    return pl.pallas_call(
        _proj_kernel,
        grid=(SEQ // tm,),
        in_specs=[row(D_MODEL), _resident(g1.shape), _resident(wqT.shape), _resident(wc.shape),
                  _resident(wqiT.shape), _resident(wkk.shape), _resident(wwT.shape),
                  _resident(kvg.shape)],
        out_specs=[row(D_MODEL), col(ATTN_WIDTH), row(KV_LATENT),
                   pl.BlockSpec((1, KV_LATENT, tm), lambda i: (i, 0, 0)),
                   col(IDX_HEADS * IDX_DIM),
                   pl.BlockSpec((1, 2 * tm, LANES), lambda i: (i, 0, 0)),
                   col(IDX_HEADS)],
        out_shape=[
            jax.ShapeDtypeStruct((SEQ, D_MODEL), BF16),
            jax.ShapeDtypeStruct((ATTN_WIDTH, SEQ), BF16),
            jax.ShapeDtypeStruct((SEQ, KV_LATENT), BF16),
            jax.ShapeDtypeStruct((NKB, KV_LATENT, KB), BF16),
            jax.ShapeDtypeStruct((IDX_HEADS * IDX_DIM, SEQ), BF16),
            jax.ShapeDtypeStruct((NKB, 2 * KB, LANES), BF16),
            jax.ShapeDtypeStruct((IDX_HEADS, SEQ), F32),
        ],
        compiler_params=_params(),
        name="proj",
    )(x2, g1, wqT, wc, wqiT, wkk, wwT, kvg)


def _gmlp_kernel(n1_ref, wuv_ref, ws_ref, bsT_ref, lng_ref, lnb_ref, m_ref):
    uv = _dot(n1_ref[...], wuv_ref[...])
    z = jax.nn.gelu(uv)
    u = z[:, :GMLP_WIDTH]
    v = z[:, GMLP_WIDTH:]
    mu = jnp.mean(v, axis=-1, keepdims=True)
    var = jnp.mean(jnp.square(v - mu), axis=-1, keepdims=True)
    vn = ((v - mu) * lax.rsqrt(var + EPS) * lng_ref[...] + lnb_ref[...]).astype(BF16)
    t_pos = lax.broadcasted_iota(jnp.int32, (CHUNK, CHUNK), 0)
    s_pos = lax.broadcasted_iota(jnp.int32, (CHUNK, CHUNK), 1)
    causal = s_pos <= t_pos
    for g in range(GMLP_GROUPS):
        wm = jnp.where(causal, ws_ref[g], 0.0).astype(BF16)
        bias = bsT_ref[:, g:g + 1]
        cols = slice(g * LANES, (g + 1) * LANES)
        for c in range(ROW_TILE // CHUNK):
            rows = slice(c * CHUNK, (c + 1) * CHUNK)
            y = _dot(wm, vn[rows, cols]) + bias
            m_ref[rows, cols] = (u[rows, cols] * y).astype(BF16)


def _gmlp(n1, wuv, ws, bsT, lng, lnb):
    tm = ROW_TILE
    return pl.pallas_call(
        _gmlp_kernel,
        grid=(SEQ // tm,),
        in_specs=[pl.BlockSpec((tm, D_MODEL), lambda i: (i, 0)), _resident(wuv.shape),
                  _resident(ws.shape), _resident(bsT.shape), _resident(lng.shape),
                  _resident(lnb.shape)],
        out_specs=pl.BlockSpec((tm, GMLP_WIDTH), lambda i: (i, 0)),
        out_shape=jax.ShapeDtypeStruct((SEQ, GMLP_WIDTH), BF16),
        compiler_params=_params(),
        name="gmlp",
    )(n1, wuv, ws, bsT, lng, lnb)


def _attn_kernel(qT_ref, qiT_ref, wiT_ref, kk2_ref, ckv_ref, ckvT_ref, wukT_ref, wuvT_ref,
                 a_ref,
                 sc_ref, qabs_ref, thr_ref, keep_ref, m_ref, l_ref, acc_ref, lg_ref):
    i = pl.program_id(0)
    nkb = i + 1

    for h in range(N_HEADS):
        qa = _dot(wukT_ref[h], qT_ref[h * HEAD_DIM:(h + 1) * HEAD_DIM, :]) * (
            HEAD_DIM ** -0.5 * LOG2_E)
        qabs_ref[h] = qa.astype(BF16)

    q_pos = i * TQ + lax.broadcasted_iota(jnp.int32, (KB, TQ), 1)
    k_off = lax.broadcasted_iota(jnp.int32, (KB, TQ), 0)

    def score_body(kb, carry):
        smax, smin = carry
        keys = kk2_ref[kb]
        acc = jnp.zeros((KB, TQ), F32)
        for j in range(IDX_HEADS // 2):
            d = _dot(keys, qiT_ref[j * LANES:(j + 1) * LANES, :])
            acc = acc + jnp.maximum(d[0:KB], 0.0) * wiT_ref[2 * j:2 * j + 1, :]
            acc = acc + jnp.maximum(d[KB:2 * KB], 0.0) * wiT_ref[2 * j + 1:2 * j + 2, :]
        causal = (kb * KB + k_off) <= q_pos
        sc_ref[kb] = jnp.where(causal, acc, -jnp.inf)
        smax = jnp.maximum(smax, jnp.max(jnp.where(causal, acc, -jnp.inf), axis=0, keepdims=True))
        smin = jnp.minimum(smin, jnp.min(jnp.where(causal, acc, jnp.inf), axis=0, keepdims=True))
        return smax, smin

    n_pairs = (nkb + 1) // 2
    smax, smin = lax.fori_loop(
        0, n_pairs, lambda j, c: score_body(2 * j + 1, score_body(2 * j, c)),
        (jnp.full((1, TQ), -jnp.inf, F32), jnp.full((1, TQ), jnp.inf, F32)))

    def count_ge(x):
        def body(j, cnt):
            for kb in (2 * j, 2 * j + 1):
                ge = jnp.where(sc_ref[kb] >= x, 1.0, 0.0)
                cnt = cnt + jnp.sum(ge.reshape(KB // COUNT_ROWS, COUNT_ROWS, TQ), axis=0)
            return cnt
        cnt = lax.fori_loop(0, n_pairs, body, jnp.zeros((COUNT_ROWS, TQ), F32))
        return jnp.sum(cnt, axis=0, keepdims=True)

    n_causal = (i * TQ + 1 + lax.broadcasted_iota(jnp.int32, (1, TQ), 1)).astype(F32)
    want = jnp.minimum(n_causal, float(TOPK))
    hi0 = smax + jnp.maximum(jnp.abs(smax), 1e-30) * 1e-6
    open0 = (n_causal != want).astype(F32)

    def search_cond(state):
        it, _, _, _, _, flag = state
        return jnp.logical_and(it < MAX_SEARCH_ITERS, flag > 0.0)

    def search_body(state):
        it, lo, hi, c_lo, open_, _ = state
        mid = 0.5 * lo + 0.5 * hi
        c = count_ge(mid)
        live = jnp.logical_and(open_ > 0.0, jnp.logical_and(mid > lo, mid < hi))
        ge = c >= want
        go_lo = jnp.logical_and(live, ge)
        lo = jnp.where(go_lo, mid, lo)
        c_lo = jnp.where(go_lo, c, c_lo)
        hi = jnp.where(jnp.logical_and(live, jnp.logical_not(ge)), mid, hi)
        open_ = jnp.logical_and(live, c != want).astype(F32)
        return it + 1, lo, hi, c_lo, open_, jnp.max(open_)

    _, lo_f, _, c_lo_f, _, _ = lax.while_loop(
        search_cond, search_body,
        (jnp.int32(0), smin, hi0, n_causal, open0, jnp.max(open0)))
    thr_ref[...] = jnp.broadcast_to(lo_f, (SUBLANES, TQ))
    keep_ref[...] = jnp.full((SUBLANES, TQ), float(SEQ), F32)
    unresolved0 = (c_lo_f != want).astype(F32)

    @pl.when(jnp.max(unresolved0) > 0.0)
    def _():
        def next_value(lo, below):
            def body(kb, u):
                s = sc_ref[kb]
                cand = jnp.logical_and(s >= lo, s > below)
                return jnp.minimum(u, jnp.min(jnp.where(cand, s, jnp.inf), axis=0, keepdims=True))
            return lax.fori_loop(0, nkb, body, jnp.full((1, TQ), jnp.inf, F32))

        def count_gt(x):
            def body(kb, cnt):
                gt = jnp.where(sc_ref[kb] > x, 1.0, 0.0)
                return cnt + jnp.sum(gt, axis=0, keepdims=True)
            return lax.fori_loop(0, nkb, body, jnp.zeros((1, TQ), F32))

        def peel_cond(state):
            return state[-1] > 0.0

        def peel_body(state):
            below, unres, thr, keep, _ = state
            u = next_value(lo_f, below)
            c_gt = count_gt(u)
            hit = jnp.logical_and(unres > 0.0, c_gt < want)
            thr = jnp.where(hit, u, thr)
            keep = jnp.where(hit, want - c_gt, keep)
            unres = jnp.logical_and(unres > 0.0, jnp.logical_not(hit)).astype(F32)
            return u, unres, thr, keep, jnp.max(unres)

        _, _, thr_t, keep_t, _ = lax.while_loop(
            peel_cond, peel_body,
            (jnp.full((1, TQ), -jnp.inf, F32), unresolved0, lo_f,
             jnp.full((1, TQ), float(SEQ), F32), jnp.float32(1.0)))
        thr_ref[...] = jnp.broadcast_to(thr_t, (SUBLANES, TQ))
        keep_ref[...] = jnp.broadcast_to(keep_t, (SUBLANES, TQ))

        r_i = lax.broadcasted_iota(jnp.int32, (KB, KB), 0)
        c_i = lax.broadcasted_iota(jnp.int32, (KB, KB), 1)
        before = (c_i < r_i).astype(BF16)

        def drop_body(kb, seen):
            s = sc_ref[kb]
            eq = jnp.logical_and(s == thr_t, unresolved0 > 0.0)
            eq_f = eq.astype(F32)
            rank = seen + _dot(before, eq_f.astype(BF16))
            sc_ref[kb] = jnp.where(jnp.logical_and(eq, rank >= keep_t), -jnp.inf, s)
            return seen + jnp.sum(eq_f, axis=0, keepdims=True)

        lax.fori_loop(0, nkb, drop_body, jnp.zeros((1, TQ), F32))

    m_ref[...] = jnp.full(m_ref.shape, MASK_VALUE, F32)
    l_ref[...] = jnp.zeros(l_ref.shape, F32)
    acc_ref[...] = jnp.zeros(acc_ref.shape, F32)

    def logits_stage(j, slot):
        thr = thr_ref[0:1, :]
        bias0 = jnp.where(sc_ref[2 * j] >= thr, 0.0, MASK_VALUE)
        bias1 = jnp.where(sc_ref[2 * j + 1] >= thr, 0.0, MASK_VALUE)
        c_n = ckv_ref[pl.ds(pl.multiple_of(j * (2 * KB), 2 * KB), 2 * KB), :]
        for h in range(N_HEADS):
            lg = _dot(c_n, qabs_ref[h])
            lg_ref[slot, h, 0:KB, :] = lg[0:KB] + bias0
            lg_ref[slot, h, KB:2 * KB, :] = lg[KB:2 * KB] + bias1

    def softmax_stage(j, slot):
        c_t0 = ckvT_ref[2 * j]
        c_t1 = ckvT_ref[2 * j + 1]
        for h in range(N_HEADS):
            lg0 = lg_ref[slot, h, 0:KB, :]
            lg1 = lg_ref[slot, h, KB:2 * KB, :]
            m_old = m_ref[h, 0:1, :]
            m_new = jnp.maximum(m_old, jnp.max(jnp.maximum(lg0, lg1), axis=0, keepdims=True))
            alpha = jnp.exp2(m_old - m_new)
            p0 = jnp.exp2(lg0 - m_new)
            p1 = jnp.exp2(lg1 - m_new)
            l_new = alpha * l_ref[h, 0:1, :] + jnp.sum(p0 + p1, axis=0, keepdims=True)
            acc_ref[h] = (acc_ref[h] * alpha + _dot(c_t0, p0.astype(BF16))
                          + _dot(c_t1, p1.astype(BF16)))
            m_ref[h] = jnp.broadcast_to(m_new, (SUBLANES, TQ))
            l_ref[h] = jnp.broadcast_to(l_new, (SUBLANES, TQ))

    logits_stage(0, 0)

    def att_body(j, carry):
        for slot in range(2):
            @pl.when(j % 2 == slot)
            def _():
                logits_stage(j + 1, 1 - slot)
                softmax_stage(j, slot)
        return carry

    lax.fori_loop(0, n_pairs - 1, att_body, 0)
    for slot in range(2):
        @pl.when((n_pairs - 1) % 2 == slot)
        def _():
            softmax_stage(n_pairs - 1, slot)

    for h in range(N_HEADS):
        o_t = (acc_ref[h] / l_ref[h, 0:1, :]).astype(BF16)
        a_t = _dot(wuvT_ref[h], o_t)
        a_ref[:, h * HEAD_DIM:(h + 1) * HEAD_DIM] = a_t.T.astype(BF16)


def _attn(qT, qiT, wiT, kk2, ckv, ckvT, wukT, wuvT):
    col = lambda h: pl.BlockSpec((h, TQ), lambda i: (0, i))
    return pl.pallas_call(
        _attn_kernel,
        grid=(SEQ // TQ,),
        in_specs=[col(ATTN_WIDTH), col(IDX_HEADS * IDX_DIM), col(IDX_HEADS), _resident(kk2.shape),
                  _resident(ckv.shape), _resident(ckvT.shape), _resident(wukT.shape),
                  _resident(wuvT.shape)],
        out_specs=pl.BlockSpec((TQ, ATTN_WIDTH), lambda i: (i, 0)),
        out_shape=jax.ShapeDtypeStruct((SEQ, ATTN_WIDTH), BF16),
        scratch_shapes=[
            pltpu.VMEM((NKB, KB, TQ), F32),
            pltpu.VMEM((N_HEADS, KV_LATENT, TQ), BF16),
            pltpu.VMEM((SUBLANES, TQ), F32),
            pltpu.VMEM((SUBLANES, TQ), F32),
            pltpu.VMEM((N_HEADS, SUBLANES, TQ), F32),
            pltpu.VMEM((N_HEADS, SUBLANES, TQ), F32),
            pltpu.VMEM((N_HEADS, KV_LATENT, TQ), F32),
            pltpu.VMEM((2, N_HEADS, 2 * KB, TQ), F32),
        ],
        compiler_params=_params(),
        name="attn",
    )(qT, qiT, wiT, kk2, ckv, ckvT, wukT, wuvT)


def _merge_kernel(n1_ref, a_ref, m_ref, wga_ref, wgb_ref, wba_ref, wbg_ref, o_ref):
    n1 = n1_ref[...]
    br_a = jax.nn.sigmoid(_dot(n1, wga_ref[...])) * _dot(a_ref[...], wba_ref[...])
    br_b = jax.nn.sigmoid(_dot(n1, wgb_ref[...])) * _dot(m_ref[...], wbg_ref[...])
    o_ref[...] = (br_a + br_b).astype(BF16)


def _merge(n1, a, m, wga, wgb, wba, wbg):
    tm = ROW_TILE
    row = lambda w: pl.BlockSpec((tm, w), lambda i: (i, 0))
    return pl.pallas_call(
        _merge_kernel,
        grid=(SEQ // tm,),
        in_specs=[row(D_MODEL), row(ATTN_WIDTH), row(GMLP_WIDTH), _resident(wga.shape),
                  _resident(wgb.shape), _resident(wba.shape), _resident(wbg.shape)],
        out_specs=row(D_MODEL),
        out_shape=jax.ShapeDtypeStruct((SEQ, D_MODEL), BF16),
        compiler_params=_params(),
        name="merge",
    )(n1, a, m, wga, wgb, wba, wbg)


ROUTE_ROWS = 8


def _outproj_kernel(mg_ref, x_ref, wo_ref, g2_ref, wrh_ref, wrl_ref, br_ref,
                    h_ref, n2_ref, route_ref, counts_ref, carry_ref):
    i = pl.program_id(0)
    tm = ROW_TILE

    @pl.when(i == 0)
    def _():
        carry_ref[...] = jnp.zeros(carry_ref.shape, F32)

    h = x_ref[...] + _dot(mg_ref[...], wo_ref[...])
    h_ref[...] = h
    n2 = h * lax.rsqrt(jnp.mean(h * h, axis=-1, keepdims=True) + EPS) * g2_ref[...]
    n2_ref[...] = n2
    n2_hi = n2.astype(BF16)
    n2_lo = (n2 - n2_hi.astype(F32)).astype(BF16)
    logits = (_dot_nt(wrh_ref[...], n2_hi) + _dot_nt(wrh_ref[...], n2_lo)
              + _dot_nt(wrl_ref[...], n2_hi)) + br_ref[...]
    row = lax.broadcasted_iota(jnp.int32, (LANES, tm), 0).astype(F32)
    is_group = row < N_GROUPS
    gl = jnp.where(is_group, logits, -jnp.inf)
    gmax = jnp.max(gl, axis=0, keepdims=True)
    gsum = jnp.sum(jnp.where(is_group, jnp.exp(logits - gmax), 0.0), axis=0, keepdims=True)
    g_val = 1.0 / gsum
    g_idx = jnp.min(jnp.where(gl == gmax, row, float(LANES)), axis=0, keepdims=True)
    e_id = row - N_GROUPS
    in_group = jnp.logical_and(
        jnp.logical_and(e_id >= 0, e_id < N_EXPERTS),
        jnp.floor(e_id * (1.0 / EXPERTS_PER_GROUP)) == g_idx)
    sel = jnp.where(in_group, logits, -jnp.inf)
    v1 = jnp.max(sel, axis=0, keepdims=True)
    i1 = jnp.min(jnp.where(sel == v1, row, float(LANES)), axis=0, keepdims=True)
    sel2 = jnp.where(row == i1, -jnp.inf, sel)
    v2 = jnp.max(sel2, axis=0, keepdims=True)
    i2 = jnp.min(jnp.where(sel2 == v2, row, float(LANES)), axis=0, keepdims=True)
    x2 = jnp.exp(v2 - v1)
    den = 1.0 + x2
    w1 = g_val * (1.0 / den)
    w2 = g_val * (x2 / den)
    e1 = i1 - N_GROUPS
    e2 = i2 - N_GROUPS

    e_row = lax.broadcasted_iota(jnp.int32, (N_EXPERTS, tm), 0).astype(F32)
    hit1 = (e_row == e1).astype(F32)
    hit2 = (e_row == e2).astype(F32)
    hits = hit1 + hit2
    t_from = lax.broadcasted_iota(jnp.int32, (tm, tm), 0)
    t_to = lax.broadcasted_iota(jnp.int32, (tm, tm), 1)
    earlier = (t_from < t_to).astype(BF16)
    before = carry_ref[:, 0:1] + _dot(hits.astype(BF16), earlier)
    rank1 = jnp.sum(hit1 * before, axis=0, keepdims=True)
    rank2 = jnp.sum(hit2 * before, axis=0, keepdims=True)
    carry = carry_ref[...] + jnp.sum(hits, axis=1, keepdims=True)
    carry_ref[...] = carry
    counts_ref[...] = carry

    r = lax.broadcasted_iota(jnp.int32, (ROUTE_ROWS, tm), 0)
    route = jnp.where(r == 0, e1, 0.0)
    for k, v in enumerate((e2, w1, w2, rank1, rank2), start=1):
        route = jnp.where(r == k, v, route)
    route_ref[...] = route


def _outproj(mg, x2, wo, g2, wr_hi, wr_lo, br):
    tm = ROW_TILE
    row = lambda w: pl.BlockSpec((tm, w), lambda i: (i, 0))
    return pl.pallas_call(
        _outproj_kernel,
        grid=(SEQ // tm,),
        in_specs=[row(D_MODEL), row(D_MODEL), _resident(wo.shape), _resident(g2.shape),
                  _resident(wr_hi.shape), _resident(wr_lo.shape), _resident(br.shape)],
        out_specs=[row(D_MODEL), row(D_MODEL),
                   pl.BlockSpec((ROUTE_ROWS, tm), lambda i: (0, i)),
                   pl.BlockSpec((N_EXPERTS, LANES), lambda i: (0, 0))],
        out_shape=[jax.ShapeDtypeStruct((SEQ, D_MODEL), F32),
                   jax.ShapeDtypeStruct((SEQ, D_MODEL), F32),
                   jax.ShapeDtypeStruct((ROUTE_ROWS, SEQ), F32),
                   jax.ShapeDtypeStruct((N_EXPERTS, LANES), F32)],
        scratch_shapes=[pltpu.VMEM((N_EXPERTS, LANES), F32)],
        compiler_params=_params(),
        name="outproj",
    )(mg, x2, wo, g2, wr_hi, wr_lo, br)


def _start_row_gather(src_hbm, row_of, dst, slot, sem, rows):
    for r in rows:
        pltpu.make_async_copy(src_hbm.at[pl.ds(row_of(r), 1), :],
                              dst.at[slot, pl.ds(r, 1), :], sem.at[slot]).start()


def _wait_row_gather(src_hbm, dst, slot, sem, n_rows):
    pltpu.make_async_copy(src_hbm.at[pl.ds(0, n_rows), :], dst.at[slot], sem.at[slot]).wait()


def _sort_kernel(tbase_ref, trows_ref, ntot_ref, posu_ref, x_hbm, xs_ref, order, xbuf, gsem):
    t = pl.program_id(0)
    ntot = ntot_ref[0]

    def start_gather(tile, slot):
        base = tbase_ref[tile]
        last = trows_ref[tile] - 1
        token = lambda r: lax.shift_right_logical(order[base + jnp.minimum(r, last)], 1)
        _start_row_gather(x_hbm, token, xbuf, slot, gsem, range(EXPERT_TILE))

    @pl.when(t == 0)
    def _():
        def invert(p, carry):
            order[posu_ref[p]] = p
            return carry
        lax.fori_loop(0, 2 * SEQ, invert, 0, unroll=DMA_UNROLL)
        start_gather(0, 0)

    @pl.when(t + 1 < ntot)
    def _():
        start_gather(t + 1, (t + 1) % 2)

    @pl.when(t < ntot)
    def _():
        _wait_row_gather(x_hbm, xbuf, t % 2, gsem, EXPERT_TILE)
        xs_ref[...] = xbuf[t % 2].astype(BF16)


def _sort_rows(tbase, trows, ntot, posu, n2):
    last = lambda t, ntot: jnp.minimum(t, ntot[0] - 1)
    grid_spec = pltpu.PrefetchScalarGridSpec(
        num_scalar_prefetch=3,
        grid=(N_EXPERT_TILES,),
        in_specs=[pl.BlockSpec(memory_space=pltpu.SMEM), pl.BlockSpec(memory_space=pl.ANY)],
        out_specs=pl.BlockSpec((EXPERT_TILE, D_MODEL), lambda t, tb, tr, ntot: (last(t, ntot), 0)),
        scratch_shapes=[pltpu.SMEM((2 * SEQ,), jnp.int32),
                        pltpu.VMEM((2, EXPERT_TILE, D_MODEL), F32),
                        pltpu.SemaphoreType.DMA((2,))],
    )
    return pl.pallas_call(
        _sort_kernel,
        grid_spec=grid_spec,
        out_shape=jax.ShapeDtypeStruct((N_EXPERT_TILES * EXPERT_TILE, D_MODEL), BF16),
        compiler_params=_params(),
        name="sort_rows",
    )(tbase, trows, ntot, posu, n2)


def _expert_kernel(ntl_ref, tst_ref, ntot_ref,
                   xs_hbm, wg_hbm, wu_hbm, wd_hbm,
                   y_hbm,
                   xbuf, ybuf, xsem, ysem, wg_st, wu_st, wd_st, wsem, wgb, wub, wdb):
    e = pl.program_id(0)
    ntot = ntot_ref[0]

    def weight_copies(ex, slot):
        return [pltpu.make_async_copy(src.at[ex], dst.at[slot], wsem.at[slot, j])
                for j, (src, dst) in enumerate(((wg_hbm, wg_st), (wu_hbm, wu_st), (wd_hbm, wd_st)))]

    def tile_rows(t):
        return pl.ds(pl.multiple_of(t * EXPERT_TILE, EXPERT_TILE), EXPERT_TILE)

    def x_copy(t, slot):
        return pltpu.make_async_copy(xs_hbm.at[tile_rows(t), :], xbuf.at[slot], xsem.at[slot])

    def y_copy(t, slot):
        return pltpu.make_async_copy(ybuf.at[slot], y_hbm.at[tile_rows(t), :], ysem.at[slot])

    @pl.when(e == 0)
    def _():
        for c in weight_copies(0, 0):
            c.start()
        x_copy(0, 0).start()

    @pl.when(e + 1 < pl.num_programs(0))
    def _():
        for c in weight_copies(e + 1, (e + 1) % 2):
            c.start()

    n_here = ntl_ref[e]
    wslot = e % 2
    for c in weight_copies(e, wslot):
        c.wait()

    @pl.when(n_here > 0)
    def _():
        wgb[...] = wg_st[wslot].astype(BF16)
        wub[...] = wu_st[wslot].astype(BF16)
        wdb[...] = wd_st[wslot].astype(BF16)

        def tile_body(k, carry):
            t = tst_ref[e] + k
            slot = t % 2
            x_copy(t, slot).wait()

            @pl.when(t + 1 < ntot)
            def _():
                x_copy(t + 1, 1 - slot).start()

            xt = xbuf[slot]
            hid = jax.nn.silu(_dot(xt, wgb[...])) * _dot(xt, wub[...])
            y = _dot(hid.astype(BF16), wdb[...])

            @pl.when(t >= 2)
            def _():
                y_copy(t - 2, slot).wait()

            ybuf[slot] = y
            y_copy(t, slot).start()
            return carry
        lax.fori_loop(0, n_here, tile_body, 0)

    @pl.when(e == pl.num_programs(0) - 1)
    def _():
        @pl.when(ntot >= 2)
        def _():
            y_copy(ntot - 2, (ntot - 2) % 2).wait()
        y_copy(ntot - 1, (ntot - 1) % 2).wait()


def _experts(ntl, tst, ntot, x_sorted, w_gate, w_up, w_down):
    hbm = pl.BlockSpec(memory_space=pl.ANY)
    grid_spec = pltpu.PrefetchScalarGridSpec(
        num_scalar_prefetch=3,
        grid=(N_EXPERTS,),
        in_specs=[hbm, hbm, hbm, hbm],
        out_specs=hbm,
        scratch_shapes=[
            pltpu.VMEM((2, EXPERT_TILE, D_MODEL), BF16),
            pltpu.VMEM((2, EXPERT_TILE, D_MODEL), F32),
            pltpu.SemaphoreType.DMA((2,)),
            pltpu.SemaphoreType.DMA((2,)),
            pltpu.VMEM((2, D_MODEL, EXPERT_FF), F32),
            pltpu.VMEM((2, D_MODEL, EXPERT_FF), F32),
            pltpu.VMEM((2, EXPERT_FF, D_MODEL), F32),
            pltpu.SemaphoreType.DMA((2, 3)),
            pltpu.VMEM((D_MODEL, EXPERT_FF), BF16),
            pltpu.VMEM((D_MODEL, EXPERT_FF), BF16),
            pltpu.VMEM((EXPERT_FF, D_MODEL), BF16),
        ],
    )
    return pl.pallas_call(
        _expert_kernel,
        grid_spec=grid_spec,
        out_shape=jax.ShapeDtypeStruct((N_EXPERT_TILES * EXPERT_TILE, D_MODEL), F32),
        compiler_params=_params(),
        name="experts",
    )(ntl, tst, ntot, x_sorted, w_gate, w_up, w_down)


def _final_kernel(pos_ref, posn_ref, h_ref, route_ref, gf_ref, y_hbm, o_ref, ybuf, sem):
    i = pl.program_id(0)
    n = pl.num_programs(0)
    tm = ROW_TILE

    @pl.when(i == 0)
    def _():
        _start_row_gather(y_hbm, lambda r: pos_ref[0, 0, r], ybuf, 0, sem, range(2 * tm))

    @pl.when(i + 1 < n)
    def _():
        _start_row_gather(y_hbm, lambda r: posn_ref[0, 0, r], ybuf, (i + 1) % 2, sem,
                          range(2 * tm))

    slot = i % 2
    _wait_row_gather(y_hbm, ybuf, slot, sem, 2 * tm)
    w1 = route_ref[:, 0:1]
    w2 = route_ref[:, 1:2]
    h2 = h_ref[...] + (w1 * ybuf[slot, 0:tm, :] + w2 * ybuf[slot, tm:2 * tm, :])
    o_ref[...] = h2 * lax.rsqrt(jnp.mean(h2 * h2, axis=-1, keepdims=True) + EPS) * gf_ref[...]


def _final(pos, h, route, gf, y_sorted):
    tm = ROW_TILE
    n = SEQ // tm
    row = lambda w: pl.BlockSpec((tm, w), lambda i: (i, 0))
    smem_blk = lambda f: pl.BlockSpec((1, 1, 2 * tm), f, memory_space=pltpu.SMEM)
    return pl.pallas_call(
        _final_kernel,
        grid=(n,),
        in_specs=[smem_blk(lambda i: (i, 0, 0)),
                  smem_blk(lambda i: (jnp.minimum(i + 1, n - 1), 0, 0)),
                  row(D_MODEL), row(2), _resident(gf.shape),
                  pl.BlockSpec(memory_space=pl.ANY)],
        out_specs=row(D_MODEL),
        out_shape=jax.ShapeDtypeStruct((SEQ, D_MODEL), F32),
        scratch_shapes=[pltpu.VMEM((2, 2 * tm, D_MODEL), F32), pltpu.SemaphoreType.DMA((2,))],
        compiler_params=_params(),
        name="final",
    )(pos, pos, h, route, gf, y_sorted)


def _dispatch(route, counts):
    i32 = jnp.int32
    e_pair = route[0:2].T.astype(i32).reshape(-1)
    rank = route[4:6].T.astype(i32).reshape(-1)
    cnt = counts[:, 0].astype(i32)
    ntl = (cnt + EXPERT_TILE - 1) // EXPERT_TILE
    tend = jnp.cumsum(ntl)
    tst = tend - ntl
    ust = jnp.cumsum(cnt) - cnt
    experts = jnp.arange(N_EXPERTS, dtype=i32)
    pick = lambda onehot, table: jnp.sum(jnp.where(onehot, table[None, :], 0), axis=1)
    of_pair = e_pair[:, None] == experts[None, :]
    posu = pick(of_pair, ust) + rank
    pos = pick(of_pair, tst * EXPERT_TILE) + rank
    tile_ids = jnp.arange(N_EXPERT_TILES, dtype=i32)
    tile_expert = jnp.sum((tend[None, :] <= tile_ids[:, None]).astype(i32), axis=1)
    of_tile = tile_expert[:, None] == experts[None, :]
    k = tile_ids - pick(of_tile, tst)
    tbase = pick(of_tile, ust) + k * EXPERT_TILE
    trows = jnp.clip(pick(of_tile, cnt) - k * EXPERT_TILE, 0, EXPERT_TILE)
    meta = (ntl, tst, tbase, trows, tend[-1:])
    return meta, posu, pos


def kernel(x, norm1_g, w_in, kv_norm_g, w_uk, w_uv, gmlp_ws, gmlp_bs, ln_v_g, ln_v_b, w_br_attn,
           w_br_gmlp, w_out, norm2_g, w_group, b_group, w_router, b_router, w_e_gate, w_e_up,
           w_e_down, norm_f_g):
    assert x.shape == (1, SEQ, D_MODEL)
    x2 = x.reshape(SEQ, D_MODEL)
    row_vec = lambda v: v.reshape(1, -1).astype(F32)

    c_q = ATTN_WIDTH
    c_kv = c_q + KV_LATENT
    c_qi = c_kv + IDX_HEADS * IDX_DIM
    c_k = c_qi + IDX_DIM
    c_w = c_k + IDX_HEADS
    c_uv = c_w + 2 * GMLP_WIDTH
    w_bf = w_in.astype(BF16)
    wqT, wc, wqiT = w_bf[:, :c_q].T, w_bf[:, c_q:c_kv], w_bf[:, c_kv:c_qi].T
    wk, wwT = w_bf[:, c_qi:c_k], w_bf[:, c_k:c_w].T
    wuv_in, wga, wgb = w_bf[:, c_w:c_uv], w_bf[:, c_uv:c_uv + D_MODEL], w_bf[:, c_uv + D_MODEL:]
    zk = jnp.zeros_like(wk)
    wkk = jnp.concatenate([wk, zk, zk, wk], axis=1)

    n1, qT, ckv, ckvT, qiT, kk2, wiT = _proj(x2, row_vec(norm1_g), wqT, wc, wqiT, wkk, wwT,
                                             row_vec(kv_norm_g))
    m = _gmlp(n1, wuv_in, gmlp_ws, jnp.pad(gmlp_bs.T, ((0, 0), (0, LANES - GMLP_GROUPS))),
              row_vec(ln_v_g), row_vec(ln_v_b))
    a = _attn(qT, qiT, wiT, kk2, ckv, ckvT,
              jnp.swapaxes(w_uk, 1, 2).astype(BF16), jnp.swapaxes(w_uv, 1, 2).astype(BF16))
    mg = _merge(n1, a, m, wga, wgb, w_br_attn.astype(BF16), w_br_gmlp.astype(BF16))

    w_route = jnp.pad(jnp.concatenate([w_group, w_router], axis=1),
                      ((0, 0), (0, LANES - N_GROUPS - N_EXPERTS)))
    b_route = jnp.pad(jnp.concatenate([b_group, b_router]), (0, LANES - N_GROUPS - N_EXPERTS))
    w_route_t = w_route.T
    wr_hi = w_route_t.astype(BF16)
    wr_lo = (w_route_t - wr_hi.astype(F32)).astype(BF16)
    h, n2, route, counts = _outproj(mg, x2, w_out.astype(BF16), row_vec(norm2_g), wr_hi, wr_lo,
                                    b_route.reshape(LANES, 1))

    meta, posu, pos = _dispatch(route, counts)
    ntl, tst, tbase, trows, ntot = meta
    x_sorted = _sort_rows(tbase, trows, ntot, posu, n2)
    y_sorted = _experts(ntl, tst, ntot, x_sorted, w_e_gate, w_e_up, w_e_down)
    pos_tiles = pos.reshape(SEQ // ROW_TILE, ROW_TILE, 2).transpose(0, 2, 1).reshape(
        SEQ // ROW_TILE, 1, 2 * ROW_TILE)
    out = _final(pos_tiles, h, route[2:4].T, row_vec(norm_f_g), y_sorted)
    return out.reshape(1, SEQ, D_MODEL)
```

```python
import jax
import jax.numpy as jnp
import numpy as np
from jax import lax
from jax.experimental import pallas as pl
from jax.experimental.pallas import tpu as pltpu

F32 = jnp.float32
BF16 = jnp.bfloat16

D_MODEL = 2048
SEQ = 8192
N_HEADS = 8
HEAD_DIM = 128
KV_LATENT = 256
IDX_HEADS = 16
IDX_DIM = 64
TOPK = 256
ATTN_WIDTH = N_HEADS * HEAD_DIM
GMLP_GROUPS = 8
GMLP_WIDTH = 1024
CHUNK = 128
N_GROUPS = 8
EXPERTS_PER_GROUP = 8
N_EXPERTS = 64
EXPERT_FF = 512
EPS = 1e-6

LANES = 128
SUBLANES = 8
VMEM_LIMIT = 60 * 1024 * 1024
MASK_VALUE = -0.7 * float(np.finfo(np.float32).max)
LOG2_E = float(np.log2(np.e))

ROW_TILE = 256
TQ = 256
KB = 256
NKB = SEQ // KB
COUNT_ROWS = 32
MAX_SEARCH_ITERS = 64
EXPERT_TILE = 256
N_EXPERT_TILES = (2 * SEQ) // EXPERT_TILE + N_EXPERTS
PAD_BITS = 8


def _dot(a, b):
    return jnp.dot(a, b, preferred_element_type=F32)


def _dot_nt(a, b):
    return lax.dot_general(a, b, (((1,), (1,)), ((), ())), preferred_element_type=F32)


def _resident(shape):
    zeros = (0,) * len(shape)
    return pl.BlockSpec(shape, lambda *_: zeros, pipeline_mode=pl.Buffered(1))


def _params(n_axes=1):
    return pltpu.CompilerParams(
        dimension_semantics=("arbitrary",) * n_axes, vmem_limit_bytes=VMEM_LIMIT)


def _proj_kernel(x_ref, g1_ref, wqT_ref, wc_ref, wqiT_ref, wkk_ref, wwT_ref, kvg_ref,
                 n1_ref, qT_ref, ckv_ref, ckvT_ref, qiT_ref, kk2_ref, wiT_ref):
    x = x_ref[...]
    ms = jnp.mean(x * x, axis=-1, keepdims=True)
    n1 = (x * lax.rsqrt(ms + EPS) * g1_ref[...]).astype(BF16)
    n1_ref[...] = n1
    qT_ref[...] = _dot_nt(wqT_ref[...], n1).astype(BF16)
    qiT_ref[...] = (_dot_nt(wqiT_ref[...], n1) * (IDX_DIM ** -0.5)).astype(BF16)
    wiT_ref[...] = _dot_nt(wwT_ref[...], n1) * (IDX_HEADS ** -0.5)
    c = _dot(n1, wc_ref[...])
    c = c * lax.rsqrt(jnp.mean(c * c, axis=-1, keepdims=True) + EPS) * kvg_ref[...]
    ckv_ref[...] = c.astype(BF16)
    ckvT_ref[0] = c.T.astype(BF16)
    kk = _dot(n1, wkk_ref[...]).astype(BF16)
    kk2_ref[0, 0:KB, :] = kk[:, 0:LANES]
    kk2_ref[0, KB:2 * KB, :] = kk[:, LANES:2 * LANES]


def _proj(x2, g1, wqT, wc, wqiT, wkk, wwT, kvg):
    tm = KB
    row = lambda w: pl.BlockSpec((tm, w), lambda i: (i, 0))
    col = lambda h: pl.BlockSpec((h, tm), lambda i: (0, i))
    return pl.pallas_call(
        _proj_kernel,
        grid=(SEQ // tm,),
        in_specs=[row(D_MODEL), _resident(g1.shape), _resident(wqT.shape), _resident(wc.shape),
                  _resident(wqiT.shape), _resident(wkk.shape), _resident(wwT.shape),
                  _resident(kvg.shape)],
        out_specs=[row(D_MODEL), col(ATTN_WIDTH), row(KV_LATENT),
                   pl.BlockSpec((1, KV_LATENT, tm), lambda i: (i, 0, 0)),
                   col(IDX_HEADS * IDX_DIM),
                   pl.BlockSpec((1, 2 * tm, LANES), lambda i: (i, 0, 0)),
                   col(IDX_HEADS)],
        out_shape=[
            jax.ShapeDtypeStruct((SEQ, D_MODEL), BF16),
            jax.ShapeDtypeStruct((ATTN_WIDTH, SEQ), BF16),
            jax.ShapeDtypeStruct((SEQ, KV_LATENT), BF16),
            jax.ShapeDtypeStruct((NKB, KV_LATENT, KB), BF16),
            jax.ShapeDtypeStruct((IDX_HEADS * IDX_DIM, SEQ), BF16),
            jax.ShapeDtypeStruct((NKB, 2 * KB, LANES), BF16),
            jax.ShapeDtypeStruct((IDX_HEADS, SEQ), F32),
        ],
        compiler_params=_params(),
        name="proj",
    )(x2, g1, wqT, wc, wqiT, wkk, wwT, kvg)


def _gmlp_kernel(n1_ref, wuv_ref, ws_ref, bsT_ref, lng_ref, lnb_ref, m_ref):
    uv = _dot(n1_ref[...], wuv_ref[...])
    z = jax.nn.gelu(uv)
    u = z[:, :GMLP_WIDTH]
    v = z[:, GMLP_WIDTH:]
    mu = jnp.mean(v, axis=-1, keepdims=True)
    var = jnp.mean(jnp.square(v - mu), axis=-1, keepdims=True)
    vn = ((v - mu) * lax.rsqrt(var + EPS) * lng_ref[...] + lnb_ref[...]).astype(BF16)
    t_pos = lax.broadcasted_iota(jnp.int32, (CHUNK, CHUNK), 0)
    s_pos = lax.broadcasted_iota(jnp.int32, (CHUNK, CHUNK), 1)
    causal = s_pos <= t_pos
    for g in range(GMLP_GROUPS):
        wm = jnp.where(causal, ws_ref[g], 0.0).astype(BF16)
        bias = bsT_ref[:, g:g + 1]
        cols = slice(g * LANES, (g + 1) * LANES)
        for c in range(ROW_TILE // CHUNK):
            rows = slice(c * CHUNK, (c + 1) * CHUNK)
            y = _dot(wm, vn[rows, cols]) + bias
            m_ref[rows, cols] = (u[rows, cols] * y).astype(BF16)


def _gmlp(n1, wuv, ws, bsT, lng, lnb):
    tm = ROW_TILE
    return pl.pallas_call(
        _gmlp_kernel,
        grid=(SEQ // tm,),
        in_specs=[pl.BlockSpec((tm, D_MODEL), lambda i: (i, 0)), _resident(wuv.shape),
                  _resident(ws.shape), _resident(bsT.shape), _resident(lng.shape),
                  _resident(lnb.shape)],
        out_specs=pl.BlockSpec((tm, GMLP_WIDTH), lambda i: (i, 0)),
        out_shape=jax.ShapeDtypeStruct((SEQ, GMLP_WIDTH), BF16),
        compiler_params=_params(),
        name="gmlp",
    )(n1, wuv, ws, bsT, lng, lnb)


def _attn_kernel(qT_ref, qiT_ref, wiT_ref, kk2_ref, ckv_ref, ckvT_ref, wukT_ref, wuvT_ref,
                 a_ref,
                 sc_ref, qabs_ref, thr_ref, keep_ref, m_ref, l_ref, acc_ref, lg_ref):
    i = pl.program_id(0)
    nkb = i + 1

    for h in range(N_HEADS):
        qa = _dot(wukT_ref[h], qT_ref[h * HEAD_DIM:(h + 1) * HEAD_DIM, :]) * (
            HEAD_DIM ** -0.5 * LOG2_E)
        qabs_ref[h] = qa.astype(BF16)

    q_pos = i * TQ + lax.broadcasted_iota(jnp.int32, (KB, TQ), 1)
    k_off = lax.broadcasted_iota(jnp.int32, (KB, TQ), 0)

    def score_body(kb, carry):
        smax, smin = carry
        keys = kk2_ref[kb]
        acc = jnp.zeros((KB, TQ), F32)
        for j in range(IDX_HEADS // 2):
            d = _dot(keys, qiT_ref[j * LANES:(j + 1) * LANES, :])
            acc = acc + jnp.maximum(d[0:KB], 0.0) * wiT_ref[2 * j:2 * j + 1, :]
            acc = acc + jnp.maximum(d[KB:2 * KB], 0.0) * wiT_ref[2 * j + 1:2 * j + 2, :]
        causal = (kb * KB + k_off) <= q_pos
        sc_ref[kb] = jnp.where(causal, acc, -jnp.inf)
        smax = jnp.maximum(smax, jnp.max(jnp.where(causal, acc, -jnp.inf), axis=0, keepdims=True))
        smin = jnp.minimum(smin, jnp.min(jnp.where(causal, acc, jnp.inf), axis=0, keepdims=True))
        return smax, smin

    n_pairs = (nkb + 1) // 2
    smax, smin = lax.fori_loop(
        0, n_pairs, lambda j, c: score_body(2 * j + 1, score_body(2 * j, c)),
        (jnp.full((1, TQ), -jnp.inf, F32), jnp.full((1, TQ), jnp.inf, F32)))

    def count_ge(x):
        def body(j, cnt):
            for kb in (2 * j, 2 * j + 1):
                ge = jnp.where(sc_ref[kb] >= x, 1.0, 0.0)
                cnt = cnt + jnp.sum(ge.reshape(KB // COUNT_ROWS, COUNT_ROWS, TQ), axis=0)
            return cnt
        cnt = lax.fori_loop(0, n_pairs, body, jnp.zeros((COUNT_ROWS, TQ), F32))
        return jnp.sum(cnt, axis=0, keepdims=True)

    n_causal = (i * TQ + 1 + lax.broadcasted_iota(jnp.int32, (1, TQ), 1)).astype(F32)
    want = jnp.minimum(n_causal, float(TOPK))
    hi0 = smax + jnp.maximum(jnp.abs(smax), 1e-30) * 1e-6
    open0 = (n_causal != want).astype(F32)

    def search_cond(state):
        it, _, _, _, _, flag = state
        return jnp.logical_and(it < MAX_SEARCH_ITERS, flag > 0.0)

    def search_body(state):
        it, lo, hi, c_lo, open_, _ = state
        mid = 0.5 * lo + 0.5 * hi
        c = count_ge(mid)
        live = jnp.logical_and(open_ > 0.0, jnp.logical_and(mid > lo, mid < hi))
        ge = c >= want
        go_lo = jnp.logical_and(live, ge)
        lo = jnp.where(go_lo, mid, lo)
        c_lo = jnp.where(go_lo, c, c_lo)
        hi = jnp.where(jnp.logical_and(live, jnp.logical_not(ge)), mid, hi)
        open_ = jnp.logical_and(live, c != want).astype(F32)
        return it + 1, lo, hi, c_lo, open_, jnp.max(open_)

    _, lo_f, _, c_lo_f, _, _ = lax.while_loop(
        search_cond, search_body,
        (jnp.int32(0), smin, hi0, n_causal, open0, jnp.max(open0)))
    thr_ref[...] = jnp.broadcast_to(lo_f, (SUBLANES, TQ))
    keep_ref[...] = jnp.full((SUBLANES, TQ), float(SEQ), F32)
    unresolved0 = (c_lo_f != want).astype(F32)

    @pl.when(jnp.max(unresolved0) > 0.0)
    def _():
        def next_value(lo, below):
            def body(kb, u):
                s = sc_ref[kb]
                cand = jnp.logical_and(s >= lo, s > below)
                return jnp.minimum(u, jnp.min(jnp.where(cand, s, jnp.inf), axis=0, keepdims=True))
            return lax.fori_loop(0, nkb, body, jnp.full((1, TQ), jnp.inf, F32))

        def count_gt(x):
            def body(kb, cnt):
                gt = jnp.where(sc_ref[kb] > x, 1.0, 0.0)
                return cnt + jnp.sum(gt, axis=0, keepdims=True)
            return lax.fori_loop(0, nkb, body, jnp.zeros((1, TQ), F32))

        def peel_cond(state):
            return state[-1] > 0.0

        def peel_body(state):
            below, unres, thr, keep, _ = state
            u = next_value(lo_f, below)
            c_gt = count_gt(u)
            hit = jnp.logical_and(unres > 0.0, c_gt < want)
            thr = jnp.where(hit, u, thr)
            keep = jnp.where(hit, want - c_gt, keep)
            unres = jnp.logical_and(unres > 0.0, jnp.logical_not(hit)).astype(F32)
            return u, unres, thr, keep, jnp.max(unres)

        _, _, thr_t, keep_t, _ = lax.while_loop(
            peel_cond, peel_body,
            (jnp.full((1, TQ), -jnp.inf, F32), unresolved0, lo_f,
             jnp.full((1, TQ), float(SEQ), F32), jnp.float32(1.0)))
        thr_ref[...] = jnp.broadcast_to(thr_t, (SUBLANES, TQ))
        keep_ref[...] = jnp.broadcast_to(keep_t, (SUBLANES, TQ))

        r_i = lax.broadcasted_iota(jnp.int32, (KB, KB), 0)
        c_i = lax.broadcasted_iota(jnp.int32, (KB, KB), 1)
        before = (c_i < r_i).astype(BF16)

        def drop_body(kb, seen):
            s = sc_ref[kb]
            eq = jnp.logical_and(s == thr_t, unresolved0 > 0.0)
            eq_f = eq.astype(F32)
            rank = seen + _dot(before, eq_f.astype(BF16))
            sc_ref[kb] = jnp.where(jnp.logical_and(eq, rank >= keep_t), -jnp.inf, s)
            return seen + jnp.sum(eq_f, axis=0, keepdims=True)

        lax.fori_loop(0, nkb, drop_body, jnp.zeros((1, TQ), F32))

    m_ref[...] = jnp.full(m_ref.shape, MASK_VALUE, F32)
    l_ref[...] = jnp.zeros(l_ref.shape, F32)
    acc_ref[...] = jnp.zeros(acc_ref.shape, F32)

    def logits_stage(j, slot):
        thr = thr_ref[0:1, :]
        bias0 = jnp.where(sc_ref[2 * j] >= thr, 0.0, MASK_VALUE)
        bias1 = jnp.where(sc_ref[2 * j + 1] >= thr, 0.0, MASK_VALUE)
        c_n = ckv_ref[pl.ds(pl.multiple_of(j * (2 * KB), 2 * KB), 2 * KB), :]
        for h in range(N_HEADS):
            lg = _dot(c_n, qabs_ref[h])
            lg_ref[slot, h, 0:KB, :] = lg[0:KB] + bias0
            lg_ref[slot, h, KB:2 * KB, :] = lg[KB:2 * KB] + bias1

    def softmax_stage(j, slot):
        c_t0 = ckvT_ref[2 * j]
        c_t1 = ckvT_ref[2 * j + 1]
        for h in range(N_HEADS):
            lg0 = lg_ref[slot, h, 0:KB, :]
            lg1 = lg_ref[slot, h, KB:2 * KB, :]
            m_old = m_ref[h, 0:1, :]
            m_new = jnp.maximum(m_old, jnp.max(jnp.maximum(lg0, lg1), axis=0, keepdims=True))
            alpha = jnp.exp2(m_old - m_new)
            p0 = jnp.exp2(lg0 - m_new)
            p1 = jnp.exp2(lg1 - m_new)
            l_new = alpha * l_ref[h, 0:1, :] + jnp.sum(p0 + p1, axis=0, keepdims=True)
            acc_ref[h] = (acc_ref[h] * alpha + _dot(c_t0, p0.astype(BF16))
                          + _dot(c_t1, p1.astype(BF16)))
            m_ref[h] = jnp.broadcast_to(m_new, (SUBLANES, TQ))
            l_ref[h] = jnp.broadcast_to(l_new, (SUBLANES, TQ))

    logits_stage(0, 0)

    def att_body(j, carry):
        for slot in range(2):
            @pl.when(j % 2 == slot)
            def _():
                logits_stage(j + 1, 1 - slot)
                softmax_stage(j, slot)
        return carry

    lax.fori_loop(0, n_pairs - 1, att_body, 0)
    for slot in range(2):
        @pl.when((n_pairs - 1) % 2 == slot)
        def _():
            softmax_stage(n_pairs - 1, slot)

    for h in range(N_HEADS):
        o_t = (acc_ref[h] / l_ref[h, 0:1, :]).astype(BF16)
        a_t = _dot(wuvT_ref[h], o_t)
        a_ref[:, h * HEAD_DIM:(h + 1) * HEAD_DIM] = a_t.T.astype(BF16)


def _attn(qT, qiT, wiT, kk2, ckv, ckvT, wukT, wuvT):
    col = lambda h: pl.BlockSpec((h, TQ), lambda i: (0, i))
    return pl.pallas_call(
        _attn_kernel,
        grid=(SEQ // TQ,),
        in_specs=[col(ATTN_WIDTH), col(IDX_HEADS * IDX_DIM), col(IDX_HEADS), _resident(kk2.shape),
                  _resident(ckv.shape), _resident(ckvT.shape), _resident(wukT.shape),
                  _resident(wuvT.shape)],
        out_specs=pl.BlockSpec((TQ, ATTN_WIDTH), lambda i: (i, 0)),
        out_shape=jax.ShapeDtypeStruct((SEQ, ATTN_WIDTH), BF16),
        scratch_shapes=[
            pltpu.VMEM((NKB, KB, TQ), F32),
            pltpu.VMEM((N_HEADS, KV_LATENT, TQ), BF16),
            pltpu.VMEM((SUBLANES, TQ), F32),
            pltpu.VMEM((SUBLANES, TQ), F32),
            pltpu.VMEM((N_HEADS, SUBLANES, TQ), F32),
            pltpu.VMEM((N_HEADS, SUBLANES, TQ), F32),
            pltpu.VMEM((N_HEADS, KV_LATENT, TQ), F32),
            pltpu.VMEM((2, N_HEADS, 2 * KB, TQ), F32),
        ],
        compiler_params=_params(),
        name="attn",
    )(qT, qiT, wiT, kk2, ckv, ckvT, wukT, wuvT)


def _merge_kernel(n1_ref, a_ref, m_ref, wga_ref, wgb_ref, wba_ref, wbg_ref, o_ref):
    n1 = n1_ref[...]
    br_a = jax.nn.sigmoid(_dot(n1, wga_ref[...])) * _dot(a_ref[...], wba_ref[...])
    br_b = jax.nn.sigmoid(_dot(n1, wgb_ref[...])) * _dot(m_ref[...], wbg_ref[...])
    o_ref[...] = (br_a + br_b).astype(BF16)


def _merge(n1, a, m, wga, wgb, wba, wbg):
    tm = ROW_TILE
    row = lambda w: pl.BlockSpec((tm, w), lambda i: (i, 0))
    return pl.pallas_call(
        _merge_kernel,
        grid=(SEQ // tm,),
        in_specs=[row(D_MODEL), row(ATTN_WIDTH), row(GMLP_WIDTH), _resident(wga.shape),
                  _resident(wgb.shape), _resident(wba.shape), _resident(wbg.shape)],
        out_specs=row(D_MODEL),
        out_shape=jax.ShapeDtypeStruct((SEQ, D_MODEL), BF16),
        compiler_params=_params(),
        name="merge",
    )(n1, a, m, wga, wgb, wba, wbg)


ROUTE_ROWS = 8


def _outproj_kernel(mg_ref, x_ref, wo_ref, g2_ref, wrh_ref, wrl_ref, br_ref,
                    h_ref, n2_ref, route_ref, counts_ref, carry_ref):
    i = pl.program_id(0)
    tm = ROW_TILE

    @pl.when(i == 0)
    def _():
        carry_ref[...] = jnp.zeros(carry_ref.shape, F32)

    h = x_ref[...] + _dot(mg_ref[...], wo_ref[...])
    h_ref[...] = h
    n2 = h * lax.rsqrt(jnp.mean(h * h, axis=-1, keepdims=True) + EPS) * g2_ref[...]
    n2_ref[...] = n2
    n2_hi = n2.astype(BF16)
    n2_lo = (n2 - n2_hi.astype(F32)).astype(BF16)
    logits = (_dot_nt(wrh_ref[...], n2_hi) + _dot_nt(wrh_ref[...], n2_lo)
              + _dot_nt(wrl_ref[...], n2_hi)) + br_ref[...]
    row = lax.broadcasted_iota(jnp.int32, (LANES, tm), 0).astype(F32)
    is_group = row < N_GROUPS
    gl = jnp.where(is_group, logits, -jnp.inf)
    gmax = jnp.max(gl, axis=0, keepdims=True)
    gsum = jnp.sum(jnp.where(is_group, jnp.exp(logits - gmax), 0.0), axis=0, keepdims=True)
    g_val = 1.0 / gsum
    g_idx = jnp.min(jnp.where(gl == gmax, row, float(LANES)), axis=0, keepdims=True)
    e_id = row - N_GROUPS
    in_group = jnp.logical_and(
        jnp.logical_and(e_id >= 0, e_id < N_EXPERTS),
        jnp.floor(e_id * (1.0 / EXPERTS_PER_GROUP)) == g_idx)
    sel = jnp.where(in_group, logits, -jnp.inf)
    v1 = jnp.max(sel, axis=0, keepdims=True)
    i1 = jnp.min(jnp.where(sel == v1, row, float(LANES)), axis=0, keepdims=True)
    sel2 = jnp.where(row == i1, -jnp.inf, sel)
    v2 = jnp.max(sel2, axis=0, keepdims=True)
    i2 = jnp.min(jnp.where(sel2 == v2, row, float(LANES)), axis=0, keepdims=True)
    x2 = jnp.exp(v2 - v1)
    den = 1.0 + x2
    w1 = g_val * (1.0 / den)
    w2 = g_val * (x2 / den)
    e1 = i1 - N_GROUPS
    e2 = i2 - N_GROUPS

    e_row = lax.broadcasted_iota(jnp.int32, (N_EXPERTS, tm), 0).astype(F32)
    hit1 = (e_row == e1).astype(F32)
    hit2 = (e_row == e2).astype(F32)
    hits = hit1 + hit2
    t_from = lax.broadcasted_iota(jnp.int32, (tm, tm), 0)
    t_to = lax.broadcasted_iota(jnp.int32, (tm, tm), 1)
    earlier = (t_from < t_to).astype(BF16)
    before = carry_ref[:, 0:1] + _dot(hits.astype(BF16), earlier)
    rank1 = jnp.sum(hit1 * before, axis=0, keepdims=True)
    rank2 = jnp.sum(hit2 * before, axis=0, keepdims=True)
    carry = carry_ref[...] + jnp.sum(hits, axis=1, keepdims=True)
    carry_ref[...] = carry
    counts_ref[...] = carry

    r = lax.broadcasted_iota(jnp.int32, (ROUTE_ROWS, tm), 0)
    route = jnp.where(r == 0, e1, 0.0)
    for k, v in enumerate((e2, w1, w2, rank1, rank2), start=1):
        route = jnp.where(r == k, v, route)
    route_ref[...] = route


def _outproj(mg, x2, wo, g2, wr_hi, wr_lo, br):
    tm = ROW_TILE
    row = lambda w: pl.BlockSpec((tm, w), lambda i: (i, 0))
    return pl.pallas_call(
        _outproj_kernel,
        grid=(SEQ // tm,),
        in_specs=[row(D_MODEL), row(D_MODEL), _resident(wo.shape), _resident(g2.shape),
                  _resident(wr_hi.shape), _resident(wr_lo.shape), _resident(br.shape)],
        out_specs=[row(D_MODEL), row(D_MODEL),
                   pl.BlockSpec((ROUTE_ROWS, tm), lambda i: (0, i)),
                   pl.BlockSpec((N_EXPERTS, LANES), lambda i: (0, 0))],
        out_shape=[jax.ShapeDtypeStruct((SEQ, D_MODEL), F32),
                   jax.ShapeDtypeStruct((SEQ, D_MODEL), F32),
                   jax.ShapeDtypeStruct((ROUTE_ROWS, SEQ), F32),
                   jax.ShapeDtypeStruct((N_EXPERTS, LANES), F32)],
        scratch_shapes=[pltpu.VMEM((N_EXPERTS, LANES), F32)],
        compiler_params=_params(),
        name="outproj",
    )(mg, x2, wo, g2, wr_hi, wr_lo, br)


def _start_row_gather(src_hbm, row_of, dst, slot, sem, rows):
    for r in rows:
        pltpu.make_async_copy(src_hbm.at[pl.ds(row_of(r), 1), :],
                              dst.at[slot, pl.ds(r, 1), :], sem.at[slot]).start()


def _wait_row_gather(src_hbm, dst, slot, sem, n_rows):
    pltpu.make_async_copy(src_hbm.at[pl.ds(0, n_rows), :], dst.at[slot], sem.at[slot]).wait()


def _scatter_kernel(padstart_ref, padlen_ref, pos_ref, x_ref, xs_hbm, xbuf, zbuf, sem, zsem):
    i = pl.program_id(0)
    n = pl.num_programs(0)
    tm = ROW_TILE
    slot = i % 2

    def row_copies_done(s):
        for _ in range(2):
            pltpu.make_async_copy(xbuf.at[s], xs_hbm.at[pl.ds(0, tm), :], sem.at[s]).wait()

    def pad_copies(e, bit):
        rows = 1 << bit
        start = padstart_ref[e] + (padlen_ref[e] & (rows - 1))
        if rows < SUBLANES:
            return [pltpu.make_async_copy(zbuf.at[pl.ds(0, 1), :], xs_hbm.at[pl.ds(start + k, 1), :],
                                          zsem.at[0]) for k in range(rows)]
        return [pltpu.make_async_copy(zbuf.at[pl.ds(0, rows), :],
                                      xs_hbm.at[pl.ds(pl.multiple_of(start, rows), rows), :],
                                      zsem.at[0])]

    def for_each_pad_piece(fn):
        def body(e, carry):
            for bit in range(PAD_BITS):
                @pl.when((lax.shift_right_logical(padlen_ref[e], bit) & 1) == 1)
                def _():
                    for c in pad_copies(e, bit):
                        fn(c)
            return carry
        lax.fori_loop(0, N_EXPERTS, body, 0)

    @pl.when(i == 0)
    def _():
        zbuf[...] = jnp.zeros(zbuf.shape, F32)
        for_each_pad_piece(lambda c: c.start())

    @pl.when(i >= 2)
    def _():
        row_copies_done(slot)

    xbuf[slot] = x_ref[...]
    for r in range(2 * tm):
        pltpu.make_async_copy(xbuf.at[slot, pl.ds(r % tm, 1), :],
                              xs_hbm.at[pl.ds(pos_ref[0, 0, r], 1), :], sem.at[slot]).start()

    @pl.when(i == n - 1)
    def _():
        row_copies_done(1 - slot)
        row_copies_done(slot)
        for_each_pad_piece(lambda c: c.wait())


def _scatter_rows(padstart, padlen, pos_tiles, n2):
    tm = ROW_TILE
    grid_spec = pltpu.PrefetchScalarGridSpec(
        num_scalar_prefetch=2,
        grid=(SEQ // tm,),
        in_specs=[pl.BlockSpec((1, 1, 2 * tm), lambda i, *_: (i, 0, 0), memory_space=pltpu.SMEM),
                  pl.BlockSpec((tm, D_MODEL), lambda i, *_: (i, 0))],
        out_specs=pl.BlockSpec(memory_space=pl.ANY),
        scratch_shapes=[pltpu.VMEM((2, tm, D_MODEL), F32),
                        pltpu.VMEM((1 << (PAD_BITS - 1), D_MODEL), F32),
                        pltpu.SemaphoreType.DMA((2,)),
                        pltpu.SemaphoreType.DMA((1,))],
    )
    return pl.pallas_call(
        _scatter_kernel,
        grid_spec=grid_spec,
        out_shape=jax.ShapeDtypeStruct((N_EXPERT_TILES * EXPERT_TILE, D_MODEL), F32),
        compiler_params=_params(),
        name="scatter_rows",
    )(padstart, padlen, pos_tiles, n2)


def _expert_kernel(ntl_ref, tst_ref, ntot_ref,
                   xs_hbm, wg_hbm, wu_hbm, wd_hbm,
                   y_hbm,
                   xbuf, ybuf, xsem, ysem, wg_st, wu_st, wd_st, wsem, wgb, wub, wdb):
    e = pl.program_id(0)
    ntot = ntot_ref[0]

    def weight_copies(ex, slot):
        return [pltpu.make_async_copy(src.at[ex], dst.at[slot], wsem.at[slot, j])
                for j, (src, dst) in enumerate(((wg_hbm, wg_st), (wu_hbm, wu_st), (wd_hbm, wd_st)))]

    def tile_rows(t):
        return pl.ds(pl.multiple_of(t * EXPERT_TILE, EXPERT_TILE), EXPERT_TILE)

    def x_copy(t, slot):
        return pltpu.make_async_copy(xs_hbm.at[tile_rows(t), :], xbuf.at[slot], xsem.at[slot])

    def y_copy(t, slot):
        return pltpu.make_async_copy(ybuf.at[slot], y_hbm.at[tile_rows(t), :], ysem.at[slot])

    @pl.when(e == 0)
    def _():
        for c in weight_copies(0, 0):
            c.start()
        x_copy(0, 0).start()

    @pl.when(e + 1 < pl.num_programs(0))
    def _():
        for c in weight_copies(e + 1, (e + 1) % 2):
            c.start()

    n_here = ntl_ref[e]
    wslot = e % 2
    for c in weight_copies(e, wslot):
        c.wait()

    @pl.when(n_here > 0)
    def _():
        wgb[...] = wg_st[wslot].astype(BF16)
        wub[...] = wu_st[wslot].astype(BF16)
        wdb[...] = wd_st[wslot].astype(BF16)

        def tile_body(k, carry):
            t = tst_ref[e] + k
            slot = t % 2
            x_copy(t, slot).wait()

            @pl.when(t + 1 < ntot)
            def _():
                x_copy(t + 1, 1 - slot).start()

            xt = xbuf[slot].astype(BF16)
            hid = jax.nn.silu(_dot(xt, wgb[...])) * _dot(xt, wub[...])
            y = _dot(hid.astype(BF16), wdb[...])

            @pl.when(t >= 2)
            def _():
                y_copy(t - 2, slot).wait()

            ybuf[slot] = y
            y_copy(t, slot).start()
            return carry
        lax.fori_loop(0, n_here, tile_body, 0)

    @pl.when(e == pl.num_programs(0) - 1)
    def _():
        @pl.when(ntot >= 2)
        def _():
            y_copy(ntot - 2, (ntot - 2) % 2).wait()
        y_copy(ntot - 1, (ntot - 1) % 2).wait()


def _experts(ntl, tst, ntot, x_sorted, w_gate, w_up, w_down):
    hbm = pl.BlockSpec(memory_space=pl.ANY)
    grid_spec = pltpu.PrefetchScalarGridSpec(
        num_scalar_prefetch=3,
        grid=(N_EXPERTS,),
        in_specs=[hbm, hbm, hbm, hbm],
        out_specs=hbm,
        scratch_shapes=[
            pltpu.VMEM((2, EXPERT_TILE, D_MODEL), F32),
            pltpu.VMEM((2, EXPERT_TILE, D_MODEL), F32),
            pltpu.SemaphoreType.DMA((2,)),
            pltpu.SemaphoreType.DMA((2,)),
            pltpu.VMEM((2, D_MODEL, EXPERT_FF), F32),
            pltpu.VMEM((2, D_MODEL, EXPERT_FF), F32),
            pltpu.VMEM((2, EXPERT_FF, D_MODEL), F32),
            pltpu.SemaphoreType.DMA((2, 3)),
            pltpu.VMEM((D_MODEL, EXPERT_FF), BF16),
            pltpu.VMEM((D_MODEL, EXPERT_FF), BF16),
            pltpu.VMEM((EXPERT_FF, D_MODEL), BF16),
        ],
    )
    return pl.pallas_call(
        _expert_kernel,
        grid_spec=grid_spec,
        out_shape=jax.ShapeDtypeStruct((N_EXPERT_TILES * EXPERT_TILE, D_MODEL), F32),
        compiler_params=_params(),
        name="experts",
    )(ntl, tst, ntot, x_sorted, w_gate, w_up, w_down)


def _final_kernel(pos_ref, posn_ref, h_ref, route_ref, gf_ref, y_hbm, o_ref, ybuf, sem):
    i = pl.program_id(0)
    n = pl.num_programs(0)
    tm = ROW_TILE

    @pl.when(i == 0)
    def _():
        _start_row_gather(y_hbm, lambda r: pos_ref[0, 0, r], ybuf, 0, sem, range(2 * tm))

    @pl.when(i + 1 < n)
    def _():
        _start_row_gather(y_hbm, lambda r: posn_ref[0, 0, r], ybuf, (i + 1) % 2, sem,
                          range(2 * tm))

    slot = i % 2
    _wait_row_gather(y_hbm, ybuf, slot, sem, 2 * tm)
    w1 = route_ref[:, 0:1]
    w2 = route_ref[:, 1:2]
    h2 = h_ref[...] + (w1 * ybuf[slot, 0:tm, :] + w2 * ybuf[slot, tm:2 * tm, :])
    o_ref[...] = h2 * lax.rsqrt(jnp.mean(h2 * h2, axis=-1, keepdims=True) + EPS) * gf_ref[...]


def _final(pos, h, route, gf, y_sorted):
    tm = ROW_TILE
    n = SEQ // tm
    row = lambda w: pl.BlockSpec((tm, w), lambda i: (i, 0))
    smem_blk = lambda f: pl.BlockSpec((1, 1, 2 * tm), f, memory_space=pltpu.SMEM)
    return pl.pallas_call(
        _final_kernel,
        grid=(n,),
        in_specs=[smem_blk(lambda i: (i, 0, 0)),
                  smem_blk(lambda i: (jnp.minimum(i + 1, n - 1), 0, 0)),
                  row(D_MODEL), row(2), _resident(gf.shape),
                  pl.BlockSpec(memory_space=pl.ANY)],
        out_specs=row(D_MODEL),
        out_shape=jax.ShapeDtypeStruct((SEQ, D_MODEL), F32),
        scratch_shapes=[pltpu.VMEM((2, 2 * tm, D_MODEL), F32), pltpu.SemaphoreType.DMA((2,))],
        compiler_params=_params(),
        name="final",
    )(pos, pos, h, route, gf, y_sorted)


def _dispatch(route, counts):
    i32 = jnp.int32
    e_pair = route[0:2].T.astype(i32).reshape(-1)
    rank = route[4:6].T.astype(i32).reshape(-1)
    cnt = counts[:, 0].astype(i32)
    ntl = (cnt + EXPERT_TILE - 1) // EXPERT_TILE
    tend = jnp.cumsum(ntl)
    tst = tend - ntl
    experts = jnp.arange(N_EXPERTS, dtype=i32)
    of_pair = e_pair[:, None] == experts[None, :]
    pos = jnp.sum(jnp.where(of_pair, (tst * EXPERT_TILE)[None, :], 0), axis=1) + rank
    padstart = tst * EXPERT_TILE + cnt
    padlen = ntl * EXPERT_TILE - cnt
    return ntl, tst, tend[-1:], padstart, padlen, pos


def kernel(x, norm1_g, w_in, kv_norm_g, w_uk, w_uv, gmlp_ws, gmlp_bs, ln_v_g, ln_v_b, w_br_attn,
           w_br_gmlp, w_out, norm2_g, w_group, b_group, w_router, b_router, w_e_gate, w_e_up,
           w_e_down, norm_f_g):
    assert x.shape == (1, SEQ, D_MODEL)
    x2 = x.reshape(SEQ, D_MODEL)
    row_vec = lambda v: v.reshape(1, -1).astype(F32)

    c_q = ATTN_WIDTH
    c_kv = c_q + KV_LATENT
    c_qi = c_kv + IDX_HEADS * IDX_DIM
    c_k = c_qi + IDX_DIM
    c_w = c_k + IDX_HEADS
    c_uv = c_w + 2 * GMLP_WIDTH
    w_bf = w_in.astype(BF16)
    wqT, wc, wqiT = w_bf[:, :c_q].T, w_bf[:, c_q:c_kv], w_bf[:, c_kv:c_qi].T
    wk, wwT = w_bf[:, c_qi:c_k], w_bf[:, c_k:c_w].T
    wuv_in, wga, wgb = w_bf[:, c_w:c_uv], w_bf[:, c_uv:c_uv + D_MODEL], w_bf[:, c_uv + D_MODEL:]
    zk = jnp.zeros_like(wk)
    wkk = jnp.concatenate([wk, zk, zk, wk], axis=1)

    n1, qT, ckv, ckvT, qiT, kk2, wiT = _proj(x2, row_vec(norm1_g), wqT, wc, wqiT, wkk, wwT,
                                             row_vec(kv_norm_g))
    m = _gmlp(n1, wuv_in, gmlp_ws, jnp.pad(gmlp_bs.T, ((0, 0), (0, LANES - GMLP_GROUPS))),
              row_vec(ln_v_g), row_vec(ln_v_b))
    a = _attn(qT, qiT, wiT, kk2, ckv, ckvT,
              jnp.swapaxes(w_uk, 1, 2).astype(BF16), jnp.swapaxes(w_uv, 1, 2).astype(BF16))
    mg = _merge(n1, a, m, wga, wgb, w_br_attn.astype(BF16), w_br_gmlp.astype(BF16))

    w_route = jnp.pad(jnp.concatenate([w_group, w_router], axis=1),
                      ((0, 0), (0, LANES - N_GROUPS - N_EXPERTS)))
    b_route = jnp.pad(jnp.concatenate([b_group, b_router]), (0, LANES - N_GROUPS - N_EXPERTS))
    w_route_t = w_route.T
    wr_hi = w_route_t.astype(BF16)
    wr_lo = (w_route_t - wr_hi.astype(F32)).astype(BF16)
    h, n2, route, counts = _outproj(mg, x2, w_out.astype(BF16), row_vec(norm2_g), wr_hi, wr_lo,
                                    b_route.reshape(LANES, 1))

    ntl, tst, ntot, padstart, padlen, pos = _dispatch(route, counts)
    pos_tiles = pos.reshape(SEQ // ROW_TILE, ROW_TILE, 2).transpose(0, 2, 1).reshape(
        SEQ // ROW_TILE, 1, 2 * ROW_TILE)
    x_sorted = _scatter_rows(padstart, padlen, pos_tiles, n2)
    y_sorted = _experts(ntl, tst, ntot, x_sorted, w_e_gate, w_e_up, w_e_down)
    out = _final(pos_tiles, h, route[2:4].T, row_vec(norm_f_g), y_sorted)
    return out.reshape(1, SEQ, D_MODEL)
```

```python
import jax
import jax.numpy as jnp
import numpy as np
from jax import lax
from jax.experimental import pallas as pl
from jax.experimental.pallas import tpu as pltpu

F32 = jnp.float32
BF16 = jnp.bfloat16

D_MODEL = 2048
SEQ = 8192
N_HEADS = 8
HEAD_DIM = 128
KV_LATENT = 256
IDX_HEADS = 16
IDX_DIM = 64
TOPK = 256
ATTN_WIDTH = N_HEADS * HEAD_DIM
GMLP_GROUPS = 8
GMLP_WIDTH = 1024
CHUNK = 128
N_GROUPS = 8
EXPERTS_PER_GROUP = 8
N_EXPERTS = 64
EXPERT_FF = 512
EPS = 1e-6

LANES = 128
SUBLANES = 8
VMEM_LIMIT = 60 * 1024 * 1024
MASK_VALUE = -0.7 * float(np.finfo(np.float32).max)
LOG2_E = float(np.log2(np.e))

ROW_TILE = 256
TQ = 256
KB = 256
NKB = SEQ // KB
COUNT_ROWS = 32
MAX_SEARCH_ITERS = 64
MAX_BRACKET = 4
EXPERT_TILE = 256
N_EXPERT_TILES = (2 * SEQ) // EXPERT_TILE + N_EXPERTS
PAD_BITS = 8


def _dot(a, b):
    return jnp.dot(a, b, preferred_element_type=F32)


def _dot_nt(a, b):
    return lax.dot_general(a, b, (((1,), (1,)), ((), ())), preferred_element_type=F32)


def _resident(shape):
    zeros = (0,) * len(shape)
    return pl.BlockSpec(shape, lambda *_: zeros, pipeline_mode=pl.Buffered(1))


def _params(n_axes=1):
    return pltpu.CompilerParams(
        dimension_semantics=("arbitrary",) * n_axes, vmem_limit_bytes=VMEM_LIMIT)


def _proj_kernel(x_ref, g1_ref, wqT_ref, wc_ref, wqiT_ref, wkk_ref, wwT_ref, kvg_ref,
                 n1_ref, qT_ref, ckv_ref, ckvT_ref, qiT_ref, kk2_ref, wiT_ref):
    x = x_ref[...]
    ms = jnp.mean(x * x, axis=-1, keepdims=True)
    n1 = (x * lax.rsqrt(ms + EPS) * g1_ref[...]).astype(BF16)
    n1_ref[...] = n1
    qT_ref[...] = _dot_nt(wqT_ref[...], n1).astype(BF16)
    qiT_ref[...] = (_dot_nt(wqiT_ref[...], n1) * (IDX_DIM ** -0.5)).astype(BF16)
    wiT_ref[...] = _dot_nt(wwT_ref[...], n1) * (IDX_HEADS ** -0.5)
    c = _dot(n1, wc_ref[...])
    c = c * lax.rsqrt(jnp.mean(c * c, axis=-1, keepdims=True) + EPS) * kvg_ref[...]
    ckv_ref[...] = c.astype(BF16)
    ckvT_ref[0] = c.T.astype(BF16)
    kk = _dot(n1, wkk_ref[...]).astype(BF16)
    kk2_ref[0, 0:KB, :] = kk[:, 0:LANES]
    kk2_ref[0, KB:2 * KB, :] = kk[:, LANES:2 * LANES]


def _proj(x2, g1, wqT, wc, wqiT, wkk, wwT, kvg):
    tm = KB
    row = lambda w: pl.BlockSpec((tm, w), lambda i: (i, 0))
    col = lambda h: pl.BlockSpec((h, tm), lambda i: (0, i))
    return pl.pallas_call(
        _proj_kernel,
        grid=(SEQ // tm,),
        in_specs=[row(D_MODEL), _resident(g1.shape), _resident(wqT.shape), _resident(wc.shape),
                  _resident(wqiT.shape), _resident(wkk.shape), _resident(wwT.shape),
                  _resident(kvg.shape)],
        out_specs=[row(D_MODEL), col(ATTN_WIDTH), row(KV_LATENT),
                   pl.BlockSpec((1, KV_LATENT, tm), lambda i: (i, 0, 0)),
                   col(IDX_HEADS * IDX_DIM),
                   pl.BlockSpec((1, 2 * tm, LANES), lambda i: (i, 0, 0)),
                   col(IDX_HEADS)],
        out_shape=[
            jax.ShapeDtypeStruct((SEQ, D_MODEL), BF16),
            jax.ShapeDtypeStruct((ATTN_WIDTH, SEQ), BF16),
            jax.ShapeDtypeStruct((SEQ, KV_LATENT), BF16),
            jax.ShapeDtypeStruct((NKB, KV_LATENT, KB), BF16),
            jax.ShapeDtypeStruct((IDX_HEADS * IDX_DIM, SEQ), BF16),
            jax.ShapeDtypeStruct((NKB, 2 * KB, LANES), BF16),
            jax.ShapeDtypeStruct((IDX_HEADS, SEQ), F32),
        ],
        compiler_params=_params(),
        name="proj",
    )(x2, g1, wqT, wc, wqiT, wkk, wwT, kvg)


def _gmlp_kernel(n1_ref, wuv_ref, ws_ref, bsT_ref, lng_ref, lnb_ref, m_ref):
    uv = _dot(n1_ref[...], wuv_ref[...])
    z = jax.nn.gelu(uv)
    u = z[:, :GMLP_WIDTH]
    v = z[:, GMLP_WIDTH:]
    mu = jnp.mean(v, axis=-1, keepdims=True)
    var = jnp.mean(jnp.square(v - mu), axis=-1, keepdims=True)
    vn = ((v - mu) * lax.rsqrt(var + EPS) * lng_ref[...] + lnb_ref[...]).astype(BF16)
    t_pos = lax.broadcasted_iota(jnp.int32, (CHUNK, CHUNK), 0)
    s_pos = lax.broadcasted_iota(jnp.int32, (CHUNK, CHUNK), 1)
    causal = s_pos <= t_pos
    for g in range(GMLP_GROUPS):
        wm = jnp.where(causal, ws_ref[g], 0.0).astype(BF16)
        bias = bsT_ref[:, g:g + 1]
        cols = slice(g * LANES, (g + 1) * LANES)
        for c in range(ROW_TILE // CHUNK):
            rows = slice(c * CHUNK, (c + 1) * CHUNK)
            y = _dot(wm, vn[rows, cols]) + bias
            m_ref[rows, cols] = (u[rows, cols] * y).astype(BF16)


def _gmlp(n1, wuv, ws, bsT, lng, lnb):
    tm = ROW_TILE
    return pl.pallas_call(
        _gmlp_kernel,
        grid=(SEQ // tm,),
        in_specs=[pl.BlockSpec((tm, D_MODEL), lambda i: (i, 0)), _resident(wuv.shape),
                  _resident(ws.shape), _resident(bsT.shape), _resident(lng.shape),
                  _resident(lnb.shape)],
        out_specs=pl.BlockSpec((tm, GMLP_WIDTH), lambda i: (i, 0)),
        out_shape=jax.ShapeDtypeStruct((SEQ, GMLP_WIDTH), BF16),
        compiler_params=_params(),
        name="gmlp",
    )(n1, wuv, ws, bsT, lng, lnb)


def _attn_kernel(qT_ref, qiT_ref, wiT_ref, kk2_ref, ckv_ref, ckvT_ref, wukT_ref, wuvT_ref,
                 a_ref,
                 sc_ref, qabs_ref, thr_ref, keep_ref, m_ref, l_ref, acc_ref, lg_ref):
    i = pl.program_id(0)
    nkb = i + 1

    for h in range(N_HEADS):
        qa = _dot(wukT_ref[h], qT_ref[h * HEAD_DIM:(h + 1) * HEAD_DIM, :]) * (
            HEAD_DIM ** -0.5 * LOG2_E)
        qabs_ref[h] = qa.astype(BF16)

    q_pos = i * TQ + lax.broadcasted_iota(jnp.int32, (KB, TQ), 1)
    k_off = lax.broadcasted_iota(jnp.int32, (KB, TQ), 0)

    def score_body(kb, carry):
        smax, smin = carry
        keys = kk2_ref[kb]
        acc = jnp.zeros((KB, TQ), F32)
        for j in range(IDX_HEADS // 2):
            d = _dot(keys, qiT_ref[j * LANES:(j + 1) * LANES, :])
            acc = acc + jnp.maximum(d[0:KB], 0.0) * wiT_ref[2 * j:2 * j + 1, :]
            acc = acc + jnp.maximum(d[KB:2 * KB], 0.0) * wiT_ref[2 * j + 1:2 * j + 2, :]
        causal = (kb * KB + k_off) <= q_pos
        sc_ref[kb] = jnp.where(causal, acc, -jnp.inf)
        smax = jnp.maximum(smax, jnp.max(jnp.where(causal, acc, -jnp.inf), axis=0, keepdims=True))
        smin = jnp.minimum(smin, jnp.min(jnp.where(causal, acc, jnp.inf), axis=0, keepdims=True))
        return smax, smin

    n_pairs = (nkb + 1) // 2
    smax, smin = lax.fori_loop(
        0, n_pairs, lambda j, c: score_body(2 * j + 1, score_body(2 * j, c)),
        (jnp.full((1, TQ), -jnp.inf, F32), jnp.full((1, TQ), jnp.inf, F32)))

    def count_ge(x):
        def body(j, cnt):
            for kb in (2 * j, 2 * j + 1):
                ge = jnp.where(sc_ref[kb] >= x, 1.0, 0.0)
                cnt = cnt + jnp.sum(ge.reshape(KB // COUNT_ROWS, COUNT_ROWS, TQ), axis=0)
            return cnt
        cnt = lax.fori_loop(0, n_pairs, body, jnp.zeros((COUNT_ROWS, TQ), F32))
        return jnp.sum(cnt, axis=0, keepdims=True)

    def max_below(x):
        def body(j, best):
            for kb in (2 * j, 2 * j + 1):
                s = sc_ref[kb]
                v = jnp.where(s < x, s, -jnp.inf)
                best = jnp.maximum(best, jnp.max(v.reshape(KB // COUNT_ROWS, COUNT_ROWS, TQ), axis=0))
            return best
        best = lax.fori_loop(0, n_pairs, body, jnp.full((COUNT_ROWS, TQ), -jnp.inf, F32))
        return jnp.max(best, axis=0, keepdims=True)

    n_causal = (i * TQ + 1 + lax.broadcasted_iota(jnp.int32, (1, TQ), 1)).astype(F32)
    want = jnp.minimum(n_causal, float(TOPK))
    hi0 = smax + jnp.maximum(jnp.abs(smax), 1e-30) * 1e-6
    open0 = (n_causal != want).astype(F32)

    def search_cond(state):
        return jnp.logical_and(state[0] < MAX_SEARCH_ITERS, state[-1] > 0.0)

    def search_body(state):
        it, lo, hi, c_lo, c_hi, open_, _ = state
        mid = 0.5 * lo + 0.5 * hi
        c = count_ge(mid)
        live = jnp.logical_and(open_ > 0.0, jnp.logical_and(mid > lo, mid < hi))
        ge = c >= want
        go_lo = jnp.logical_and(live, ge)
        go_hi = jnp.logical_and(live, jnp.logical_not(ge))
        lo = jnp.where(go_lo, mid, lo)
        c_lo = jnp.where(go_lo, c, c_lo)
        hi = jnp.where(go_hi, mid, hi)
        c_hi = jnp.where(go_hi, c, c_hi)
        wide = jnp.logical_and(c_lo != want, c_lo - c_hi > MAX_BRACKET)
        open_ = jnp.logical_and(live, wide).astype(F32)
        return it + 1, lo, hi, c_lo, c_hi, open_, jnp.max(open_)

    _, lo_f, hi_f, c_lo_f, c_hi_f, _, _ = lax.while_loop(
        search_cond, search_body,
        (jnp.int32(0), smin, hi0, n_causal, jnp.zeros((1, TQ), F32), open0, jnp.max(open0)))

    stepping = c_lo_f != want

    def step_cond(state):
        return jnp.logical_and(state[0] < MAX_BRACKET, state[-1] > 0.0)

    def step_body(state):
        it, cur, c, _ = state
        act = jnp.logical_and(stepping, c < want)
        cur = jnp.where(act, max_below(cur), cur)
        c = jnp.where(act, c + 1.0, c)
        return it + 1, cur, c, jnp.max(jnp.logical_and(stepping, c < want).astype(F32))

    _, cur_f, _, _ = lax.while_loop(
        step_cond, step_body,
        (jnp.int32(0), hi_f, c_hi_f, jnp.max(stepping.astype(F32))))
    thr0 = jnp.where(stepping, cur_f, lo_f)
    thr_ref[...] = jnp.broadcast_to(thr0, (SUBLANES, TQ))
    keep_ref[...] = jnp.full((SUBLANES, TQ), float(SEQ), F32)
    unresolved0 = (count_ge(thr0) != want).astype(F32)

    @pl.when(jnp.max(unresolved0) > 0.0)
    def _():
        def next_value(lo, below):
            def body(kb, u):
                s = sc_ref[kb]
                cand = jnp.logical_and(s >= lo, s > below)
                return jnp.minimum(u, jnp.min(jnp.where(cand, s, jnp.inf), axis=0, keepdims=True))
            return lax.fori_loop(0, nkb, body, jnp.full((1, TQ), jnp.inf, F32))

        def count_gt(x):
            def body(kb, cnt):
                gt = jnp.where(sc_ref[kb] > x, 1.0, 0.0)
                return cnt + jnp.sum(gt, axis=0, keepdims=True)
            return lax.fori_loop(0, nkb, body, jnp.zeros((1, TQ), F32))

        def peel_cond(state):
            return state[-1] > 0.0

        def peel_body(state):
            below, unres, thr, keep, _ = state
            u = next_value(lo_f, below)
            c_gt = count_gt(u)
            hit = jnp.logical_and(unres > 0.0, c_gt < want)
            thr = jnp.where(hit, u, thr)
            keep = jnp.where(hit, want - c_gt, keep)
            unres = jnp.logical_and(unres > 0.0, jnp.logical_not(hit)).astype(F32)
            return u, unres, thr, keep, jnp.max(unres)

        _, _, thr_t, keep_t, _ = lax.while_loop(
            peel_cond, peel_body,
            (jnp.full((1, TQ), -jnp.inf, F32), unresolved0, thr0,
             jnp.full((1, TQ), float(SEQ), F32), jnp.float32(1.0)))
        thr_ref[...] = jnp.broadcast_to(thr_t, (SUBLANES, TQ))
        keep_ref[...] = jnp.broadcast_to(keep_t, (SUBLANES, TQ))

        r_i = lax.broadcasted_iota(jnp.int32, (KB, KB), 0)
        c_i = lax.broadcasted_iota(jnp.int32, (KB, KB), 1)
        before = (c_i < r_i).astype(BF16)

        def drop_body(kb, seen):
            s = sc_ref[kb]
            eq = jnp.logical_and(s == thr_t, unresolved0 > 0.0)
            eq_f = eq.astype(F32)
            rank = seen + _dot(before, eq_f.astype(BF16))
            sc_ref[kb] = jnp.where(jnp.logical_and(eq, rank >= keep_t), -jnp.inf, s)
            return seen + jnp.sum(eq_f, axis=0, keepdims=True)

        lax.fori_loop(0, nkb, drop_body, jnp.zeros((1, TQ), F32))

    m_ref[...] = jnp.full(m_ref.shape, MASK_VALUE, F32)
    l_ref[...] = jnp.zeros(l_ref.shape, F32)
    acc_ref[...] = jnp.zeros(acc_ref.shape, F32)

    def logits_stage(j, slot):
        thr = thr_ref[0:1, :]
        bias0 = jnp.where(sc_ref[2 * j] >= thr, 0.0, MASK_VALUE)
        bias1 = jnp.where(sc_ref[2 * j + 1] >= thr, 0.0, MASK_VALUE)
        c_n = ckv_ref[pl.ds(pl.multiple_of(j * (2 * KB), 2 * KB), 2 * KB), :]
        for h in range(N_HEADS):
            lg = _dot(c_n, qabs_ref[h])
            lg_ref[slot, h, 0:KB, :] = lg[0:KB] + bias0
            lg_ref[slot, h, KB:2 * KB, :] = lg[KB:2 * KB] + bias1

    def softmax_stage(j, slot):
        c_t0 = ckvT_ref[2 * j]
        c_t1 = ckvT_ref[2 * j + 1]
        for h in range(N_HEADS):
            lg0 = lg_ref[slot, h, 0:KB, :]
            lg1 = lg_ref[slot, h, KB:2 * KB, :]
            m_old = m_ref[h, 0:1, :]
            m_new = jnp.maximum(m_old, jnp.max(jnp.maximum(lg0, lg1), axis=0, keepdims=True))
            alpha = jnp.exp2(m_old - m_new)
            p0 = jnp.exp2(lg0 - m_new)
            p1 = jnp.exp2(lg1 - m_new)
            l_new = alpha * l_ref[h, 0:1, :] + jnp.sum(p0 + p1, axis=0, keepdims=True)
            acc_ref[h] = (acc_ref[h] * alpha + _dot(c_t0, p0.astype(BF16))
                          + _dot(c_t1, p1.astype(BF16)))
            m_ref[h] = jnp.broadcast_to(m_new, (SUBLANES, TQ))
            l_ref[h] = jnp.broadcast_to(l_new, (SUBLANES, TQ))

    logits_stage(0, 0)

    def att_body(j, carry):
        for slot in range(2):
            @pl.when(j % 2 == slot)
            def _():
                logits_stage(j + 1, 1 - slot)
                softmax_stage(j, slot)
        return carry

    lax.fori_loop(0, n_pairs - 1, att_body, 0)
    for slot in range(2):
        @pl.when((n_pairs - 1) % 2 == slot)
        def _():
            softmax_stage(n_pairs - 1, slot)

    for h in range(N_HEADS):
        o_t = (acc_ref[h] / l_ref[h, 0:1, :]).astype(BF16)
        a_t = _dot(wuvT_ref[h], o_t)
        a_ref[:, h * HEAD_DIM:(h + 1) * HEAD_DIM] = a_t.T.astype(BF16)


def _attn(qT, qiT, wiT, kk2, ckv, ckvT, wukT, wuvT):
    col = lambda h: pl.BlockSpec((h, TQ), lambda i: (0, i))
    return pl.pallas_call(
        _attn_kernel,
        grid=(SEQ // TQ,),
        in_specs=[col(ATTN_WIDTH), col(IDX_HEADS * IDX_DIM), col(IDX_HEADS), _resident(kk2.shape),
                  _resident(ckv.shape), _resident(ckvT.shape), _resident(wukT.shape),
                  _resident(wuvT.shape)],
        out_specs=pl.BlockSpec((TQ, ATTN_WIDTH), lambda i: (i, 0)),
        out_shape=jax.ShapeDtypeStruct((SEQ, ATTN_WIDTH), BF16),
        scratch_shapes=[
            pltpu.VMEM((NKB, KB, TQ), F32),
            pltpu.VMEM((N_HEADS, KV_LATENT, TQ), BF16),
            pltpu.VMEM((SUBLANES, TQ), F32),
            pltpu.VMEM((SUBLANES, TQ), F32),
            pltpu.VMEM((N_HEADS, SUBLANES, TQ), F32),
            pltpu.VMEM((N_HEADS, SUBLANES, TQ), F32),
            pltpu.VMEM((N_HEADS, KV_LATENT, TQ), F32),
            pltpu.VMEM((2, N_HEADS, 2 * KB, TQ), F32),
        ],
        compiler_params=_params(),
        name="attn",
    )(qT, qiT, wiT, kk2, ckv, ckvT, wukT, wuvT)


def _merge_kernel(n1_ref, a_ref, m_ref, wga_ref, wgb_ref, wba_ref, wbg_ref, o_ref):
    n1 = n1_ref[...]
    br_a = jax.nn.sigmoid(_dot(n1, wga_ref[...])) * _dot(a_ref[...], wba_ref[...])
    br_b = jax.nn.sigmoid(_dot(n1, wgb_ref[...])) * _dot(m_ref[...], wbg_ref[...])
    o_ref[...] = (br_a + br_b).astype(BF16)


def _merge(n1, a, m, wga, wgb, wba, wbg):
    tm = ROW_TILE
    row = lambda w: pl.BlockSpec((tm, w), lambda i: (i, 0))
    return pl.pallas_call(
        _merge_kernel,
        grid=(SEQ // tm,),
        in_specs=[row(D_MODEL), row(ATTN_WIDTH), row(GMLP_WIDTH), _resident(wga.shape),
                  _resident(wgb.shape), _resident(wba.shape), _resident(wbg.shape)],
        out_specs=row(D_MODEL),
        out_shape=jax.ShapeDtypeStruct((SEQ, D_MODEL), BF16),
        compiler_params=_params(),
        name="merge",
    )(n1, a, m, wga, wgb, wba, wbg)


ROUTE_ROWS = 8


def _outproj_kernel(mg_ref, x_ref, wo_ref, g2_ref, wrh_ref, wrl_ref, br_ref,
                    h_ref, n2_ref, route_ref, counts_ref, carry_ref):
    i = pl.program_id(0)
    tm = ROW_TILE

    @pl.when(i == 0)
    def _():
        carry_ref[...] = jnp.zeros(carry_ref.shape, F32)

    h = x_ref[...] + _dot(mg_ref[...], wo_ref[...])
    h_ref[...] = h
    n2 = h * lax.rsqrt(jnp.mean(h * h, axis=-1, keepdims=True) + EPS) * g2_ref[...]
    n2_ref[...] = n2
    n2_hi = n2.astype(BF16)
    n2_lo = (n2 - n2_hi.astype(F32)).astype(BF16)
    logits = (_dot_nt(wrh_ref[...], n2_hi) + _dot_nt(wrh_ref[...], n2_lo)
              + _dot_nt(wrl_ref[...], n2_hi)) + br_ref[...]
    row = lax.broadcasted_iota(jnp.int32, (LANES, tm), 0).astype(F32)
    is_group = row < N_GROUPS
    gl = jnp.where(is_group, logits, -jnp.inf)
    gmax = jnp.max(gl, axis=0, keepdims=True)
    gsum = jnp.sum(jnp.where(is_group, jnp.exp(logits - gmax), 0.0), axis=0, keepdims=True)
    g_val = 1.0 / gsum
    g_idx = jnp.min(jnp.where(gl == gmax, row, float(LANES)), axis=0, keepdims=True)
    e_id = row - N_GROUPS
    in_group = jnp.logical_and(
        jnp.logical_and(e_id >= 0, e_id < N_EXPERTS),
        jnp.floor(e_id * (1.0 / EXPERTS_PER_GROUP)) == g_idx)
    sel = jnp.where(in_group, logits, -jnp.inf)
    v1 = jnp.max(sel, axis=0, keepdims=True)
    i1 = jnp.min(jnp.where(sel == v1, row, float(LANES)), axis=0, keepdims=True)
    sel2 = jnp.where(row == i1, -jnp.inf, sel)
    v2 = jnp.max(sel2, axis=0, keepdims=True)
    i2 = jnp.min(jnp.where(sel2 == v2, row, float(LANES)), axis=0, keepdims=True)
    x2 = jnp.exp(v2 - v1)
    den = 1.0 + x2
    w1 = g_val * (1.0 / den)
    w2 = g_val * (x2 / den)
    e1 = i1 - N_GROUPS
    e2 = i2 - N_GROUPS

    e_row = lax.broadcasted_iota(jnp.int32, (N_EXPERTS, tm), 0).astype(F32)
    hit1 = (e_row == e1).astype(F32)
    hit2 = (e_row == e2).astype(F32)
    hits = hit1 + hit2
    t_from = lax.broadcasted_iota(jnp.int32, (tm, tm), 0)
    t_to = lax.broadcasted_iota(jnp.int32, (tm, tm), 1)
    earlier = (t_from < t_to).astype(BF16)
    before = carry_ref[:, 0:1] + _dot(hits.astype(BF16), earlier)
    rank1 = jnp.sum(hit1 * before, axis=0, keepdims=True)
    rank2 = jnp.sum(hit2 * before, axis=0, keepdims=True)
    carry = carry_ref[...] + jnp.sum(hits, axis=1, keepdims=True)
    carry_ref[...] = carry
    counts_ref[...] = carry

    r = lax.broadcasted_iota(jnp.int32, (ROUTE_ROWS, tm), 0)
    route = jnp.where(r == 0, e1, 0.0)
    for k, v in enumerate((e2, w1, w2, rank1, rank2), start=1):
        route = jnp.where(r == k, v, route)
    route_ref[...] = route


def _outproj(mg, x2, wo, g2, wr_hi, wr_lo, br):
    tm = ROW_TILE
    row = lambda w: pl.BlockSpec((tm, w), lambda i: (i, 0))
    return pl.pallas_call(
        _outproj_kernel,
        grid=(SEQ // tm,),
        in_specs=[row(D_MODEL), row(D_MODEL), _resident(wo.shape), _resident(g2.shape),
                  _resident(wr_hi.shape), _resident(wr_lo.shape), _resident(br.shape)],
        out_specs=[row(D_MODEL), row(D_MODEL),
                   pl.BlockSpec((ROUTE_ROWS, tm), lambda i: (0, i)),
                   pl.BlockSpec((N_EXPERTS, LANES), lambda i: (0, 0))],
        out_shape=[jax.ShapeDtypeStruct((SEQ, D_MODEL), F32),
                   jax.ShapeDtypeStruct((SEQ, D_MODEL), F32),
                   jax.ShapeDtypeStruct((ROUTE_ROWS, SEQ), F32),
                   jax.ShapeDtypeStruct((N_EXPERTS, LANES), F32)],
        scratch_shapes=[pltpu.VMEM((N_EXPERTS, LANES), F32)],
        compiler_params=_params(),
        name="outproj",
    )(mg, x2, wo, g2, wr_hi, wr_lo, br)


def _start_row_gather(src_hbm, row_of, dst, slot, sem, rows):
    for r in rows:
        pltpu.make_async_copy(src_hbm.at[pl.ds(row_of(r), 1), :],
                              dst.at[slot, pl.ds(r, 1), :], sem.at[slot]).start()


def _wait_row_gather(src_hbm, dst, slot, sem, n_rows):
    pltpu.make_async_copy(src_hbm.at[pl.ds(0, n_rows), :], dst.at[slot], sem.at[slot]).wait()


def _scatter_kernel(padstart_ref, padlen_ref, pos_ref, x_ref, xs_hbm, xbuf, zbuf, sem, zsem):
    i = pl.program_id(0)
    n = pl.num_programs(0)
    tm = ROW_TILE
    slot = i % 2

    def row_copies_done(s):
        for _ in range(2):
            pltpu.make_async_copy(xbuf.at[s], xs_hbm.at[pl.ds(0, tm), :], sem.at[s]).wait()

    def pad_copies(e, bit):
        rows = 1 << bit
        start = padstart_ref[e] + (padlen_ref[e] & (rows - 1))
        if rows < SUBLANES:
            return [pltpu.make_async_copy(zbuf.at[pl.ds(0, 1), :], xs_hbm.at[pl.ds(start + k, 1), :],
                                          zsem.at[0]) for k in range(rows)]
        return [pltpu.make_async_copy(zbuf.at[pl.ds(0, rows), :],
                                      xs_hbm.at[pl.ds(pl.multiple_of(start, rows), rows), :],
                                      zsem.at[0])]

    def for_each_pad_piece(fn):
        def body(e, carry):
            for bit in range(PAD_BITS):
                @pl.when((lax.shift_right_logical(padlen_ref[e], bit) & 1) == 1)
                def _():
                    for c in pad_copies(e, bit):
                        fn(c)
            return carry
        lax.fori_loop(0, N_EXPERTS, body, 0)

    @pl.when(i == 0)
    def _():
        zbuf[...] = jnp.zeros(zbuf.shape, F32)
        for_each_pad_piece(lambda c: c.start())

    @pl.when(i >= 2)
    def _():
        row_copies_done(slot)

    xbuf[slot] = x_ref[...]
    for r in range(2 * tm):
        pltpu.make_async_copy(xbuf.at[slot, pl.ds(r % tm, 1), :],
                              xs_hbm.at[pl.ds(pos_ref[0, 0, r], 1), :], sem.at[slot]).start()

    @pl.when(i == n - 1)
    def _():
        row_copies_done(1 - slot)
        row_copies_done(slot)
        for_each_pad_piece(lambda c: c.wait())


def _scatter_rows(padstart, padlen, pos_tiles, n2):
    tm = ROW_TILE
    grid_spec = pltpu.PrefetchScalarGridSpec(
        num_scalar_prefetch=2,
        grid=(SEQ // tm,),
        in_specs=[pl.BlockSpec((1, 1, 2 * tm), lambda i, *_: (i, 0, 0), memory_space=pltpu.SMEM),
                  pl.BlockSpec((tm, D_MODEL), lambda i, *_: (i, 0))],
        out_specs=pl.BlockSpec(memory_space=pl.ANY),
        scratch_shapes=[pltpu.VMEM((2, tm, D_MODEL), F32),
                        pltpu.VMEM((1 << (PAD_BITS - 1), D_MODEL), F32),
                        pltpu.SemaphoreType.DMA((2,)),
                        pltpu.SemaphoreType.DMA((1,))],
    )
    return pl.pallas_call(
        _scatter_kernel,
        grid_spec=grid_spec,
        out_shape=jax.ShapeDtypeStruct((N_EXPERT_TILES * EXPERT_TILE, D_MODEL), F32),
        compiler_params=_params(),
        name="scatter_rows",
    )(padstart, padlen, pos_tiles, n2)


def _expert_kernel(ntl_ref, tst_ref, ntot_ref,
                   xs_hbm, wg_hbm, wu_hbm, wd_hbm,
                   y_hbm,
                   xbuf, ybuf, xsem, ysem, wg_st, wu_st, wd_st, wsem, wgb, wub, wdb):
    e = pl.program_id(0)
    ntot = ntot_ref[0]

    def weight_copies(ex, slot):
        return [pltpu.make_async_copy(src.at[ex], dst.at[slot], wsem.at[slot, j])
                for j, (src, dst) in enumerate(((wg_hbm, wg_st), (wu_hbm, wu_st), (wd_hbm, wd_st)))]

    def tile_rows(t):
        return pl.ds(pl.multiple_of(t * EXPERT_TILE, EXPERT_TILE), EXPERT_TILE)

    def x_copy(t, slot):
        return pltpu.make_async_copy(xs_hbm.at[tile_rows(t), :], xbuf.at[slot], xsem.at[slot])

    def y_copy(t, slot):
        return pltpu.make_async_copy(ybuf.at[slot], y_hbm.at[tile_rows(t), :], ysem.at[slot])

    @pl.when(e == 0)
    def _():
        for c in weight_copies(0, 0):
            c.start()
        x_copy(0, 0).start()

    @pl.when(e + 1 < pl.num_programs(0))
    def _():
        for c in weight_copies(e + 1, (e + 1) % 2):
            c.start()

    n_here = ntl_ref[e]
    wslot = e % 2
    for c in weight_copies(e, wslot):
        c.wait()

    @pl.when(n_here > 0)
    def _():
        wgb[...] = wg_st[wslot].astype(BF16)
        wub[...] = wu_st[wslot].astype(BF16)
        wdb[...] = wd_st[wslot].astype(BF16)

        def tile_body(k, carry):
            t = tst_ref[e] + k
            slot = t % 2
            x_copy(t, slot).wait()

            @pl.when(t + 1 < ntot)
            def _():
                x_copy(t + 1, 1 - slot).start()

            xt = xbuf[slot].astype(BF16)
            hid = jax.nn.silu(_dot(xt, wgb[...])) * _dot(xt, wub[...])
            y = _dot(hid.astype(BF16), wdb[...])

            @pl.when(t >= 2)
            def _():
                y_copy(t - 2, slot).wait()

            ybuf[slot] = y
            y_copy(t, slot).start()
            return carry
        lax.fori_loop(0, n_here, tile_body, 0)

    @pl.when(e == pl.num_programs(0) - 1)
    def _():
        @pl.when(ntot >= 2)
        def _():
            y_copy(ntot - 2, (ntot - 2) % 2).wait()
        y_copy(ntot - 1, (ntot - 1) % 2).wait()


def _experts(ntl, tst, ntot, x_sorted, w_gate, w_up, w_down):
    hbm = pl.BlockSpec(memory_space=pl.ANY)
    grid_spec = pltpu.PrefetchScalarGridSpec(
        num_scalar_prefetch=3,
        grid=(N_EXPERTS,),
        in_specs=[hbm, hbm, hbm, hbm],
        out_specs=hbm,
        scratch_shapes=[
            pltpu.VMEM((2, EXPERT_TILE, D_MODEL), F32),
            pltpu.VMEM((2, EXPERT_TILE, D_MODEL), F32),
            pltpu.SemaphoreType.DMA((2,)),
            pltpu.SemaphoreType.DMA((2,)),
            pltpu.VMEM((2, D_MODEL, EXPERT_FF), F32),
            pltpu.VMEM((2, D_MODEL, EXPERT_FF), F32),
            pltpu.VMEM((2, EXPERT_FF, D_MODEL), F32),
            pltpu.SemaphoreType.DMA((2, 3)),
            pltpu.VMEM((D_MODEL, EXPERT_FF), BF16),
            pltpu.VMEM((D_MODEL, EXPERT_FF), BF16),
            pltpu.VMEM((EXPERT_FF, D_MODEL), BF16),
        ],
    )
    return pl.pallas_call(
        _expert_kernel,
        grid_spec=grid_spec,
        out_shape=jax.ShapeDtypeStruct((N_EXPERT_TILES * EXPERT_TILE, D_MODEL), F32),
        compiler_params=_params(),
        name="experts",
    )(ntl, tst, ntot, x_sorted, w_gate, w_up, w_down)


def _final_kernel(pos_ref, posn_ref, h_ref, route_ref, gf_ref, y_hbm, o_ref, ybuf, sem):
    i = pl.program_id(0)
    n = pl.num_programs(0)
    tm = ROW_TILE

    @pl.when(i == 0)
    def _():
        _start_row_gather(y_hbm, lambda r: pos_ref[0, 0, r], ybuf, 0, sem, range(2 * tm))

    @pl.when(i + 1 < n)
    def _():
        _start_row_gather(y_hbm, lambda r: posn_ref[0, 0, r], ybuf, (i + 1) % 2, sem,
                          range(2 * tm))

    slot = i % 2
    _wait_row_gather(y_hbm, ybuf, slot, sem, 2 * tm)
    w1 = route_ref[:, 0:1]
    w2 = route_ref[:, 1:2]
    h2 = h_ref[...] + (w1 * ybuf[slot, 0:tm, :] + w2 * ybuf[slot, tm:2 * tm, :])
    o_ref[...] = h2 * lax.rsqrt(jnp.mean(h2 * h2, axis=-1, keepdims=True) + EPS) * gf_ref[...]


def _final(pos, h, route, gf, y_sorted):
    tm = ROW_TILE
    n = SEQ // tm
    row = lambda w: pl.BlockSpec((tm, w), lambda i: (i, 0))
    smem_blk = lambda f: pl.BlockSpec((1, 1, 2 * tm), f, memory_space=pltpu.SMEM)
    return pl.pallas_call(
        _final_kernel,
        grid=(n,),
        in_specs=[smem_blk(lambda i: (i, 0, 0)),
                  smem_blk(lambda i: (jnp.minimum(i + 1, n - 1), 0, 0)),
                  row(D_MODEL), row(2), _resident(gf.shape),
                  pl.BlockSpec(memory_space=pl.ANY)],
        out_specs=row(D_MODEL),
        out_shape=jax.ShapeDtypeStruct((SEQ, D_MODEL), F32),
        scratch_shapes=[pltpu.VMEM((2, 2 * tm, D_MODEL), F32), pltpu.SemaphoreType.DMA((2,))],
        compiler_params=_params(),
        name="final",
    )(pos, pos, h, route, gf, y_sorted)


def _dispatch(route, counts):
    i32 = jnp.int32
    e_pair = route[0:2].T.astype(i32).reshape(-1)
    rank = route[4:6].T.astype(i32).reshape(-1)
    cnt = counts[:, 0].astype(i32)
    ntl = (cnt + EXPERT_TILE - 1) // EXPERT_TILE
    tend = jnp.cumsum(ntl)
    tst = tend - ntl
    experts = jnp.arange(N_EXPERTS, dtype=i32)
    of_pair = e_pair[:, None] == experts[None, :]
    pos = jnp.sum(jnp.where(of_pair, (tst * EXPERT_TILE)[None, :], 0), axis=1) + rank
    padstart = tst * EXPERT_TILE + cnt
    padlen = ntl * EXPERT_TILE - cnt
    return ntl, tst, tend[-1:], padstart, padlen, pos


def kernel(x, norm1_g, w_in, kv_norm_g, w_uk, w_uv, gmlp_ws, gmlp_bs, ln_v_g, ln_v_b, w_br_attn,
           w_br_gmlp, w_out, norm2_g, w_group, b_group, w_router, b_router, w_e_gate, w_e_up,
           w_e_down, norm_f_g):
    assert x.shape == (1, SEQ, D_MODEL)
    x2 = x.reshape(SEQ, D_MODEL)
    row_vec = lambda v: v.reshape(1, -1).astype(F32)

    c_q = ATTN_WIDTH
    c_kv = c_q + KV_LATENT
    c_qi = c_kv + IDX_HEADS * IDX_DIM
    c_k = c_qi + IDX_DIM
    c_w = c_k + IDX_HEADS
    c_uv = c_w + 2 * GMLP_WIDTH
    w_bf = w_in.astype(BF16)
    wqT, wc, wqiT = w_bf[:, :c_q].T, w_bf[:, c_q:c_kv], w_bf[:, c_kv:c_qi].T
    wk, wwT = w_bf[:, c_qi:c_k], w_bf[:, c_k:c_w].T
    wuv_in, wga, wgb = w_bf[:, c_w:c_uv], w_bf[:, c_uv:c_uv + D_MODEL], w_bf[:, c_uv + D_MODEL:]
    zk = jnp.zeros_like(wk)
    wkk = jnp.concatenate([wk, zk, zk, wk], axis=1)

    n1, qT, ckv, ckvT, qiT, kk2, wiT = _proj(x2, row_vec(norm1_g), wqT, wc, wqiT, wkk, wwT,
                                             row_vec(kv_norm_g))
    m = _gmlp(n1, wuv_in, gmlp_ws, jnp.pad(gmlp_bs.T, ((0, 0), (0, LANES - GMLP_GROUPS))),
              row_vec(ln_v_g), row_vec(ln_v_b))
    a = _attn(qT, qiT, wiT, kk2, ckv, ckvT,
              jnp.swapaxes(w_uk, 1, 2).astype(BF16), jnp.swapaxes(w_uv, 1, 2).astype(BF16))
    mg = _merge(n1, a, m, wga, wgb, w_br_attn.astype(BF16), w_br_gmlp.astype(BF16))

    w_route = jnp.pad(jnp.concatenate([w_group, w_router], axis=1),
                      ((0, 0), (0, LANES - N_GROUPS - N_EXPERTS)))
    b_route = jnp.pad(jnp.concatenate([b_group, b_router]), (0, LANES - N_GROUPS - N_EXPERTS))
    w_route_t = w_route.T
    wr_hi = w_route_t.astype(BF16)
    wr_lo = (w_route_t - wr_hi.astype(F32)).astype(BF16)
    h, n2, route, counts = _outproj(mg, x2, w_out.astype(BF16), row_vec(norm2_g), wr_hi, wr_lo,
                                    b_route.reshape(LANES, 1))

    ntl, tst, ntot, padstart, padlen, pos = _dispatch(route, counts)
    pos_tiles = pos.reshape(SEQ // ROW_TILE, ROW_TILE, 2).transpose(0, 2, 1).reshape(
        SEQ // ROW_TILE, 1, 2 * ROW_TILE)
    x_sorted = _scatter_rows(padstart, padlen, pos_tiles, n2)
    y_sorted = _experts(ntl, tst, ntot, x_sorted, w_e_gate, w_e_up, w_e_down)
    out = _final(pos_tiles, h, route[2:4].T, row_vec(norm_f_g), y_sorted)
    return out.reshape(1, SEQ, D_MODEL)
```

```python
import jax
import jax.numpy as jnp
import numpy as np
from jax import lax
from jax.experimental import pallas as pl
from jax.experimental.pallas import tpu as pltpu

F32 = jnp.float32
BF16 = jnp.bfloat16

D_MODEL = 2048
SEQ = 8192
N_HEADS = 8
HEAD_DIM = 128
KV_LATENT = 256
IDX_HEADS = 16
IDX_DIM = 64
TOPK = 256
ATTN_WIDTH = N_HEADS * HEAD_DIM
GMLP_GROUPS = 8
GMLP_WIDTH = 1024
CHUNK = 128
N_GROUPS = 8
EXPERTS_PER_GROUP = 8
N_EXPERTS = 64
EXPERT_FF = 512
EPS = 1e-6

LANES = 128
SUBLANES = 8
VMEM_LIMIT = 60 * 1024 * 1024
MASK_VALUE = -0.7 * float(np.finfo(np.float32).max)
LOG2_E = float(np.log2(np.e))

ROW_TILE = 256
TQ = 256
KB = 256
NKB = SEQ // KB
COUNT_ROWS = 32
MAX_SEARCH_ITERS = 64
MAX_BRACKET = 4
EXPERT_TILE = 256
N_EXPERT_TILES = (2 * SEQ) // EXPERT_TILE + N_EXPERTS
PAD_BITS = 8


def _dot(a, b):
    return jnp.dot(a, b, preferred_element_type=F32)


def _dot_nt(a, b):
    return lax.dot_general(a, b, (((1,), (1,)), ((), ())), preferred_element_type=F32)


def _resident(shape):
    zeros = (0,) * len(shape)
    return pl.BlockSpec(shape, lambda *_: zeros, pipeline_mode=pl.Buffered(1))


def _params(n_axes=1):
    return pltpu.CompilerParams(
        dimension_semantics=("arbitrary",) * n_axes, vmem_limit_bytes=VMEM_LIMIT)


def _proj_kernel(x_ref, g1_ref, wqT_ref, wc_ref, wqiT_ref, wkk_ref, wwT_ref, kvg_ref,
                 n1_ref, qT_ref, ckv_ref, ckvT_ref, qiT_ref, kk2_ref, wiT_ref):
    x = x_ref[...]
    ms = jnp.mean(x * x, axis=-1, keepdims=True)
    n1 = (x * lax.rsqrt(ms + EPS) * g1_ref[...]).astype(BF16)
    n1_ref[...] = n1
    qT_ref[...] = _dot_nt(wqT_ref[...], n1).astype(BF16)
    qiT_ref[...] = (_dot_nt(wqiT_ref[...], n1) * (IDX_DIM ** -0.5)).astype(BF16)
    wiT_ref[...] = _dot_nt(wwT_ref[...], n1) * (IDX_HEADS ** -0.5)
    c = _dot(n1, wc_ref[...])
    c = c * lax.rsqrt(jnp.mean(c * c, axis=-1, keepdims=True) + EPS) * kvg_ref[...]
    ckv_ref[...] = c.astype(BF16)
    ckvT_ref[0] = c.T.astype(BF16)
    kk = _dot(n1, wkk_ref[...]).astype(BF16)
    kk2_ref[0, 0:KB, :] = kk[:, 0:LANES]
    kk2_ref[0, KB:2 * KB, :] = kk[:, LANES:2 * LANES]


def _proj(x2, g1, wqT, wc, wqiT, wkk, wwT, kvg):
    tm = KB
    row = lambda w: pl.BlockSpec((tm, w), lambda i: (i, 0))
    col = lambda h: pl.BlockSpec((h, tm), lambda i: (0, i))
    return pl.pallas_call(
        _proj_kernel,
        grid=(SEQ // tm,),
        in_specs=[row(D_MODEL), _resident(g1.shape), _resident(wqT.shape), _resident(wc.shape),
                  _resident(wqiT.shape), _resident(wkk.shape), _resident(wwT.shape),
                  _resident(kvg.shape)],
        out_specs=[row(D_MODEL), col(ATTN_WIDTH), row(KV_LATENT),
                   pl.BlockSpec((1, KV_LATENT, tm), lambda i: (i // 2, 0, i % 2)),
                   col(IDX_HEADS * IDX_DIM),
                   pl.BlockSpec((1, 2 * tm, LANES), lambda i: (i, 0, 0)),
                   col(IDX_HEADS)],
        out_shape=[
            jax.ShapeDtypeStruct((SEQ, D_MODEL), BF16),
            jax.ShapeDtypeStruct((ATTN_WIDTH, SEQ), BF16),
            jax.ShapeDtypeStruct((SEQ, KV_LATENT), BF16),
            jax.ShapeDtypeStruct((NKB // 2, KV_LATENT, 2 * KB), BF16),
            jax.ShapeDtypeStruct((IDX_HEADS * IDX_DIM, SEQ), BF16),
            jax.ShapeDtypeStruct((NKB, 2 * KB, LANES), BF16),
            jax.ShapeDtypeStruct((IDX_HEADS, SEQ), F32),
        ],
        compiler_params=_params(),
        name="proj",
    )(x2, g1, wqT, wc, wqiT, wkk, wwT, kvg)


def _gmlp_kernel(n1_ref, wuv_ref, ws_ref, bsT_ref, lng_ref, lnb_ref, m_ref):
    uv = _dot(n1_ref[...], wuv_ref[...])
    z = jax.nn.gelu(uv)
    u = z[:, :GMLP_WIDTH]
    v = z[:, GMLP_WIDTH:]
    mu = jnp.mean(v, axis=-1, keepdims=True)
    var = jnp.mean(jnp.square(v - mu), axis=-1, keepdims=True)
    vn = ((v - mu) * lax.rsqrt(var + EPS) * lng_ref[...] + lnb_ref[...]).astype(BF16)
    t_pos = lax.broadcasted_iota(jnp.int32, (CHUNK, CHUNK), 0)
    s_pos = lax.broadcasted_iota(jnp.int32, (CHUNK, CHUNK), 1)
    causal = s_pos <= t_pos
    for g in range(GMLP_GROUPS):
        wm = jnp.where(causal, ws_ref[g], 0.0).astype(BF16)
        bias = bsT_ref[:, g:g + 1]
        cols = slice(g * LANES, (g + 1) * LANES)
        for c in range(ROW_TILE // CHUNK):
            rows = slice(c * CHUNK, (c + 1) * CHUNK)
            y = _dot(wm, vn[rows, cols]) + bias
            m_ref[rows, cols] = (u[rows, cols] * y).astype(BF16)


def _gmlp(n1, wuv, ws, bsT, lng, lnb):
    tm = ROW_TILE
    return pl.pallas_call(
        _gmlp_kernel,
        grid=(SEQ // tm,),
        in_specs=[pl.BlockSpec((tm, D_MODEL), lambda i: (i, 0)), _resident(wuv.shape),
                  _resident(ws.shape), _resident(bsT.shape), _resident(lng.shape),
                  _resident(lnb.shape)],
        out_specs=pl.BlockSpec((tm, GMLP_WIDTH), lambda i: (i, 0)),
        out_shape=jax.ShapeDtypeStruct((SEQ, GMLP_WIDTH), BF16),
        compiler_params=_params(),
        name="gmlp",
    )(n1, wuv, ws, bsT, lng, lnb)


def _attn_kernel(qT_ref, qiT_ref, wiT_ref, kk2_ref, ckv_ref, ckvT_ref, wukT_ref, wuvT_ref,
                 a_ref,
                 sc_ref, qabs_ref, thr_ref, keep_ref, m_ref, l_ref, acc_ref, lg_ref, bmax_ref):
    i = pl.program_id(0)
    nkb = i + 1

    for h in range(N_HEADS):
        qa = _dot(wukT_ref[h], qT_ref[h * HEAD_DIM:(h + 1) * HEAD_DIM, :]) * (
            HEAD_DIM ** -0.5 * LOG2_E)
        qabs_ref[h] = qa.astype(BF16)

    q_pos = i * TQ + lax.broadcasted_iota(jnp.int32, (KB, TQ), 1)
    k_off = lax.broadcasted_iota(jnp.int32, (KB, TQ), 0)

    def score_body(kb, carry):
        smax, smin = carry
        keys = kk2_ref[kb]
        acc = jnp.zeros((KB, TQ), F32)
        for j in range(IDX_HEADS // 2):
            d = _dot(keys, qiT_ref[j * LANES:(j + 1) * LANES, :])
            acc = acc + jnp.maximum(d[0:KB], 0.0) * wiT_ref[2 * j:2 * j + 1, :]
            acc = acc + jnp.maximum(d[KB:2 * KB], 0.0) * wiT_ref[2 * j + 1:2 * j + 2, :]
        causal = (kb * KB + k_off) <= q_pos
        sc_ref[kb] = jnp.where(causal, acc, -jnp.inf)
        smax = jnp.maximum(smax, jnp.max(jnp.where(causal, acc, -jnp.inf), axis=0, keepdims=True))
        smin = jnp.minimum(smin, jnp.min(jnp.where(causal, acc, jnp.inf), axis=0, keepdims=True))
        return smax, smin

    n_pairs = (nkb + 1) // 2
    smax, smin = lax.fori_loop(
        0, n_pairs, lambda j, c: score_body(2 * j + 1, score_body(2 * j, c)),
        (jnp.full((1, TQ), -jnp.inf, F32), jnp.full((1, TQ), jnp.inf, F32)))

    def count_ge(x):
        def body(j, cnt):
            for kb in (2 * j, 2 * j + 1):
                ge = jnp.where(sc_ref[kb] >= x, 1.0, 0.0)
                cnt = cnt + jnp.sum(ge.reshape(KB // COUNT_ROWS, COUNT_ROWS, TQ), axis=0)
            return cnt
        cnt = lax.fori_loop(0, n_pairs, body, jnp.zeros((COUNT_ROWS, TQ), F32))
        return jnp.sum(cnt, axis=0, keepdims=True)

    def max_below(x):
        def body(j, best):
            for kb in (2 * j, 2 * j + 1):
                s = sc_ref[kb]
                v = jnp.where(s < x, s, -jnp.inf)
                best = jnp.maximum(best, jnp.max(v.reshape(KB // COUNT_ROWS, COUNT_ROWS, TQ), axis=0))
            return best
        best = lax.fori_loop(0, n_pairs, body, jnp.full((COUNT_ROWS, TQ), -jnp.inf, F32))
        return jnp.max(best, axis=0, keepdims=True)

    n_causal = (i * TQ + 1 + lax.broadcasted_iota(jnp.int32, (1, TQ), 1)).astype(F32)
    want = jnp.minimum(n_causal, float(TOPK))
    hi0 = smax + jnp.maximum(jnp.abs(smax), 1e-30) * 1e-6
    open0 = (n_causal != want).astype(F32)

    def search_cond(state):
        return jnp.logical_and(state[0] < MAX_SEARCH_ITERS, state[-1] > 0.0)

    def search_body(state):
        it, lo, hi, c_lo, c_hi, open_, _ = state
        mid = 0.5 * lo + 0.5 * hi
        c = count_ge(mid)
        live = jnp.logical_and(open_ > 0.0, jnp.logical_and(mid > lo, mid < hi))
        ge = c >= want
        go_lo = jnp.logical_and(live, ge)
        go_hi = jnp.logical_and(live, jnp.logical_not(ge))
        lo = jnp.where(go_lo, mid, lo)
        c_lo = jnp.where(go_lo, c, c_lo)
        hi = jnp.where(go_hi, mid, hi)
        c_hi = jnp.where(go_hi, c, c_hi)
        wide = jnp.logical_and(c_lo != want, c_lo - c_hi > MAX_BRACKET)
        open_ = jnp.logical_and(live, wide).astype(F32)
        return it + 1, lo, hi, c_lo, c_hi, open_, jnp.max(open_)

    _, lo_f, hi_f, c_lo_f, c_hi_f, _, _ = lax.while_loop(
        search_cond, search_body,
        (jnp.int32(0), smin, hi0, n_causal, jnp.zeros((1, TQ), F32), open0, jnp.max(open0)))

    stepping = c_lo_f != want

    def step_cond(state):
        return jnp.logical_and(state[0] < MAX_BRACKET, state[-1] > 0.0)

    def step_body(state):
        it, cur, c, _ = state
        act = jnp.logical_and(stepping, c < want)
        cur = jnp.where(act, max_below(cur), cur)
        c = jnp.where(act, c + 1.0, c)
        return it + 1, cur, c, jnp.max(jnp.logical_and(stepping, c < want).astype(F32))

    _, cur_f, _, _ = lax.while_loop(
        step_cond, step_body,
        (jnp.int32(0), hi_f, c_hi_f, jnp.max(stepping.astype(F32))))
    thr0 = jnp.where(stepping, cur_f, lo_f)
    thr_ref[...] = jnp.broadcast_to(thr0, (SUBLANES, TQ))
    keep_ref[...] = jnp.full((SUBLANES, TQ), float(SEQ), F32)
    unresolved0 = (count_ge(thr0) != want).astype(F32)

    @pl.when(jnp.max(unresolved0) > 0.0)
    def _():
        def next_value(lo, below):
            def body(kb, u):
                s = sc_ref[kb]
                cand = jnp.logical_and(s >= lo, s > below)
                return jnp.minimum(u, jnp.min(jnp.where(cand, s, jnp.inf), axis=0, keepdims=True))
            return lax.fori_loop(0, nkb, body, jnp.full((1, TQ), jnp.inf, F32))

        def count_gt(x):
            def body(kb, cnt):
                gt = jnp.where(sc_ref[kb] > x, 1.0, 0.0)
                return cnt + jnp.sum(gt, axis=0, keepdims=True)
            return lax.fori_loop(0, nkb, body, jnp.zeros((1, TQ), F32))

        def peel_cond(state):
            return state[-1] > 0.0

        def peel_body(state):
            below, unres, thr, keep, _ = state
            u = next_value(lo_f, below)
            c_gt = count_gt(u)
            hit = jnp.logical_and(unres > 0.0, c_gt < want)
            thr = jnp.where(hit, u, thr)
            keep = jnp.where(hit, want - c_gt, keep)
            unres = jnp.logical_and(unres > 0.0, jnp.logical_not(hit)).astype(F32)
            return u, unres, thr, keep, jnp.max(unres)

        _, _, thr_t, keep_t, _ = lax.while_loop(
            peel_cond, peel_body,
            (jnp.full((1, TQ), -jnp.inf, F32), unresolved0, thr0,
             jnp.full((1, TQ), float(SEQ), F32), jnp.float32(1.0)))
        thr_ref[...] = jnp.broadcast_to(thr_t, (SUBLANES, TQ))
        keep_ref[...] = jnp.broadcast_to(keep_t, (SUBLANES, TQ))

        r_i = lax.broadcasted_iota(jnp.int32, (KB, KB), 0)
        c_i = lax.broadcasted_iota(jnp.int32, (KB, KB), 1)
        before = (c_i < r_i).astype(BF16)

        def drop_body(kb, seen):
            s = sc_ref[kb]
            eq = jnp.logical_and(s == thr_t, unresolved0 > 0.0)
            eq_f = eq.astype(F32)
            rank = seen + _dot(before, eq_f.astype(BF16))
            sc_ref[kb] = jnp.where(jnp.logical_and(eq, rank >= keep_t), -jnp.inf, s)
            return seen + jnp.sum(eq_f, axis=0, keepdims=True)

        lax.fori_loop(0, nkb, drop_body, jnp.zeros((1, TQ), F32))

    m_ref[...] = jnp.full(m_ref.shape, MASK_VALUE, F32)
    l_ref[...] = jnp.zeros(l_ref.shape, F32)
    acc_ref[...] = jnp.zeros(acc_ref.shape, F32)

    def logits_stage(j, slot):
        scores = sc_ref[pl.ds(2 * j, 2)].reshape(2 * KB, TQ)
        bias = jnp.where(scores >= thr_ref[0:1, :], 0.0, MASK_VALUE)
        c_n = ckv_ref[pl.ds(pl.multiple_of(j * (2 * KB), 2 * KB), 2 * KB), :]
        for h in range(N_HEADS):
            lg = _dot(c_n, qabs_ref[h]) + bias
            lg_ref[slot, h] = lg
            bmax_ref[slot, h] = jnp.broadcast_to(jnp.max(lg, axis=0, keepdims=True), (SUBLANES, TQ))

    def softmax_stage(j, slot):
        c_t = ckvT_ref[j]
        for h in range(N_HEADS):
            m_old = m_ref[h, 0:1, :]
            m_new = jnp.maximum(m_old, bmax_ref[slot, h, 0:1, :])
            alpha = jnp.exp2(m_old - m_new)
            p = jnp.exp2(lg_ref[slot, h] - m_new)
            l_new = alpha * l_ref[h, 0:1, :] + jnp.sum(p, axis=0, keepdims=True)
            acc_ref[h] = acc_ref[h] * alpha + _dot(c_t, p.astype(BF16))
            m_ref[h] = jnp.broadcast_to(m_new, (SUBLANES, TQ))
            l_ref[h] = jnp.broadcast_to(l_new, (SUBLANES, TQ))

    logits_stage(0, 0)

    def att_body(j, carry):
        for slot in range(2):
            @pl.when(j % 2 == slot)
            def _():
                logits_stage(j + 1, 1 - slot)
                softmax_stage(j, slot)
        return carry

    lax.fori_loop(0, n_pairs - 1, att_body, 0)
    for slot in range(2):
        @pl.when((n_pairs - 1) % 2 == slot)
        def _():
            softmax_stage(n_pairs - 1, slot)

    for h in range(N_HEADS):
        o_t = (acc_ref[h] / l_ref[h, 0:1, :]).astype(BF16)
        a_t = _dot(wuvT_ref[h], o_t)
        a_ref[:, h * HEAD_DIM:(h + 1) * HEAD_DIM] = a_t.T.astype(BF16)


def _attn(qT, qiT, wiT, kk2, ckv, ckvT, wukT, wuvT):
    col = lambda h: pl.BlockSpec((h, TQ), lambda i: (0, i))
    return pl.pallas_call(
        _attn_kernel,
        grid=(SEQ // TQ,),
        in_specs=[col(ATTN_WIDTH), col(IDX_HEADS * IDX_DIM), col(IDX_HEADS), _resident(kk2.shape),
                  _resident(ckv.shape), _resident(ckvT.shape), _resident(wukT.shape),
                  _resident(wuvT.shape)],
        out_specs=pl.BlockSpec((TQ, ATTN_WIDTH), lambda i: (i, 0)),
        out_shape=jax.ShapeDtypeStruct((SEQ, ATTN_WIDTH), BF16),
        scratch_shapes=[
            pltpu.VMEM((NKB, KB, TQ), F32),
            pltpu.VMEM((N_HEADS, KV_LATENT, TQ), BF16),
            pltpu.VMEM((SUBLANES, TQ), F32),
            pltpu.VMEM((SUBLANES, TQ), F32),
            pltpu.VMEM((N_HEADS, SUBLANES, TQ), F32),
            pltpu.VMEM((N_HEADS, SUBLANES, TQ), F32),
            pltpu.VMEM((N_HEADS, KV_LATENT, TQ), F32),
            pltpu.VMEM((2, N_HEADS, 2 * KB, TQ), F32),
            pltpu.VMEM((2, N_HEADS, SUBLANES, TQ), F32),
        ],
        compiler_params=_params(),
        name="attn",
    )(qT, qiT, wiT, kk2, ckv, ckvT, wukT, wuvT)


def _merge_kernel(n1_ref, a_ref, m_ref, wga_ref, wgb_ref, wba_ref, wbg_ref, o_ref):
    n1 = n1_ref[...]
    br_a = jax.nn.sigmoid(_dot(n1, wga_ref[...])) * _dot(a_ref[...], wba_ref[...])
    br_b = jax.nn.sigmoid(_dot(n1, wgb_ref[...])) * _dot(m_ref[...], wbg_ref[...])
    o_ref[...] = (br_a + br_b).astype(BF16)


def _merge(n1, a, m, wga, wgb, wba, wbg):
    tm = ROW_TILE
    row = lambda w: pl.BlockSpec((tm, w), lambda i: (i, 0))
    return pl.pallas_call(
        _merge_kernel,
        grid=(SEQ // tm,),
        in_specs=[row(D_MODEL), row(ATTN_WIDTH), row(GMLP_WIDTH), _resident(wga.shape),
                  _resident(wgb.shape), _resident(wba.shape), _resident(wbg.shape)],
        out_specs=row(D_MODEL),
        out_shape=jax.ShapeDtypeStruct((SEQ, D_MODEL), BF16),
        compiler_params=_params(),
        name="merge",
    )(n1, a, m, wga, wgb, wba, wbg)


ROUTE_ROWS = 8


def _outproj_kernel(mg_ref, x_ref, wo_ref, g2_ref, wrh_ref, wrl_ref, br_ref,
                    h_ref, n2_ref, route_ref, counts_ref, carry_ref):
    i = pl.program_id(0)
    tm = ROW_TILE

    @pl.when(i == 0)
    def _():
        carry_ref[...] = jnp.zeros(carry_ref.shape, F32)

    h = x_ref[...] + _dot(mg_ref[...], wo_ref[...])
    h_ref[...] = h
    n2 = h * lax.rsqrt(jnp.mean(h * h, axis=-1, keepdims=True) + EPS) * g2_ref[...]
    n2_ref[...] = n2
    n2_hi = n2.astype(BF16)
    n2_lo = (n2 - n2_hi.astype(F32)).astype(BF16)
    logits = (_dot_nt(wrh_ref[...], n2_hi) + _dot_nt(wrh_ref[...], n2_lo)
              + _dot_nt(wrl_ref[...], n2_hi)) + br_ref[...]
    row = lax.broadcasted_iota(jnp.int32, (LANES, tm), 0).astype(F32)
    is_group = row < N_GROUPS
    gl = jnp.where(is_group, logits, -jnp.inf)
    gmax = jnp.max(gl, axis=0, keepdims=True)
    gsum = jnp.sum(jnp.where(is_group, jnp.exp(logits - gmax), 0.0), axis=0, keepdims=True)
    g_val = 1.0 / gsum
    g_idx = jnp.min(jnp.where(gl == gmax, row, float(LANES)), axis=0, keepdims=True)
    e_id = row - N_GROUPS
    in_group = jnp.logical_and(
        jnp.logical_and(e_id >= 0, e_id < N_EXPERTS),
        jnp.floor(e_id * (1.0 / EXPERTS_PER_GROUP)) == g_idx)
    sel = jnp.where(in_group, logits, -jnp.inf)
    v1 = jnp.max(sel, axis=0, keepdims=True)
    i1 = jnp.min(jnp.where(sel == v1, row, float(LANES)), axis=0, keepdims=True)
    sel2 = jnp.where(row == i1, -jnp.inf, sel)
    v2 = jnp.max(sel2, axis=0, keepdims=True)
    i2 = jnp.min(jnp.where(sel2 == v2, row, float(LANES)), axis=0, keepdims=True)
    x2 = jnp.exp(v2 - v1)
    den = 1.0 + x2
    w1 = g_val * (1.0 / den)
    w2 = g_val * (x2 / den)
    e1 = i1 - N_GROUPS
    e2 = i2 - N_GROUPS

    e_row = lax.broadcasted_iota(jnp.int32, (N_EXPERTS, tm), 0).astype(F32)
    hit1 = (e_row == e1).astype(F32)
    hit2 = (e_row == e2).astype(F32)
    hits = hit1 + hit2
    t_from = lax.broadcasted_iota(jnp.int32, (tm, tm), 0)
    t_to = lax.broadcasted_iota(jnp.int32, (tm, tm), 1)
    earlier = (t_from < t_to).astype(BF16)
    before = carry_ref[:, 0:1] + _dot(hits.astype(BF16), earlier)
    rank1 = jnp.sum(hit1 * before, axis=0, keepdims=True)
    rank2 = jnp.sum(hit2 * before, axis=0, keepdims=True)
    carry = carry_ref[...] + jnp.sum(hits, axis=1, keepdims=True)
    carry_ref[...] = carry
    counts_ref[...] = carry

    r = lax.broadcasted_iota(jnp.int32, (ROUTE_ROWS, tm), 0)
    route = jnp.where(r == 0, e1, 0.0)
    for k, v in enumerate((e2, w1, w2, rank1, rank2), start=1):
        route = jnp.where(r == k, v, route)
    route_ref[...] = route


def _outproj(mg, x2, wo, g2, wr_hi, wr_lo, br):
    tm = ROW_TILE
    row = lambda w: pl.BlockSpec((tm, w), lambda i: (i, 0))
    return pl.pallas_call(
        _outproj_kernel,
        grid=(SEQ // tm,),
        in_specs=[row(D_MODEL), row(D_MODEL), _resident(wo.shape), _resident(g2.shape),
                  _resident(wr_hi.shape), _resident(wr_lo.shape), _resident(br.shape)],
        out_specs=[row(D_MODEL), row(D_MODEL),
                   pl.BlockSpec((ROUTE_ROWS, tm), lambda i: (0, i)),
                   pl.BlockSpec((N_EXPERTS, LANES), lambda i: (0, 0))],
        out_shape=[jax.ShapeDtypeStruct((SEQ, D_MODEL), F32),
                   jax.ShapeDtypeStruct((SEQ, D_MODEL), F32),
                   jax.ShapeDtypeStruct((ROUTE_ROWS, SEQ), F32),
                   jax.ShapeDtypeStruct((N_EXPERTS, LANES), F32)],
        scratch_shapes=[pltpu.VMEM((N_EXPERTS, LANES), F32)],
        compiler_params=_params(),
        name="outproj",
    )(mg, x2, wo, g2, wr_hi, wr_lo, br)


def _start_row_gather(src_hbm, row_of, dst, slot, sem, rows):
    for r in rows:
        pltpu.make_async_copy(src_hbm.at[pl.ds(row_of(r), 1), :],
                              dst.at[slot, pl.ds(r, 1), :], sem.at[slot]).start()


def _wait_row_gather(src_hbm, dst, slot, sem, n_rows):
    pltpu.make_async_copy(src_hbm.at[pl.ds(0, n_rows), :], dst.at[slot], sem.at[slot]).wait()


def _scatter_kernel(padstart_ref, padlen_ref, pos_ref, x_ref, xs_hbm, xbuf, zbuf, sem, zsem):
    i = pl.program_id(0)
    n = pl.num_programs(0)
    tm = ROW_TILE
    slot = i % 2

    def row_copies_done(s):
        for _ in range(2):
            pltpu.make_async_copy(xbuf.at[s], xs_hbm.at[pl.ds(0, tm), :], sem.at[s]).wait()

    def pad_copies(e, bit):
        rows = 1 << bit
        start = padstart_ref[e] + (padlen_ref[e] & (rows - 1))
        if rows < SUBLANES:
            return [pltpu.make_async_copy(zbuf.at[pl.ds(0, 1), :], xs_hbm.at[pl.ds(start + k, 1), :],
                                          zsem.at[0]) for k in range(rows)]
        return [pltpu.make_async_copy(zbuf.at[pl.ds(0, rows), :],
                                      xs_hbm.at[pl.ds(pl.multiple_of(start, rows), rows), :],
                                      zsem.at[0])]

    def for_each_pad_piece(fn):
        def body(e, carry):
            for bit in range(PAD_BITS):
                @pl.when((lax.shift_right_logical(padlen_ref[e], bit) & 1) == 1)
                def _():
                    for c in pad_copies(e, bit):
                        fn(c)
            return carry
        lax.fori_loop(0, N_EXPERTS, body, 0)

    @pl.when(i == 0)
    def _():
        zbuf[...] = jnp.zeros(zbuf.shape, F32)
        for_each_pad_piece(lambda c: c.start())

    @pl.when(i >= 2)
    def _():
        row_copies_done(slot)

    xbuf[slot] = x_ref[...]
    for r in range(2 * tm):
        pltpu.make_async_copy(xbuf.at[slot, pl.ds(r % tm, 1), :],
                              xs_hbm.at[pl.ds(pos_ref[0, 0, r], 1), :], sem.at[slot]).start()

    @pl.when(i == n - 1)
    def _():
        row_copies_done(1 - slot)
        row_copies_done(slot)
        for_each_pad_piece(lambda c: c.wait())


def _scatter_rows(padstart, padlen, pos_tiles, n2):
    tm = ROW_TILE
    grid_spec = pltpu.PrefetchScalarGridSpec(
        num_scalar_prefetch=2,
        grid=(SEQ // tm,),
        in_specs=[pl.BlockSpec((1, 1, 2 * tm), lambda i, *_: (i, 0, 0), memory_space=pltpu.SMEM),
                  pl.BlockSpec((tm, D_MODEL), lambda i, *_: (i, 0))],
        out_specs=pl.BlockSpec(memory_space=pl.ANY),
        scratch_shapes=[pltpu.VMEM((2, tm, D_MODEL), F32),
                        pltpu.VMEM((1 << (PAD_BITS - 1), D_MODEL), F32),
                        pltpu.SemaphoreType.DMA((2,)),
                        pltpu.SemaphoreType.DMA((1,))],
    )
    return pl.pallas_call(
        _scatter_kernel,
        grid_spec=grid_spec,
        out_shape=jax.ShapeDtypeStruct((N_EXPERT_TILES * EXPERT_TILE, D_MODEL), F32),
        compiler_params=_params(),
        name="scatter_rows",
    )(padstart, padlen, pos_tiles, n2)


def _expert_kernel(ntl_ref, tst_ref, ntot_ref,
                   xs_hbm, wg_hbm, wu_hbm, wd_hbm,
                   y_hbm,
                   xbuf, ybuf, xsem, ysem, wg_st, wu_st, wd_st, wsem, wgb, wub, wdb):
    e = pl.program_id(0)
    ntot = ntot_ref[0]

    def weight_copies(ex, slot):
        return [pltpu.make_async_copy(src.at[ex], dst.at[slot], wsem.at[slot, j])
                for j, (src, dst) in enumerate(((wg_hbm, wg_st), (wu_hbm, wu_st), (wd_hbm, wd_st)))]

    def tile_rows(t):
        return pl.ds(pl.multiple_of(t * EXPERT_TILE, EXPERT_TILE), EXPERT_TILE)

    def x_copy(t, slot):
        return pltpu.make_async_copy(xs_hbm.at[tile_rows(t), :], xbuf.at[slot], xsem.at[slot])

    def y_copy(t, slot):
        return pltpu.make_async_copy(ybuf.at[slot], y_hbm.at[tile_rows(t), :], ysem.at[slot])

    @pl.when(e == 0)
    def _():
        for c in weight_copies(0, 0):
            c.start()
        x_copy(0, 0).start()

    @pl.when(e + 1 < pl.num_programs(0))
    def _():
        for c in weight_copies(e + 1, (e + 1) % 2):
            c.start()

    n_here = ntl_ref[e]
    wslot = e % 2
    for c in weight_copies(e, wslot):
        c.wait()

    @pl.when(n_here > 0)
    def _():
        wgb[...] = wg_st[wslot].astype(BF16)
        wub[...] = wu_st[wslot].astype(BF16)
        wdb[...] = wd_st[wslot].astype(BF16)

        def tile_body(k, carry):
            t = tst_ref[e] + k
            slot = t % 2
            x_copy(t, slot).wait()

            @pl.when(t + 1 < ntot)
            def _():
                x_copy(t + 1, 1 - slot).start()

            xt = xbuf[slot].astype(BF16)
            hid = jax.nn.silu(_dot(xt, wgb[...])) * _dot(xt, wub[...])
            y = _dot(hid.astype(BF16), wdb[...])

            @pl.when(t >= 2)
            def _():
                y_copy(t - 2, slot).wait()

            ybuf[slot] = y
            y_copy(t, slot).start()
            return carry
        lax.fori_loop(0, n_here, tile_body, 0)

    @pl.when(e == pl.num_programs(0) - 1)
    def _():
        @pl.when(ntot >= 2)
        def _():
            y_copy(ntot - 2, (ntot - 2) % 2).wait()
        y_copy(ntot - 1, (ntot - 1) % 2).wait()


def _experts(ntl, tst, ntot, x_sorted, w_gate, w_up, w_down):
    hbm = pl.BlockSpec(memory_space=pl.ANY)
    grid_spec = pltpu.PrefetchScalarGridSpec(
        num_scalar_prefetch=3,
        grid=(N_EXPERTS,),
        in_specs=[hbm, hbm, hbm, hbm],
        out_specs=hbm,
        scratch_shapes=[
            pltpu.VMEM((2, EXPERT_TILE, D_MODEL), F32),
            pltpu.VMEM((2, EXPERT_TILE, D_MODEL), F32),
            pltpu.SemaphoreType.DMA((2,)),
            pltpu.SemaphoreType.DMA((2,)),
            pltpu.VMEM((2, D_MODEL, EXPERT_FF), F32),
            pltpu.VMEM((2, D_MODEL, EXPERT_FF), F32),
            pltpu.VMEM((2, EXPERT_FF, D_MODEL), F32),
            pltpu.SemaphoreType.DMA((2, 3)),
            pltpu.VMEM((D_MODEL, EXPERT_FF), BF16),
            pltpu.VMEM((D_MODEL, EXPERT_FF), BF16),
            pltpu.VMEM((EXPERT_FF, D_MODEL), BF16),
        ],
    )
    return pl.pallas_call(
        _expert_kernel,
        grid_spec=grid_spec,
        out_shape=jax.ShapeDtypeStruct((N_EXPERT_TILES * EXPERT_TILE, D_MODEL), F32),
        compiler_params=_params(),
        name="experts",
    )(ntl, tst, ntot, x_sorted, w_gate, w_up, w_down)


def _final_kernel(pos_ref, posn_ref, h_ref, route_ref, gf_ref, y_hbm, o_ref, ybuf, sem):
    i = pl.program_id(0)
    n = pl.num_programs(0)
    tm = ROW_TILE

    @pl.when(i == 0)
    def _():
        _start_row_gather(y_hbm, lambda r: pos_ref[0, 0, r], ybuf, 0, sem, range(2 * tm))

    @pl.when(i + 1 < n)
    def _():
        _start_row_gather(y_hbm, lambda r: posn_ref[0, 0, r], ybuf, (i + 1) % 2, sem,
                          range(2 * tm))

    slot = i % 2
    _wait_row_gather(y_hbm, ybuf, slot, sem, 2 * tm)
    w1 = route_ref[:, 0:1]
    w2 = route_ref[:, 1:2]
    h2 = h_ref[...] + (w1 * ybuf[slot, 0:tm, :] + w2 * ybuf[slot, tm:2 * tm, :])
    o_ref[...] = h2 * lax.rsqrt(jnp.mean(h2 * h2, axis=-1, keepdims=True) + EPS) * gf_ref[...]


def _final(pos, h, route, gf, y_sorted):
    tm = ROW_TILE
    n = SEQ // tm
    row = lambda w: pl.BlockSpec((tm, w), lambda i: (i, 0))
    smem_blk = lambda f: pl.BlockSpec((1, 1, 2 * tm), f, memory_space=pltpu.SMEM)
    return pl.pallas_call(
        _final_kernel,
        grid=(n,),
        in_specs=[smem_blk(lambda i: (i, 0, 0)),
                  smem_blk(lambda i: (jnp.minimum(i + 1, n - 1), 0, 0)),
                  row(D_MODEL), row(2), _resident(gf.shape),
                  pl.BlockSpec(memory_space=pl.ANY)],
        out_specs=row(D_MODEL),
        out_shape=jax.ShapeDtypeStruct((SEQ, D_MODEL), F32),
        scratch_shapes=[pltpu.VMEM((2, 2 * tm, D_MODEL), F32), pltpu.SemaphoreType.DMA((2,))],
        compiler_params=_params(),
        name="final",
    )(pos, pos, h, route, gf, y_sorted)


def _dispatch(route, counts):
    i32 = jnp.int32
    e_pair = route[0:2].T.astype(i32).reshape(-1)
    rank = route[4:6].T.astype(i32).reshape(-1)
    cnt = counts[:, 0].astype(i32)
    ntl = (cnt + EXPERT_TILE - 1) // EXPERT_TILE
    tend = jnp.cumsum(ntl)
    tst = tend - ntl
    experts = jnp.arange(N_EXPERTS, dtype=i32)
    of_pair = e_pair[:, None] == experts[None, :]
    pos = jnp.sum(jnp.where(of_pair, (tst * EXPERT_TILE)[None, :], 0), axis=1) + rank
    padstart = tst * EXPERT_TILE + cnt
    padlen = ntl * EXPERT_TILE - cnt
    return ntl, tst, tend[-1:], padstart, padlen, pos


def kernel(x, norm1_g, w_in, kv_norm_g, w_uk, w_uv, gmlp_ws, gmlp_bs, ln_v_g, ln_v_b, w_br_attn,
           w_br_gmlp, w_out, norm2_g, w_group, b_group, w_router, b_router, w_e_gate, w_e_up,
           w_e_down, norm_f_g):
    assert x.shape == (1, SEQ, D_MODEL)
    x2 = x.reshape(SEQ, D_MODEL)
    row_vec = lambda v: v.reshape(1, -1).astype(F32)

    c_q = ATTN_WIDTH
    c_kv = c_q + KV_LATENT
    c_qi = c_kv + IDX_HEADS * IDX_DIM
    c_k = c_qi + IDX_DIM
    c_w = c_k + IDX_HEADS
    c_uv = c_w + 2 * GMLP_WIDTH
    w_bf = w_in.astype(BF16)
    wqT, wc, wqiT = w_bf[:, :c_q].T, w_bf[:, c_q:c_kv], w_bf[:, c_kv:c_qi].T
    wk, wwT = w_bf[:, c_qi:c_k], w_bf[:, c_k:c_w].T
    wuv_in, wga, wgb = w_bf[:, c_w:c_uv], w_bf[:, c_uv:c_uv + D_MODEL], w_bf[:, c_uv + D_MODEL:]
    zk = jnp.zeros_like(wk)
    wkk = jnp.concatenate([wk, zk, zk, wk], axis=1)

    n1, qT, ckv, ckvT, qiT, kk2, wiT = _proj(x2, row_vec(norm1_g), wqT, wc, wqiT, wkk, wwT,
                                             row_vec(kv_norm_g))
    m = _gmlp(n1, wuv_in, gmlp_ws, jnp.pad(gmlp_bs.T, ((0, 0), (0, LANES - GMLP_GROUPS))),
              row_vec(ln_v_g), row_vec(ln_v_b))
    a = _attn(qT, qiT, wiT, kk2, ckv, ckvT,
              jnp.swapaxes(w_uk, 1, 2).astype(BF16), jnp.swapaxes(w_uv, 1, 2).astype(BF16))
    mg = _merge(n1, a, m, wga, wgb, w_br_attn.astype(BF16), w_br_gmlp.astype(BF16))

    w_route = jnp.pad(jnp.concatenate([w_group, w_router], axis=1),
                      ((0, 0), (0, LANES - N_GROUPS - N_EXPERTS)))
    b_route = jnp.pad(jnp.concatenate([b_group, b_router]), (0, LANES - N_GROUPS - N_EXPERTS))
    w_route_t = w_route.T
    wr_hi = w_route_t.astype(BF16)
    wr_lo = (w_route_t - wr_hi.astype(F32)).astype(BF16)
    h, n2, route, counts = _outproj(mg, x2, w_out.astype(BF16), row_vec(norm2_g), wr_hi, wr_lo,
                                    b_route.reshape(LANES, 1))

    ntl, tst, ntot, padstart, padlen, pos = _dispatch(route, counts)
    pos_tiles = pos.reshape(SEQ // ROW_TILE, ROW_TILE, 2).transpose(0, 2, 1).reshape(
        SEQ // ROW_TILE, 1, 2 * ROW_TILE)
    x_sorted = _scatter_rows(padstart, padlen, pos_tiles, n2)
    y_sorted = _experts(ntl, tst, ntot, x_sorted, w_e_gate, w_e_up, w_e_down)
    out = _final(pos_tiles, h, route[2:4].T, row_vec(norm_f_g), y_sorted)
    return out.reshape(1, SEQ, D_MODEL)
```

```python
import jax
import jax.numpy as jnp
import numpy as np
from jax import lax
from jax.experimental import pallas as pl
from jax.experimental.pallas import tpu as pltpu

F32 = jnp.float32
BF16 = jnp.bfloat16

D_MODEL = 2048
SEQ = 8192
N_HEADS = 8
HEAD_DIM = 128
KV_LATENT = 256
IDX_HEADS = 16
IDX_DIM = 64
TOPK = 256
ATTN_WIDTH = N_HEADS * HEAD_DIM
GMLP_GROUPS = 8
GMLP_WIDTH = 1024
CHUNK = 128
N_GROUPS = 8
EXPERTS_PER_GROUP = 8
N_EXPERTS = 64
EXPERT_FF = 512
EPS = 1e-6

LANES = 128
SUBLANES = 8
VMEM_LIMIT = 60 * 1024 * 1024
MASK_VALUE = -0.7 * float(np.finfo(np.float32).max)
LOG2_E = float(np.log2(np.e))

ROW_TILE = 256
GMLP_TILE = 512
TQ = 256
KB = 256
NKB = SEQ // KB
COUNT_ROWS = 32
MAX_SEARCH_ITERS = 64
MAX_BRACKET = 4
EXPERT_TILE = 256
N_EXPERT_TILES = (2 * SEQ) // EXPERT_TILE + N_EXPERTS
PAD_BITS = 8


def _dot(a, b):
    return jnp.dot(a, b, preferred_element_type=F32)


def _dot_nt(a, b):
    return lax.dot_general(a, b, (((1,), (1,)), ((), ())), preferred_element_type=F32)


def _resident(shape):
    zeros = (0,) * len(shape)
    return pl.BlockSpec(shape, lambda *_: zeros, pipeline_mode=pl.Buffered(1))


def _params(n_axes=1):
    return pltpu.CompilerParams(
        dimension_semantics=("arbitrary",) * n_axes, vmem_limit_bytes=VMEM_LIMIT)


def _proj_kernel(x_ref, g1_ref, wqT_ref, wc_ref, wqiT_ref, wkk_ref, wwT_ref, kvg_ref,
                 n1_ref, qT_ref, ckv_ref, ckvT_ref, qiT_ref, kk2_ref, wiT_ref):
    x = x_ref[...]
    ms = jnp.mean(x * x, axis=-1, keepdims=True)
    n1 = (x * lax.rsqrt(ms + EPS) * g1_ref[...]).astype(BF16)
    n1_ref[...] = n1
    qT_ref[...] = _dot_nt(wqT_ref[...], n1).astype(BF16)
    qiT_ref[...] = (_dot_nt(wqiT_ref[...], n1) * (IDX_DIM ** -0.5)).astype(BF16)
    wiT_ref[...] = _dot_nt(wwT_ref[...], n1) * (IDX_HEADS ** -0.5)
    c = _dot(n1, wc_ref[...])
    c = c * lax.rsqrt(jnp.mean(c * c, axis=-1, keepdims=True) + EPS) * kvg_ref[...]
    ckv_ref[...] = c.astype(BF16)
    ckvT_ref[0] = c.T.astype(BF16)
    kk = _dot(n1, wkk_ref[...]).astype(BF16)
    kk2_ref[0, 0:KB, :] = kk[:, 0:LANES]
    kk2_ref[0, KB:2 * KB, :] = kk[:, LANES:2 * LANES]


def _proj(x2, g1, wqT, wc, wqiT, wkk, wwT, kvg):
    tm = KB
    row = lambda w: pl.BlockSpec((tm, w), lambda i: (i, 0))
    col = lambda h: pl.BlockSpec((h, tm), lambda i: (0, i))
    return pl.pallas_call(
        _proj_kernel,
        grid=(SEQ // tm,),
        in_specs=[row(D_MODEL), _resident(g1.shape), _resident(wqT.shape), _resident(wc.shape),
                  _resident(wqiT.shape), _resident(wkk.shape), _resident(wwT.shape),
                  _resident(kvg.shape)],
        out_specs=[row(D_MODEL), col(ATTN_WIDTH), row(KV_LATENT),
                   pl.BlockSpec((1, KV_LATENT, tm), lambda i: (i // 2, 0, i % 2)),
                   col(IDX_HEADS * IDX_DIM),
                   pl.BlockSpec((1, 2 * tm, LANES), lambda i: (i, 0, 0)),
                   col(IDX_HEADS)],
        out_shape=[
            jax.ShapeDtypeStruct((SEQ, D_MODEL), BF16),
            jax.ShapeDtypeStruct((ATTN_WIDTH, SEQ), BF16),
            jax.ShapeDtypeStruct((SEQ, KV_LATENT), BF16),
            jax.ShapeDtypeStruct((NKB // 2, KV_LATENT, 2 * KB), BF16),
            jax.ShapeDtypeStruct((IDX_HEADS * IDX_DIM, SEQ), BF16),
            jax.ShapeDtypeStruct((NKB, 2 * KB, LANES), BF16),
            jax.ShapeDtypeStruct((IDX_HEADS, SEQ), F32),
        ],
        compiler_params=_params(),
        name="proj",
    )(x2, g1, wqT, wc, wqiT, wkk, wwT, kvg)


def _gmlp_kernel(n1_ref, wuv_ref, ws_ref, bsT_ref, lng_ref, lnb_ref, m_ref):
    uv = _dot(n1_ref[...], wuv_ref[...])
    z = jax.nn.gelu(uv)
    u = z[:, :GMLP_WIDTH]
    v = z[:, GMLP_WIDTH:]
    mu = jnp.mean(v, axis=-1, keepdims=True)
    var = jnp.mean(jnp.square(v - mu), axis=-1, keepdims=True)
    vn = ((v - mu) * lax.rsqrt(var + EPS) * lng_ref[...] + lnb_ref[...]).astype(BF16)
    t_pos = lax.broadcasted_iota(jnp.int32, (CHUNK, CHUNK), 0)
    s_pos = lax.broadcasted_iota(jnp.int32, (CHUNK, CHUNK), 1)
    causal = s_pos <= t_pos
    for g in range(GMLP_GROUPS):
        wm = jnp.where(causal, ws_ref[g], 0.0).astype(BF16)
        bias = bsT_ref[:, g:g + 1]
        cols = slice(g * LANES, (g + 1) * LANES)
        for c in range(GMLP_TILE // CHUNK):
            rows = slice(c * CHUNK, (c + 1) * CHUNK)
            y = _dot(wm, vn[rows, cols]) + bias
            m_ref[rows, cols] = (u[rows, cols] * y).astype(BF16)


def _gmlp(n1, wuv, ws, bsT, lng, lnb):
    tm = GMLP_TILE
    return pl.pallas_call(
        _gmlp_kernel,
        grid=(SEQ // tm,),
        in_specs=[pl.BlockSpec((tm, D_MODEL), lambda i: (i, 0)), _resident(wuv.shape),
                  _resident(ws.shape), _resident(bsT.shape), _resident(lng.shape),
                  _resident(lnb.shape)],
        out_specs=pl.BlockSpec((tm, GMLP_WIDTH), lambda i: (i, 0)),
        out_shape=jax.ShapeDtypeStruct((SEQ, GMLP_WIDTH), BF16),
        compiler_params=_params(),
        name="gmlp",
    )(n1, wuv, ws, bsT, lng, lnb)


def _attn_kernel(qT_ref, qiT_ref, wiT_ref, kk2_ref, ckv_ref, ckvT_ref, wukT_ref, wuvT_ref,
                 a_ref,
                 sc_ref, qabs_ref, thr_ref, keep_ref, m_ref, l_ref, acc_ref, lg_ref, bmax_ref):
    i = pl.program_id(0)
    nkb = i + 1

    for h in range(N_HEADS):
        qa = _dot(wukT_ref[h], qT_ref[h * HEAD_DIM:(h + 1) * HEAD_DIM, :]) * (
            HEAD_DIM ** -0.5 * LOG2_E)
        qabs_ref[h] = qa.astype(BF16)

    q_pos = i * TQ + lax.broadcasted_iota(jnp.int32, (KB, TQ), 1)
    k_off = lax.broadcasted_iota(jnp.int32, (KB, TQ), 0)

    def score_body(kb, carry):
        smax, smin = carry
        keys = kk2_ref[kb]
        acc = jnp.zeros((KB, TQ), F32)
        for j in range(IDX_HEADS // 2):
            d = _dot(keys, qiT_ref[j * LANES:(j + 1) * LANES, :])
            acc = acc + jnp.maximum(d[0:KB], 0.0) * wiT_ref[2 * j:2 * j + 1, :]
            acc = acc + jnp.maximum(d[KB:2 * KB], 0.0) * wiT_ref[2 * j + 1:2 * j + 2, :]
        causal = (kb * KB + k_off) <= q_pos
        sc_ref[kb] = jnp.where(causal, acc, -jnp.inf)
        smax = jnp.maximum(smax, jnp.max(jnp.where(causal, acc, -jnp.inf), axis=0, keepdims=True))
        smin = jnp.minimum(smin, jnp.min(jnp.where(causal, acc, jnp.inf), axis=0, keepdims=True))
        return smax, smin

    n_pairs = (nkb + 1) // 2
    smax, smin = lax.fori_loop(
        0, n_pairs, lambda j, c: score_body(2 * j + 1, score_body(2 * j, c)),
        (jnp.full((1, TQ), -jnp.inf, F32), jnp.full((1, TQ), jnp.inf, F32)))

    def count_ge(x):
        def body(j, cnt):
            for kb in (2 * j, 2 * j + 1):
                ge = jnp.where(sc_ref[kb] >= x, 1.0, 0.0)
                cnt = cnt + jnp.sum(ge.reshape(KB // COUNT_ROWS, COUNT_ROWS, TQ), axis=0)
            return cnt
        cnt = lax.fori_loop(0, n_pairs, body, jnp.zeros((COUNT_ROWS, TQ), F32))
        return jnp.sum(cnt, axis=0, keepdims=True)

    def max_below(x):
        def body(j, best):
            for kb in (2 * j, 2 * j + 1):
                s = sc_ref[kb]
                v = jnp.where(s < x, s, -jnp.inf)
                best = jnp.maximum(best, jnp.max(v.reshape(KB // COUNT_ROWS, COUNT_ROWS, TQ), axis=0))
            return best
        best = lax.fori_loop(0, n_pairs, body, jnp.full((COUNT_ROWS, TQ), -jnp.inf, F32))
        return jnp.max(best, axis=0, keepdims=True)

    n_causal = (i * TQ + 1 + lax.broadcasted_iota(jnp.int32, (1, TQ), 1)).astype(F32)
    want = jnp.minimum(n_causal, float(TOPK))
    hi0 = smax + jnp.maximum(jnp.abs(smax), 1e-30) * 1e-6
    open0 = (n_causal != want).astype(F32)

    def search_cond(state):
        return jnp.logical_and(state[0] < MAX_SEARCH_ITERS, state[-1] > 0.0)

    def search_body(state):
        it, lo, hi, c_lo, c_hi, open_, _ = state
        mid = 0.5 * lo + 0.5 * hi
        c = count_ge(mid)
        live = jnp.logical_and(open_ > 0.0, jnp.logical_and(mid > lo, mid < hi))
        ge = c >= want
        go_lo = jnp.logical_and(live, ge)
        go_hi = jnp.logical_and(live, jnp.logical_not(ge))
        lo = jnp.where(go_lo, mid, lo)
        c_lo = jnp.where(go_lo, c, c_lo)
        hi = jnp.where(go_hi, mid, hi)
        c_hi = jnp.where(go_hi, c, c_hi)
        wide = jnp.logical_and(c_lo != want, c_lo - c_hi > MAX_BRACKET)
        open_ = jnp.logical_and(live, wide).astype(F32)
        return it + 1, lo, hi, c_lo, c_hi, open_, jnp.max(open_)

    _, lo_f, hi_f, c_lo_f, c_hi_f, _, _ = lax.while_loop(
        search_cond, search_body,
        (jnp.int32(0), smin, hi0, n_causal, jnp.zeros((1, TQ), F32), open0, jnp.max(open0)))

    stepping = c_lo_f != want

    def step_cond(state):
        return jnp.logical_and(state[0] < MAX_BRACKET, state[-1] > 0.0)

    def step_body(state):
        it, cur, c, _ = state
        act = jnp.logical_and(stepping, c < want)
        cur = jnp.where(act, max_below(cur), cur)
        c = jnp.where(act, c + 1.0, c)
        return it + 1, cur, c, jnp.max(jnp.logical_and(stepping, c < want).astype(F32))

    _, cur_f, _, _ = lax.while_loop(
        step_cond, step_body,
        (jnp.int32(0), hi_f, c_hi_f, jnp.max(stepping.astype(F32))))
    thr0 = jnp.where(stepping, cur_f, lo_f)
    thr_ref[...] = jnp.broadcast_to(thr0, (SUBLANES, TQ))
    keep_ref[...] = jnp.full((SUBLANES, TQ), float(SEQ), F32)
    unresolved0 = (count_ge(thr0) != want).astype(F32)

    @pl.when(jnp.max(unresolved0) > 0.0)
    def _():
        def next_value(lo, below):
            def body(kb, u):
                s = sc_ref[kb]
                cand = jnp.logical_and(s >= lo, s > below)
                return jnp.minimum(u, jnp.min(jnp.where(cand, s, jnp.inf), axis=0, keepdims=True))
            return lax.fori_loop(0, nkb, body, jnp.full((1, TQ), jnp.inf, F32))

        def count_gt(x):
            def body(kb, cnt):
                gt = jnp.where(sc_ref[kb] > x, 1.0, 0.0)
                return cnt + jnp.sum(gt, axis=0, keepdims=True)
            return lax.fori_loop(0, nkb, body, jnp.zeros((1, TQ), F32))

        def peel_cond(state):
            return state[-1] > 0.0

        def peel_body(state):
            below, unres, thr, keep, _ = state
            u = next_value(lo_f, below)
            c_gt = count_gt(u)
            hit = jnp.logical_and(unres > 0.0, c_gt < want)
            thr = jnp.where(hit, u, thr)
            keep = jnp.where(hit, want - c_gt, keep)
            unres = jnp.logical_and(unres > 0.0, jnp.logical_not(hit)).astype(F32)
            return u, unres, thr, keep, jnp.max(unres)

        _, _, thr_t, keep_t, _ = lax.while_loop(
            peel_cond, peel_body,
            (jnp.full((1, TQ), -jnp.inf, F32), unresolved0, thr0,
             jnp.full((1, TQ), float(SEQ), F32), jnp.float32(1.0)))
        thr_ref[...] = jnp.broadcast_to(thr_t, (SUBLANES, TQ))
        keep_ref[...] = jnp.broadcast_to(keep_t, (SUBLANES, TQ))

        r_i = lax.broadcasted_iota(jnp.int32, (KB, KB), 0)
        c_i = lax.broadcasted_iota(jnp.int32, (KB, KB), 1)
        before = (c_i < r_i).astype(BF16)

        def drop_body(kb, seen):
            s = sc_ref[kb]
            eq = jnp.logical_and(s == thr_t, unresolved0 > 0.0)
            eq_f = eq.astype(F32)
            rank = seen + _dot(before, eq_f.astype(BF16))
            sc_ref[kb] = jnp.where(jnp.logical_and(eq, rank >= keep_t), -jnp.inf, s)
            return seen + jnp.sum(eq_f, axis=0, keepdims=True)

        lax.fori_loop(0, nkb, drop_body, jnp.zeros((1, TQ), F32))

    m_ref[...] = jnp.full(m_ref.shape, MASK_VALUE, F32)
    l_ref[...] = jnp.zeros(l_ref.shape, F32)
    acc_ref[...] = jnp.zeros(acc_ref.shape, F32)

    def logits_stage(j, slot):
        scores = sc_ref[pl.ds(2 * j, 2)].reshape(2 * KB, TQ)
        bias = jnp.where(scores >= thr_ref[0:1, :], 0.0, MASK_VALUE)
        c_n = ckv_ref[pl.ds(pl.multiple_of(j * (2 * KB), 2 * KB), 2 * KB), :]
        for h in range(N_HEADS):
            lg = _dot(c_n, qabs_ref[h]) + bias
            lg_ref[slot, h] = lg
            bmax_ref[slot, h] = jnp.broadcast_to(jnp.max(lg, axis=0, keepdims=True), (SUBLANES, TQ))

    def softmax_stage(j, slot):
        c_t = ckvT_ref[j]
        for h in range(N_HEADS):
            m_old = m_ref[h, 0:1, :]
            m_new = jnp.maximum(m_old, bmax_ref[slot, h, 0:1, :])
            alpha = jnp.exp2(m_old - m_new)
            p = jnp.exp2(lg_ref[slot, h] - m_new)
            l_new = alpha * l_ref[h, 0:1, :] + jnp.sum(p, axis=0, keepdims=True)
            acc_ref[h] = acc_ref[h] * alpha + _dot(c_t, p.astype(BF16))
            m_ref[h] = jnp.broadcast_to(m_new, (SUBLANES, TQ))
            l_ref[h] = jnp.broadcast_to(l_new, (SUBLANES, TQ))

    logits_stage(0, 0)

    def att_body(j, carry):
        for slot in range(2):
            @pl.when(j % 2 == slot)
            def _():
                logits_stage(j + 1, 1 - slot)
                softmax_stage(j, slot)
        return carry

    lax.fori_loop(0, n_pairs - 1, att_body, 0)
    for slot in range(2):
        @pl.when((n_pairs - 1) % 2 == slot)
        def _():
            softmax_stage(n_pairs - 1, slot)

    for h in range(N_HEADS):
        o_t = (acc_ref[h] / l_ref[h, 0:1, :]).astype(BF16)
        a_t = _dot(wuvT_ref[h], o_t)
        a_ref[:, h * HEAD_DIM:(h + 1) * HEAD_DIM] = a_t.T.astype(BF16)


def _attn(qT, qiT, wiT, kk2, ckv, ckvT, wukT, wuvT):
    col = lambda h: pl.BlockSpec((h, TQ), lambda i: (0, i))
    return pl.pallas_call(
        _attn_kernel,
        grid=(SEQ // TQ,),
        in_specs=[col(ATTN_WIDTH), col(IDX_HEADS * IDX_DIM), col(IDX_HEADS), _resident(kk2.shape),
                  _resident(ckv.shape), _resident(ckvT.shape), _resident(wukT.shape),
                  _resident(wuvT.shape)],
        out_specs=pl.BlockSpec((TQ, ATTN_WIDTH), lambda i: (i, 0)),
        out_shape=jax.ShapeDtypeStruct((SEQ, ATTN_WIDTH), BF16),
        scratch_shapes=[
            pltpu.VMEM((NKB, KB, TQ), F32),
            pltpu.VMEM((N_HEADS, KV_LATENT, TQ), BF16),
            pltpu.VMEM((SUBLANES, TQ), F32),
            pltpu.VMEM((SUBLANES, TQ), F32),
            pltpu.VMEM((N_HEADS, SUBLANES, TQ), F32),
            pltpu.VMEM((N_HEADS, SUBLANES, TQ), F32),
            pltpu.VMEM((N_HEADS, KV_LATENT, TQ), F32),
            pltpu.VMEM((2, N_HEADS, 2 * KB, TQ), F32),
            pltpu.VMEM((2, N_HEADS, SUBLANES, TQ), F32),
        ],
        compiler_params=_params(),
        name="attn",
    )(qT, qiT, wiT, kk2, ckv, ckvT, wukT, wuvT)


def _merge_kernel(n1_ref, a_ref, m_ref, wga_ref, wgb_ref, wba_ref, wbg_ref, o_ref):
    n1 = n1_ref[...]
    br_a = jax.nn.sigmoid(_dot(n1, wga_ref[...])) * _dot(a_ref[...], wba_ref[...])
    br_b = jax.nn.sigmoid(_dot(n1, wgb_ref[...])) * _dot(m_ref[...], wbg_ref[...])
    o_ref[...] = (br_a + br_b).astype(BF16)


def _merge(n1, a, m, wga, wgb, wba, wbg):
    tm = ROW_TILE
    row = lambda w: pl.BlockSpec((tm, w), lambda i: (i, 0))
    return pl.pallas_call(
        _merge_kernel,
        grid=(SEQ // tm,),
        in_specs=[row(D_MODEL), row(ATTN_WIDTH), row(GMLP_WIDTH), _resident(wga.shape),
                  _resident(wgb.shape), _resident(wba.shape), _resident(wbg.shape)],
        out_specs=row(D_MODEL),
        out_shape=jax.ShapeDtypeStruct((SEQ, D_MODEL), BF16),
        compiler_params=_params(),
        name="merge",
    )(n1, a, m, wga, wgb, wba, wbg)


ROUTE_ROWS = 8


def _outproj_kernel(mg_ref, x_ref, wo_ref, g2_ref, wrc_ref, br_ref,
                    h_ref, n2_ref, route_ref, counts_ref, carry_ref):
    i = pl.program_id(0)
    tm = ROW_TILE

    @pl.when(i == 0)
    def _():
        carry_ref[...] = jnp.zeros(carry_ref.shape, F32)

    h = x_ref[...] + _dot(mg_ref[...], wo_ref[...])
    h_ref[...] = h
    n2 = h * lax.rsqrt(jnp.mean(h * h, axis=-1, keepdims=True) + EPS) * g2_ref[...]
    n2_ref[...] = n2
    n2_hi = n2.astype(BF16)
    n2_lo = (n2 - n2_hi.astype(F32)).astype(BF16)
    hh_hl = _dot(n2_hi, wrc_ref[...])
    lh = _dot(n2_lo, wrc_ref[:, 0:LANES])
    logits = (hh_hl[:, 0:LANES] + hh_hl[:, LANES:2 * LANES] + lh).T + br_ref[...]
    row = lax.broadcasted_iota(jnp.int32, (LANES, tm), 0).astype(F32)
    is_group = row < N_GROUPS
    gl = jnp.where(is_group, logits, -jnp.inf)
    gmax = jnp.max(gl, axis=0, keepdims=True)
    gsum = jnp.sum(jnp.where(is_group, jnp.exp(logits - gmax), 0.0), axis=0, keepdims=True)
    g_val = 1.0 / gsum
    g_idx = jnp.min(jnp.where(gl == gmax, row, float(LANES)), axis=0, keepdims=True)
    e_id = row - N_GROUPS
    in_group = jnp.logical_and(
        jnp.logical_and(e_id >= 0, e_id < N_EXPERTS),
        jnp.floor(e_id * (1.0 / EXPERTS_PER_GROUP)) == g_idx)
    sel = jnp.where(in_group, logits, -jnp.inf)
    v1 = jnp.max(sel, axis=0, keepdims=True)
    i1 = jnp.min(jnp.where(sel == v1, row, float(LANES)), axis=0, keepdims=True)
    sel2 = jnp.where(row == i1, -jnp.inf, sel)
    v2 = jnp.max(sel2, axis=0, keepdims=True)
    i2 = jnp.min(jnp.where(sel2 == v2, row, float(LANES)), axis=0, keepdims=True)
    x2 = jnp.exp(v2 - v1)
    den = 1.0 + x2
    w1 = g_val * (1.0 / den)
    w2 = g_val * (x2 / den)
    e1 = i1 - N_GROUPS
    e2 = i2 - N_GROUPS

    e_row = lax.broadcasted_iota(jnp.int32, (N_EXPERTS, tm), 0).astype(F32)
    hit1 = (e_row == e1).astype(F32)
    hit2 = (e_row == e2).astype(F32)
    hits = hit1 + hit2
    t_from = lax.broadcasted_iota(jnp.int32, (tm, tm), 0)
    t_to = lax.broadcasted_iota(jnp.int32, (tm, tm), 1)
    earlier = (t_from < t_to).astype(BF16)
    before = carry_ref[:, 0:1] + _dot(hits.astype(BF16), earlier)
    rank1 = jnp.sum(hit1 * before, axis=0, keepdims=True)
    rank2 = jnp.sum(hit2 * before, axis=0, keepdims=True)
    carry = carry_ref[...] + jnp.sum(hits, axis=1, keepdims=True)
    carry_ref[...] = carry
    counts_ref[...] = carry

    r = lax.broadcasted_iota(jnp.int32, (ROUTE_ROWS, tm), 0)
    route = jnp.where(r == 0, e1, 0.0)
    for k, v in enumerate((e2, w1, w2, rank1, rank2), start=1):
        route = jnp.where(r == k, v, route)
    route_ref[...] = route


def _outproj(mg, x2, wo, g2, wr_cat, br):
    tm = ROW_TILE
    row = lambda w: pl.BlockSpec((tm, w), lambda i: (i, 0))
    return pl.pallas_call(
        _outproj_kernel,
        grid=(SEQ // tm,),
        in_specs=[row(D_MODEL), row(D_MODEL), _resident(wo.shape), _resident(g2.shape),
                  _resident(wr_cat.shape), _resident(br.shape)],
        out_specs=[row(D_MODEL), row(D_MODEL),
                   pl.BlockSpec((ROUTE_ROWS, tm), lambda i: (0, i)),
                   pl.BlockSpec((N_EXPERTS, LANES), lambda i: (0, 0))],
        out_shape=[jax.ShapeDtypeStruct((SEQ, D_MODEL), F32),
                   jax.ShapeDtypeStruct((SEQ, D_MODEL), F32),
                   jax.ShapeDtypeStruct((ROUTE_ROWS, SEQ), F32),
                   jax.ShapeDtypeStruct((N_EXPERTS, LANES), F32)],
        scratch_shapes=[pltpu.VMEM((N_EXPERTS, LANES), F32)],
        compiler_params=_params(),
        name="outproj",
    )(mg, x2, wo, g2, wr_cat, br)


def _start_row_gather(src_hbm, row_of, dst, slot, sem, rows):
    for r in rows:
        pltpu.make_async_copy(src_hbm.at[pl.ds(row_of(r), 1), :],
                              dst.at[slot, pl.ds(r, 1), :], sem.at[slot]).start()


def _wait_row_gather(src_hbm, dst, slot, sem, n_rows):
    pltpu.make_async_copy(src_hbm.at[pl.ds(0, n_rows), :], dst.at[slot], sem.at[slot]).wait()


def _scatter_kernel(padstart_ref, padlen_ref, pos_ref, x_ref, xs_hbm, xbuf, zbuf, sem, zsem):
    i = pl.program_id(0)
    n = pl.num_programs(0)
    tm = ROW_TILE
    slot = i % 2

    def row_copies_done(s):
        for _ in range(2):
            pltpu.make_async_copy(xbuf.at[s], xs_hbm.at[pl.ds(0, tm), :], sem.at[s]).wait()

    def pad_copies(e, bit):
        rows = 1 << bit
        start = padstart_ref[e] + (padlen_ref[e] & (rows - 1))
        if rows < SUBLANES:
            return [pltpu.make_async_copy(zbuf.at[pl.ds(0, 1), :], xs_hbm.at[pl.ds(start + k, 1), :],
                                          zsem.at[0]) for k in range(rows)]
        return [pltpu.make_async_copy(zbuf.at[pl.ds(0, rows), :],
                                      xs_hbm.at[pl.ds(pl.multiple_of(start, rows), rows), :],
                                      zsem.at[0])]

    def for_each_pad_piece(fn):
        def body(e, carry):
            for bit in range(PAD_BITS):
                @pl.when((lax.shift_right_logical(padlen_ref[e], bit) & 1) == 1)
                def _():
                    for c in pad_copies(e, bit):
                        fn(c)
            return carry
        lax.fori_loop(0, N_EXPERTS, body, 0)

    @pl.when(i == 0)
    def _():
        zbuf[...] = jnp.zeros(zbuf.shape, F32)
        for_each_pad_piece(lambda c: c.start())

    @pl.when(i >= 2)
    def _():
        row_copies_done(slot)

    xbuf[slot] = x_ref[...]
    for r in range(2 * tm):
        pltpu.make_async_copy(xbuf.at[slot, pl.ds(r % tm, 1), :],
                              xs_hbm.at[pl.ds(pos_ref[0, 0, r], 1), :], sem.at[slot]).start()

    @pl.when(i == n - 1)
    def _():
        row_copies_done(1 - slot)
        row_copies_done(slot)
        for_each_pad_piece(lambda c: c.wait())


def _scatter_rows(padstart, padlen, pos_tiles, n2):
    tm = ROW_TILE
    grid_spec = pltpu.PrefetchScalarGridSpec(
        num_scalar_prefetch=2,
        grid=(SEQ // tm,),
        in_specs=[pl.BlockSpec((1, 1, 2 * tm), lambda i, *_: (i, 0, 0), memory_space=pltpu.SMEM),
                  pl.BlockSpec((tm, D_MODEL), lambda i, *_: (i, 0))],
        out_specs=pl.BlockSpec(memory_space=pl.ANY),
        scratch_shapes=[pltpu.VMEM((2, tm, D_MODEL), F32),
                        pltpu.VMEM((1 << (PAD_BITS - 1), D_MODEL), F32),
                        pltpu.SemaphoreType.DMA((2,)),
                        pltpu.SemaphoreType.DMA((1,))],
    )
    return pl.pallas_call(
        _scatter_kernel,
        grid_spec=grid_spec,
        out_shape=jax.ShapeDtypeStruct((N_EXPERT_TILES * EXPERT_TILE, D_MODEL), F32),
        compiler_params=_params(),
        name="scatter_rows",
    )(padstart, padlen, pos_tiles, n2)


def _expert_kernel(ntl_ref, tst_ref, ntot_ref,
                   xs_hbm, wg_hbm, wu_hbm, wd_hbm,
                   y_hbm,
                   xbuf, ybuf, xsem, ysem, wg_st, wu_st, wd_st, wsem, wgb, wub, wdb):
    e = pl.program_id(0)
    ntot = ntot_ref[0]

    def weight_copies(ex, slot):
        return [pltpu.make_async_copy(src.at[ex], dst.at[slot], wsem.at[slot, j])
                for j, (src, dst) in enumerate(((wg_hbm, wg_st), (wu_hbm, wu_st), (wd_hbm, wd_st)))]

    def tile_rows(t):
        return pl.ds(pl.multiple_of(t * EXPERT_TILE, EXPERT_TILE), EXPERT_TILE)

    def x_copy(t, slot):
        return pltpu.make_async_copy(xs_hbm.at[tile_rows(t), :], xbuf.at[slot], xsem.at[slot])

    def y_copy(t, slot):
        return pltpu.make_async_copy(ybuf.at[slot], y_hbm.at[tile_rows(t), :], ysem.at[slot])

    @pl.when(e == 0)
    def _():
        for c in weight_copies(0, 0):
            c.start()
        x_copy(0, 0).start()

    @pl.when(e + 1 < pl.num_programs(0))
    def _():
        for c in weight_copies(e + 1, (e + 1) % 2):
            c.start()

    n_here = ntl_ref[e]
    wslot = e % 2
    for c in weight_copies(e, wslot):
        c.wait()

    @pl.when(n_here > 0)
    def _():
        wgb[...] = wg_st[wslot].astype(BF16)
        wub[...] = wu_st[wslot].astype(BF16)
        wdb[...] = wd_st[wslot].astype(BF16)

        def tile_body(k, carry):
            t = tst_ref[e] + k
            slot = t % 2
            x_copy(t, slot).wait()

            @pl.when(t + 1 < ntot)
            def _():
                x_copy(t + 1, 1 - slot).start()

            xt = xbuf[slot].astype(BF16)
            hid = jax.nn.silu(_dot(xt, wgb[...])) * _dot(xt, wub[...])
            y = _dot(hid.astype(BF16), wdb[...])

            @pl.when(t >= 2)
            def _():
                y_copy(t - 2, slot).wait()

            ybuf[slot] = y
            y_copy(t, slot).start()
            return carry
        lax.fori_loop(0, n_here, tile_body, 0)

    @pl.when(e == pl.num_programs(0) - 1)
    def _():
        @pl.when(ntot >= 2)
        def _():
            y_copy(ntot - 2, (ntot - 2) % 2).wait()
        y_copy(ntot - 1, (ntot - 1) % 2).wait()


def _experts(ntl, tst, ntot, x_sorted, w_gate, w_up, w_down):
    hbm = pl.BlockSpec(memory_space=pl.ANY)
    grid_spec = pltpu.PrefetchScalarGridSpec(
        num_scalar_prefetch=3,
        grid=(N_EXPERTS,),
        in_specs=[hbm, hbm, hbm, hbm],
        out_specs=hbm,
        scratch_shapes=[
            pltpu.VMEM((2, EXPERT_TILE, D_MODEL), F32),
            pltpu.VMEM((2, EXPERT_TILE, D_MODEL), F32),
            pltpu.SemaphoreType.DMA((2,)),
            pltpu.SemaphoreType.DMA((2,)),
            pltpu.VMEM((2, D_MODEL, EXPERT_FF), F32),
            pltpu.VMEM((2, D_MODEL, EXPERT_FF), F32),
            pltpu.VMEM((2, EXPERT_FF, D_MODEL), F32),
            pltpu.SemaphoreType.DMA((2, 3)),
            pltpu.VMEM((D_MODEL, EXPERT_FF), BF16),
            pltpu.VMEM((D_MODEL, EXPERT_FF), BF16),
            pltpu.VMEM((EXPERT_FF, D_MODEL), BF16),
        ],
    )
    return pl.pallas_call(
        _expert_kernel,
        grid_spec=grid_spec,
        out_shape=jax.ShapeDtypeStruct((N_EXPERT_TILES * EXPERT_TILE, D_MODEL), F32),
        compiler_params=_params(),
        name="experts",
    )(ntl, tst, ntot, x_sorted, w_gate, w_up, w_down)


def _final_kernel(pos_ref, posn_ref, h_ref, route_ref, gf_ref, y_hbm, o_ref, ybuf, sem):
    i = pl.program_id(0)
    n = pl.num_programs(0)
    tm = ROW_TILE

    @pl.when(i == 0)
    def _():
        _start_row_gather(y_hbm, lambda r: pos_ref[0, 0, r], ybuf, 0, sem, range(2 * tm))

    @pl.when(i + 1 < n)
    def _():
        _start_row_gather(y_hbm, lambda r: posn_ref[0, 0, r], ybuf, (i + 1) % 2, sem,
                          range(2 * tm))

    slot = i % 2
    _wait_row_gather(y_hbm, ybuf, slot, sem, 2 * tm)
    w1 = route_ref[:, 0:1]
    w2 = route_ref[:, 1:2]
    h2 = h_ref[...] + (w1 * ybuf[slot, 0:tm, :] + w2 * ybuf[slot, tm:2 * tm, :])
    o_ref[...] = h2 * lax.rsqrt(jnp.mean(h2 * h2, axis=-1, keepdims=True) + EPS) * gf_ref[...]


def _final(pos, h, route, gf, y_sorted):
    tm = ROW_TILE
    n = SEQ // tm
    row = lambda w: pl.BlockSpec((tm, w), lambda i: (i, 0))
    smem_blk = lambda f: pl.BlockSpec((1, 1, 2 * tm), f, memory_space=pltpu.SMEM)
    return pl.pallas_call(
        _final_kernel,
        grid=(n,),
        in_specs=[smem_blk(lambda i: (i, 0, 0)),
                  smem_blk(lambda i: (jnp.minimum(i + 1, n - 1), 0, 0)),
                  row(D_MODEL), row(2), _resident(gf.shape),
                  pl.BlockSpec(memory_space=pl.ANY)],
        out_specs=row(D_MODEL),
        out_shape=jax.ShapeDtypeStruct((SEQ, D_MODEL), F32),
        scratch_shapes=[pltpu.VMEM((2, 2 * tm, D_MODEL), F32), pltpu.SemaphoreType.DMA((2,))],
        compiler_params=_params(),
        name="final",
    )(pos, pos, h, route, gf, y_sorted)


def _dispatch(route, counts):
    i32 = jnp.int32
    e_pair = route[0:2].T.astype(i32).reshape(-1)
    rank = route[4:6].T.astype(i32).reshape(-1)
    cnt = counts[:, 0].astype(i32)
    ntl = (cnt + EXPERT_TILE - 1) // EXPERT_TILE
    tend = jnp.cumsum(ntl)
    tst = tend - ntl
    experts = jnp.arange(N_EXPERTS, dtype=i32)
    of_pair = e_pair[:, None] == experts[None, :]
    pos = jnp.sum(jnp.where(of_pair, (tst * EXPERT_TILE)[None, :], 0), axis=1) + rank
    padstart = tst * EXPERT_TILE + cnt
    padlen = ntl * EXPERT_TILE - cnt
    return ntl, tst, tend[-1:], padstart, padlen, pos


def kernel(x, norm1_g, w_in, kv_norm_g, w_uk, w_uv, gmlp_ws, gmlp_bs, ln_v_g, ln_v_b, w_br_attn,
           w_br_gmlp, w_out, norm2_g, w_group, b_group, w_router, b_router, w_e_gate, w_e_up,
           w_e_down, norm_f_g):
    assert x.shape == (1, SEQ, D_MODEL)
    x2 = x.reshape(SEQ, D_MODEL)
    row_vec = lambda v: v.reshape(1, -1).astype(F32)

    c_q = ATTN_WIDTH
    c_kv = c_q + KV_LATENT
    c_qi = c_kv + IDX_HEADS * IDX_DIM
    c_k = c_qi + IDX_DIM
    c_w = c_k + IDX_HEADS
    c_uv = c_w + 2 * GMLP_WIDTH
    w_bf = w_in.astype(BF16)
    wqT, wc, wqiT = w_bf[:, :c_q].T, w_bf[:, c_q:c_kv], w_bf[:, c_kv:c_qi].T
    wk, wwT = w_bf[:, c_qi:c_k], w_bf[:, c_k:c_w].T
    wuv_in, wga, wgb = w_bf[:, c_w:c_uv], w_bf[:, c_uv:c_uv + D_MODEL], w_bf[:, c_uv + D_MODEL:]
    zk = jnp.zeros_like(wk)
    wkk = jnp.concatenate([wk, zk, zk, wk], axis=1)

    n1, qT, ckv, ckvT, qiT, kk2, wiT = _proj(x2, row_vec(norm1_g), wqT, wc, wqiT, wkk, wwT,
                                             row_vec(kv_norm_g))
    m = _gmlp(n1, wuv_in, gmlp_ws, jnp.pad(gmlp_bs.T, ((0, 0), (0, LANES - GMLP_GROUPS))),
              row_vec(ln_v_g), row_vec(ln_v_b))
    a = _attn(qT, qiT, wiT, kk2, ckv, ckvT,
              jnp.swapaxes(w_uk, 1, 2).astype(BF16), jnp.swapaxes(w_uv, 1, 2).astype(BF16))
    mg = _merge(n1, a, m, wga, wgb, w_br_attn.astype(BF16), w_br_gmlp.astype(BF16))

    w_route = jnp.pad(jnp.concatenate([w_group, w_router], axis=1),
                      ((0, 0), (0, LANES - N_GROUPS - N_EXPERTS)))
    b_route = jnp.pad(jnp.concatenate([b_group, b_router]), (0, LANES - N_GROUPS - N_EXPERTS))
    wr_hi = w_route.astype(BF16)
    wr_lo = (w_route - wr_hi.astype(F32)).astype(BF16)
    h, n2, route, counts = _outproj(mg, x2, w_out.astype(BF16), row_vec(norm2_g),
                                    jnp.concatenate([wr_hi, wr_lo], axis=1),
                                    b_route.reshape(LANES, 1))

    ntl, tst, ntot, padstart, padlen, pos = _dispatch(route, counts)
    pos_tiles = pos.reshape(SEQ // ROW_TILE, ROW_TILE, 2).transpose(0, 2, 1).reshape(
        SEQ // ROW_TILE, 1, 2 * ROW_TILE)
    x_sorted = _scatter_rows(padstart, padlen, pos_tiles, n2)
    y_sorted = _experts(ntl, tst, ntot, x_sorted, w_e_gate, w_e_up, w_e_down)
    out = _final(pos_tiles, h, route[2:4].T, row_vec(norm_f_g), y_sorted)
    return out.reshape(1, SEQ, D_MODEL)
```

```python
import jax
import jax.numpy as jnp
import numpy as np
from jax import lax
from jax.experimental import pallas as pl
from jax.experimental.pallas import tpu as pltpu

F32 = jnp.float32
BF16 = jnp.bfloat16

D_MODEL = 2048
SEQ = 8192
N_HEADS = 8
HEAD_DIM = 128
KV_LATENT = 256
IDX_HEADS = 16
IDX_DIM = 64
TOPK = 256
ATTN_WIDTH = N_HEADS * HEAD_DIM
GMLP_GROUPS = 8
GMLP_WIDTH = 1024
CHUNK = 128
N_GROUPS = 8
EXPERTS_PER_GROUP = 8
N_EXPERTS = 64
EXPERT_FF = 512
EPS = 1e-6

LANES = 128
SUBLANES = 8
VMEM_LIMIT = 60 * 1024 * 1024
MASK_VALUE = -0.7 * float(np.finfo(np.float32).max)
LOG2_E = float(np.log2(np.e))

ROW_TILE = 256
GMLP_TILE = 512
TQ = 256
KB = 256
NKB = SEQ // KB
COUNT_ROWS = 32
MAX_SEARCH_ITERS = 64
MAX_BRACKET = 4
EXPERT_TILE = 256
N_EXPERT_TILES = (2 * SEQ) // EXPERT_TILE + N_EXPERTS
PAD_BITS = 8


def _dot(a, b):
    return jnp.dot(a, b, preferred_element_type=F32)


def _dot_nt(a, b):
    return lax.dot_general(a, b, (((1,), (1,)), ((), ())), preferred_element_type=F32)


def _resident(shape):
    zeros = (0,) * len(shape)
    return pl.BlockSpec(shape, lambda *_: zeros, pipeline_mode=pl.Buffered(1))


def _params(n_axes=1):
    return pltpu.CompilerParams(
        dimension_semantics=("arbitrary",) * n_axes, vmem_limit_bytes=VMEM_LIMIT)


C_Q = ATTN_WIDTH
C_KV = C_Q + KV_LATENT
C_QI = C_KV + IDX_HEADS * IDX_DIM
C_K = C_QI + IDX_DIM
C_W = C_K + IDX_HEADS
C_UV = C_W + 2 * GMLP_WIDTH
C_GA = C_UV + D_MODEL
IN_COLS = C_GA + D_MODEL


def _split_w_in_kernel(wt_ref, wqT_ref, wc_ref, wqiT_ref, wkk_ref, wwT_ref, wuv_ref, wga_ref, wgb_ref):
    wqT_ref[...] = wt_ref[0:C_Q, :].astype(BF16)
    wc_ref[...] = wt_ref[C_Q:C_KV, :].T.astype(BF16)
    wqiT_ref[...] = wt_ref[C_KV:C_QI, :].astype(BF16)
    k_t = wt_ref[C_QI:C_K, :].T.astype(BF16)
    zero = jnp.zeros_like(k_t)
    wkk_ref[...] = jnp.concatenate([k_t, zero, zero, k_t], axis=1)
    wwT_ref[...] = wt_ref[C_K:C_W, :].astype(BF16)
    wuv_ref[...] = wt_ref[C_W:C_UV, :].T.astype(BF16)
    wga_ref[...] = wt_ref[C_UV:C_GA, :].T.astype(BF16)
    wgb_ref[...] = wt_ref[C_GA:IN_COLS, :].T.astype(BF16)


def _split_w_in(w_in_t):
    tr = ROW_TILE
    rows = lambda w: pl.BlockSpec((tr, w), lambda i: (i, 0))
    cols = lambda h: pl.BlockSpec((h, tr), lambda i: (0, i))
    sds = jax.ShapeDtypeStruct
    return pl.pallas_call(
        _split_w_in_kernel,
        grid=(D_MODEL // tr,),
        in_specs=[cols(IN_COLS)],
        out_specs=[cols(C_Q), rows(KV_LATENT), cols(C_QI - C_KV), rows(2 * LANES), cols(IDX_HEADS),
                   rows(2 * GMLP_WIDTH), rows(D_MODEL), rows(D_MODEL)],
        out_shape=[sds((C_Q, D_MODEL), BF16), sds((D_MODEL, KV_LATENT), BF16),
                   sds((C_QI - C_KV, D_MODEL), BF16), sds((D_MODEL, 2 * LANES), BF16),
                   sds((IDX_HEADS, D_MODEL), BF16), sds((D_MODEL, 2 * GMLP_WIDTH), BF16),
                   sds((D_MODEL, D_MODEL), BF16), sds((D_MODEL, D_MODEL), BF16)],
        compiler_params=_params(),
        name="split_w_in",
    )(w_in_t)


def _proj_kernel(x_ref, g1_ref, wqT_ref, wc_ref, wqiT_ref, wkk_ref, wwT_ref, kvg_ref,
                 n1_ref, qT_ref, ckv_ref, ckvT_ref, qiT_ref, kk2_ref, wiT_ref):
    x = x_ref[...]
    ms = jnp.mean(x * x, axis=-1, keepdims=True)
    n1 = (x * lax.rsqrt(ms + EPS) * g1_ref[...]).astype(BF16)
    n1_ref[...] = n1
    qT_ref[...] = _dot_nt(wqT_ref[...], n1).astype(BF16)
    qiT_ref[...] = (_dot_nt(wqiT_ref[...], n1) * (IDX_DIM ** -0.5)).astype(BF16)
    wiT_ref[...] = _dot_nt(wwT_ref[...], n1) * (IDX_HEADS ** -0.5)
    c = _dot(n1, wc_ref[...])
    c = c * lax.rsqrt(jnp.mean(c * c, axis=-1, keepdims=True) + EPS) * kvg_ref[...]
    ckv_ref[...] = c.astype(BF16)
    ckvT_ref[0] = c.T.astype(BF16)
    kk = _dot(n1, wkk_ref[...]).astype(BF16)
    kk2_ref[0, 0:KB, :] = kk[:, 0:LANES]
    kk2_ref[0, KB:2 * KB, :] = kk[:, LANES:2 * LANES]


def _proj(x2, g1, wqT, wc, wqiT, wkk, wwT, kvg):
    tm = KB
    row = lambda w: pl.BlockSpec((tm, w), lambda i: (i, 0))
    col = lambda h: pl.BlockSpec((h, tm), lambda i: (0, i))
    return pl.pallas_call(
        _proj_kernel,
        grid=(SEQ // tm,),
        in_specs=[row(D_MODEL), _resident(g1.shape), _resident(wqT.shape), _resident(wc.shape),
                  _resident(wqiT.shape), _resident(wkk.shape), _resident(wwT.shape),
                  _resident(kvg.shape)],
        out_specs=[row(D_MODEL), col(ATTN_WIDTH), row(KV_LATENT),
                   pl.BlockSpec((1, KV_LATENT, tm), lambda i: (i // 2, 0, i % 2)),
                   col(IDX_HEADS * IDX_DIM),
                   pl.BlockSpec((1, 2 * tm, LANES), lambda i: (i, 0, 0)),
                   col(IDX_HEADS)],
        out_shape=[
            jax.ShapeDtypeStruct((SEQ, D_MODEL), BF16),
            jax.ShapeDtypeStruct((ATTN_WIDTH, SEQ), BF16),
            jax.ShapeDtypeStruct((SEQ, KV_LATENT), BF16),
            jax.ShapeDtypeStruct((NKB // 2, KV_LATENT, 2 * KB), BF16),
            jax.ShapeDtypeStruct((IDX_HEADS * IDX_DIM, SEQ), BF16),
            jax.ShapeDtypeStruct((NKB, 2 * KB, LANES), BF16),
            jax.ShapeDtypeStruct((IDX_HEADS, SEQ), F32),
        ],
        compiler_params=_params(),
        name="proj",
    )(x2, g1, wqT, wc, wqiT, wkk, wwT, kvg)


def _gmlp_kernel(n1_ref, wuv_ref, ws_ref, bsT_ref, lng_ref, lnb_ref, m_ref):
    uv = _dot(n1_ref[...], wuv_ref[...])
    z = jax.nn.gelu(uv)
    u = z[:, :GMLP_WIDTH]
    v = z[:, GMLP_WIDTH:]
    mu = jnp.mean(v, axis=-1, keepdims=True)
    var = jnp.mean(jnp.square(v - mu), axis=-1, keepdims=True)
    vn = ((v - mu) * lax.rsqrt(var + EPS) * lng_ref[...] + lnb_ref[...]).astype(BF16)
    t_pos = lax.broadcasted_iota(jnp.int32, (CHUNK, CHUNK), 0)
    s_pos = lax.broadcasted_iota(jnp.int32, (CHUNK, CHUNK), 1)
    causal = s_pos <= t_pos
    for g in range(GMLP_GROUPS):
        wm = jnp.where(causal, ws_ref[g], 0.0).astype(BF16)
        bias = bsT_ref[:, g:g + 1]
        cols = slice(g * LANES, (g + 1) * LANES)
        for c in range(GMLP_TILE // CHUNK):
            rows = slice(c * CHUNK, (c + 1) * CHUNK)
            y = _dot(wm, vn[rows, cols]) + bias
            m_ref[rows, cols] = (u[rows, cols] * y).astype(BF16)


def _gmlp(n1, wuv, ws, bsT, lng, lnb):
    tm = GMLP_TILE
    return pl.pallas_call(
        _gmlp_kernel,
        grid=(SEQ // tm,),
        in_specs=[pl.BlockSpec((tm, D_MODEL), lambda i: (i, 0)), _resident(wuv.shape),
                  _resident(ws.shape), _resident(bsT.shape), _resident(lng.shape),
                  _resident(lnb.shape)],
        out_specs=pl.BlockSpec((tm, GMLP_WIDTH), lambda i: (i, 0)),
        out_shape=jax.ShapeDtypeStruct((SEQ, GMLP_WIDTH), BF16),
        compiler_params=_params(),
        name="gmlp",
    )(n1, wuv, ws, bsT, lng, lnb)


def _attn_kernel(qT_ref, qiT_ref, wiT_ref, kk2_ref, ckv_ref, ckvT_ref, wukT_ref, wuvT_ref,
                 a_ref,
                 sc_ref, qabs_ref, thr_ref, keep_ref, m_ref, l_ref, acc_ref, lg_ref, bmax_ref):
    i = pl.program_id(0)
    nkb = i + 1

    for h in range(N_HEADS):
        qa = _dot(wukT_ref[h], qT_ref[h * HEAD_DIM:(h + 1) * HEAD_DIM, :]) * (
            HEAD_DIM ** -0.5 * LOG2_E)
        qabs_ref[h] = qa.astype(BF16)

    q_pos = i * TQ + lax.broadcasted_iota(jnp.int32, (KB, TQ), 1)
    k_off = lax.broadcasted_iota(jnp.int32, (KB, TQ), 0)

    def score_body(kb, carry):
        smax, smin = carry
        keys = kk2_ref[kb]
        acc = jnp.zeros((KB, TQ), F32)
        for j in range(IDX_HEADS // 2):
            d = _dot(keys, qiT_ref[j * LANES:(j + 1) * LANES, :])
            acc = acc + jnp.maximum(d[0:KB], 0.0) * wiT_ref[2 * j:2 * j + 1, :]
            acc = acc + jnp.maximum(d[KB:2 * KB], 0.0) * wiT_ref[2 * j + 1:2 * j + 2, :]
        causal = (kb * KB + k_off) <= q_pos
        sc_ref[kb] = jnp.where(causal, acc, -jnp.inf)
        smax = jnp.maximum(smax, jnp.max(jnp.where(causal, acc, -jnp.inf), axis=0, keepdims=True))
        smin = jnp.minimum(smin, jnp.min(jnp.where(causal, acc, jnp.inf), axis=0, keepdims=True))
        return smax, smin

    n_pairs = (nkb + 1) // 2
    smax, smin = lax.fori_loop(
        0, n_pairs, lambda j, c: score_body(2 * j + 1, score_body(2 * j, c)),
        (jnp.full((1, TQ), -jnp.inf, F32), jnp.full((1, TQ), jnp.inf, F32)))

    def count_ge(x):
        def body(j, cnt):
            for kb in (2 * j, 2 * j + 1):
                ge = jnp.where(sc_ref[kb] >= x, 1.0, 0.0)
                cnt = cnt + jnp.sum(ge.reshape(KB // COUNT_ROWS, COUNT_ROWS, TQ), axis=0)
            return cnt
        cnt = lax.fori_loop(0, n_pairs, body, jnp.zeros((COUNT_ROWS, TQ), F32))
        return jnp.sum(cnt, axis=0, keepdims=True)

    def max_below(x):
        def body(j, best):
            for kb in (2 * j, 2 * j + 1):
                s = sc_ref[kb]
                v = jnp.where(s < x, s, -jnp.inf)
                best = jnp.maximum(best, jnp.max(v.reshape(KB // COUNT_ROWS, COUNT_ROWS, TQ), axis=0))
            return best
        best = lax.fori_loop(0, n_pairs, body, jnp.full((COUNT_ROWS, TQ), -jnp.inf, F32))
        return jnp.max(best, axis=0, keepdims=True)

    n_causal = (i * TQ + 1 + lax.broadcasted_iota(jnp.int32, (1, TQ), 1)).astype(F32)
    want = jnp.minimum(n_causal, float(TOPK))
    hi0 = smax + jnp.maximum(jnp.abs(smax), 1e-30) * 1e-6
    open0 = (n_causal != want).astype(F32)

    def search_cond(state):
        return jnp.logical_and(state[0] < MAX_SEARCH_ITERS, state[-1] > 0.0)

    def search_body(state):
        it, lo, hi, c_lo, c_hi, open_, _ = state
        mid = 0.5 * lo + 0.5 * hi
        c = count_ge(mid)
        live = jnp.logical_and(open_ > 0.0, jnp.logical_and(mid > lo, mid < hi))
        ge = c >= want
        go_lo = jnp.logical_and(live, ge)
        go_hi = jnp.logical_and(live, jnp.logical_not(ge))
        lo = jnp.where(go_lo, mid, lo)
        c_lo = jnp.where(go_lo, c, c_lo)
        hi = jnp.where(go_hi, mid, hi)
        c_hi = jnp.where(go_hi, c, c_hi)
        wide = jnp.logical_and(c_lo != want, c_lo - c_hi > MAX_BRACKET)
        open_ = jnp.logical_and(live, wide).astype(F32)
        return it + 1, lo, hi, c_lo, c_hi, open_, jnp.max(open_)

    _, lo_f, hi_f, c_lo_f, c_hi_f, _, _ = lax.while_loop(
        search_cond, search_body,
        (jnp.int32(0), smin, hi0, n_causal, jnp.zeros((1, TQ), F32), open0, jnp.max(open0)))

    stepping = c_lo_f != want

    def step_cond(state):
        return jnp.logical_and(state[0] < MAX_BRACKET, state[-1] > 0.0)

    def step_body(state):
        it, cur, c, _ = state
        act = jnp.logical_and(stepping, c < want)
        cur = jnp.where(act, max_below(cur), cur)
        c = jnp.where(act, c + 1.0, c)
        return it + 1, cur, c, jnp.max(jnp.logical_and(stepping, c < want).astype(F32))

    _, cur_f, _, _ = lax.while_loop(
        step_cond, step_body,
        (jnp.int32(0), hi_f, c_hi_f, jnp.max(stepping.astype(F32))))
    thr0 = jnp.where(stepping, cur_f, lo_f)
    thr_ref[...] = jnp.broadcast_to(thr0, (SUBLANES, TQ))
    keep_ref[...] = jnp.full((SUBLANES, TQ), float(SEQ), F32)
    unresolved0 = (count_ge(thr0) != want).astype(F32)

    @pl.when(jnp.max(unresolved0) > 0.0)
    def _():
        def next_value(lo, below):
            def body(kb, u):
                s = sc_ref[kb]
                cand = jnp.logical_and(s >= lo, s > below)
                return jnp.minimum(u, jnp.min(jnp.where(cand, s, jnp.inf), axis=0, keepdims=True))
            return lax.fori_loop(0, nkb, body, jnp.full((1, TQ), jnp.inf, F32))

        def count_gt(x):
            def body(kb, cnt):
                gt = jnp.where(sc_ref[kb] > x, 1.0, 0.0)
                return cnt + jnp.sum(gt, axis=0, keepdims=True)
            return lax.fori_loop(0, nkb, body, jnp.zeros((1, TQ), F32))

        def peel_cond(state):
            return state[-1] > 0.0

        def peel_body(state):
            below, unres, thr, keep, _ = state
            u = next_value(lo_f, below)
            c_gt = count_gt(u)
            hit = jnp.logical_and(unres > 0.0, c_gt < want)
            thr = jnp.where(hit, u, thr)
            keep = jnp.where(hit, want - c_gt, keep)
            unres = jnp.logical_and(unres > 0.0, jnp.logical_not(hit)).astype(F32)
            return u, unres, thr, keep, jnp.max(unres)

        _, _, thr_t, keep_t, _ = lax.while_loop(
            peel_cond, peel_body,
            (jnp.full((1, TQ), -jnp.inf, F32), unresolved0, thr0,
             jnp.full((1, TQ), float(SEQ), F32), jnp.float32(1.0)))
        thr_ref[...] = jnp.broadcast_to(thr_t, (SUBLANES, TQ))
        keep_ref[...] = jnp.broadcast_to(keep_t, (SUBLANES, TQ))

        r_i = lax.broadcasted_iota(jnp.int32, (KB, KB), 0)
        c_i = lax.broadcasted_iota(jnp.int32, (KB, KB), 1)
        before = (c_i < r_i).astype(BF16)

        def drop_body(kb, seen):
            s = sc_ref[kb]
            eq = jnp.logical_and(s == thr_t, unresolved0 > 0.0)
            eq_f = eq.astype(F32)
            rank = seen + _dot(before, eq_f.astype(BF16))
            sc_ref[kb] = jnp.where(jnp.logical_and(eq, rank >= keep_t), -jnp.inf, s)
            return seen + jnp.sum(eq_f, axis=0, keepdims=True)

        lax.fori_loop(0, nkb, drop_body, jnp.zeros((1, TQ), F32))

    m_ref[...] = jnp.full(m_ref.shape, MASK_VALUE, F32)
    l_ref[...] = jnp.zeros(l_ref.shape, F32)
    acc_ref[...] = jnp.zeros(acc_ref.shape, F32)

    def logits_stage(j, slot):
        scores = sc_ref[pl.ds(2 * j, 2)].reshape(2 * KB, TQ)
        bias = jnp.where(scores >= thr_ref[0:1, :], 0.0, MASK_VALUE)
        c_n = ckv_ref[pl.ds(pl.multiple_of(j * (2 * KB), 2 * KB), 2 * KB), :]
        for h in range(N_HEADS):
            lg = _dot(c_n, qabs_ref[h]) + bias
            lg_ref[slot, h] = lg
            bmax_ref[slot, h] = jnp.broadcast_to(jnp.max(lg, axis=0, keepdims=True), (SUBLANES, TQ))

    def softmax_stage(j, slot):
        c_t = ckvT_ref[j]
        for h in range(N_HEADS):
            m_old = m_ref[h, 0:1, :]
            m_new = jnp.maximum(m_old, bmax_ref[slot, h, 0:1, :])
            alpha = jnp.exp2(m_old - m_new)
            p = jnp.exp2(lg_ref[slot, h] - m_new)
            l_new = alpha * l_ref[h, 0:1, :] + jnp.sum(p, axis=0, keepdims=True)
            acc_ref[h] = acc_ref[h] * alpha + _dot(c_t, p.astype(BF16))
            m_ref[h] = jnp.broadcast_to(m_new, (SUBLANES, TQ))
            l_ref[h] = jnp.broadcast_to(l_new, (SUBLANES, TQ))

    logits_stage(0, 0)

    def att_body(j, carry):
        for slot in range(2):
            @pl.when(j % 2 == slot)
            def _():
                logits_stage(j + 1, 1 - slot)
                softmax_stage(j, slot)
        return carry

    lax.fori_loop(0, n_pairs - 1, att_body, 0)
    for slot in range(2):
        @pl.when((n_pairs - 1) % 2 == slot)
        def _():
            softmax_stage(n_pairs - 1, slot)

    for h in range(N_HEADS):
        o_t = (acc_ref[h] / l_ref[h, 0:1, :]).astype(BF16)
        a_t = _dot(wuvT_ref[h], o_t)
        a_ref[:, h * HEAD_DIM:(h + 1) * HEAD_DIM] = a_t.T.astype(BF16)


def _attn(qT, qiT, wiT, kk2, ckv, ckvT, wukT, wuvT):
    col = lambda h: pl.BlockSpec((h, TQ), lambda i: (0, i))
    return pl.pallas_call(
        _attn_kernel,
        grid=(SEQ // TQ,),
        in_specs=[col(ATTN_WIDTH), col(IDX_HEADS * IDX_DIM), col(IDX_HEADS), _resident(kk2.shape),
                  _resident(ckv.shape), _resident(ckvT.shape), _resident(wukT.shape),
                  _resident(wuvT.shape)],
        out_specs=pl.BlockSpec((TQ, ATTN_WIDTH), lambda i: (i, 0)),
        out_shape=jax.ShapeDtypeStruct((SEQ, ATTN_WIDTH), BF16),
        scratch_shapes=[
            pltpu.VMEM((NKB, KB, TQ), F32),
            pltpu.VMEM((N_HEADS, KV_LATENT, TQ), BF16),
            pltpu.VMEM((SUBLANES, TQ), F32),
            pltpu.VMEM((SUBLANES, TQ), F32),
            pltpu.VMEM((N_HEADS, SUBLANES, TQ), F32),
            pltpu.VMEM((N_HEADS, SUBLANES, TQ), F32),
            pltpu.VMEM((N_HEADS, KV_LATENT, TQ), F32),
            pltpu.VMEM((2, N_HEADS, 2 * KB, TQ), F32),
            pltpu.VMEM((2, N_HEADS, SUBLANES, TQ), F32),
        ],
        compiler_params=_params(),
        name="attn",
    )(qT, qiT, wiT, kk2, ckv, ckvT, wukT, wuvT)


def _merge_kernel(n1_ref, a_ref, m_ref, wga_ref, wgb_ref, wba_ref, wbg_ref, o_ref):
    n1 = n1_ref[...]
    br_a = jax.nn.sigmoid(_dot(n1, wga_ref[...])) * _dot(a_ref[...], wba_ref[...])
    br_b = jax.nn.sigmoid(_dot(n1, wgb_ref[...])) * _dot(m_ref[...], wbg_ref[...])
    o_ref[...] = (br_a + br_b).astype(BF16)


def _merge(n1, a, m, wga, wgb, wba, wbg):
    tm = ROW_TILE
    row = lambda w: pl.BlockSpec((tm, w), lambda i: (i, 0))
    return pl.pallas_call(
        _merge_kernel,
        grid=(SEQ // tm,),
        in_specs=[row(D_MODEL), row(ATTN_WIDTH), row(GMLP_WIDTH), _resident(wga.shape),
                  _resident(wgb.shape), _resident(wba.shape), _resident(wbg.shape)],
        out_specs=row(D_MODEL),
        out_shape=jax.ShapeDtypeStruct((SEQ, D_MODEL), BF16),
        compiler_params=_params(),
        name="merge",
    )(n1, a, m, wga, wgb, wba, wbg)


ROUTE_ROWS = 8


def _outproj_kernel(mg_ref, x_ref, wo_ref, g2_ref, wrc_ref, br_ref,
                    h_ref, n2_ref, route_ref, counts_ref, carry_ref):
    i = pl.program_id(0)
    tm = ROW_TILE

    @pl.when(i == 0)
    def _():
        carry_ref[...] = jnp.zeros(carry_ref.shape, F32)

    h = x_ref[...] + _dot(mg_ref[...], wo_ref[...])
    h_ref[...] = h
    n2 = h * lax.rsqrt(jnp.mean(h * h, axis=-1, keepdims=True) + EPS) * g2_ref[...]
    n2_ref[...] = n2
    n2_hi = n2.astype(BF16)
    n2_lo = (n2 - n2_hi.astype(F32)).astype(BF16)
    hh_hl = _dot(n2_hi, wrc_ref[...])
    lh = _dot(n2_lo, wrc_ref[:, 0:LANES])
    logits = (hh_hl[:, 0:LANES] + hh_hl[:, LANES:2 * LANES] + lh).T + br_ref[...]
    row = lax.broadcasted_iota(jnp.int32, (LANES, tm), 0).astype(F32)
    is_group = row < N_GROUPS
    gl = jnp.where(is_group, logits, -jnp.inf)
    gmax = jnp.max(gl, axis=0, keepdims=True)
    gsum = jnp.sum(jnp.where(is_group, jnp.exp(logits - gmax), 0.0), axis=0, keepdims=True)
    g_val = 1.0 / gsum
    g_idx = jnp.min(jnp.where(gl == gmax, row, float(LANES)), axis=0, keepdims=True)
    e_id = row - N_GROUPS
    in_group = jnp.logical_and(
        jnp.logical_and(e_id >= 0, e_id < N_EXPERTS),
        jnp.floor(e_id * (1.0 / EXPERTS_PER_GROUP)) == g_idx)
    sel = jnp.where(in_group, logits, -jnp.inf)
    v1 = jnp.max(sel, axis=0, keepdims=True)
    i1 = jnp.min(jnp.where(sel == v1, row, float(LANES)), axis=0, keepdims=True)
    sel2 = jnp.where(row == i1, -jnp.inf, sel)
    v2 = jnp.max(sel2, axis=0, keepdims=True)
    i2 = jnp.min(jnp.where(sel2 == v2, row, float(LANES)), axis=0, keepdims=True)
    x2 = jnp.exp(v2 - v1)
    den = 1.0 + x2
    w1 = g_val * (1.0 / den)
    w2 = g_val * (x2 / den)
    e1 = i1 - N_GROUPS
    e2 = i2 - N_GROUPS

    e_row = lax.broadcasted_iota(jnp.int32, (N_EXPERTS, tm), 0).astype(F32)
    hit1 = (e_row == e1).astype(F32)
    hit2 = (e_row == e2).astype(F32)
    hits = hit1 + hit2
    t_from = lax.broadcasted_iota(jnp.int32, (tm, tm), 0)
    t_to = lax.broadcasted_iota(jnp.int32, (tm, tm), 1)
    earlier = (t_from < t_to).astype(BF16)
    before = carry_ref[:, 0:1] + _dot(hits.astype(BF16), earlier)
    rank1 = jnp.sum(hit1 * before, axis=0, keepdims=True)
    rank2 = jnp.sum(hit2 * before, axis=0, keepdims=True)
    carry = carry_ref[...] + jnp.sum(hits, axis=1, keepdims=True)
    carry_ref[...] = carry
    counts_ref[...] = carry

    r = lax.broadcasted_iota(jnp.int32, (ROUTE_ROWS, tm), 0)
    route = jnp.where(r == 0, e1, 0.0)
    for k, v in enumerate((e2, w1, w2, rank1, rank2), start=1):
        route = jnp.where(r == k, v, route)
    route_ref[...] = route


def _outproj(mg, x2, wo, g2, wr_cat, br):
    tm = ROW_TILE
    row = lambda w: pl.BlockSpec((tm, w), lambda i: (i, 0))
    return pl.pallas_call(
        _outproj_kernel,
        grid=(SEQ // tm,),
        in_specs=[row(D_MODEL), row(D_MODEL), _resident(wo.shape), _resident(g2.shape),
                  _resident(wr_cat.shape), _resident(br.shape)],
        out_specs=[row(D_MODEL), row(D_MODEL),
                   pl.BlockSpec((ROUTE_ROWS, tm), lambda i: (0, i)),
                   pl.BlockSpec((N_EXPERTS, LANES), lambda i: (0, 0))],
        out_shape=[jax.ShapeDtypeStruct((SEQ, D_MODEL), F32),
                   jax.ShapeDtypeStruct((SEQ, D_MODEL), F32),
                   jax.ShapeDtypeStruct((ROUTE_ROWS, SEQ), F32),
                   jax.ShapeDtypeStruct((N_EXPERTS, LANES), F32)],
        scratch_shapes=[pltpu.VMEM((N_EXPERTS, LANES), F32)],
        compiler_params=_params(),
        name="outproj",
    )(mg, x2, wo, g2, wr_cat, br)


def _start_row_gather(src_hbm, row_of, dst, slot, sem, rows):
    for r in rows:
        pltpu.make_async_copy(src_hbm.at[pl.ds(row_of(r), 1), :],
                              dst.at[slot, pl.ds(r, 1), :], sem.at[slot]).start()


def _wait_row_gather(src_hbm, dst, slot, sem, n_rows):
    pltpu.make_async_copy(src_hbm.at[pl.ds(0, n_rows), :], dst.at[slot], sem.at[slot]).wait()


def _scatter_kernel(padstart_ref, padlen_ref, pos_ref, x_ref, xs_hbm, xbuf, zbuf, sem, zsem):
    i = pl.program_id(0)
    n = pl.num_programs(0)
    tm = ROW_TILE
    slot = i % 2

    def row_copies_done(s):
        for _ in range(2):
            pltpu.make_async_copy(xbuf.at[s], xs_hbm.at[pl.ds(0, tm), :], sem.at[s]).wait()

    def pad_copies(e, bit):
        rows = 1 << bit
        start = padstart_ref[e] + (padlen_ref[e] & (rows - 1))
        if rows < SUBLANES:
            return [pltpu.make_async_copy(zbuf.at[pl.ds(0, 1), :], xs_hbm.at[pl.ds(start + k, 1), :],
                                          zsem.at[0]) for k in range(rows)]
        return [pltpu.make_async_copy(zbuf.at[pl.ds(0, rows), :],
                                      xs_hbm.at[pl.ds(pl.multiple_of(start, rows), rows), :],
                                      zsem.at[0])]

    def for_each_pad_piece(fn):
        def body(e, carry):
            for bit in range(PAD_BITS):
                @pl.when((lax.shift_right_logical(padlen_ref[e], bit) & 1) == 1)
                def _():
                    for c in pad_copies(e, bit):
                        fn(c)
            return carry
        lax.fori_loop(0, N_EXPERTS, body, 0)

    @pl.when(i == 0)
    def _():
        zbuf[...] = jnp.zeros(zbuf.shape, F32)
        for_each_pad_piece(lambda c: c.start())

    @pl.when(i >= 2)
    def _():
        row_copies_done(slot)

    xbuf[slot] = x_ref[...]
    for r in range(2 * tm):
        pltpu.make_async_copy(xbuf.at[slot, pl.ds(r % tm, 1), :],
                              xs_hbm.at[pl.ds(pos_ref[0, 0, r], 1), :], sem.at[slot]).start()

    @pl.when(i == n - 1)
    def _():
        row_copies_done(1 - slot)
        row_copies_done(slot)
        for_each_pad_piece(lambda c: c.wait())


def _scatter_rows(padstart, padlen, pos_tiles, n2):
    tm = ROW_TILE
    grid_spec = pltpu.PrefetchScalarGridSpec(
        num_scalar_prefetch=2,
        grid=(SEQ // tm,),
        in_specs=[pl.BlockSpec((1, 1, 2 * tm), lambda i, *_: (i, 0, 0), memory_space=pltpu.SMEM),
                  pl.BlockSpec((tm, D_MODEL), lambda i, *_: (i, 0))],
        out_specs=pl.BlockSpec(memory_space=pl.ANY),
        scratch_shapes=[pltpu.VMEM((2, tm, D_MODEL), F32),
                        pltpu.VMEM((1 << (PAD_BITS - 1), D_MODEL), F32),
                        pltpu.SemaphoreType.DMA((2,)),
                        pltpu.SemaphoreType.DMA((1,))],
    )
    return pl.pallas_call(
        _scatter_kernel,
        grid_spec=grid_spec,
        out_shape=jax.ShapeDtypeStruct((N_EXPERT_TILES * EXPERT_TILE, D_MODEL), F32),
        compiler_params=_params(),
        name="scatter_rows",
    )(padstart, padlen, pos_tiles, n2)


def _expert_kernel(ntl_ref, tst_ref, ntot_ref,
                   xs_hbm, wg_hbm, wu_hbm, wd_hbm,
                   y_hbm,
                   xbuf, ybuf, xsem, ysem, wg_st, wu_st, wd_st, wsem, wgb, wub, wdb):
    e = pl.program_id(0)
    ntot = ntot_ref[0]

    def weight_copies(ex, slot):
        return [pltpu.make_async_copy(src.at[ex], dst.at[slot], wsem.at[slot, j])
                for j, (src, dst) in enumerate(((wg_hbm, wg_st), (wu_hbm, wu_st), (wd_hbm, wd_st)))]

    def tile_rows(t):
        return pl.ds(pl.multiple_of(t * EXPERT_TILE, EXPERT_TILE), EXPERT_TILE)

    def x_copy(t, slot):
        return pltpu.make_async_copy(xs_hbm.at[tile_rows(t), :], xbuf.at[slot], xsem.at[slot])

    def y_copy(t, slot):
        return pltpu.make_async_copy(ybuf.at[slot], y_hbm.at[tile_rows(t), :], ysem.at[slot])

    @pl.when(e == 0)
    def _():
        for c in weight_copies(0, 0):
            c.start()
        x_copy(0, 0).start()

    @pl.when(e + 1 < pl.num_programs(0))
    def _():
        for c in weight_copies(e + 1, (e + 1) % 2):
            c.start()

    n_here = ntl_ref[e]
    wslot = e % 2
    for c in weight_copies(e, wslot):
        c.wait()

    @pl.when(n_here > 0)
    def _():
        wgb[...] = wg_st[wslot].astype(BF16)
        wub[...] = wu_st[wslot].astype(BF16)
        wdb[...] = wd_st[wslot].astype(BF16)

        def tile_body(k, carry):
            t = tst_ref[e] + k
            slot = t % 2
            x_copy(t, slot).wait()

            @pl.when(t + 1 < ntot)
            def _():
                x_copy(t + 1, 1 - slot).start()

            xt = xbuf[slot].astype(BF16)
            hid = jax.nn.silu(_dot(xt, wgb[...])) * _dot(xt, wub[...])
            y = _dot(hid.astype(BF16), wdb[...])

            @pl.when(t >= 2)
            def _():
                y_copy(t - 2, slot).wait()

            ybuf[slot] = y
            y_copy(t, slot).start()
            return carry
        lax.fori_loop(0, n_here, tile_body, 0)

    @pl.when(e == pl.num_programs(0) - 1)
    def _():
        @pl.when(ntot >= 2)
        def _():
            y_copy(ntot - 2, (ntot - 2) % 2).wait()
        y_copy(ntot - 1, (ntot - 1) % 2).wait()


def _experts(ntl, tst, ntot, x_sorted, w_gate, w_up, w_down):
    hbm = pl.BlockSpec(memory_space=pl.ANY)
    grid_spec = pltpu.PrefetchScalarGridSpec(
        num_scalar_prefetch=3,
        grid=(N_EXPERTS,),
        in_specs=[hbm, hbm, hbm, hbm],
        out_specs=hbm,
        scratch_shapes=[
            pltpu.VMEM((2, EXPERT_TILE, D_MODEL), F32),
            pltpu.VMEM((2, EXPERT_TILE, D_MODEL), F32),
            pltpu.SemaphoreType.DMA((2,)),
            pltpu.SemaphoreType.DMA((2,)),
            pltpu.VMEM((2, D_MODEL, EXPERT_FF), F32),
            pltpu.VMEM((2, D_MODEL, EXPERT_FF), F32),
            pltpu.VMEM((2, EXPERT_FF, D_MODEL), F32),
            pltpu.SemaphoreType.DMA((2, 3)),
            pltpu.VMEM((D_MODEL, EXPERT_FF), BF16),
            pltpu.VMEM((D_MODEL, EXPERT_FF), BF16),
            pltpu.VMEM((EXPERT_FF, D_MODEL), BF16),
        ],
    )
    return pl.pallas_call(
        _expert_kernel,
        grid_spec=grid_spec,
        out_shape=jax.ShapeDtypeStruct((N_EXPERT_TILES * EXPERT_TILE, D_MODEL), F32),
        compiler_params=_params(),
        name="experts",
    )(ntl, tst, ntot, x_sorted, w_gate, w_up, w_down)


def _final_kernel(pos_ref, posn_ref, h_ref, route_ref, gf_ref, y_hbm, o_ref, ybuf, sem):
    i = pl.program_id(0)
    n = pl.num_programs(0)
    tm = ROW_TILE

    @pl.when(i == 0)
    def _():
        _start_row_gather(y_hbm, lambda r: pos_ref[0, 0, r], ybuf, 0, sem, range(2 * tm))

    @pl.when(i + 1 < n)
    def _():
        _start_row_gather(y_hbm, lambda r: posn_ref[0, 0, r], ybuf, (i + 1) % 2, sem,
                          range(2 * tm))

    slot = i % 2
    _wait_row_gather(y_hbm, ybuf, slot, sem, 2 * tm)
    w1 = route_ref[:, 0:1]
    w2 = route_ref[:, 1:2]
    h2 = h_ref[...] + (w1 * ybuf[slot, 0:tm, :] + w2 * ybuf[slot, tm:2 * tm, :])
    o_ref[...] = h2 * lax.rsqrt(jnp.mean(h2 * h2, axis=-1, keepdims=True) + EPS) * gf_ref[...]


def _final(pos, h, route, gf, y_sorted):
    tm = ROW_TILE
    n = SEQ // tm
    row = lambda w: pl.BlockSpec((tm, w), lambda i: (i, 0))
    smem_blk = lambda f: pl.BlockSpec((1, 1, 2 * tm), f, memory_space=pltpu.SMEM)
    return pl.pallas_call(
        _final_kernel,
        grid=(n,),
        in_specs=[smem_blk(lambda i: (i, 0, 0)),
                  smem_blk(lambda i: (jnp.minimum(i + 1, n - 1), 0, 0)),
                  row(D_MODEL), row(2), _resident(gf.shape),
                  pl.BlockSpec(memory_space=pl.ANY)],
        out_specs=row(D_MODEL),
        out_shape=jax.ShapeDtypeStruct((SEQ, D_MODEL), F32),
        scratch_shapes=[pltpu.VMEM((2, 2 * tm, D_MODEL), F32), pltpu.SemaphoreType.DMA((2,))],
        compiler_params=_params(),
        name="final",
    )(pos, pos, h, route, gf, y_sorted)


def _dispatch(route, counts):
    i32 = jnp.int32
    e_pair = route[0:2].T.astype(i32).reshape(-1)
    rank = route[4:6].T.astype(i32).reshape(-1)
    cnt = counts[:, 0].astype(i32)
    ntl = (cnt + EXPERT_TILE - 1) // EXPERT_TILE
    tend = jnp.cumsum(ntl)
    tst = tend - ntl
    experts = jnp.arange(N_EXPERTS, dtype=i32)
    of_pair = e_pair[:, None] == experts[None, :]
    pos = jnp.sum(jnp.where(of_pair, (tst * EXPERT_TILE)[None, :], 0), axis=1) + rank
    padstart = tst * EXPERT_TILE + cnt
    padlen = ntl * EXPERT_TILE - cnt
    return ntl, tst, tend[-1:], padstart, padlen, pos


def kernel(x, norm1_g, w_in, kv_norm_g, w_uk, w_uv, gmlp_ws, gmlp_bs, ln_v_g, ln_v_b, w_br_attn,
           w_br_gmlp, w_out, norm2_g, w_group, b_group, w_router, b_router, w_e_gate, w_e_up,
           w_e_down, norm_f_g):
    assert x.shape == (1, SEQ, D_MODEL)
    x2 = x.reshape(SEQ, D_MODEL)
    row_vec = lambda v: v.reshape(1, -1).astype(F32)

    assert w_in.shape == (D_MODEL, IN_COLS)
    wqT, wc, wqiT, wkk, wwT, wuv_in, wga, wgb = _split_w_in(w_in.T)

    n1, qT, ckv, ckvT, qiT, kk2, wiT = _proj(x2, row_vec(norm1_g), wqT, wc, wqiT, wkk, wwT,
                                             row_vec(kv_norm_g))
    m = _gmlp(n1, wuv_in, gmlp_ws, jnp.pad(gmlp_bs.T, ((0, 0), (0, LANES - GMLP_GROUPS))),
              row_vec(ln_v_g), row_vec(ln_v_b))
    a = _attn(qT, qiT, wiT, kk2, ckv, ckvT,
              jnp.swapaxes(w_uk, 1, 2).astype(BF16), jnp.swapaxes(w_uv, 1, 2).astype(BF16))
    mg = _merge(n1, a, m, wga, wgb, w_br_attn.astype(BF16), w_br_gmlp.astype(BF16))

    w_route = jnp.pad(jnp.concatenate([w_group, w_router], axis=1),
                      ((0, 0), (0, LANES - N_GROUPS - N_EXPERTS)))
    b_route = jnp.pad(jnp.concatenate([b_group, b_router]), (0, LANES - N_GROUPS - N_EXPERTS))
    wr_hi = w_route.astype(BF16)
    wr_lo = (w_route - wr_hi.astype(F32)).astype(BF16)
    h, n2, route, counts = _outproj(mg, x2, w_out.astype(BF16), row_vec(norm2_g),
                                    jnp.concatenate([wr_hi, wr_lo], axis=1),
                                    b_route.reshape(LANES, 1))

    ntl, tst, ntot, padstart, padlen, pos = _dispatch(route, counts)
    pos_tiles = pos.reshape(SEQ // ROW_TILE, ROW_TILE, 2).transpose(0, 2, 1).reshape(
        SEQ // ROW_TILE, 1, 2 * ROW_TILE)
    x_sorted = _scatter_rows(padstart, padlen, pos_tiles, n2)
    y_sorted = _experts(ntl, tst, ntot, x_sorted, w_e_gate, w_e_up, w_e_down)
    out = _final(pos_tiles, h, route[2:4].T, row_vec(norm_f_g), y_sorted)
    return out.reshape(1, SEQ, D_MODEL)
```

```python
import jax
import jax.numpy as jnp
import numpy as np
from jax import lax
from jax.experimental import pallas as pl
from jax.experimental.pallas import tpu as pltpu

F32 = jnp.float32
BF16 = jnp.bfloat16

D_MODEL = 2048
SEQ = 8192
N_HEADS = 8
HEAD_DIM = 128
KV_LATENT = 256
IDX_HEADS = 16
IDX_DIM = 64
TOPK = 256
ATTN_WIDTH = N_HEADS * HEAD_DIM
GMLP_GROUPS = 8
GMLP_WIDTH = 1024
CHUNK = 128
N_GROUPS = 8
EXPERTS_PER_GROUP = 8
N_EXPERTS = 64
EXPERT_FF = 512
EPS = 1e-6

LANES = 128
SUBLANES = 8
VMEM_LIMIT = 60 * 1024 * 1024
MASK_VALUE = -0.7 * float(np.finfo(np.float32).max)
LOG2_E = float(np.log2(np.e))

ROW_TILE = 256
GMLP_TILE = 512
TQ = 256
KB = 256
NKB = SEQ // KB
COUNT_ROWS = 32
MAX_SEARCH_ITERS = 64
MAX_BRACKET = 4
EXPERT_TILE = 256
N_EXPERT_TILES = (2 * SEQ) // EXPERT_TILE + N_EXPERTS
PAD_BITS = 8
TOKEN_ROWS = D_MODEL // LANES


def _dot(a, b):
    return jnp.dot(a, b, preferred_element_type=F32)


def _dot_nt(a, b):
    return lax.dot_general(a, b, (((1,), (1,)), ((), ())), preferred_element_type=F32)


def _resident(shape):
    zeros = (0,) * len(shape)
    return pl.BlockSpec(shape, lambda *_: zeros, pipeline_mode=pl.Buffered(1))


def _params(n_axes=1):
    return pltpu.CompilerParams(
        dimension_semantics=("arbitrary",) * n_axes, vmem_limit_bytes=VMEM_LIMIT)


C_Q = ATTN_WIDTH
C_KV = C_Q + KV_LATENT
C_QI = C_KV + IDX_HEADS * IDX_DIM
C_K = C_QI + IDX_DIM
C_W = C_K + IDX_HEADS
C_UV = C_W + 2 * GMLP_WIDTH
C_GA = C_UV + D_MODEL
IN_COLS = C_GA + D_MODEL


def _split_w_in_kernel(wt_ref, wqT_ref, wc_ref, wqiT_ref, wkk_ref, wwT_ref, wuv_ref, wga_ref, wgb_ref):
    wqT_ref[...] = wt_ref[0:C_Q, :].astype(BF16)
    wc_ref[...] = wt_ref[C_Q:C_KV, :].T.astype(BF16)
    wqiT_ref[...] = wt_ref[C_KV:C_QI, :].astype(BF16)
    k_t = wt_ref[C_QI:C_K, :].T.astype(BF16)
    zero = jnp.zeros_like(k_t)
    wkk_ref[...] = jnp.concatenate([k_t, zero, zero, k_t], axis=1)
    wwT_ref[...] = wt_ref[C_K:C_W, :].astype(BF16)
    wuv_ref[...] = wt_ref[C_W:C_UV, :].T.astype(BF16)
    wga_ref[...] = wt_ref[C_UV:C_GA, :].T.astype(BF16)
    wgb_ref[...] = wt_ref[C_GA:IN_COLS, :].T.astype(BF16)


def _split_w_in(w_in_t):
    tr = ROW_TILE
    rows = lambda w: pl.BlockSpec((tr, w), lambda i: (i, 0))
    cols = lambda h: pl.BlockSpec((h, tr), lambda i: (0, i))
    sds = jax.ShapeDtypeStruct
    return pl.pallas_call(
        _split_w_in_kernel,
        grid=(D_MODEL // tr,),
        in_specs=[cols(IN_COLS)],
        out_specs=[cols(C_Q), rows(KV_LATENT), cols(C_QI - C_KV), rows(2 * LANES), cols(IDX_HEADS),
                   rows(2 * GMLP_WIDTH), rows(D_MODEL), rows(D_MODEL)],
        out_shape=[sds((C_Q, D_MODEL), BF16), sds((D_MODEL, KV_LATENT), BF16),
                   sds((C_QI - C_KV, D_MODEL), BF16), sds((D_MODEL, 2 * LANES), BF16),
                   sds((IDX_HEADS, D_MODEL), BF16), sds((D_MODEL, 2 * GMLP_WIDTH), BF16),
                   sds((D_MODEL, D_MODEL), BF16), sds((D_MODEL, D_MODEL), BF16)],
        compiler_params=_params(),
        name="split_w_in",
    )(w_in_t)


def _proj_kernel(x_ref, g1_ref, wqT_ref, wc_ref, wqiT_ref, wkk_ref, wwT_ref, kvg_ref,
                 n1_ref, qT_ref, ckv_ref, ckvT_ref, qiT_ref, kk2_ref, wiT_ref):
    x = x_ref[...]
    ms = jnp.mean(x * x, axis=-1, keepdims=True)
    n1 = (x * lax.rsqrt(ms + EPS) * g1_ref[...]).astype(BF16)
    n1_ref[...] = n1
    qT_ref[...] = _dot_nt(wqT_ref[...], n1).astype(BF16)
    qiT_ref[...] = (_dot_nt(wqiT_ref[...], n1) * (IDX_DIM ** -0.5)).astype(BF16)
    wiT_ref[...] = _dot_nt(wwT_ref[...], n1) * (IDX_HEADS ** -0.5)
    c = _dot(n1, wc_ref[...])
    c = c * lax.rsqrt(jnp.mean(c * c, axis=-1, keepdims=True) + EPS) * kvg_ref[...]
    ckv_ref[...] = c.astype(BF16)
    ckvT_ref[0] = c.T.astype(BF16)
    kk = _dot(n1, wkk_ref[...]).astype(BF16)
    kk2_ref[0, 0:KB, :] = kk[:, 0:LANES]
    kk2_ref[0, KB:2 * KB, :] = kk[:, LANES:2 * LANES]


def _proj(x2, g1, wqT, wc, wqiT, wkk, wwT, kvg):
    tm = KB
    row = lambda w: pl.BlockSpec((tm, w), lambda i: (i, 0))
    col = lambda h: pl.BlockSpec((h, tm), lambda i: (0, i))
    return pl.pallas_call(
        _proj_kernel,
        grid=(SEQ // tm,),
        in_specs=[row(D_MODEL), _resident(g1.shape), _resident(wqT.shape), _resident(wc.shape),
                  _resident(wqiT.shape), _resident(wkk.shape), _resident(wwT.shape),
                  _resident(kvg.shape)],
        out_specs=[row(D_MODEL), col(ATTN_WIDTH), row(KV_LATENT),
                   pl.BlockSpec((1, KV_LATENT, tm), lambda i: (i // 2, 0, i % 2)),
                   col(IDX_HEADS * IDX_DIM),
                   pl.BlockSpec((1, 2 * tm, LANES), lambda i: (i, 0, 0)),
                   col(IDX_HEADS)],
        out_shape=[
            jax.ShapeDtypeStruct((SEQ, D_MODEL), BF16),
            jax.ShapeDtypeStruct((ATTN_WIDTH, SEQ), BF16),
            jax.ShapeDtypeStruct((SEQ, KV_LATENT), BF16),
            jax.ShapeDtypeStruct((NKB // 2, KV_LATENT, 2 * KB), BF16),
            jax.ShapeDtypeStruct((IDX_HEADS * IDX_DIM, SEQ), BF16),
            jax.ShapeDtypeStruct((NKB, 2 * KB, LANES), BF16),
            jax.ShapeDtypeStruct((IDX_HEADS, SEQ), F32),
        ],
        compiler_params=_params(),
        name="proj",
    )(x2, g1, wqT, wc, wqiT, wkk, wwT, kvg)


def _gmlp_kernel(n1_ref, wuv_ref, ws_ref, bsT_ref, lng_ref, lnb_ref, m_ref):
    uv = _dot(n1_ref[...], wuv_ref[...])
    z = jax.nn.gelu(uv)
    u = z[:, :GMLP_WIDTH]
    v = z[:, GMLP_WIDTH:]
    mu = jnp.mean(v, axis=-1, keepdims=True)
    var = jnp.mean(jnp.square(v - mu), axis=-1, keepdims=True)
    vn = ((v - mu) * lax.rsqrt(var + EPS) * lng_ref[...] + lnb_ref[...]).astype(BF16)
    t_pos = lax.broadcasted_iota(jnp.int32, (CHUNK, CHUNK), 0)
    s_pos = lax.broadcasted_iota(jnp.int32, (CHUNK, CHUNK), 1)
    causal = s_pos <= t_pos
    for g in range(GMLP_GROUPS):
        wm = jnp.where(causal, ws_ref[g], 0.0).astype(BF16)
        bias = bsT_ref[:, g:g + 1]
        cols = slice(g * LANES, (g + 1) * LANES)
        for c in range(GMLP_TILE // CHUNK):
            rows = slice(c * CHUNK, (c + 1) * CHUNK)
            y = _dot(wm, vn[rows, cols]) + bias
            m_ref[rows, cols] = (u[rows, cols] * y).astype(BF16)


def _gmlp(n1, wuv, ws, bsT, lng, lnb):
    tm = GMLP_TILE
    return pl.pallas_call(
        _gmlp_kernel,
        grid=(SEQ // tm,),
        in_specs=[pl.BlockSpec((tm, D_MODEL), lambda i: (i, 0)), _resident(wuv.shape),
                  _resident(ws.shape), _resident(bsT.shape), _resident(lng.shape),
                  _resident(lnb.shape)],
        out_specs=pl.BlockSpec((tm, GMLP_WIDTH), lambda i: (i, 0)),
        out_shape=jax.ShapeDtypeStruct((SEQ, GMLP_WIDTH), BF16),
        compiler_params=_params(),
        name="gmlp",
    )(n1, wuv, ws, bsT, lng, lnb)


def _attn_kernel(qT_ref, qiT_ref, wiT_ref, kk2_ref, ckv_ref, ckvT_ref, wukT_ref, wuvT_ref,
                 a_ref,
                 sc_ref, qabs_ref, thr_ref, keep_ref, m_ref, l_ref, acc_ref, lg_ref, bmax_ref):
    i = pl.program_id(0)
    nkb = i + 1

    for h in range(N_HEADS):
        qa = _dot(wukT_ref[h], qT_ref[h * HEAD_DIM:(h + 1) * HEAD_DIM, :]) * (
            HEAD_DIM ** -0.5 * LOG2_E)
        qabs_ref[h] = qa.astype(BF16)

    q_pos = i * TQ + lax.broadcasted_iota(jnp.int32, (KB, TQ), 1)
    k_off = lax.broadcasted_iota(jnp.int32, (KB, TQ), 0)

    def score_body(kb, carry):
        smax, smin = carry
        keys = kk2_ref[kb]
        acc = jnp.zeros((KB, TQ), F32)
        for j in range(IDX_HEADS // 2):
            d = _dot(keys, qiT_ref[j * LANES:(j + 1) * LANES, :])
            acc = acc + jnp.maximum(d[0:KB], 0.0) * wiT_ref[2 * j:2 * j + 1, :]
            acc = acc + jnp.maximum(d[KB:2 * KB], 0.0) * wiT_ref[2 * j + 1:2 * j + 2, :]
        causal = (kb * KB + k_off) <= q_pos
        sc_ref[kb] = jnp.where(causal, acc, -jnp.inf)
        smax = jnp.maximum(smax, jnp.max(jnp.where(causal, acc, -jnp.inf), axis=0, keepdims=True))
        smin = jnp.minimum(smin, jnp.min(jnp.where(causal, acc, jnp.inf), axis=0, keepdims=True))
        return smax, smin

    n_pairs = (nkb + 1) // 2
    smax, smin = lax.fori_loop(
        0, n_pairs, lambda j, c: score_body(2 * j + 1, score_body(2 * j, c)),
        (jnp.full((1, TQ), -jnp.inf, F32), jnp.full((1, TQ), jnp.inf, F32)))

    def count_ge(x):
        def body(j, cnt):
            for kb in (2 * j, 2 * j + 1):
                ge = jnp.where(sc_ref[kb] >= x, 1.0, 0.0)
                cnt = cnt + jnp.sum(ge.reshape(KB // COUNT_ROWS, COUNT_ROWS, TQ), axis=0)
            return cnt
        cnt = lax.fori_loop(0, n_pairs, body, jnp.zeros((COUNT_ROWS, TQ), F32))
        return jnp.sum(cnt, axis=0, keepdims=True)

    def max_below(x):
        def body(j, best):
            for kb in (2 * j, 2 * j + 1):
                s = sc_ref[kb]
                v = jnp.where(s < x, s, -jnp.inf)
                best = jnp.maximum(best, jnp.max(v.reshape(KB // COUNT_ROWS, COUNT_ROWS, TQ), axis=0))
            return best
        best = lax.fori_loop(0, n_pairs, body, jnp.full((COUNT_ROWS, TQ), -jnp.inf, F32))
        return jnp.max(best, axis=0, keepdims=True)

    n_causal = (i * TQ + 1 + lax.broadcasted_iota(jnp.int32, (1, TQ), 1)).astype(F32)
    want = jnp.minimum(n_causal, float(TOPK))
    hi0 = smax + jnp.maximum(jnp.abs(smax), 1e-30) * 1e-6
    open0 = (n_causal != want).astype(F32)

    def search_cond(state):
        return jnp.logical_and(state[0] < MAX_SEARCH_ITERS, state[-1] > 0.0)

    def search_body(state):
        it, lo, hi, c_lo, c_hi, open_, _ = state
        mid = 0.5 * lo + 0.5 * hi
        c = count_ge(mid)
        live = jnp.logical_and(open_ > 0.0, jnp.logical_and(mid > lo, mid < hi))
        ge = c >= want
        go_lo = jnp.logical_and(live, ge)
        go_hi = jnp.logical_and(live, jnp.logical_not(ge))
        lo = jnp.where(go_lo, mid, lo)
        c_lo = jnp.where(go_lo, c, c_lo)
        hi = jnp.where(go_hi, mid, hi)
        c_hi = jnp.where(go_hi, c, c_hi)
        wide = jnp.logical_and(c_lo != want, c_lo - c_hi > MAX_BRACKET)
        open_ = jnp.logical_and(live, wide).astype(F32)
        return it + 1, lo, hi, c_lo, c_hi, open_, jnp.max(open_)

    _, lo_f, hi_f, c_lo_f, c_hi_f, _, _ = lax.while_loop(
        search_cond, search_body,
        (jnp.int32(0), smin, hi0, n_causal, jnp.zeros((1, TQ), F32), open0, jnp.max(open0)))

    stepping = c_lo_f != want

    def step_cond(state):
        return jnp.logical_and(state[0] < MAX_BRACKET, state[-1] > 0.0)

    def step_body(state):
        it, cur, c, _ = state
        act = jnp.logical_and(stepping, c < want)
        cur = jnp.where(act, max_below(cur), cur)
        c = jnp.where(act, c + 1.0, c)
        return it + 1, cur, c, jnp.max(jnp.logical_and(stepping, c < want).astype(F32))

    _, cur_f, _, _ = lax.while_loop(
        step_cond, step_body,
        (jnp.int32(0), hi_f, c_hi_f, jnp.max(stepping.astype(F32))))
    thr0 = jnp.where(stepping, cur_f, lo_f)
    thr_ref[...] = jnp.broadcast_to(thr0, (SUBLANES, TQ))
    keep_ref[...] = jnp.full((SUBLANES, TQ), float(SEQ), F32)
    unresolved0 = (count_ge(thr0) != want).astype(F32)

    @pl.when(jnp.max(unresolved0) > 0.0)
    def _():
        def next_value(lo, below):
            def body(kb, u):
                s = sc_ref[kb]
                cand = jnp.logical_and(s >= lo, s > below)
                return jnp.minimum(u, jnp.min(jnp.where(cand, s, jnp.inf), axis=0, keepdims=True))
            return lax.fori_loop(0, nkb, body, jnp.full((1, TQ), jnp.inf, F32))

        def count_gt(x):
            def body(kb, cnt):
                gt = jnp.where(sc_ref[kb] > x, 1.0, 0.0)
                return cnt + jnp.sum(gt, axis=0, keepdims=True)
            return lax.fori_loop(0, nkb, body, jnp.zeros((1, TQ), F32))

        def peel_cond(state):
            return state[-1] > 0.0

        def peel_body(state):
            below, unres, thr, keep, _ = state
            u = next_value(lo_f, below)
            c_gt = count_gt(u)
            hit = jnp.logical_and(unres > 0.0, c_gt < want)
            thr = jnp.where(hit, u, thr)
            keep = jnp.where(hit, want - c_gt, keep)
            unres = jnp.logical_and(unres > 0.0, jnp.logical_not(hit)).astype(F32)
            return u, unres, thr, keep, jnp.max(unres)

        _, _, thr_t, keep_t, _ = lax.while_loop(
            peel_cond, peel_body,
            (jnp.full((1, TQ), -jnp.inf, F32), unresolved0, thr0,
             jnp.full((1, TQ), float(SEQ), F32), jnp.float32(1.0)))
        thr_ref[...] = jnp.broadcast_to(thr_t, (SUBLANES, TQ))
        keep_ref[...] = jnp.broadcast_to(keep_t, (SUBLANES, TQ))

        r_i = lax.broadcasted_iota(jnp.int32, (KB, KB), 0)
        c_i = lax.broadcasted_iota(jnp.int32, (KB, KB), 1)
        before = (c_i < r_i).astype(BF16)

        def drop_body(kb, seen):
            s = sc_ref[kb]
            eq = jnp.logical_and(s == thr_t, unresolved0 > 0.0)
            eq_f = eq.astype(F32)
            rank = seen + _dot(before, eq_f.astype(BF16))
            sc_ref[kb] = jnp.where(jnp.logical_and(eq, rank >= keep_t), -jnp.inf, s)
            return seen + jnp.sum(eq_f, axis=0, keepdims=True)

        lax.fori_loop(0, nkb, drop_body, jnp.zeros((1, TQ), F32))

    m_ref[...] = jnp.full(m_ref.shape, MASK_VALUE, F32)
    l_ref[...] = jnp.zeros(l_ref.shape, F32)
    acc_ref[...] = jnp.zeros(acc_ref.shape, F32)

    def logits_stage(j, slot):
        scores = sc_ref[pl.ds(2 * j, 2)].reshape(2 * KB, TQ)
        bias = jnp.where(scores >= thr_ref[0:1, :], 0.0, MASK_VALUE)
        c_n = ckv_ref[pl.ds(pl.multiple_of(j * (2 * KB), 2 * KB), 2 * KB), :]
        for h in range(N_HEADS):
            lg = _dot(c_n, qabs_ref[h]) + bias
            lg_ref[slot, h] = lg
            bmax_ref[slot, h] = jnp.broadcast_to(jnp.max(lg, axis=0, keepdims=True), (SUBLANES, TQ))

    def softmax_stage(j, slot):
        c_t = ckvT_ref[j]
        for h in range(N_HEADS):
            m_old = m_ref[h, 0:1, :]
            m_new = jnp.maximum(m_old, bmax_ref[slot, h, 0:1, :])
            alpha = jnp.exp2(m_old - m_new)
            p = jnp.exp2(lg_ref[slot, h] - m_new)
            l_new = alpha * l_ref[h, 0:1, :] + jnp.sum(p, axis=0, keepdims=True)
            acc_ref[h] = acc_ref[h] * alpha + _dot(c_t, p.astype(BF16))
            m_ref[h] = jnp.broadcast_to(m_new, (SUBLANES, TQ))
            l_ref[h] = jnp.broadcast_to(l_new, (SUBLANES, TQ))

    logits_stage(0, 0)

    def att_body(j, carry):
        for slot in range(2):
            @pl.when(j % 2 == slot)
            def _():
                logits_stage(j + 1, 1 - slot)
                softmax_stage(j, slot)
        return carry

    lax.fori_loop(0, n_pairs - 1, att_body, 0)
    for slot in range(2):
        @pl.when((n_pairs - 1) % 2 == slot)
        def _():
            softmax_stage(n_pairs - 1, slot)

    for h in range(N_HEADS):
        o_t = (acc_ref[h] / l_ref[h, 0:1, :]).astype(BF16)
        a_t = _dot(wuvT_ref[h], o_t)
        a_ref[:, h * HEAD_DIM:(h + 1) * HEAD_DIM] = a_t.T.astype(BF16)


def _attn(qT, qiT, wiT, kk2, ckv, ckvT, wukT, wuvT):
    col = lambda h: pl.BlockSpec((h, TQ), lambda i: (0, i))
    return pl.pallas_call(
        _attn_kernel,
        grid=(SEQ // TQ,),
        in_specs=[col(ATTN_WIDTH), col(IDX_HEADS * IDX_DIM), col(IDX_HEADS), _resident(kk2.shape),
                  _resident(ckv.shape), _resident(ckvT.shape), _resident(wukT.shape),
                  _resident(wuvT.shape)],
        out_specs=pl.BlockSpec((TQ, ATTN_WIDTH), lambda i: (i, 0)),
        out_shape=jax.ShapeDtypeStruct((SEQ, ATTN_WIDTH), BF16),
        scratch_shapes=[
            pltpu.VMEM((NKB, KB, TQ), F32),
            pltpu.VMEM((N_HEADS, KV_LATENT, TQ), BF16),
            pltpu.VMEM((SUBLANES, TQ), F32),
            pltpu.VMEM((SUBLANES, TQ), F32),
            pltpu.VMEM((N_HEADS, SUBLANES, TQ), F32),
            pltpu.VMEM((N_HEADS, SUBLANES, TQ), F32),
            pltpu.VMEM((N_HEADS, KV_LATENT, TQ), F32),
            pltpu.VMEM((2, N_HEADS, 2 * KB, TQ), F32),
            pltpu.VMEM((2, N_HEADS, SUBLANES, TQ), F32),
        ],
        compiler_params=_params(),
        name="attn",
    )(qT, qiT, wiT, kk2, ckv, ckvT, wukT, wuvT)


def _merge_kernel(n1_ref, a_ref, m_ref, wga_ref, wgb_ref, wba_ref, wbg_ref, o_ref):
    n1 = n1_ref[...]
    br_a = jax.nn.sigmoid(_dot(n1, wga_ref[...])) * _dot(a_ref[...], wba_ref[...])
    br_b = jax.nn.sigmoid(_dot(n1, wgb_ref[...])) * _dot(m_ref[...], wbg_ref[...])
    o_ref[...] = (br_a + br_b).astype(BF16)


def _merge(n1, a, m, wga, wgb, wba, wbg):
    tm = ROW_TILE
    row = lambda w: pl.BlockSpec((tm, w), lambda i: (i, 0))
    return pl.pallas_call(
        _merge_kernel,
        grid=(SEQ // tm,),
        in_specs=[row(D_MODEL), row(ATTN_WIDTH), row(GMLP_WIDTH), _resident(wga.shape),
                  _resident(wgb.shape), _resident(wba.shape), _resident(wbg.shape)],
        out_specs=row(D_MODEL),
        out_shape=jax.ShapeDtypeStruct((SEQ, D_MODEL), BF16),
        compiler_params=_params(),
        name="merge",
    )(n1, a, m, wga, wgb, wba, wbg)


ROUTE_ROWS = 8


def _outproj_kernel(mg_ref, x_ref, wo_ref, g2_ref, wrc_ref, br_ref,
                    h_ref, n2_ref, route_ref, counts_ref, carry_ref):
    i = pl.program_id(0)
    tm = ROW_TILE

    @pl.when(i == 0)
    def _():
        carry_ref[...] = jnp.zeros(carry_ref.shape, F32)

    h = x_ref[...] + _dot(mg_ref[...], wo_ref[...])
    h_ref[...] = h
    n2 = h * lax.rsqrt(jnp.mean(h * h, axis=-1, keepdims=True) + EPS) * g2_ref[...]
    n2_ref[...] = n2
    n2_hi = n2.astype(BF16)
    n2_lo = (n2 - n2_hi.astype(F32)).astype(BF16)
    hh_hl = _dot(n2_hi, wrc_ref[...])
    lh = _dot(n2_lo, wrc_ref[:, 0:LANES])
    logits = (hh_hl[:, 0:LANES] + hh_hl[:, LANES:2 * LANES] + lh).T + br_ref[...]
    row = lax.broadcasted_iota(jnp.int32, (LANES, tm), 0).astype(F32)
    is_group = row < N_GROUPS
    gl = jnp.where(is_group, logits, -jnp.inf)
    gmax = jnp.max(gl, axis=0, keepdims=True)
    gsum = jnp.sum(jnp.where(is_group, jnp.exp(logits - gmax), 0.0), axis=0, keepdims=True)
    g_val = 1.0 / gsum
    g_idx = jnp.min(jnp.where(gl == gmax, row, float(LANES)), axis=0, keepdims=True)
    e_id = row - N_GROUPS
    in_group = jnp.logical_and(
        jnp.logical_and(e_id >= 0, e_id < N_EXPERTS),
        jnp.floor(e_id * (1.0 / EXPERTS_PER_GROUP)) == g_idx)
    sel = jnp.where(in_group, logits, -jnp.inf)
    v1 = jnp.max(sel, axis=0, keepdims=True)
    i1 = jnp.min(jnp.where(sel == v1, row, float(LANES)), axis=0, keepdims=True)
    sel2 = jnp.where(row == i1, -jnp.inf, sel)
    v2 = jnp.max(sel2, axis=0, keepdims=True)
    i2 = jnp.min(jnp.where(sel2 == v2, row, float(LANES)), axis=0, keepdims=True)
    x2 = jnp.exp(v2 - v1)
    den = 1.0 + x2
    w1 = g_val * (1.0 / den)
    w2 = g_val * (x2 / den)
    e1 = i1 - N_GROUPS
    e2 = i2 - N_GROUPS

    e_row = lax.broadcasted_iota(jnp.int32, (N_EXPERTS, tm), 0).astype(F32)
    hit1 = (e_row == e1).astype(F32)
    hit2 = (e_row == e2).astype(F32)
    hits = hit1 + hit2
    t_from = lax.broadcasted_iota(jnp.int32, (tm, tm), 0)
    t_to = lax.broadcasted_iota(jnp.int32, (tm, tm), 1)
    earlier = (t_from < t_to).astype(BF16)
    before = carry_ref[:, 0:1] + _dot(hits.astype(BF16), earlier)
    rank1 = jnp.sum(hit1 * before, axis=0, keepdims=True)
    rank2 = jnp.sum(hit2 * before, axis=0, keepdims=True)
    carry = carry_ref[...] + jnp.sum(hits, axis=1, keepdims=True)
    carry_ref[...] = carry
    counts_ref[...] = carry

    r = lax.broadcasted_iota(jnp.int32, (ROUTE_ROWS, tm), 0)
    route = jnp.where(r == 0, e1, 0.0)
    for k, v in enumerate((e2, w1, w2, rank1, rank2), start=1):
        route = jnp.where(r == k, v, route)
    route_ref[...] = route


def _outproj(mg, x2, wo, g2, wr_cat, br):
    tm = ROW_TILE
    row = lambda w: pl.BlockSpec((tm, w), lambda i: (i, 0))
    return pl.pallas_call(
        _outproj_kernel,
        grid=(SEQ // tm,),
        in_specs=[row(D_MODEL), row(D_MODEL), _resident(wo.shape), _resident(g2.shape),
                  _resident(wr_cat.shape), _resident(br.shape)],
        out_specs=[row(D_MODEL), row(D_MODEL),
                   pl.BlockSpec((ROUTE_ROWS, tm), lambda i: (0, i)),
                   pl.BlockSpec((N_EXPERTS, LANES), lambda i: (0, 0))],
        out_shape=[jax.ShapeDtypeStruct((SEQ, D_MODEL), F32),
                   jax.ShapeDtypeStruct((SEQ, D_MODEL), F32),
                   jax.ShapeDtypeStruct((ROUTE_ROWS, SEQ), F32),
                   jax.ShapeDtypeStruct((N_EXPERTS, LANES), F32)],
        scratch_shapes=[pltpu.VMEM((N_EXPERTS, LANES), F32)],
        compiler_params=_params(),
        name="outproj",
    )(mg, x2, wo, g2, wr_cat, br)


def _start_row_gather(src_hbm, row_of, dst, slot, sem, rows):
    for r in rows:
        pltpu.make_async_copy(src_hbm.at[pl.ds(row_of(r), 1), :],
                              dst.at[slot, pl.ds(r, 1), :], sem.at[slot]).start()


def _wait_row_gather(src_hbm, dst, slot, sem, n_rows):
    pltpu.make_async_copy(src_hbm.at[pl.ds(0, n_rows), :], dst.at[slot], sem.at[slot]).wait()


def _scatter_kernel(padstart_ref, padlen_ref, pos_ref, x_ref, xs_hbm, xf, xbuf, zbuf, sem, zsem):
    i = pl.program_id(0)
    n = pl.num_programs(0)
    tm = ROW_TILE
    slot = i % 2

    def row_copies_done(s):
        for _ in range(2):
            pltpu.make_async_copy(xbuf.at[s], xs_hbm.at[pl.ds(0, tm)], sem.at[s]).wait()

    def pad_copy(e, bit):
        rows = 1 << bit
        start = padstart_ref[e] + (padlen_ref[e] & (rows - 1))
        return pltpu.make_async_copy(zbuf.at[pl.ds(0, rows)], xs_hbm.at[pl.ds(start, rows)], zsem.at[0])

    def for_each_pad_piece(fn):
        def body(e, carry):
            for bit in range(PAD_BITS):
                @pl.when((lax.shift_right_logical(padlen_ref[e], bit) & 1) == 1)
                def _():
                    fn(pad_copy(e, bit))
            return carry
        lax.fori_loop(0, N_EXPERTS, body, 0)

    @pl.when(i == 0)
    def _():
        zbuf[...] = jnp.zeros(zbuf.shape, BF16)
        for_each_pad_piece(lambda c: c.start())

    @pl.when(i >= 2)
    def _():
        row_copies_done(slot)

    for a in range(TOKEN_ROWS):
        xf[pl.ds(a, tm, stride=TOKEN_ROWS), :] = x_ref[:, a * LANES:(a + 1) * LANES]
    for s in range(2):
        @pl.when(slot == s)
        def _():
            xbuf[s] = xf[...].reshape(tm, TOKEN_ROWS, LANES).astype(BF16)
            for r in range(2 * tm):
                pltpu.make_async_copy(xbuf.at[s, r % tm], xs_hbm.at[pos_ref[0, 0, r]],
                                      sem.at[s]).start()

    @pl.when(i == n - 1)
    def _():
        row_copies_done(1 - slot)
        row_copies_done(slot)
        for_each_pad_piece(lambda c: c.wait())


def _scatter_rows(padstart, padlen, pos_tiles, n2):
    tm = ROW_TILE
    grid_spec = pltpu.PrefetchScalarGridSpec(
        num_scalar_prefetch=2,
        grid=(SEQ // tm,),
        in_specs=[pl.BlockSpec((1, 1, 2 * tm), lambda i, *_: (i, 0, 0), memory_space=pltpu.SMEM),
                  pl.BlockSpec((tm, D_MODEL), lambda i, *_: (i, 0))],
        out_specs=pl.BlockSpec(memory_space=pl.ANY),
        scratch_shapes=[pltpu.VMEM((tm * TOKEN_ROWS, LANES), F32),
                        pltpu.VMEM((2, tm, TOKEN_ROWS, LANES), BF16),
                        pltpu.VMEM((1 << (PAD_BITS - 1), TOKEN_ROWS, LANES), BF16),
                        pltpu.SemaphoreType.DMA((2,)),
                        pltpu.SemaphoreType.DMA((1,))],
    )
    return pl.pallas_call(
        _scatter_kernel,
        grid_spec=grid_spec,
        out_shape=jax.ShapeDtypeStruct((N_EXPERT_TILES * EXPERT_TILE, TOKEN_ROWS, LANES), BF16),
        compiler_params=_params(),
        name="scatter_rows",
    )(padstart, padlen, pos_tiles, n2)


def _expert_kernel(ntl_ref, tst_ref, ntot_ref,
                   xs_hbm, wg_hbm, wu_hbm, wd_hbm,
                   y_hbm,
                   xbuf, ybuf, xsem, ysem, wg_st, wu_st, wd_st, wsem, wgb, wub, wdb, rows_f):
    e = pl.program_id(0)
    ntot = ntot_ref[0]
    n_rows = EXPERT_TILE * TOKEN_ROWS

    def weight_copies(ex, slot):
        return [pltpu.make_async_copy(src.at[ex], dst.at[slot], wsem.at[slot, j])
                for j, (src, dst) in enumerate(((wg_hbm, wg_st), (wu_hbm, wu_st), (wd_hbm, wd_st)))]

    def tile_rows(t):
        return pl.ds(pl.multiple_of(t * EXPERT_TILE, EXPERT_TILE), EXPERT_TILE)

    def x_copy(t, slot):
        return pltpu.make_async_copy(xs_hbm.at[tile_rows(t)], xbuf.at[slot], xsem.at[slot])

    def y_copy(t, slot):
        return pltpu.make_async_copy(ybuf.at[slot], y_hbm.at[tile_rows(t), :], ysem.at[slot])

    @pl.when(e == 0)
    def _():
        for c in weight_copies(0, 0):
            c.start()
        x_copy(0, 0).start()

    @pl.when(e + 1 < pl.num_programs(0))
    def _():
        for c in weight_copies(e + 1, (e + 1) % 2):
            c.start()

    n_here = ntl_ref[e]
    wslot = e % 2
    for c in weight_copies(e, wslot):
        c.wait()

    @pl.when(n_here > 0)
    def _():
        wgb[...] = wg_st[wslot].astype(BF16)
        wub[...] = wu_st[wslot].astype(BF16)
        wdb[...] = wd_st[wslot].astype(BF16)

        def tile_body(k, carry):
            t = tst_ref[e] + k
            slot = t % 2
            x_copy(t, slot).wait()

            @pl.when(t + 1 < ntot)
            def _():
                x_copy(t + 1, 1 - slot).start()

            rows_f[...] = xbuf[slot].astype(F32).reshape(n_rows, LANES)
            xt = jnp.concatenate(
                [rows_f[pl.ds(a, EXPERT_TILE, stride=TOKEN_ROWS), :] for a in range(TOKEN_ROWS)],
                axis=1).astype(BF16)
            hid = jax.nn.silu(_dot(xt, wgb[...])) * _dot(xt, wub[...])
            y = _dot(hid.astype(BF16), wdb[...])

            @pl.when(t >= 2)
            def _():
                y_copy(t - 2, slot).wait()

            ybuf[slot] = y
            y_copy(t, slot).start()
            return carry
        lax.fori_loop(0, n_here, tile_body, 0)

    @pl.when(e == pl.num_programs(0) - 1)
    def _():
        @pl.when(ntot >= 2)
        def _():
            y_copy(ntot - 2, (ntot - 2) % 2).wait()
        y_copy(ntot - 1, (ntot - 1) % 2).wait()


def _experts(ntl, tst, ntot, x_sorted, w_gate, w_up, w_down):
    hbm = pl.BlockSpec(memory_space=pl.ANY)
    grid_spec = pltpu.PrefetchScalarGridSpec(
        num_scalar_prefetch=3,
        grid=(N_EXPERTS,),
        in_specs=[hbm, hbm, hbm, hbm],
        out_specs=hbm,
        scratch_shapes=[
            pltpu.VMEM((2, EXPERT_TILE, TOKEN_ROWS, LANES), BF16),
            pltpu.VMEM((2, EXPERT_TILE, D_MODEL), F32),
            pltpu.SemaphoreType.DMA((2,)),
            pltpu.SemaphoreType.DMA((2,)),
            pltpu.VMEM((2, D_MODEL, EXPERT_FF), F32),
            pltpu.VMEM((2, D_MODEL, EXPERT_FF), F32),
            pltpu.VMEM((2, EXPERT_FF, D_MODEL), F32),
            pltpu.SemaphoreType.DMA((2, 3)),
            pltpu.VMEM((D_MODEL, EXPERT_FF), BF16),
            pltpu.VMEM((D_MODEL, EXPERT_FF), BF16),
            pltpu.VMEM((EXPERT_FF, D_MODEL), BF16),
            pltpu.VMEM((EXPERT_TILE * TOKEN_ROWS, LANES), F32),
        ],
    )
    return pl.pallas_call(
        _expert_kernel,
        grid_spec=grid_spec,
        out_shape=jax.ShapeDtypeStruct((N_EXPERT_TILES * EXPERT_TILE, D_MODEL), F32),
        compiler_params=_params(),
        name="experts",
    )(ntl, tst, ntot, x_sorted, w_gate, w_up, w_down)


def _final_kernel(pos_ref, posn_ref, h_ref, route_ref, gf_ref, y_hbm, o_ref, ybuf, sem):
    i = pl.program_id(0)
    n = pl.num_programs(0)
    tm = ROW_TILE

    @pl.when(i == 0)
    def _():
        _start_row_gather(y_hbm, lambda r: pos_ref[0, 0, r], ybuf, 0, sem, range(2 * tm))

    for s in range(2):
        @pl.when(jnp.logical_and(i + 1 < n, (i + 1) % 2 == s))
        def _():
            _start_row_gather(y_hbm, lambda r: posn_ref[0, 0, r], ybuf, s, sem, range(2 * tm))

    slot = i % 2
    _wait_row_gather(y_hbm, ybuf, slot, sem, 2 * tm)
    w1 = route_ref[:, 0:1]
    w2 = route_ref[:, 1:2]
    h2 = h_ref[...] + (w1 * ybuf[slot, 0:tm, :] + w2 * ybuf[slot, tm:2 * tm, :])
    o_ref[...] = h2 * lax.rsqrt(jnp.mean(h2 * h2, axis=-1, keepdims=True) + EPS) * gf_ref[...]


def _final(pos, h, route, gf, y_sorted):
    tm = ROW_TILE
    n = SEQ // tm
    row = lambda w: pl.BlockSpec((tm, w), lambda i: (i, 0))
    smem_blk = lambda f: pl.BlockSpec((1, 1, 2 * tm), f, memory_space=pltpu.SMEM)
    return pl.pallas_call(
        _final_kernel,
        grid=(n,),
        in_specs=[smem_blk(lambda i: (i, 0, 0)),
                  smem_blk(lambda i: (jnp.minimum(i + 1, n - 1), 0, 0)),
                  row(D_MODEL), row(2), _resident(gf.shape),
                  pl.BlockSpec(memory_space=pl.ANY)],
        out_specs=row(D_MODEL),
        out_shape=jax.ShapeDtypeStruct((SEQ, D_MODEL), F32),
        scratch_shapes=[pltpu.VMEM((2, 2 * tm, D_MODEL), F32), pltpu.SemaphoreType.DMA((2,))],
        compiler_params=_params(),
        name="final",
    )(pos, pos, h, route, gf, y_sorted)


def _dispatch(route, counts):
    i32 = jnp.int32
    e_pair = route[0:2].T.astype(i32).reshape(-1)
    rank = route[4:6].T.astype(i32).reshape(-1)
    cnt = counts[:, 0].astype(i32)
    ntl = (cnt + EXPERT_TILE - 1) // EXPERT_TILE
    tend = jnp.cumsum(ntl)
    tst = tend - ntl
    experts = jnp.arange(N_EXPERTS, dtype=i32)
    of_pair = e_pair[:, None] == experts[None, :]
    pos = jnp.sum(jnp.where(of_pair, (tst * EXPERT_TILE)[None, :], 0), axis=1) + rank
    padstart = tst * EXPERT_TILE + cnt
    padlen = ntl * EXPERT_TILE - cnt
    return ntl, tst, tend[-1:], padstart, padlen, pos


def kernel(x, norm1_g, w_in, kv_norm_g, w_uk, w_uv, gmlp_ws, gmlp_bs, ln_v_g, ln_v_b, w_br_attn,
           w_br_gmlp, w_out, norm2_g, w_group, b_group, w_router, b_router, w_e_gate, w_e_up,
           w_e_down, norm_f_g):
    assert x.shape == (1, SEQ, D_MODEL)
    x2 = x.reshape(SEQ, D_MODEL)
    row_vec = lambda v: v.reshape(1, -1).astype(F32)

    assert w_in.shape == (D_MODEL, IN_COLS)
    wqT, wc, wqiT, wkk, wwT, wuv_in, wga, wgb = _split_w_in(w_in.T)

    n1, qT, ckv, ckvT, qiT, kk2, wiT = _proj(x2, row_vec(norm1_g), wqT, wc, wqiT, wkk, wwT,
                                             row_vec(kv_norm_g))
    m = _gmlp(n1, wuv_in, gmlp_ws, jnp.pad(gmlp_bs.T, ((0, 0), (0, LANES - GMLP_GROUPS))),
              row_vec(ln_v_g), row_vec(ln_v_b))
    a = _attn(qT, qiT, wiT, kk2, ckv, ckvT,
              jnp.swapaxes(w_uk, 1, 2).astype(BF16), jnp.swapaxes(w_uv, 1, 2).astype(BF16))
    mg = _merge(n1, a, m, wga, wgb, w_br_attn.astype(BF16), w_br_gmlp.astype(BF16))

    w_route = jnp.pad(jnp.concatenate([w_group, w_router], axis=1),
                      ((0, 0), (0, LANES - N_GROUPS - N_EXPERTS)))
    b_route = jnp.pad(jnp.concatenate([b_group, b_router]), (0, LANES - N_GROUPS - N_EXPERTS))
    wr_hi = w_route.astype(BF16)
    wr_lo = (w_route - wr_hi.astype(F32)).astype(BF16)
    h, n2, route, counts = _outproj(mg, x2, w_out.astype(BF16), row_vec(norm2_g),
                                    jnp.concatenate([wr_hi, wr_lo], axis=1),
                                    b_route.reshape(LANES, 1))

    ntl, tst, ntot, padstart, padlen, pos = _dispatch(route, counts)
    pos_tiles = pos.reshape(SEQ // ROW_TILE, ROW_TILE, 2).transpose(0, 2, 1).reshape(
        SEQ // ROW_TILE, 1, 2 * ROW_TILE)
    x_sorted = _scatter_rows(padstart, padlen, pos_tiles, n2)
    y_sorted = _experts(ntl, tst, ntot, x_sorted, w_e_gate, w_e_up, w_e_down)
    out = _final(pos_tiles, h, route[2:4].T, row_vec(norm_f_g), y_sorted)
    return out.reshape(1, SEQ, D_MODEL)
```

```python
import jax
import jax.numpy as jnp
import numpy as np
from jax import lax
from jax.experimental import pallas as pl
from jax.experimental.pallas import tpu as pltpu

F32 = jnp.float32
BF16 = jnp.bfloat16

D_MODEL = 2048
SEQ = 8192
N_HEADS = 8
HEAD_DIM = 128
KV_LATENT = 256
IDX_HEADS = 16
IDX_DIM = 64
TOPK = 256
ATTN_WIDTH = N_HEADS * HEAD_DIM
GMLP_GROUPS = 8
GMLP_WIDTH = 1024
CHUNK = 128
N_GROUPS = 8
EXPERTS_PER_GROUP = 8
N_EXPERTS = 64
EXPERT_FF = 512
EPS = 1e-6

LANES = 128
SUBLANES = 8
VMEM_LIMIT = 60 * 1024 * 1024
MASK_VALUE = -0.7 * float(np.finfo(np.float32).max)
LOG2_E = float(np.log2(np.e))

ROW_TILE = 256
GMLP_TILE = 512
TQ = 256
KB = 256
NKB = SEQ // KB
COUNT_ROWS = 32
MAX_SEARCH_ITERS = 64
MAX_BRACKET = 4
BF16_STEP = 2.0 ** -7
EXPERT_TILE = 256
N_EXPERT_TILES = (2 * SEQ) // EXPERT_TILE + N_EXPERTS
PAD_BITS = 8
TOKEN_ROWS = D_MODEL // LANES


def _dot(a, b):
    return jnp.dot(a, b, preferred_element_type=F32)


def _dot_nt(a, b):
    return lax.dot_general(a, b, (((1,), (1,)), ((), ())), preferred_element_type=F32)


def _resident(shape):
    zeros = (0,) * len(shape)
    return pl.BlockSpec(shape, lambda *_: zeros, pipeline_mode=pl.Buffered(1))


def _params(n_axes=1):
    return pltpu.CompilerParams(
        dimension_semantics=("arbitrary",) * n_axes, vmem_limit_bytes=VMEM_LIMIT)


C_Q = ATTN_WIDTH
C_KV = C_Q + KV_LATENT
C_QI = C_KV + IDX_HEADS * IDX_DIM
C_K = C_QI + IDX_DIM
C_W = C_K + IDX_HEADS
C_UV = C_W + 2 * GMLP_WIDTH
C_GA = C_UV + D_MODEL
IN_COLS = C_GA + D_MODEL


def _split_w_in_kernel(wt_ref, wqT_ref, wc_ref, wqiT_ref, wkk_ref, wwT_ref, wuv_ref, wga_ref, wgb_ref):
    wqT_ref[...] = wt_ref[0:C_Q, :].astype(BF16)
    wc_ref[...] = wt_ref[C_Q:C_KV, :].T.astype(BF16)
    wqiT_ref[...] = wt_ref[C_KV:C_QI, :].astype(BF16)
    k_t = wt_ref[C_QI:C_K, :].T.astype(BF16)
    zero = jnp.zeros_like(k_t)
    wkk_ref[...] = jnp.concatenate([k_t, zero, zero, k_t], axis=1)
    wwT_ref[...] = wt_ref[C_K:C_W, :].astype(BF16)
    wuv_ref[...] = wt_ref[C_W:C_UV, :].T.astype(BF16)
    wga_ref[...] = wt_ref[C_UV:C_GA, :].T.astype(BF16)
    wgb_ref[...] = wt_ref[C_GA:IN_COLS, :].T.astype(BF16)


def _split_w_in(w_in_t):
    tr = ROW_TILE
    rows = lambda w: pl.BlockSpec((tr, w), lambda i: (i, 0))
    cols = lambda h: pl.BlockSpec((h, tr), lambda i: (0, i))
    sds = jax.ShapeDtypeStruct
    return pl.pallas_call(
        _split_w_in_kernel,
        grid=(D_MODEL // tr,),
        in_specs=[cols(IN_COLS)],
        out_specs=[cols(C_Q), rows(KV_LATENT), cols(C_QI - C_KV), rows(2 * LANES), cols(IDX_HEADS),
                   rows(2 * GMLP_WIDTH), rows(D_MODEL), rows(D_MODEL)],
        out_shape=[sds((C_Q, D_MODEL), BF16), sds((D_MODEL, KV_LATENT), BF16),
                   sds((C_QI - C_KV, D_MODEL), BF16), sds((D_MODEL, 2 * LANES), BF16),
                   sds((IDX_HEADS, D_MODEL), BF16), sds((D_MODEL, 2 * GMLP_WIDTH), BF16),
                   sds((D_MODEL, D_MODEL), BF16), sds((D_MODEL, D_MODEL), BF16)],
        compiler_params=_params(),
        name="split_w_in",
    )(w_in_t)


def _proj_kernel(x_ref, g1_ref, wqT_ref, wc_ref, wqiT_ref, wkk_ref, wwT_ref, kvg_ref,
                 n1_ref, qT_ref, ckv_ref, ckvT_ref, qiT_ref, kk2_ref, wiT_ref):
    x = x_ref[...]
    ms = jnp.mean(x * x, axis=-1, keepdims=True)
    n1 = (x * lax.rsqrt(ms + EPS) * g1_ref[...]).astype(BF16)
    n1_ref[...] = n1
    qT_ref[...] = _dot_nt(wqT_ref[...], n1).astype(BF16)
    qiT_ref[...] = (_dot_nt(wqiT_ref[...], n1) * (IDX_DIM ** -0.5)).astype(BF16)
    wiT_ref[...] = _dot_nt(wwT_ref[...], n1) * (IDX_HEADS ** -0.5)
    c = _dot(n1, wc_ref[...])
    c = c * lax.rsqrt(jnp.mean(c * c, axis=-1, keepdims=True) + EPS) * kvg_ref[...]
    ckv_ref[...] = c.astype(BF16)
    ckvT_ref[0] = c.T.astype(BF16)
    kk = _dot(n1, wkk_ref[...]).astype(BF16)
    kk2_ref[0, 0:KB, :] = kk[:, 0:LANES]
    kk2_ref[0, KB:2 * KB, :] = kk[:, LANES:2 * LANES]


def _proj(x2, g1, wqT, wc, wqiT, wkk, wwT, kvg):
    tm = KB
    row = lambda w: pl.BlockSpec((tm, w), lambda i: (i, 0))
    col = lambda h: pl.BlockSpec((h, tm), lambda i: (0, i))
    return pl.pallas_call(
        _proj_kernel,
        grid=(SEQ // tm,),
        in_specs=[row(D_MODEL), _resident(g1.shape), _resident(wqT.shape), _resident(wc.shape),
                  _resident(wqiT.shape), _resident(wkk.shape), _resident(wwT.shape),
                  _resident(kvg.shape)],
        out_specs=[row(D_MODEL), col(ATTN_WIDTH), row(KV_LATENT),
                   pl.BlockSpec((1, KV_LATENT, tm), lambda i: (i // 2, 0, i % 2)),
                   col(IDX_HEADS * IDX_DIM),
                   pl.BlockSpec((1, 2 * tm, LANES), lambda i: (i, 0, 0)),
                   col(IDX_HEADS)],
        out_shape=[
            jax.ShapeDtypeStruct((SEQ, D_MODEL), BF16),
            jax.ShapeDtypeStruct((ATTN_WIDTH, SEQ), BF16),
            jax.ShapeDtypeStruct((SEQ, KV_LATENT), BF16),
            jax.ShapeDtypeStruct((NKB // 2, KV_LATENT, 2 * KB), BF16),
            jax.ShapeDtypeStruct((IDX_HEADS * IDX_DIM, SEQ), BF16),
            jax.ShapeDtypeStruct((NKB, 2 * KB, LANES), BF16),
            jax.ShapeDtypeStruct((IDX_HEADS, SEQ), F32),
        ],
        compiler_params=_params(),
        name="proj",
    )(x2, g1, wqT, wc, wqiT, wkk, wwT, kvg)


def _gmlp_kernel(n1_ref, wuv_ref, ws_ref, bsT_ref, lng_ref, lnb_ref, m_ref):
    uv = _dot(n1_ref[...], wuv_ref[...])
    z = jax.nn.gelu(uv)
    u = z[:, :GMLP_WIDTH]
    v = z[:, GMLP_WIDTH:]
    mu = jnp.mean(v, axis=-1, keepdims=True)
    var = jnp.mean(jnp.square(v - mu), axis=-1, keepdims=True)
    vn = ((v - mu) * lax.rsqrt(var + EPS) * lng_ref[...] + lnb_ref[...]).astype(BF16)
    t_pos = lax.broadcasted_iota(jnp.int32, (CHUNK, CHUNK), 0)
    s_pos = lax.broadcasted_iota(jnp.int32, (CHUNK, CHUNK), 1)
    causal = s_pos <= t_pos
    for g in range(GMLP_GROUPS):
        wm = jnp.where(causal, ws_ref[g], 0.0).astype(BF16)
        bias = bsT_ref[:, g:g + 1]
        cols = slice(g * LANES, (g + 1) * LANES)
        for c in range(GMLP_TILE // CHUNK):
            rows = slice(c * CHUNK, (c + 1) * CHUNK)
            y = _dot(wm, vn[rows, cols]) + bias
            m_ref[rows, cols] = (u[rows, cols] * y).astype(BF16)


def _gmlp(n1, wuv, ws, bsT, lng, lnb):
    tm = GMLP_TILE
    return pl.pallas_call(
        _gmlp_kernel,
        grid=(SEQ // tm,),
        in_specs=[pl.BlockSpec((tm, D_MODEL), lambda i: (i, 0)), _resident(wuv.shape),
                  _resident(ws.shape), _resident(bsT.shape), _resident(lng.shape),
                  _resident(lnb.shape)],
        out_specs=pl.BlockSpec((tm, GMLP_WIDTH), lambda i: (i, 0)),
        out_shape=jax.ShapeDtypeStruct((SEQ, GMLP_WIDTH), BF16),
        compiler_params=_params(),
        name="gmlp",
    )(n1, wuv, ws, bsT, lng, lnb)


def _attn_kernel(qT_ref, qiT_ref, wiT_ref, kk2_ref, ckv_ref, ckvT_ref, wukT_ref, wuvT_ref,
                 a_ref,
                 sc_ref, qabs_ref, thr_ref, keep_ref, m_ref, l_ref, acc_ref, lg_ref, bmax_ref, sb_ref):
    i = pl.program_id(0)
    nkb = i + 1

    for h in range(N_HEADS):
        qa = _dot(wukT_ref[h], qT_ref[h * HEAD_DIM:(h + 1) * HEAD_DIM, :]) * (
            HEAD_DIM ** -0.5 * LOG2_E)
        qabs_ref[h] = qa.astype(BF16)

    q_pos = i * TQ + lax.broadcasted_iota(jnp.int32, (KB, TQ), 1)
    k_off = lax.broadcasted_iota(jnp.int32, (KB, TQ), 0)

    def score_body(kb, carry):
        smax, smin = carry
        keys = kk2_ref[kb]
        acc = jnp.zeros((KB, TQ), F32)
        for j in range(IDX_HEADS // 2):
            d = _dot(keys, qiT_ref[j * LANES:(j + 1) * LANES, :])
            acc = acc + jnp.maximum(d[0:KB], 0.0) * wiT_ref[2 * j:2 * j + 1, :]
            acc = acc + jnp.maximum(d[KB:2 * KB], 0.0) * wiT_ref[2 * j + 1:2 * j + 2, :]
        causal = (kb * KB + k_off) <= q_pos
        sc_ref[kb] = jnp.where(causal, acc, -jnp.inf)
        sb_ref[kb] = jnp.where(causal, acc, -jnp.inf).astype(BF16)
        smax = jnp.maximum(smax, jnp.max(jnp.where(causal, acc, -jnp.inf), axis=0, keepdims=True))
        smin = jnp.minimum(smin, jnp.min(jnp.where(causal, acc, jnp.inf), axis=0, keepdims=True))
        return smax, smin

    n_pairs = (nkb + 1) // 2
    smax, smin = lax.fori_loop(
        0, n_pairs, lambda j, c: score_body(2 * j + 1, score_body(2 * j, c)),
        (jnp.full((1, TQ), -jnp.inf, F32), jnp.full((1, TQ), jnp.inf, F32)))

    def count_ge(x):
        def body(j, cnt):
            for kb in (2 * j, 2 * j + 1):
                ge = jnp.where(sc_ref[kb] >= x, 1.0, 0.0)
                cnt = cnt + jnp.sum(ge.reshape(KB // COUNT_ROWS, COUNT_ROWS, TQ), axis=0)
            return cnt
        cnt = lax.fori_loop(0, n_pairs, body, jnp.zeros((COUNT_ROWS, TQ), F32))
        return jnp.sum(cnt, axis=0, keepdims=True)

    def max_below(x):
        def body(j, best):
            for kb in (2 * j, 2 * j + 1):
                s = sc_ref[kb]
                v = jnp.where(s < x, s, -jnp.inf)
                best = jnp.maximum(best, jnp.max(v.reshape(KB // COUNT_ROWS, COUNT_ROWS, TQ), axis=0))
            return best
        best = lax.fori_loop(0, n_pairs, body, jnp.full((COUNT_ROWS, TQ), -jnp.inf, F32))
        return jnp.max(best, axis=0, keepdims=True)

    n_causal = (i * TQ + 1 + lax.broadcasted_iota(jnp.int32, (1, TQ), 1)).astype(F32)
    want = jnp.minimum(n_causal, float(TOPK))
    open0 = (n_causal != want).astype(F32)

    def search_cond(state):
        return jnp.logical_and(state[0] < MAX_SEARCH_ITERS, state[-1] > 0.0)

    def count_ge_coarse(x):
        xb = x.astype(BF16)
        one = jnp.ones((), BF16)
        zero = jnp.zeros((), BF16)

        def body(j, cnt):
            for kb in (2 * j, 2 * j + 1):
                ge = jnp.where(sb_ref[kb] >= xb, one, zero).reshape(KB // COUNT_ROWS, COUNT_ROWS, TQ)
                for part in range(KB // COUNT_ROWS):
                    cnt = cnt + ge[part]
            return cnt
        cnt = lax.fori_loop(0, n_pairs, body, jnp.zeros((COUNT_ROWS, TQ), BF16))
        return jnp.sum(cnt.astype(F32), axis=0, keepdims=True)

    step_out = lambda v: jnp.maximum(jnp.abs(v), 1e-30) * BF16_STEP
    as_bf16 = lambda v: v.astype(BF16).astype(F32)
    lo_c0 = as_bf16(smin - 2.0 * step_out(smin))
    hi_c0 = as_bf16(smax + 2.0 * step_out(smax))

    def coarse_body(state):
        it, lo, hi, c_lo, c_hi, open_, _ = state
        mid = as_bf16(0.5 * lo + 0.5 * hi)
        c = count_ge_coarse(mid)
        live = jnp.logical_and(open_ > 0.0, jnp.logical_and(mid > lo, mid < hi))
        ge = c >= want
        go_lo = jnp.logical_and(live, ge)
        go_hi = jnp.logical_and(live, jnp.logical_not(ge))
        lo = jnp.where(go_lo, mid, lo)
        c_lo = jnp.where(go_lo, c, c_lo)
        hi = jnp.where(go_hi, mid, hi)
        c_hi = jnp.where(go_hi, c, c_hi)
        open_ = jnp.logical_and(live, c_lo - c_hi > MAX_BRACKET).astype(F32)
        return it + 1, lo, hi, c_lo, c_hi, open_, jnp.max(open_)

    _, lo_c, hi_c, _, _, _, _ = lax.while_loop(
        search_cond, coarse_body,
        (jnp.int32(0), lo_c0, hi_c0, n_causal, jnp.zeros((1, TQ), F32), open0, jnp.max(open0)))
    lo0 = lo_c - step_out(lo_c)
    hi0 = hi_c + step_out(hi_c)
    c_lo0 = count_ge(lo0)
    c_hi0 = count_ge(hi0)
    open1 = jnp.logical_and(c_lo0 != want, c_lo0 - c_hi0 > MAX_BRACKET).astype(F32)

    def search_body(state):
        it, lo, hi, c_lo, c_hi, open_, _ = state
        mid = 0.5 * lo + 0.5 * hi
        c = count_ge(mid)
        live = jnp.logical_and(open_ > 0.0, jnp.logical_and(mid > lo, mid < hi))
        ge = c >= want
        go_lo = jnp.logical_and(live, ge)
        go_hi = jnp.logical_and(live, jnp.logical_not(ge))
        lo = jnp.where(go_lo, mid, lo)
        c_lo = jnp.where(go_lo, c, c_lo)
        hi = jnp.where(go_hi, mid, hi)
        c_hi = jnp.where(go_hi, c, c_hi)
        wide = jnp.logical_and(c_lo != want, c_lo - c_hi > MAX_BRACKET)
        open_ = jnp.logical_and(live, wide).astype(F32)
        return it + 1, lo, hi, c_lo, c_hi, open_, jnp.max(open_)

    _, lo_f, hi_f, c_lo_f, c_hi_f, _, _ = lax.while_loop(
        search_cond, search_body,
        (jnp.int32(0), lo0, hi0, c_lo0, c_hi0, open1, jnp.max(open1)))

    stepping = c_lo_f != want

    def step_cond(state):
        return jnp.logical_and(state[0] < MAX_BRACKET, state[-1] > 0.0)

    def step_body(state):
        it, cur, c, _ = state
        act = jnp.logical_and(stepping, c < want)
        cur = jnp.where(act, max_below(cur), cur)
        c = jnp.where(act, c + 1.0, c)
        return it + 1, cur, c, jnp.max(jnp.logical_and(stepping, c < want).astype(F32))

    _, cur_f, _, _ = lax.while_loop(
        step_cond, step_body,
        (jnp.int32(0), hi_f, c_hi_f, jnp.max(stepping.astype(F32))))
    thr0 = jnp.where(stepping, cur_f, lo_f)
    thr_ref[...] = jnp.broadcast_to(thr0, (SUBLANES, TQ))
    keep_ref[...] = jnp.full((SUBLANES, TQ), float(SEQ), F32)
    unresolved0 = (count_ge(thr0) != want).astype(F32)

    @pl.when(jnp.max(unresolved0) > 0.0)
    def _():
        def next_value(lo, below):
            def body(kb, u):
                s = sc_ref[kb]
                cand = jnp.logical_and(s >= lo, s > below)
                return jnp.minimum(u, jnp.min(jnp.where(cand, s, jnp.inf), axis=0, keepdims=True))
            return lax.fori_loop(0, nkb, body, jnp.full((1, TQ), jnp.inf, F32))

        def count_gt(x):
            def body(kb, cnt):
                gt = jnp.where(sc_ref[kb] > x, 1.0, 0.0)
                return cnt + jnp.sum(gt, axis=0, keepdims=True)
            return lax.fori_loop(0, nkb, body, jnp.zeros((1, TQ), F32))

        def peel_cond(state):
            return state[-1] > 0.0

        def peel_body(state):
            below, unres, thr, keep, _ = state
            u = next_value(lo_f, below)
            c_gt = count_gt(u)
            hit = jnp.logical_and(unres > 0.0, c_gt < want)
            thr = jnp.where(hit, u, thr)
            keep = jnp.where(hit, want - c_gt, keep)
            unres = jnp.logical_and(unres > 0.0, jnp.logical_not(hit)).astype(F32)
            return u, unres, thr, keep, jnp.max(unres)

        _, _, thr_t, keep_t, _ = lax.while_loop(
            peel_cond, peel_body,
            (jnp.full((1, TQ), -jnp.inf, F32), unresolved0, thr0,
             jnp.full((1, TQ), float(SEQ), F32), jnp.float32(1.0)))
        thr_ref[...] = jnp.broadcast_to(thr_t, (SUBLANES, TQ))
        keep_ref[...] = jnp.broadcast_to(keep_t, (SUBLANES, TQ))

        r_i = lax.broadcasted_iota(jnp.int32, (KB, KB), 0)
        c_i = lax.broadcasted_iota(jnp.int32, (KB, KB), 1)
        before = (c_i < r_i).astype(BF16)

        def drop_body(kb, seen):
            s = sc_ref[kb]
            eq = jnp.logical_and(s == thr_t, unresolved0 > 0.0)
            eq_f = eq.astype(F32)
            rank = seen + _dot(before, eq_f.astype(BF16))
            sc_ref[kb] = jnp.where(jnp.logical_and(eq, rank >= keep_t), -jnp.inf, s)
            return seen + jnp.sum(eq_f, axis=0, keepdims=True)

        lax.fori_loop(0, nkb, drop_body, jnp.zeros((1, TQ), F32))

    m_ref[...] = jnp.full(m_ref.shape, MASK_VALUE, F32)
    l_ref[...] = jnp.zeros(l_ref.shape, F32)
    acc_ref[...] = jnp.zeros(acc_ref.shape, F32)

    def logits_stage(j, slot):
        scores = sc_ref[pl.ds(2 * j, 2)].reshape(2 * KB, TQ)
        bias = jnp.where(scores >= thr_ref[0:1, :], 0.0, MASK_VALUE)
        c_n = ckv_ref[pl.ds(pl.multiple_of(j * (2 * KB), 2 * KB), 2 * KB), :]
        for h in range(N_HEADS):
            lg = _dot(c_n, qabs_ref[h]) + bias
            lg_ref[slot, h] = lg
            bmax_ref[slot, h] = jnp.broadcast_to(jnp.max(lg, axis=0, keepdims=True), (SUBLANES, TQ))

    def softmax_stage(j, slot):
        c_t = ckvT_ref[j]
        for h in range(N_HEADS):
            m_old = m_ref[h, 0:1, :]
            m_new = jnp.maximum(m_old, bmax_ref[slot, h, 0:1, :])
            alpha = jnp.exp2(m_old - m_new)
            p = jnp.exp2(lg_ref[slot, h] - m_new)
            l_new = alpha * l_ref[h, 0:1, :] + jnp.sum(p, axis=0, keepdims=True)
            acc_ref[h] = acc_ref[h] * alpha + _dot(c_t, p.astype(BF16))
            m_ref[h] = jnp.broadcast_to(m_new, (SUBLANES, TQ))
            l_ref[h] = jnp.broadcast_to(l_new, (SUBLANES, TQ))

    logits_stage(0, 0)

    def att_body(j, carry):
        for slot in range(2):
            @pl.when(j % 2 == slot)
            def _():
                logits_stage(j + 1, 1 - slot)
                softmax_stage(j, slot)
        return carry

    lax.fori_loop(0, n_pairs - 1, att_body, 0)
    for slot in range(2):
        @pl.when((n_pairs - 1) % 2 == slot)
        def _():
            softmax_stage(n_pairs - 1, slot)

    for h in range(N_HEADS):
        o_t = (acc_ref[h] / l_ref[h, 0:1, :]).astype(BF16)
        a_t = _dot(wuvT_ref[h], o_t)
        a_ref[:, h * HEAD_DIM:(h + 1) * HEAD_DIM] = a_t.T.astype(BF16)


def _attn(qT, qiT, wiT, kk2, ckv, ckvT, wukT, wuvT):
    col = lambda h: pl.BlockSpec((h, TQ), lambda i: (0, i))
    return pl.pallas_call(
        _attn_kernel,
        grid=(SEQ // TQ,),
        in_specs=[col(ATTN_WIDTH), col(IDX_HEADS * IDX_DIM), col(IDX_HEADS), _resident(kk2.shape),
                  _resident(ckv.shape), _resident(ckvT.shape), _resident(wukT.shape),
                  _resident(wuvT.shape)],
        out_specs=pl.BlockSpec((TQ, ATTN_WIDTH), lambda i: (i, 0)),
        out_shape=jax.ShapeDtypeStruct((SEQ, ATTN_WIDTH), BF16),
        scratch_shapes=[
            pltpu.VMEM((NKB, KB, TQ), F32),
            pltpu.VMEM((N_HEADS, KV_LATENT, TQ), BF16),
            pltpu.VMEM((SUBLANES, TQ), F32),
            pltpu.VMEM((SUBLANES, TQ), F32),
            pltpu.VMEM((N_HEADS, SUBLANES, TQ), F32),
            pltpu.VMEM((N_HEADS, SUBLANES, TQ), F32),
            pltpu.VMEM((N_HEADS, KV_LATENT, TQ), F32),
            pltpu.VMEM((2, N_HEADS, 2 * KB, TQ), F32),
            pltpu.VMEM((2, N_HEADS, SUBLANES, TQ), F32),
            pltpu.VMEM((NKB, KB, TQ), BF16),
        ],
        compiler_params=_params(),
        name="attn",
    )(qT, qiT, wiT, kk2, ckv, ckvT, wukT, wuvT)


def _merge_kernel(n1_ref, a_ref, m_ref, wga_ref, wgb_ref, wba_ref, wbg_ref, o_ref):
    n1 = n1_ref[...]
    br_a = jax.nn.sigmoid(_dot(n1, wga_ref[...])) * _dot(a_ref[...], wba_ref[...])
    br_b = jax.nn.sigmoid(_dot(n1, wgb_ref[...])) * _dot(m_ref[...], wbg_ref[...])
    o_ref[...] = (br_a + br_b).astype(BF16)


def _merge(n1, a, m, wga, wgb, wba, wbg):
    tm = ROW_TILE
    row = lambda w: pl.BlockSpec((tm, w), lambda i: (i, 0))
    return pl.pallas_call(
        _merge_kernel,
        grid=(SEQ // tm,),
        in_specs=[row(D_MODEL), row(ATTN_WIDTH), row(GMLP_WIDTH), _resident(wga.shape),
                  _resident(wgb.shape), _resident(wba.shape), _resident(wbg.shape)],
        out_specs=row(D_MODEL),
        out_shape=jax.ShapeDtypeStruct((SEQ, D_MODEL), BF16),
        compiler_params=_params(),
        name="merge",
    )(n1, a, m, wga, wgb, wba, wbg)


ROUTE_ROWS = 8


def _outproj_kernel(mg_ref, x_ref, wo_ref, g2_ref, wrc_ref, br_ref,
                    h_ref, n2_ref, route_ref, counts_ref, carry_ref):
    i = pl.program_id(0)
    tm = ROW_TILE

    @pl.when(i == 0)
    def _():
        carry_ref[...] = jnp.zeros(carry_ref.shape, F32)

    h = x_ref[...] + _dot(mg_ref[...], wo_ref[...])
    h_ref[...] = h
    n2 = h * lax.rsqrt(jnp.mean(h * h, axis=-1, keepdims=True) + EPS) * g2_ref[...]
    n2_ref[...] = n2
    n2_hi = n2.astype(BF16)
    n2_lo = (n2 - n2_hi.astype(F32)).astype(BF16)
    hh_hl = _dot(n2_hi, wrc_ref[...])
    lh = _dot(n2_lo, wrc_ref[:, 0:LANES])
    logits = (hh_hl[:, 0:LANES] + hh_hl[:, LANES:2 * LANES] + lh).T + br_ref[...]
    row = lax.broadcasted_iota(jnp.int32, (LANES, tm), 0).astype(F32)
    is_group = row < N_GROUPS
    gl = jnp.where(is_group, logits, -jnp.inf)
    gmax = jnp.max(gl, axis=0, keepdims=True)
    gsum = jnp.sum(jnp.where(is_group, jnp.exp(logits - gmax), 0.0), axis=0, keepdims=True)
    g_val = 1.0 / gsum
    g_idx = jnp.min(jnp.where(gl == gmax, row, float(LANES)), axis=0, keepdims=True)
    e_id = row - N_GROUPS
    in_group = jnp.logical_and(
        jnp.logical_and(e_id >= 0, e_id < N_EXPERTS),
        jnp.floor(e_id * (1.0 / EXPERTS_PER_GROUP)) == g_idx)
    sel = jnp.where(in_group, logits, -jnp.inf)
    v1 = jnp.max(sel, axis=0, keepdims=True)
    i1 = jnp.min(jnp.where(sel == v1, row, float(LANES)), axis=0, keepdims=True)
    sel2 = jnp.where(row == i1, -jnp.inf, sel)
    v2 = jnp.max(sel2, axis=0, keepdims=True)
    i2 = jnp.min(jnp.where(sel2 == v2, row, float(LANES)), axis=0, keepdims=True)
    x2 = jnp.exp(v2 - v1)
    den = 1.0 + x2
    w1 = g_val * (1.0 / den)
    w2 = g_val * (x2 / den)
    e1 = i1 - N_GROUPS
    e2 = i2 - N_GROUPS

    e_row = lax.broadcasted_iota(jnp.int32, (N_EXPERTS, tm), 0).astype(F32)
    hit1 = (e_row == e1).astype(F32)
    hit2 = (e_row == e2).astype(F32)
    hits = hit1 + hit2
    t_from = lax.broadcasted_iota(jnp.int32, (tm, tm), 0)
    t_to = lax.broadcasted_iota(jnp.int32, (tm, tm), 1)
    earlier = (t_from < t_to).astype(BF16)
    before = carry_ref[:, 0:1] + _dot(hits.astype(BF16), earlier)
    rank1 = jnp.sum(hit1 * before, axis=0, keepdims=True)
    rank2 = jnp.sum(hit2 * before, axis=0, keepdims=True)
    carry = carry_ref[...] + jnp.sum(hits, axis=1, keepdims=True)
    carry_ref[...] = carry
    counts_ref[...] = carry

    r = lax.broadcasted_iota(jnp.int32, (ROUTE_ROWS, tm), 0)
    route = jnp.where(r == 0, e1, 0.0)
    for k, v in enumerate((e2, w1, w2, rank1, rank2), start=1):
        route = jnp.where(r == k, v, route)
    route_ref[...] = route


def _outproj(mg, x2, wo, g2, wr_cat, br):
    tm = ROW_TILE
    row = lambda w: pl.BlockSpec((tm, w), lambda i: (i, 0))
    return pl.pallas_call(
        _outproj_kernel,
        grid=(SEQ // tm,),
        in_specs=[row(D_MODEL), row(D_MODEL), _resident(wo.shape), _resident(g2.shape),
                  _resident(wr_cat.shape), _resident(br.shape)],
        out_specs=[row(D_MODEL), row(D_MODEL),
                   pl.BlockSpec((ROUTE_ROWS, tm), lambda i: (0, i)),
                   pl.BlockSpec((N_EXPERTS, LANES), lambda i: (0, 0))],
        out_shape=[jax.ShapeDtypeStruct((SEQ, D_MODEL), F32),
                   jax.ShapeDtypeStruct((SEQ, D_MODEL), F32),
                   jax.ShapeDtypeStruct((ROUTE_ROWS, SEQ), F32),
                   jax.ShapeDtypeStruct((N_EXPERTS, LANES), F32)],
        scratch_shapes=[pltpu.VMEM((N_EXPERTS, LANES), F32)],
        compiler_params=_params(),
        name="outproj",
    )(mg, x2, wo, g2, wr_cat, br)


def _start_row_gather(src_hbm, row_of, dst, slot, sem, rows):
    for r in rows:
        pltpu.make_async_copy(src_hbm.at[pl.ds(row_of(r), 1), :],
                              dst.at[slot, pl.ds(r, 1), :], sem.at[slot]).start()


def _wait_row_gather(src_hbm, dst, slot, sem, n_rows):
    pltpu.make_async_copy(src_hbm.at[pl.ds(0, n_rows), :], dst.at[slot], sem.at[slot]).wait()


def _scatter_kernel(padstart_ref, padlen_ref, pos_ref, x_ref, xs_hbm, xf, xbuf, zbuf, sem, zsem):
    i = pl.program_id(0)
    n = pl.num_programs(0)
    tm = ROW_TILE
    slot = i % 2

    def row_copies_done(s):
        for _ in range(2):
            pltpu.make_async_copy(xbuf.at[s], xs_hbm.at[pl.ds(0, tm)], sem.at[s]).wait()

    def pad_copy(e, bit):
        rows = 1 << bit
        start = padstart_ref[e] + (padlen_ref[e] & (rows - 1))
        return pltpu.make_async_copy(zbuf.at[pl.ds(0, rows)], xs_hbm.at[pl.ds(start, rows)], zsem.at[0])

    def for_each_pad_piece(fn):
        def body(e, carry):
            for bit in range(PAD_BITS):
                @pl.when((lax.shift_right_logical(padlen_ref[e], bit) & 1) == 1)
                def _():
                    fn(pad_copy(e, bit))
            return carry
        lax.fori_loop(0, N_EXPERTS, body, 0)

    @pl.when(i == 0)
    def _():
        zbuf[...] = jnp.zeros(zbuf.shape, BF16)
        for_each_pad_piece(lambda c: c.start())

    @pl.when(i >= 2)
    def _():
        row_copies_done(slot)

    for a in range(TOKEN_ROWS):
        xf[pl.ds(a, tm, stride=TOKEN_ROWS), :] = x_ref[:, a * LANES:(a + 1) * LANES]
    for s in range(2):
        @pl.when(slot == s)
        def _():
            xbuf[s] = xf[...].reshape(tm, TOKEN_ROWS, LANES).astype(BF16)
            for r in range(2 * tm):
                pltpu.make_async_copy(xbuf.at[s, r % tm], xs_hbm.at[pos_ref[0, 0, r]],
                                      sem.at[s]).start()

    @pl.when(i == n - 1)
    def _():
        row_copies_done(1 - slot)
        row_copies_done(slot)
        for_each_pad_piece(lambda c: c.wait())


def _scatter_rows(padstart, padlen, pos_tiles, n2):
    tm = ROW_TILE
    grid_spec = pltpu.PrefetchScalarGridSpec(
        num_scalar_prefetch=2,
        grid=(SEQ // tm,),
        in_specs=[pl.BlockSpec((1, 1, 2 * tm), lambda i, *_: (i, 0, 0), memory_space=pltpu.SMEM),
                  pl.BlockSpec((tm, D_MODEL), lambda i, *_: (i, 0))],
        out_specs=pl.BlockSpec(memory_space=pl.ANY),
        scratch_shapes=[pltpu.VMEM((tm * TOKEN_ROWS, LANES), F32),
                        pltpu.VMEM((2, tm, TOKEN_ROWS, LANES), BF16),
                        pltpu.VMEM((1 << (PAD_BITS - 1), TOKEN_ROWS, LANES), BF16),
                        pltpu.SemaphoreType.DMA((2,)),
                        pltpu.SemaphoreType.DMA((1,))],
    )
    return pl.pallas_call(
        _scatter_kernel,
        grid_spec=grid_spec,
        out_shape=jax.ShapeDtypeStruct((N_EXPERT_TILES * EXPERT_TILE, TOKEN_ROWS, LANES), BF16),
        compiler_params=_params(),
        name="scatter_rows",
    )(padstart, padlen, pos_tiles, n2)


def _expert_kernel(ntl_ref, tst_ref, ntot_ref,
                   xs_hbm, wg_hbm, wu_hbm, wd_hbm,
                   y_hbm,
                   xbuf, ybuf, xsem, ysem, wg_st, wu_st, wd_st, wsem, wgb, wub, wdb, rows_f):
    e = pl.program_id(0)
    ntot = ntot_ref[0]
    n_rows = EXPERT_TILE * TOKEN_ROWS

    def weight_copies(ex, slot):
        return [pltpu.make_async_copy(src.at[ex], dst.at[slot], wsem.at[slot, j])
                for j, (src, dst) in enumerate(((wg_hbm, wg_st), (wu_hbm, wu_st), (wd_hbm, wd_st)))]

    def tile_rows(t):
        return pl.ds(pl.multiple_of(t * EXPERT_TILE, EXPERT_TILE), EXPERT_TILE)

    def x_copy(t, slot):
        return pltpu.make_async_copy(xs_hbm.at[tile_rows(t)], xbuf.at[slot], xsem.at[slot])

    def y_copy(t, slot):
        return pltpu.make_async_copy(ybuf.at[slot], y_hbm.at[tile_rows(t), :], ysem.at[slot])

    @pl.when(e == 0)
    def _():
        for c in weight_copies(0, 0):
            c.start()
        x_copy(0, 0).start()

    @pl.when(e + 1 < pl.num_programs(0))
    def _():
        for c in weight_copies(e + 1, (e + 1) % 2):
            c.start()

    n_here = ntl_ref[e]
    wslot = e % 2
    for c in weight_copies(e, wslot):
        c.wait()

    @pl.when(n_here > 0)
    def _():
        wgb[...] = wg_st[wslot].astype(BF16)
        wub[...] = wu_st[wslot].astype(BF16)
        wdb[...] = wd_st[wslot].astype(BF16)

        def tile_body(k, carry):
            t = tst_ref[e] + k
            slot = t % 2
            x_copy(t, slot).wait()

            @pl.when(t + 1 < ntot)
            def _():
                x_copy(t + 1, 1 - slot).start()

            rows_f[...] = xbuf[slot].astype(F32).reshape(n_rows, LANES)
            xt = jnp.concatenate(
                [rows_f[pl.ds(a, EXPERT_TILE, stride=TOKEN_ROWS), :] for a in range(TOKEN_ROWS)],
                axis=1).astype(BF16)
            hid = jax.nn.silu(_dot(xt, wgb[...])) * _dot(xt, wub[...])
            y = _dot(hid.astype(BF16), wdb[...])

            @pl.when(t >= 2)
            def _():
                y_copy(t - 2, slot).wait()

            ybuf[slot] = y
            y_copy(t, slot).start()
            return carry
        lax.fori_loop(0, n_here, tile_body, 0)

    @pl.when(e == pl.num_programs(0) - 1)
    def _():
        @pl.when(ntot >= 2)
        def _():
            y_copy(ntot - 2, (ntot - 2) % 2).wait()
        y_copy(ntot - 1, (ntot - 1) % 2).wait()


def _experts(ntl, tst, ntot, x_sorted, w_gate, w_up, w_down):
    hbm = pl.BlockSpec(memory_space=pl.ANY)
    grid_spec = pltpu.PrefetchScalarGridSpec(
        num_scalar_prefetch=3,
        grid=(N_EXPERTS,),
        in_specs=[hbm, hbm, hbm, hbm],
        out_specs=hbm,
        scratch_shapes=[
            pltpu.VMEM((2, EXPERT_TILE, TOKEN_ROWS, LANES), BF16),
            pltpu.VMEM((2, EXPERT_TILE, D_MODEL), F32),
            pltpu.SemaphoreType.DMA((2,)),
            pltpu.SemaphoreType.DMA((2,)),
            pltpu.VMEM((2, D_MODEL, EXPERT_FF), F32),
            pltpu.VMEM((2, D_MODEL, EXPERT_FF), F32),
            pltpu.VMEM((2, EXPERT_FF, D_MODEL), F32),
            pltpu.SemaphoreType.DMA((2, 3)),
            pltpu.VMEM((D_MODEL, EXPERT_FF), BF16),
            pltpu.VMEM((D_MODEL, EXPERT_FF), BF16),
            pltpu.VMEM((EXPERT_FF, D_MODEL), BF16),
            pltpu.VMEM((EXPERT_TILE * TOKEN_ROWS, LANES), F32),
        ],
    )
    return pl.pallas_call(
        _expert_kernel,
        grid_spec=grid_spec,
        out_shape=jax.ShapeDtypeStruct((N_EXPERT_TILES * EXPERT_TILE, D_MODEL), F32),
        compiler_params=_params(),
        name="experts",
    )(ntl, tst, ntot, x_sorted, w_gate, w_up, w_down)


def _final_kernel(pos_ref, posn_ref, h_ref, route_ref, gf_ref, y_hbm, o_ref, ybuf, sem):
    i = pl.program_id(0)
    n = pl.num_programs(0)
    tm = ROW_TILE

    @pl.when(i == 0)
    def _():
        _start_row_gather(y_hbm, lambda r: pos_ref[0, 0, r], ybuf, 0, sem, range(2 * tm))

    for s in range(2):
        @pl.when(jnp.logical_and(i + 1 < n, (i + 1) % 2 == s))
        def _():
            _start_row_gather(y_hbm, lambda r: posn_ref[0, 0, r], ybuf, s, sem, range(2 * tm))

    slot = i % 2
    _wait_row_gather(y_hbm, ybuf, slot, sem, 2 * tm)
    w1 = route_ref[:, 0:1]
    w2 = route_ref[:, 1:2]
    h2 = h_ref[...] + (w1 * ybuf[slot, 0:tm, :] + w2 * ybuf[slot, tm:2 * tm, :])
    o_ref[...] = h2 * lax.rsqrt(jnp.mean(h2 * h2, axis=-1, keepdims=True) + EPS) * gf_ref[...]


def _final(pos, h, route, gf, y_sorted):
    tm = ROW_TILE
    n = SEQ // tm
    row = lambda w: pl.BlockSpec((tm, w), lambda i: (i, 0))
    smem_blk = lambda f: pl.BlockSpec((1, 1, 2 * tm), f, memory_space=pltpu.SMEM)
    return pl.pallas_call(
        _final_kernel,
        grid=(n,),
        in_specs=[smem_blk(lambda i: (i, 0, 0)),
                  smem_blk(lambda i: (jnp.minimum(i + 1, n - 1), 0, 0)),
                  row(D_MODEL), row(2), _resident(gf.shape),
                  pl.BlockSpec(memory_space=pl.ANY)],
        out_specs=row(D_MODEL),
        out_shape=jax.ShapeDtypeStruct((SEQ, D_MODEL), F32),
        scratch_shapes=[pltpu.VMEM((2, 2 * tm, D_MODEL), F32), pltpu.SemaphoreType.DMA((2,))],
        compiler_params=_params(),
        name="final",
    )(pos, pos, h, route, gf, y_sorted)


def _dispatch(route, counts):
    i32 = jnp.int32
    e_pair = route[0:2].T.astype(i32).reshape(-1)
    rank = route[4:6].T.astype(i32).reshape(-1)
    cnt = counts[:, 0].astype(i32)
    ntl = (cnt + EXPERT_TILE - 1) // EXPERT_TILE
    tend = jnp.cumsum(ntl)
    tst = tend - ntl
    experts = jnp.arange(N_EXPERTS, dtype=i32)
    of_pair = e_pair[:, None] == experts[None, :]
    pos = jnp.sum(jnp.where(of_pair, (tst * EXPERT_TILE)[None, :], 0), axis=1) + rank
    padstart = tst * EXPERT_TILE + cnt
    padlen = ntl * EXPERT_TILE - cnt
    return ntl, tst, tend[-1:], padstart, padlen, pos


def kernel(x, norm1_g, w_in, kv_norm_g, w_uk, w_uv, gmlp_ws, gmlp_bs, ln_v_g, ln_v_b, w_br_attn,
           w_br_gmlp, w_out, norm2_g, w_group, b_group, w_router, b_router, w_e_gate, w_e_up,
           w_e_down, norm_f_g):
    assert x.shape == (1, SEQ, D_MODEL)
    x2 = x.reshape(SEQ, D_MODEL)
    row_vec = lambda v: v.reshape(1, -1).astype(F32)

    assert w_in.shape == (D_MODEL, IN_COLS)
    wqT, wc, wqiT, wkk, wwT, wuv_in, wga, wgb = _split_w_in(w_in.T)

    n1, qT, ckv, ckvT, qiT, kk2, wiT = _proj(x2, row_vec(norm1_g), wqT, wc, wqiT, wkk, wwT,
                                             row_vec(kv_norm_g))
    m = _gmlp(n1, wuv_in, gmlp_ws, jnp.pad(gmlp_bs.T, ((0, 0), (0, LANES - GMLP_GROUPS))),
              row_vec(ln_v_g), row_vec(ln_v_b))
    a = _attn(qT, qiT, wiT, kk2, ckv, ckvT,
              jnp.swapaxes(w_uk, 1, 2).astype(BF16), jnp.swapaxes(w_uv, 1, 2).astype(BF16))
    mg = _merge(n1, a, m, wga, wgb, w_br_attn.astype(BF16), w_br_gmlp.astype(BF16))

    w_route = jnp.pad(jnp.concatenate([w_group, w_router], axis=1),
                      ((0, 0), (0, LANES - N_GROUPS - N_EXPERTS)))
    b_route = jnp.pad(jnp.concatenate([b_group, b_router]), (0, LANES - N_GROUPS - N_EXPERTS))
    wr_hi = w_route.astype(BF16)
    wr_lo = (w_route - wr_hi.astype(F32)).astype(BF16)
    h, n2, route, counts = _outproj(mg, x2, w_out.astype(BF16), row_vec(norm2_g),
                                    jnp.concatenate([wr_hi, wr_lo], axis=1),
                                    b_route.reshape(LANES, 1))

    ntl, tst, ntot, padstart, padlen, pos = _dispatch(route, counts)
    pos_tiles = pos.reshape(SEQ // ROW_TILE, ROW_TILE, 2).transpose(0, 2, 1).reshape(
        SEQ // ROW_TILE, 1, 2 * ROW_TILE)
    x_sorted = _scatter_rows(padstart, padlen, pos_tiles, n2)
    y_sorted = _experts(ntl, tst, ntot, x_sorted, w_e_gate, w_e_up, w_e_down)
    out = _final(pos_tiles, h, route[2:4].T, row_vec(norm_f_g), y_sorted)
    return out.reshape(1, SEQ, D_MODEL)
```

```python
import jax
import jax.numpy as jnp
import numpy as np
from jax import lax
from jax.experimental import pallas as pl
from jax.experimental.pallas import tpu as pltpu

F32 = jnp.float32
BF16 = jnp.bfloat16

D_MODEL = 2048
SEQ = 8192
N_HEADS = 8
HEAD_DIM = 128
KV_LATENT = 256
IDX_HEADS = 16
IDX_DIM = 64
TOPK = 256
ATTN_WIDTH = N_HEADS * HEAD_DIM
GMLP_GROUPS = 8
GMLP_WIDTH = 1024
CHUNK = 128
N_GROUPS = 8
EXPERTS_PER_GROUP = 8
N_EXPERTS = 64
EXPERT_FF = 512
EPS = 1e-6

LANES = 128
SUBLANES = 8
VMEM_LIMIT = 60 * 1024 * 1024
MASK_VALUE = -0.7 * float(np.finfo(np.float32).max)
LOG2_E = float(np.log2(np.e))

ROW_TILE = 256
GMLP_TILE = 512
TQ = 256
KB = 256
NKB = SEQ // KB
COUNT_ROWS = 32
MAX_SEARCH_ITERS = 64
MAX_BRACKET = 4
EXPERT_TILE = 256
N_EXPERT_TILES = (2 * SEQ) // EXPERT_TILE + N_EXPERTS
PAD_BITS = 8
TOKEN_ROWS = D_MODEL // LANES


def _dot(a, b):
    return jnp.dot(a, b, preferred_element_type=F32)


def _dot_nt(a, b):
    return lax.dot_general(a, b, (((1,), (1,)), ((), ())), preferred_element_type=F32)


def _resident(shape):
    zeros = (0,) * len(shape)
    return pl.BlockSpec(shape, lambda *_: zeros, pipeline_mode=pl.Buffered(1))


def _params(n_axes=1):
    return pltpu.CompilerParams(
        dimension_semantics=("arbitrary",) * n_axes, vmem_limit_bytes=VMEM_LIMIT)


C_Q = ATTN_WIDTH
C_KV = C_Q + KV_LATENT
C_QI = C_KV + IDX_HEADS * IDX_DIM
C_K = C_QI + IDX_DIM
C_W = C_K + IDX_HEADS
C_UV = C_W + 2 * GMLP_WIDTH
C_GA = C_UV + D_MODEL
IN_COLS = C_GA + D_MODEL


def _split_w_in_kernel(wt_ref, wqT_ref, wc_ref, wqiT_ref, wkk_ref, wwT_ref, wuv_ref, wga_ref, wgb_ref):
    wqT_ref[...] = wt_ref[0:C_Q, :].astype(BF16)
    wc_ref[...] = wt_ref[C_Q:C_KV, :].T.astype(BF16)
    wqiT_ref[...] = wt_ref[C_KV:C_QI, :].astype(BF16)
    k_t = wt_ref[C_QI:C_K, :].T.astype(BF16)
    zero = jnp.zeros_like(k_t)
    wkk_ref[...] = jnp.concatenate([k_t, zero, zero, k_t], axis=1)
    wwT_ref[...] = wt_ref[C_K:C_W, :].astype(BF16)
    wuv_ref[...] = wt_ref[C_W:C_UV, :].T.astype(BF16)
    wga_ref[...] = wt_ref[C_UV:C_GA, :].T.astype(BF16)
    wgb_ref[...] = wt_ref[C_GA:IN_COLS, :].T.astype(BF16)


def _split_w_in(w_in_t):
    tr = ROW_TILE
    rows = lambda w: pl.BlockSpec((tr, w), lambda i: (i, 0))
    cols = lambda h: pl.BlockSpec((h, tr), lambda i: (0, i))
    sds = jax.ShapeDtypeStruct
    return pl.pallas_call(
        _split_w_in_kernel,
        grid=(D_MODEL // tr,),
        in_specs=[cols(IN_COLS)],
        out_specs=[cols(C_Q), rows(KV_LATENT), cols(C_QI - C_KV), rows(2 * LANES), cols(IDX_HEADS),
                   rows(2 * GMLP_WIDTH), rows(D_MODEL), rows(D_MODEL)],
        out_shape=[sds((C_Q, D_MODEL), BF16), sds((D_MODEL, KV_LATENT), BF16),
                   sds((C_QI - C_KV, D_MODEL), BF16), sds((D_MODEL, 2 * LANES), BF16),
                   sds((IDX_HEADS, D_MODEL), BF16), sds((D_MODEL, 2 * GMLP_WIDTH), BF16),
                   sds((D_MODEL, D_MODEL), BF16), sds((D_MODEL, D_MODEL), BF16)],
        compiler_params=_params(),
        name="split_w_in",
    )(w_in_t)


def _proj_kernel(x_ref, g1_ref, wqT_ref, wc_ref, wqiT_ref, wkk_ref, wwT_ref, kvg_ref,
                 n1_ref, qT_ref, ckv_ref, ckvT_ref, qiT_ref, kk2_ref, wiT_ref):
    x = x_ref[...]
    ms = jnp.mean(x * x, axis=-1, keepdims=True)
    n1 = (x * lax.rsqrt(ms + EPS) * g1_ref[...]).astype(BF16)
    n1_ref[...] = n1
    qT_ref[...] = _dot_nt(wqT_ref[...], n1).astype(BF16)
    qiT_ref[...] = (_dot_nt(wqiT_ref[...], n1) * (IDX_DIM ** -0.5)).astype(BF16)
    wiT_ref[...] = _dot_nt(wwT_ref[...], n1) * (IDX_HEADS ** -0.5)
    c = _dot(n1, wc_ref[...])
    c = c * lax.rsqrt(jnp.mean(c * c, axis=-1, keepdims=True) + EPS) * kvg_ref[...]
    ckv_ref[...] = c.astype(BF16)
    ckvT_ref[0] = c.T.astype(BF16)
    kk = _dot(n1, wkk_ref[...]).astype(BF16)
    kk2_ref[0, 0:KB, :] = kk[:, 0:LANES]
    kk2_ref[0, KB:2 * KB, :] = kk[:, LANES:2 * LANES]


def _proj(x2, g1, wqT, wc, wqiT, wkk, wwT, kvg):
    tm = KB
    row = lambda w: pl.BlockSpec((tm, w), lambda i: (i, 0))
    col = lambda h: pl.BlockSpec((h, tm), lambda i: (0, i))
    return pl.pallas_call(
        _proj_kernel,
        grid=(SEQ // tm,),
        in_specs=[row(D_MODEL), _resident(g1.shape), _resident(wqT.shape), _resident(wc.shape),
                  _resident(wqiT.shape), _resident(wkk.shape), _resident(wwT.shape),
                  _resident(kvg.shape)],
        out_specs=[row(D_MODEL), col(ATTN_WIDTH), row(KV_LATENT),
                   pl.BlockSpec((1, KV_LATENT, tm), lambda i: (i // 2, 0, i % 2)),
                   col(IDX_HEADS * IDX_DIM),
                   pl.BlockSpec((1, 2 * tm, LANES), lambda i: (i, 0, 0)),
                   col(IDX_HEADS)],
        out_shape=[
            jax.ShapeDtypeStruct((SEQ, D_MODEL), BF16),
            jax.ShapeDtypeStruct((ATTN_WIDTH, SEQ), BF16),
            jax.ShapeDtypeStruct((SEQ, KV_LATENT), BF16),
            jax.ShapeDtypeStruct((NKB // 2, KV_LATENT, 2 * KB), BF16),
            jax.ShapeDtypeStruct((IDX_HEADS * IDX_DIM, SEQ), BF16),
            jax.ShapeDtypeStruct((NKB, 2 * KB, LANES), BF16),
            jax.ShapeDtypeStruct((IDX_HEADS, SEQ), F32),
        ],
        compiler_params=_params(),
        name="proj",
    )(x2, g1, wqT, wc, wqiT, wkk, wwT, kvg)


def _gmlp_kernel(n1_ref, wuv_ref, ws_ref, bsT_ref, lng_ref, lnb_ref, m_ref):
    uv = _dot(n1_ref[...], wuv_ref[...])
    z = jax.nn.gelu(uv)
    u = z[:, :GMLP_WIDTH]
    v = z[:, GMLP_WIDTH:]
    mu = jnp.mean(v, axis=-1, keepdims=True)
    var = jnp.mean(jnp.square(v - mu), axis=-1, keepdims=True)
    vn = ((v - mu) * lax.rsqrt(var + EPS) * lng_ref[...] + lnb_ref[...]).astype(BF16)
    t_pos = lax.broadcasted_iota(jnp.int32, (CHUNK, CHUNK), 0)
    s_pos = lax.broadcasted_iota(jnp.int32, (CHUNK, CHUNK), 1)
    causal = s_pos <= t_pos
    for g in range(GMLP_GROUPS):
        wm = jnp.where(causal, ws_ref[g], 0.0).astype(BF16)
        bias = bsT_ref[:, g:g + 1]
        cols = slice(g * LANES, (g + 1) * LANES)
        for c in range(GMLP_TILE // CHUNK):
            rows = slice(c * CHUNK, (c + 1) * CHUNK)
            y = _dot(wm, vn[rows, cols]) + bias
            m_ref[rows, cols] = (u[rows, cols] * y).astype(BF16)


def _gmlp(n1, wuv, ws, bsT, lng, lnb):
    tm = GMLP_TILE
    return pl.pallas_call(
        _gmlp_kernel,
        grid=(SEQ // tm,),
        in_specs=[pl.BlockSpec((tm, D_MODEL), lambda i: (i, 0)), _resident(wuv.shape),
                  _resident(ws.shape), _resident(bsT.shape), _resident(lng.shape),
                  _resident(lnb.shape)],
        out_specs=pl.BlockSpec((tm, GMLP_WIDTH), lambda i: (i, 0)),
        out_shape=jax.ShapeDtypeStruct((SEQ, GMLP_WIDTH), BF16),
        compiler_params=_params(),
        name="gmlp",
    )(n1, wuv, ws, bsT, lng, lnb)


def _attn_kernel(qT_ref, qiT_ref, wiT_ref, kk2_ref, ckv_ref, ckvT_ref, wukT_ref, wuvT_ref,
                 a_ref,
                 sc_ref, qabs_ref, thr_ref, keep_ref, m_ref, l_ref, acc_ref, lg_ref, bmax_ref):
    i = pl.program_id(0)
    nkb = i + 1

    for h in range(N_HEADS):
        qa = _dot(wukT_ref[h], qT_ref[h * HEAD_DIM:(h + 1) * HEAD_DIM, :]) * (
            HEAD_DIM ** -0.5 * LOG2_E)
        qabs_ref[h] = qa.astype(BF16)

    q_pos = i * TQ + lax.broadcasted_iota(jnp.int32, (KB, TQ), 1)
    k_off = lax.broadcasted_iota(jnp.int32, (KB, TQ), 0)

    def score_body(kb, carry):
        smax, smin = carry
        keys = kk2_ref[kb]
        acc = jnp.zeros((KB, TQ), F32)
        for j in range(IDX_HEADS // 2):
            d = _dot(keys, qiT_ref[j * LANES:(j + 1) * LANES, :])
            acc = acc + jnp.maximum(d[0:KB], 0.0) * wiT_ref[2 * j:2 * j + 1, :]
            acc = acc + jnp.maximum(d[KB:2 * KB], 0.0) * wiT_ref[2 * j + 1:2 * j + 2, :]
        causal = (kb * KB + k_off) <= q_pos
        sc_ref[kb] = jnp.where(causal, acc, -jnp.inf)
        smax = jnp.maximum(smax, jnp.max(jnp.where(causal, acc, -jnp.inf), axis=0, keepdims=True))
        smin = jnp.minimum(smin, jnp.min(jnp.where(causal, acc, jnp.inf), axis=0, keepdims=True))
        return smax, smin

    n_pairs = (nkb + 1) // 2
    smax, smin = lax.fori_loop(
        0, n_pairs, lambda j, c: score_body(2 * j + 1, score_body(2 * j, c)),
        (jnp.full((1, TQ), -jnp.inf, F32), jnp.full((1, TQ), jnp.inf, F32)))

    def count_ge(x):
        def body(j, cnt):
            for kb in (2 * j, 2 * j + 1):
                ge = jnp.where(sc_ref[kb] >= x, 1.0, 0.0)
                cnt = cnt + jnp.sum(ge.reshape(KB // COUNT_ROWS, COUNT_ROWS, TQ), axis=0)
            return cnt
        cnt = lax.fori_loop(0, n_pairs, body, jnp.zeros((COUNT_ROWS, TQ), F32))
        return jnp.sum(cnt, axis=0, keepdims=True)

    def max_below(x):
        def body(j, best):
            for kb in (2 * j, 2 * j + 1):
                s = sc_ref[kb]
                v = jnp.where(s < x, s, -jnp.inf)
                best = jnp.maximum(best, jnp.max(v.reshape(KB // COUNT_ROWS, COUNT_ROWS, TQ), axis=0))
            return best
        best = lax.fori_loop(0, n_pairs, body, jnp.full((COUNT_ROWS, TQ), -jnp.inf, F32))
        return jnp.max(best, axis=0, keepdims=True)

    n_causal = (i * TQ + 1 + lax.broadcasted_iota(jnp.int32, (1, TQ), 1)).astype(F32)
    want = jnp.minimum(n_causal, float(TOPK))
    hi0 = smax + jnp.maximum(jnp.abs(smax), 1e-30) * 1e-6
    open0 = (n_causal != want).astype(F32)

    def search_cond(state):
        return jnp.logical_and(state[0] < MAX_SEARCH_ITERS, state[-1] > 0.0)

    def search_body(state):
        it, lo, hi, c_lo, c_hi, open_, _ = state
        mid = 0.5 * lo + 0.5 * hi
        c = count_ge(mid)
        live = jnp.logical_and(open_ > 0.0, jnp.logical_and(mid > lo, mid < hi))
        ge = c >= want
        go_lo = jnp.logical_and(live, ge)
        go_hi = jnp.logical_and(live, jnp.logical_not(ge))
        lo = jnp.where(go_lo, mid, lo)
        c_lo = jnp.where(go_lo, c, c_lo)
        hi = jnp.where(go_hi, mid, hi)
        c_hi = jnp.where(go_hi, c, c_hi)
        wide = jnp.logical_and(c_lo != want, c_lo - c_hi > MAX_BRACKET)
        open_ = jnp.logical_and(live, wide).astype(F32)
        return it + 1, lo, hi, c_lo, c_hi, open_, jnp.max(open_)

    _, lo_f, hi_f, c_lo_f, c_hi_f, _, _ = lax.while_loop(
        search_cond, search_body,
        (jnp.int32(0), smin, hi0, n_causal, jnp.zeros((1, TQ), F32), open0, jnp.max(open0)))

    stepping = c_lo_f != want

    def step_cond(state):
        return jnp.logical_and(state[0] < MAX_BRACKET, state[-1] > 0.0)

    def step_body(state):
        it, cur, c, _ = state
        act = jnp.logical_and(stepping, c < want)
        cur = jnp.where(act, max_below(cur), cur)
        c = jnp.where(act, c + 1.0, c)
        return it + 1, cur, c, jnp.max(jnp.logical_and(stepping, c < want).astype(F32))

    _, cur_f, _, _ = lax.while_loop(
        step_cond, step_body,
        (jnp.int32(0), hi_f, c_hi_f, jnp.max(stepping.astype(F32))))
    thr0 = jnp.where(stepping, cur_f, lo_f)
    thr_ref[...] = jnp.broadcast_to(thr0, (SUBLANES, TQ))
    keep_ref[...] = jnp.full((SUBLANES, TQ), float(SEQ), F32)
    unresolved0 = (count_ge(thr0) != want).astype(F32)

    @pl.when(jnp.max(unresolved0) > 0.0)
    def _():
        def next_value(lo, below):
            def body(kb, u):
                s = sc_ref[kb]
                cand = jnp.logical_and(s >= lo, s > below)
                return jnp.minimum(u, jnp.min(jnp.where(cand, s, jnp.inf), axis=0, keepdims=True))
            return lax.fori_loop(0, nkb, body, jnp.full((1, TQ), jnp.inf, F32))

        def count_gt(x):
            def body(kb, cnt):
                gt = jnp.where(sc_ref[kb] > x, 1.0, 0.0)
                return cnt + jnp.sum(gt, axis=0, keepdims=True)
            return lax.fori_loop(0, nkb, body, jnp.zeros((1, TQ), F32))

        def peel_cond(state):
            return state[-1] > 0.0

        def peel_body(state):
            below, unres, thr, keep, _ = state
            u = next_value(lo_f, below)
            c_gt = count_gt(u)
            hit = jnp.logical_and(unres > 0.0, c_gt < want)
            thr = jnp.where(hit, u, thr)
            keep = jnp.where(hit, want - c_gt, keep)
            unres = jnp.logical_and(unres > 0.0, jnp.logical_not(hit)).astype(F32)
            return u, unres, thr, keep, jnp.max(unres)

        _, _, thr_t, keep_t, _ = lax.while_loop(
            peel_cond, peel_body,
            (jnp.full((1, TQ), -jnp.inf, F32), unresolved0, thr0,
             jnp.full((1, TQ), float(SEQ), F32), jnp.float32(1.0)))
        thr_ref[...] = jnp.broadcast_to(thr_t, (SUBLANES, TQ))
        keep_ref[...] = jnp.broadcast_to(keep_t, (SUBLANES, TQ))

        r_i = lax.broadcasted_iota(jnp.int32, (KB, KB), 0)
        c_i = lax.broadcasted_iota(jnp.int32, (KB, KB), 1)
        before = (c_i < r_i).astype(BF16)

        def drop_body(kb, seen):
            s = sc_ref[kb]
            eq = jnp.logical_and(s == thr_t, unresolved0 > 0.0)
            eq_f = eq.astype(F32)
            rank = seen + _dot(before, eq_f.astype(BF16))
            sc_ref[kb] = jnp.where(jnp.logical_and(eq, rank >= keep_t), -jnp.inf, s)
            return seen + jnp.sum(eq_f, axis=0, keepdims=True)

        lax.fori_loop(0, nkb, drop_body, jnp.zeros((1, TQ), F32))

    m_ref[...] = jnp.full(m_ref.shape, MASK_VALUE, F32)
    l_ref[...] = jnp.zeros(l_ref.shape, F32)
    acc_ref[...] = jnp.zeros(acc_ref.shape, F32)

    def logits_stage(j, slot):
        scores = sc_ref[pl.ds(2 * j, 2)].reshape(2 * KB, TQ)
        bias = jnp.where(scores >= thr_ref[0:1, :], 0.0, MASK_VALUE)
        c_n = ckv_ref[pl.ds(pl.multiple_of(j * (2 * KB), 2 * KB), 2 * KB), :]
        for h in range(N_HEADS):
            lg = _dot(c_n, qabs_ref[h]) + bias
            lg_ref[slot, h] = lg
            bmax_ref[slot, h] = jnp.broadcast_to(jnp.max(lg, axis=0, keepdims=True), (SUBLANES, TQ))

    def softmax_stage(j, slot):
        c_t = ckvT_ref[j]
        for h in range(N_HEADS):
            m_old = m_ref[h, 0:1, :]
            m_new = jnp.maximum(m_old, bmax_ref[slot, h, 0:1, :])
            alpha = jnp.exp2(m_old - m_new)
            p = jnp.exp2(lg_ref[slot, h] - m_new)
            l_new = alpha * l_ref[h, 0:1, :] + jnp.sum(p, axis=0, keepdims=True)
            acc_ref[h] = acc_ref[h] * alpha + _dot(c_t, p.astype(BF16))
            m_ref[h] = jnp.broadcast_to(m_new, (SUBLANES, TQ))
            l_ref[h] = jnp.broadcast_to(l_new, (SUBLANES, TQ))

    logits_stage(0, 0)

    def att_body(j, carry):
        for slot in range(2):
            @pl.when(j % 2 == slot)
            def _():
                logits_stage(j + 1, 1 - slot)
                softmax_stage(j, slot)
        return carry

    lax.fori_loop(0, n_pairs - 1, att_body, 0)
    for slot in range(2):
        @pl.when((n_pairs - 1) % 2 == slot)
        def _():
            softmax_stage(n_pairs - 1, slot)

    for h in range(N_HEADS):
        o_t = (acc_ref[h] / l_ref[h, 0:1, :]).astype(BF16)
        a_t = _dot(wuvT_ref[h], o_t)
        a_ref[:, h * HEAD_DIM:(h + 1) * HEAD_DIM] = a_t.T.astype(BF16)


def _attn(qT, qiT, wiT, kk2, ckv, ckvT, wukT, wuvT):
    col = lambda h: pl.BlockSpec((h, TQ), lambda i: (0, i))
    return pl.pallas_call(
        _attn_kernel,
        grid=(SEQ // TQ,),
        in_specs=[col(ATTN_WIDTH), col(IDX_HEADS * IDX_DIM), col(IDX_HEADS), _resident(kk2.shape),
                  _resident(ckv.shape), _resident(ckvT.shape), _resident(wukT.shape),
                  _resident(wuvT.shape)],
        out_specs=pl.BlockSpec((TQ, ATTN_WIDTH), lambda i: (i, 0)),
        out_shape=jax.ShapeDtypeStruct((SEQ, ATTN_WIDTH), BF16),
        scratch_shapes=[
            pltpu.VMEM((NKB, KB, TQ), F32),
            pltpu.VMEM((N_HEADS, KV_LATENT, TQ), BF16),
            pltpu.VMEM((SUBLANES, TQ), F32),
            pltpu.VMEM((SUBLANES, TQ), F32),
            pltpu.VMEM((N_HEADS, SUBLANES, TQ), F32),
            pltpu.VMEM((N_HEADS, SUBLANES, TQ), F32),
            pltpu.VMEM((N_HEADS, KV_LATENT, TQ), F32),
            pltpu.VMEM((2, N_HEADS, 2 * KB, TQ), F32),
            pltpu.VMEM((2, N_HEADS, SUBLANES, TQ), F32),
        ],
        compiler_params=_params(),
        name="attn",
    )(qT, qiT, wiT, kk2, ckv, ckvT, wukT, wuvT)


def _merge_kernel(n1_ref, a_ref, m_ref, wga_ref, wgb_ref, wba_ref, wbg_ref, o_ref):
    n1 = n1_ref[...]
    br_a = jax.nn.sigmoid(_dot(n1, wga_ref[...])) * _dot(a_ref[...], wba_ref[...])
    br_b = jax.nn.sigmoid(_dot(n1, wgb_ref[...])) * _dot(m_ref[...], wbg_ref[...])
    o_ref[...] = (br_a + br_b).astype(BF16)


def _merge(n1, a, m, wga, wgb, wba, wbg):
    tm = ROW_TILE
    row = lambda w: pl.BlockSpec((tm, w), lambda i: (i, 0))
    return pl.pallas_call(
        _merge_kernel,
        grid=(SEQ // tm,),
        in_specs=[row(D_MODEL), row(ATTN_WIDTH), row(GMLP_WIDTH), _resident(wga.shape),
                  _resident(wgb.shape), _resident(wba.shape), _resident(wbg.shape)],
        out_specs=row(D_MODEL),
        out_shape=jax.ShapeDtypeStruct((SEQ, D_MODEL), BF16),
        compiler_params=_params(),
        name="merge",
    )(n1, a, m, wga, wgb, wba, wbg)


ROUTE_ROWS = 8


def _outproj_kernel(mg_ref, x_ref, wo_ref, g2_ref, wrc_ref, br_ref,
                    h_ref, n2_ref, route_ref, counts_ref, carry_ref):
    i = pl.program_id(0)
    tm = ROW_TILE

    @pl.when(i == 0)
    def _():
        carry_ref[...] = jnp.zeros(carry_ref.shape, F32)

    h = x_ref[...] + _dot(mg_ref[...], wo_ref[...])
    h_ref[...] = h
    n2 = h * lax.rsqrt(jnp.mean(h * h, axis=-1, keepdims=True) + EPS) * g2_ref[...]
    n2_ref[...] = n2
    n2_hi = n2.astype(BF16)
    n2_lo = (n2 - n2_hi.astype(F32)).astype(BF16)
    hh_hl = _dot(n2_hi, wrc_ref[...])
    lh = _dot(n2_lo, wrc_ref[:, 0:LANES])
    logits = (hh_hl[:, 0:LANES] + hh_hl[:, LANES:2 * LANES] + lh).T + br_ref[...]
    row = lax.broadcasted_iota(jnp.int32, (LANES, tm), 0).astype(F32)
    is_group = row < N_GROUPS
    gl = jnp.where(is_group, logits, -jnp.inf)
    gmax = jnp.max(gl, axis=0, keepdims=True)
    gsum = jnp.sum(jnp.where(is_group, jnp.exp(logits - gmax), 0.0), axis=0, keepdims=True)
    g_val = 1.0 / gsum
    g_idx = jnp.min(jnp.where(gl == gmax, row, float(LANES)), axis=0, keepdims=True)
    e_id = row - N_GROUPS
    in_group = jnp.logical_and(
        jnp.logical_and(e_id >= 0, e_id < N_EXPERTS),
        jnp.floor(e_id * (1.0 / EXPERTS_PER_GROUP)) == g_idx)
    sel = jnp.where(in_group, logits, -jnp.inf)
    v1 = jnp.max(sel, axis=0, keepdims=True)
    i1 = jnp.min(jnp.where(sel == v1, row, float(LANES)), axis=0, keepdims=True)
    sel2 = jnp.where(row == i1, -jnp.inf, sel)
    v2 = jnp.max(sel2, axis=0, keepdims=True)
    i2 = jnp.min(jnp.where(sel2 == v2, row, float(LANES)), axis=0, keepdims=True)
    x2 = jnp.exp(v2 - v1)
    den = 1.0 + x2
    w1 = g_val * (1.0 / den)
    w2 = g_val * (x2 / den)
    e1 = i1 - N_GROUPS
    e2 = i2 - N_GROUPS

    e_row = lax.broadcasted_iota(jnp.int32, (N_EXPERTS, tm), 0).astype(F32)
    hit1 = (e_row == e1).astype(F32)
    hit2 = (e_row == e2).astype(F32)
    hits = hit1 + hit2
    t_from = lax.broadcasted_iota(jnp.int32, (tm, tm), 0)
    t_to = lax.broadcasted_iota(jnp.int32, (tm, tm), 1)
    earlier = (t_from < t_to).astype(BF16)
    before = carry_ref[:, 0:1] + _dot(hits.astype(BF16), earlier)
    rank1 = jnp.sum(hit1 * before, axis=0, keepdims=True)
    rank2 = jnp.sum(hit2 * before, axis=0, keepdims=True)
    carry = carry_ref[...] + jnp.sum(hits, axis=1, keepdims=True)
    carry_ref[...] = carry
    counts_ref[...] = carry

    r = lax.broadcasted_iota(jnp.int32, (ROUTE_ROWS, tm), 0)
    route = jnp.where(r == 0, e1, 0.0)
    for k, v in enumerate((e2, w1, w2, rank1, rank2), start=1):
        route = jnp.where(r == k, v, route)
    route_ref[...] = route


def _outproj(mg, x2, wo, g2, wr_cat, br):
    tm = ROW_TILE
    row = lambda w: pl.BlockSpec((tm, w), lambda i: (i, 0))
    return pl.pallas_call(
        _outproj_kernel,
        grid=(SEQ // tm,),
        in_specs=[row(D_MODEL), row(D_MODEL), _resident(wo.shape), _resident(g2.shape),
                  _resident(wr_cat.shape), _resident(br.shape)],
        out_specs=[row(D_MODEL), row(D_MODEL),
                   pl.BlockSpec((ROUTE_ROWS, tm), lambda i: (0, i)),
                   pl.BlockSpec((N_EXPERTS, LANES), lambda i: (0, 0))],
        out_shape=[jax.ShapeDtypeStruct((SEQ, D_MODEL), F32),
                   jax.ShapeDtypeStruct((SEQ, D_MODEL), F32),
                   jax.ShapeDtypeStruct((ROUTE_ROWS, SEQ), F32),
                   jax.ShapeDtypeStruct((N_EXPERTS, LANES), F32)],
        scratch_shapes=[pltpu.VMEM((N_EXPERTS, LANES), F32)],
        compiler_params=_params(),
        name="outproj",
    )(mg, x2, wo, g2, wr_cat, br)


def _start_row_gather(src_hbm, row_of, dst, slot, sem, rows):
    for r in rows:
        pltpu.make_async_copy(src_hbm.at[row_of(r)], dst.at[slot, r], sem.at[slot]).start()


def _wait_row_gather(src_hbm, dst, slot, sem, n_rows):
    pltpu.make_async_copy(src_hbm.at[pl.ds(0, n_rows)], dst.at[slot], sem.at[slot]).wait()


def _tiles_to_matrix(rows_ref, first, n):
    return jnp.concatenate(
        [rows_ref[pl.ds(first + a, n, stride=TOKEN_ROWS), :] for a in range(TOKEN_ROWS)], axis=1)


def _scatter_kernel(padstart_ref, padlen_ref, pos_ref, x_ref, xs_hbm, xf, xbuf, zbuf, sem, zsem):
    i = pl.program_id(0)
    n = pl.num_programs(0)
    tm = ROW_TILE
    slot = i % 2

    def row_copies_done(s):
        for _ in range(2):
            pltpu.make_async_copy(xbuf.at[s], xs_hbm.at[pl.ds(0, tm)], sem.at[s]).wait()

    def pad_copy(e, bit):
        rows = 1 << bit
        start = padstart_ref[e] + (padlen_ref[e] & (rows - 1))
        return pltpu.make_async_copy(zbuf.at[pl.ds(0, rows)], xs_hbm.at[pl.ds(start, rows)], zsem.at[0])

    def for_each_pad_piece(fn):
        def body(e, carry):
            for bit in range(PAD_BITS):
                @pl.when((lax.shift_right_logical(padlen_ref[e], bit) & 1) == 1)
                def _():
                    fn(pad_copy(e, bit))
            return carry
        lax.fori_loop(0, N_EXPERTS, body, 0)

    @pl.when(i == 0)
    def _():
        zbuf[...] = jnp.zeros(zbuf.shape, BF16)
        for_each_pad_piece(lambda c: c.start())

    @pl.when(i >= 2)
    def _():
        row_copies_done(slot)

    for a in range(TOKEN_ROWS):
        xf[pl.ds(a, tm, stride=TOKEN_ROWS), :] = x_ref[:, a * LANES:(a + 1) * LANES]
    for s in range(2):
        @pl.when(slot == s)
        def _():
            xbuf[s] = xf[...].reshape(tm, TOKEN_ROWS, LANES).astype(BF16)
            for r in range(2 * tm):
                pltpu.make_async_copy(xbuf.at[s, r % tm], xs_hbm.at[pos_ref[0, 0, r]],
                                      sem.at[s]).start()

    @pl.when(i == n - 1)
    def _():
        row_copies_done(1 - slot)
        row_copies_done(slot)
        for_each_pad_piece(lambda c: c.wait())


def _scatter_rows(padstart, padlen, pos_tiles, n2):
    tm = ROW_TILE
    grid_spec = pltpu.PrefetchScalarGridSpec(
        num_scalar_prefetch=2,
        grid=(SEQ // tm,),
        in_specs=[pl.BlockSpec((1, 1, 2 * tm), lambda i, *_: (i, 0, 0), memory_space=pltpu.SMEM),
                  pl.BlockSpec((tm, D_MODEL), lambda i, *_: (i, 0))],
        out_specs=pl.BlockSpec(memory_space=pl.ANY),
        scratch_shapes=[pltpu.VMEM((tm * TOKEN_ROWS, LANES), F32),
                        pltpu.VMEM((2, tm, TOKEN_ROWS, LANES), BF16),
                        pltpu.VMEM((1 << (PAD_BITS - 1), TOKEN_ROWS, LANES), BF16),
                        pltpu.SemaphoreType.DMA((2,)),
                        pltpu.SemaphoreType.DMA((1,))],
    )
    return pl.pallas_call(
        _scatter_kernel,
        grid_spec=grid_spec,
        out_shape=jax.ShapeDtypeStruct((N_EXPERT_TILES * EXPERT_TILE, TOKEN_ROWS, LANES), BF16),
        compiler_params=_params(),
        name="scatter_rows",
    )(padstart, padlen, pos_tiles, n2)


def _expert_kernel(ntl_ref, tst_ref, ntot_ref,
                   xs_hbm, wg_hbm, wu_hbm, wd_hbm,
                   y_hbm,
                   xbuf, ybuf, xsem, ysem, wg_st, wu_st, wd_st, wsem, wgb, wub, wdb, rows_f, rows_y):
    e = pl.program_id(0)
    ntot = ntot_ref[0]
    n_rows = EXPERT_TILE * TOKEN_ROWS

    def weight_copies(ex, slot):
        return [pltpu.make_async_copy(src.at[ex], dst.at[slot], wsem.at[slot, j])
                for j, (src, dst) in enumerate(((wg_hbm, wg_st), (wu_hbm, wu_st), (wd_hbm, wd_st)))]

    def tile_rows(t):
        return pl.ds(pl.multiple_of(t * EXPERT_TILE, EXPERT_TILE), EXPERT_TILE)

    def x_copy(t, slot):
        return pltpu.make_async_copy(xs_hbm.at[tile_rows(t)], xbuf.at[slot], xsem.at[slot])

    def y_copy(t, slot):
        return pltpu.make_async_copy(ybuf.at[slot], y_hbm.at[tile_rows(t)], ysem.at[slot])

    @pl.when(e == 0)
    def _():
        for c in weight_copies(0, 0):
            c.start()
        x_copy(0, 0).start()

    @pl.when(e + 1 < pl.num_programs(0))
    def _():
        for c in weight_copies(e + 1, (e + 1) % 2):
            c.start()

    n_here = ntl_ref[e]
    wslot = e % 2
    for c in weight_copies(e, wslot):
        c.wait()

    @pl.when(n_here > 0)
    def _():
        wgb[...] = wg_st[wslot].astype(BF16)
        wub[...] = wu_st[wslot].astype(BF16)
        wdb[...] = wd_st[wslot].astype(BF16)

        def tile_body(k, carry):
            t = tst_ref[e] + k
            slot = t % 2
            x_copy(t, slot).wait()

            @pl.when(t + 1 < ntot)
            def _():
                x_copy(t + 1, 1 - slot).start()

            rows_f[...] = xbuf[slot].astype(F32).reshape(n_rows, LANES)
            xt = _tiles_to_matrix(rows_f, 0, EXPERT_TILE).astype(BF16)
            hid = jax.nn.silu(_dot(xt, wgb[...])) * _dot(xt, wub[...])
            y = _dot(hid.astype(BF16), wdb[...])
            for a in range(TOKEN_ROWS):
                rows_y[pl.ds(a, EXPERT_TILE, stride=TOKEN_ROWS), :] = y[:, a * LANES:(a + 1) * LANES]

            @pl.when(t >= 2)
            def _():
                y_copy(t - 2, slot).wait()

            ybuf[slot] = rows_y[...].reshape(EXPERT_TILE, TOKEN_ROWS, LANES).astype(BF16)
            y_copy(t, slot).start()
            return carry
        lax.fori_loop(0, n_here, tile_body, 0)

    @pl.when(e == pl.num_programs(0) - 1)
    def _():
        @pl.when(ntot >= 2)
        def _():
            y_copy(ntot - 2, (ntot - 2) % 2).wait()
        y_copy(ntot - 1, (ntot - 1) % 2).wait()


def _experts(ntl, tst, ntot, x_sorted, w_gate, w_up, w_down):
    hbm = pl.BlockSpec(memory_space=pl.ANY)
    grid_spec = pltpu.PrefetchScalarGridSpec(
        num_scalar_prefetch=3,
        grid=(N_EXPERTS,),
        in_specs=[hbm, hbm, hbm, hbm],
        out_specs=hbm,
        scratch_shapes=[
            pltpu.VMEM((2, EXPERT_TILE, TOKEN_ROWS, LANES), BF16),
            pltpu.VMEM((2, EXPERT_TILE, TOKEN_ROWS, LANES), BF16),
            pltpu.SemaphoreType.DMA((2,)),
            pltpu.SemaphoreType.DMA((2,)),
            pltpu.VMEM((2, D_MODEL, EXPERT_FF), F32),
            pltpu.VMEM((2, D_MODEL, EXPERT_FF), F32),
            pltpu.VMEM((2, EXPERT_FF, D_MODEL), F32),
            pltpu.SemaphoreType.DMA((2, 3)),
            pltpu.VMEM((D_MODEL, EXPERT_FF), BF16),
            pltpu.VMEM((D_MODEL, EXPERT_FF), BF16),
            pltpu.VMEM((EXPERT_FF, D_MODEL), BF16),
            pltpu.VMEM((EXPERT_TILE * TOKEN_ROWS, LANES), F32),
            pltpu.VMEM((EXPERT_TILE * TOKEN_ROWS, LANES), F32),
        ],
    )
    return pl.pallas_call(
        _expert_kernel,
        grid_spec=grid_spec,
        out_shape=jax.ShapeDtypeStruct((N_EXPERT_TILES * EXPERT_TILE, TOKEN_ROWS, LANES), BF16),
        compiler_params=_params(),
        name="experts",
    )(ntl, tst, ntot, x_sorted, w_gate, w_up, w_down)


def _final_kernel(pos_ref, posn_ref, h_ref, route_ref, gf_ref, y_hbm, o_ref, ybuf, sem, rows_f):
    i = pl.program_id(0)
    n = pl.num_programs(0)
    tm = ROW_TILE

    @pl.when(i == 0)
    def _():
        _start_row_gather(y_hbm, lambda r: pos_ref[0, 0, r], ybuf, 0, sem, range(2 * tm))

    for s in range(2):
        @pl.when(jnp.logical_and(i + 1 < n, (i + 1) % 2 == s))
        def _():
            _start_row_gather(y_hbm, lambda r: posn_ref[0, 0, r], ybuf, s, sem, range(2 * tm))

    slot = i % 2
    _wait_row_gather(y_hbm, ybuf, slot, sem, 2 * tm)
    rows_f[...] = ybuf[slot].astype(F32).reshape(2 * tm * TOKEN_ROWS, LANES)
    w1 = route_ref[:, 0:1]
    w2 = route_ref[:, 1:2]
    h2 = h_ref[...] + (w1 * _tiles_to_matrix(rows_f, 0, tm)
                       + w2 * _tiles_to_matrix(rows_f, tm * TOKEN_ROWS, tm))
    o_ref[...] = h2 * lax.rsqrt(jnp.mean(h2 * h2, axis=-1, keepdims=True) + EPS) * gf_ref[...]


def _final(pos, h, route, gf, y_sorted):
    tm = ROW_TILE
    n = SEQ // tm
    row = lambda w: pl.BlockSpec((tm, w), lambda i: (i, 0))
    smem_blk = lambda f: pl.BlockSpec((1, 1, 2 * tm), f, memory_space=pltpu.SMEM)
    return pl.pallas_call(
        _final_kernel,
        grid=(n,),
        in_specs=[smem_blk(lambda i: (i, 0, 0)),
                  smem_blk(lambda i: (jnp.minimum(i + 1, n - 1), 0, 0)),
                  row(D_MODEL), row(2), _resident(gf.shape),
                  pl.BlockSpec(memory_space=pl.ANY)],
        out_specs=row(D_MODEL),
        out_shape=jax.ShapeDtypeStruct((SEQ, D_MODEL), F32),
        scratch_shapes=[pltpu.VMEM((2, 2 * tm, TOKEN_ROWS, LANES), BF16),
                        pltpu.SemaphoreType.DMA((2,)),
                        pltpu.VMEM((2 * tm * TOKEN_ROWS, LANES), F32)],
        compiler_params=_params(),
        name="final",
    )(pos, pos, h, route, gf, y_sorted)


def _dispatch(route, counts):
    i32 = jnp.int32
    e_pair = route[0:2].T.astype(i32).reshape(-1)
    rank = route[4:6].T.astype(i32).reshape(-1)
    cnt = counts[:, 0].astype(i32)
    ntl = (cnt + EXPERT_TILE - 1) // EXPERT_TILE
    tend = jnp.cumsum(ntl)
    tst = tend - ntl
    experts = jnp.arange(N_EXPERTS, dtype=i32)
    of_pair = e_pair[:, None] == experts[None, :]
    pos = jnp.sum(jnp.where(of_pair, (tst * EXPERT_TILE)[None, :], 0), axis=1) + rank
    padstart = tst * EXPERT_TILE + cnt
    padlen = ntl * EXPERT_TILE - cnt
    return ntl, tst, tend[-1:], padstart, padlen, pos


def kernel(x, norm1_g, w_in, kv_norm_g, w_uk, w_uv, gmlp_ws, gmlp_bs, ln_v_g, ln_v_b, w_br_attn,
           w_br_gmlp, w_out, norm2_g, w_group, b_group, w_router, b_router, w_e_gate, w_e_up,
           w_e_down, norm_f_g):
    assert x.shape == (1, SEQ, D_MODEL)
    x2 = x.reshape(SEQ, D_MODEL)
    row_vec = lambda v: v.reshape(1, -1).astype(F32)

    assert w_in.shape == (D_MODEL, IN_COLS)
    wqT, wc, wqiT, wkk, wwT, wuv_in, wga, wgb = _split_w_in(w_in.T)

    n1, qT, ckv, ckvT, qiT, kk2, wiT = _proj(x2, row_vec(norm1_g), wqT, wc, wqiT, wkk, wwT,
                                             row_vec(kv_norm_g))
    m = _gmlp(n1, wuv_in, gmlp_ws, jnp.pad(gmlp_bs.T, ((0, 0), (0, LANES - GMLP_GROUPS))),
              row_vec(ln_v_g), row_vec(ln_v_b))
    a = _attn(qT, qiT, wiT, kk2, ckv, ckvT,
              jnp.swapaxes(w_uk, 1, 2).astype(BF16), jnp.swapaxes(w_uv, 1, 2).astype(BF16))
    mg = _merge(n1, a, m, wga, wgb, w_br_attn.astype(BF16), w_br_gmlp.astype(BF16))

    w_route = jnp.pad(jnp.concatenate([w_group, w_router], axis=1),
                      ((0, 0), (0, LANES - N_GROUPS - N_EXPERTS)))
    b_route = jnp.pad(jnp.concatenate([b_group, b_router]), (0, LANES - N_GROUPS - N_EXPERTS))
    wr_hi = w_route.astype(BF16)
    wr_lo = (w_route - wr_hi.astype(F32)).astype(BF16)
    h, n2, route, counts = _outproj(mg, x2, w_out.astype(BF16), row_vec(norm2_g),
                                    jnp.concatenate([wr_hi, wr_lo], axis=1),
                                    b_route.reshape(LANES, 1))

    ntl, tst, ntot, padstart, padlen, pos = _dispatch(route, counts)
    pos_tiles = pos.reshape(SEQ // ROW_TILE, ROW_TILE, 2).transpose(0, 2, 1).reshape(
        SEQ // ROW_TILE, 1, 2 * ROW_TILE)
    x_sorted = _scatter_rows(padstart, padlen, pos_tiles, n2)
    y_sorted = _experts(ntl, tst, ntot, x_sorted, w_e_gate, w_e_up, w_e_down)
    out = _final(pos_tiles, h, route[2:4].T, row_vec(norm_f_g), y_sorted)
    return out.reshape(1, SEQ, D_MODEL)
```

```python
import jax
import jax.numpy as jnp
import numpy as np
from jax import lax
from jax.experimental import pallas as pl
from jax.experimental.pallas import tpu as pltpu

F32 = jnp.float32
BF16 = jnp.bfloat16

D_MODEL = 2048
SEQ = 8192
N_HEADS = 8
HEAD_DIM = 128
KV_LATENT = 256
IDX_HEADS = 16
IDX_DIM = 64
TOPK = 256
ATTN_WIDTH = N_HEADS * HEAD_DIM
GMLP_GROUPS = 8
GMLP_WIDTH = 1024
CHUNK = 128
N_GROUPS = 8
EXPERTS_PER_GROUP = 8
N_EXPERTS = 64
EXPERT_FF = 512
EPS = 1e-6

LANES = 128
SUBLANES = 8
VMEM_LIMIT = 60 * 1024 * 1024
MASK_VALUE = -0.7 * float(np.finfo(np.float32).max)
LOG2_E = float(np.log2(np.e))

ROW_TILE = 256
GMLP_TILE = 512
TQ = 256
KB = 256
NKB = SEQ // KB
COUNT_ROWS = 32
MAX_SEARCH_ITERS = 64
MAX_BRACKET = 4
TINY = 1e-30
HI_MARGIN = 1e-6
EXPERT_TILE = 256
N_EXPERT_TILES = (2 * SEQ) // EXPERT_TILE + N_EXPERTS
PAD_BITS = 8
TOKEN_ROWS = D_MODEL // LANES


def _dot(a, b):
    return jnp.dot(a, b, preferred_element_type=F32)


def _dot_nt(a, b):
    return lax.dot_general(a, b, (((1,), (1,)), ((), ())), preferred_element_type=F32)


def _resident(shape):
    zeros = (0,) * len(shape)
    return pl.BlockSpec(shape, lambda *_: zeros, pipeline_mode=pl.Buffered(1))


def _params(n_axes=1):
    return pltpu.CompilerParams(
        dimension_semantics=("arbitrary",) * n_axes, vmem_limit_bytes=VMEM_LIMIT)


C_Q = ATTN_WIDTH
C_KV = C_Q + KV_LATENT
C_QI = C_KV + IDX_HEADS * IDX_DIM
C_K = C_QI + IDX_DIM
C_W = C_K + IDX_HEADS
C_UV = C_W + 2 * GMLP_WIDTH
C_GA = C_UV + D_MODEL
IN_COLS = C_GA + D_MODEL


def _split_w_in_kernel(wt_ref, wqT_ref, wc_ref, wqiT_ref, wkk_ref, wwT_ref, wuv_ref, wga_ref, wgb_ref):
    wqT_ref[...] = wt_ref[0:C_Q, :].astype(BF16)
    wc_ref[...] = wt_ref[C_Q:C_KV, :].T.astype(BF16)
    wqiT_ref[...] = wt_ref[C_KV:C_QI, :].astype(BF16)
    k_t = wt_ref[C_QI:C_K, :].T.astype(BF16)
    zero = jnp.zeros_like(k_t)
    wkk_ref[...] = jnp.concatenate([k_t, zero, zero, k_t], axis=1)
    wwT_ref[...] = wt_ref[C_K:C_W, :].astype(BF16)
    wuv_ref[...] = wt_ref[C_W:C_UV, :].T.astype(BF16)
    wga_ref[...] = wt_ref[C_UV:C_GA, :].T.astype(BF16)
    wgb_ref[...] = wt_ref[C_GA:IN_COLS, :].T.astype(BF16)


def _split_w_in(w_in_t):
    tr = ROW_TILE
    rows = lambda w: pl.BlockSpec((tr, w), lambda i: (i, 0))
    cols = lambda h: pl.BlockSpec((h, tr), lambda i: (0, i))
    sds = jax.ShapeDtypeStruct
    return pl.pallas_call(
        _split_w_in_kernel,
        grid=(D_MODEL // tr,),
        in_specs=[cols(IN_COLS)],
        out_specs=[cols(C_Q), rows(KV_LATENT), cols(C_QI - C_KV), rows(2 * LANES), cols(IDX_HEADS),
                   rows(2 * GMLP_WIDTH), rows(D_MODEL), rows(D_MODEL)],
        out_shape=[sds((C_Q, D_MODEL), BF16), sds((D_MODEL, KV_LATENT), BF16),
                   sds((C_QI - C_KV, D_MODEL), BF16), sds((D_MODEL, 2 * LANES), BF16),
                   sds((IDX_HEADS, D_MODEL), BF16), sds((D_MODEL, 2 * GMLP_WIDTH), BF16),
                   sds((D_MODEL, D_MODEL), BF16), sds((D_MODEL, D_MODEL), BF16)],
        compiler_params=_params(),
        name="split_w_in",
    )(w_in_t)


def _proj_kernel(x_ref, g1_ref, wqT_ref, wc_ref, wqiT_ref, wkk_ref, wwT_ref, kvg_ref,
                 n1_ref, qT_ref, ckv_ref, ckvT_ref, qiT_ref, kk2_ref, wiT_ref):
    x = x_ref[...]
    ms = jnp.mean(x * x, axis=-1, keepdims=True)
    n1 = (x * lax.rsqrt(ms + EPS) * g1_ref[...]).astype(BF16)
    n1_ref[...] = n1
    qT_ref[...] = _dot_nt(wqT_ref[...], n1).astype(BF16)
    qiT_ref[...] = (_dot_nt(wqiT_ref[...], n1) * (IDX_DIM ** -0.5)).astype(BF16)
    wiT_ref[...] = _dot_nt(wwT_ref[...], n1) * (IDX_HEADS ** -0.5)
    c = _dot(n1, wc_ref[...])
    c = c * lax.rsqrt(jnp.mean(c * c, axis=-1, keepdims=True) + EPS) * kvg_ref[...]
    ckv_ref[...] = c.astype(BF16)
    ckvT_ref[0] = c.T.astype(BF16)
    kk = _dot(n1, wkk_ref[...]).astype(BF16)
    for b in range(2):
        rows = slice(b * KB, (b + 1) * KB)
        kk2_ref[b, 0:KB, :] = kk[rows, 0:LANES]
        kk2_ref[b, KB:2 * KB, :] = kk[rows, LANES:2 * LANES]


def _proj(x2, g1, wqT, wc, wqiT, wkk, wwT, kvg):
    tm = 2 * KB
    row = lambda w: pl.BlockSpec((tm, w), lambda i: (i, 0))
    col = lambda h: pl.BlockSpec((h, tm), lambda i: (0, i))
    return pl.pallas_call(
        _proj_kernel,
        grid=(SEQ // tm,),
        in_specs=[row(D_MODEL), _resident(g1.shape), _resident(wqT.shape), _resident(wc.shape),
                  _resident(wqiT.shape), _resident(wkk.shape), _resident(wwT.shape),
                  _resident(kvg.shape)],
        out_specs=[row(D_MODEL), col(ATTN_WIDTH), row(KV_LATENT),
                   pl.BlockSpec((1, KV_LATENT, tm), lambda i: (i, 0, 0)),
                   col(IDX_HEADS * IDX_DIM),
                   pl.BlockSpec((2, 2 * KB, LANES), lambda i: (i, 0, 0)),
                   col(IDX_HEADS)],
        out_shape=[
            jax.ShapeDtypeStruct((SEQ, D_MODEL), BF16),
            jax.ShapeDtypeStruct((ATTN_WIDTH, SEQ), BF16),
            jax.ShapeDtypeStruct((SEQ, KV_LATENT), BF16),
            jax.ShapeDtypeStruct((NKB // 2, KV_LATENT, 2 * KB), BF16),
            jax.ShapeDtypeStruct((IDX_HEADS * IDX_DIM, SEQ), BF16),
            jax.ShapeDtypeStruct((NKB, 2 * KB, LANES), BF16),
            jax.ShapeDtypeStruct((IDX_HEADS, SEQ), F32),
        ],
        compiler_params=_params(),
        name="proj",
    )(x2, g1, wqT, wc, wqiT, wkk, wwT, kvg)


def _gmlp_kernel(n1_ref, wuv_ref, ws_ref, bsT_ref, lng_ref, lnb_ref, m_ref):
    uv = _dot(n1_ref[...], wuv_ref[...])
    z = jax.nn.gelu(uv)
    u = z[:, :GMLP_WIDTH]
    v = z[:, GMLP_WIDTH:]
    mu = jnp.mean(v, axis=-1, keepdims=True)
    var = jnp.mean(jnp.square(v - mu), axis=-1, keepdims=True)
    vn = ((v - mu) * lax.rsqrt(var + EPS) * lng_ref[...] + lnb_ref[...]).astype(BF16)
    t_pos = lax.broadcasted_iota(jnp.int32, (CHUNK, CHUNK), 0)
    s_pos = lax.broadcasted_iota(jnp.int32, (CHUNK, CHUNK), 1)
    causal = s_pos <= t_pos
    for g in range(GMLP_GROUPS):
        wm = jnp.where(causal, ws_ref[g], 0.0).astype(BF16)
        bias = bsT_ref[:, g:g + 1]
        cols = slice(g * LANES, (g + 1) * LANES)
        for c in range(GMLP_TILE // CHUNK):
            rows = slice(c * CHUNK, (c + 1) * CHUNK)
            y = _dot(wm, vn[rows, cols]) + bias
            m_ref[rows, cols] = (u[rows, cols] * y).astype(BF16)


def _gmlp(n1, wuv, ws, bsT, lng, lnb):
    tm = GMLP_TILE
    return pl.pallas_call(
        _gmlp_kernel,
        grid=(SEQ // tm,),
        in_specs=[pl.BlockSpec((tm, D_MODEL), lambda i: (i, 0)), _resident(wuv.shape),
                  _resident(ws.shape), _resident(bsT.shape), _resident(lng.shape),
                  _resident(lnb.shape)],
        out_specs=pl.BlockSpec((tm, GMLP_WIDTH), lambda i: (i, 0)),
        out_shape=jax.ShapeDtypeStruct((SEQ, GMLP_WIDTH), BF16),
        compiler_params=_params(),
        name="gmlp",
    )(n1, wuv, ws, bsT, lng, lnb)


def _attn_kernel(qT_ref, qiT_ref, wiT_ref, kk2_ref, ckv_ref, ckvT_ref, wukT_ref, wuvT_ref,
                 a_ref,
                 sc_ref, qabs_ref, thr_ref, keep_ref, m_ref, l_ref, acc_ref, lg_ref, bmax_ref):
    i = pl.program_id(0)
    nkb = i + 1

    for h in range(N_HEADS):
        qa = _dot(wukT_ref[h], qT_ref[h * HEAD_DIM:(h + 1) * HEAD_DIM, :]) * (
            HEAD_DIM ** -0.5 * LOG2_E)
        qabs_ref[h] = qa.astype(BF16)

    q_pos = i * TQ + lax.broadcasted_iota(jnp.int32, (KB, TQ), 1)
    k_off = lax.broadcasted_iota(jnp.int32, (KB, TQ), 0)

    def score_body(kb, carry):
        smax, smin = carry
        keys = kk2_ref[kb]
        acc = jnp.zeros((KB, TQ), F32)
        for j in range(IDX_HEADS // 2):
            d = _dot(keys, qiT_ref[j * LANES:(j + 1) * LANES, :])
            acc = acc + jnp.maximum(d[0:KB], 0.0) * wiT_ref[2 * j:2 * j + 1, :]
            acc = acc + jnp.maximum(d[KB:2 * KB], 0.0) * wiT_ref[2 * j + 1:2 * j + 2, :]
        causal = (kb * KB + k_off) <= q_pos
        sc_ref[kb] = jnp.where(causal, acc, -jnp.inf)
        smax = jnp.maximum(smax, jnp.max(jnp.where(causal, acc, -jnp.inf), axis=0, keepdims=True))
        smin = jnp.minimum(smin, jnp.min(jnp.where(causal, acc, jnp.inf), axis=0, keepdims=True))
        return smax, smin

    n_pairs = (nkb + 1) // 2
    smax, smin = lax.fori_loop(
        0, n_pairs, lambda j, c: score_body(2 * j + 1, score_body(2 * j, c)),
        (jnp.full((1, TQ), -jnp.inf, F32), jnp.full((1, TQ), jnp.inf, F32)))

    def count_ge(x):
        def body(j, cnt):
            for kb in (2 * j, 2 * j + 1):
                ge = jnp.where(sc_ref[kb] >= x, 1.0, 0.0)
                cnt = cnt + jnp.sum(ge.reshape(KB // COUNT_ROWS, COUNT_ROWS, TQ), axis=0)
            return cnt
        cnt = lax.fori_loop(0, n_pairs, body, jnp.zeros((COUNT_ROWS, TQ), F32))
        return jnp.sum(cnt, axis=0, keepdims=True)

    def max_below(x):
        def body(j, best):
            for kb in (2 * j, 2 * j + 1):
                s = sc_ref[kb]
                v = jnp.where(s < x, s, -jnp.inf)
                best = jnp.maximum(best, jnp.max(v.reshape(KB // COUNT_ROWS, COUNT_ROWS, TQ), axis=0))
            return best
        best = lax.fori_loop(0, n_pairs, body, jnp.full((COUNT_ROWS, TQ), -jnp.inf, F32))
        return jnp.max(best, axis=0, keepdims=True)

    n_causal = (i * TQ + 1 + lax.broadcasted_iota(jnp.int32, (1, TQ), 1)).astype(F32)
    want = jnp.minimum(n_causal, float(TOPK))
    hi0 = smax + jnp.maximum(jnp.abs(smax), TINY) * HI_MARGIN
    open0 = (n_causal != want).astype(F32)

    def search_cond(state):
        return jnp.logical_and(state[0] < MAX_SEARCH_ITERS, state[-1] > 0.0)

    def search_body(state):
        it, lo, hi, c_lo, c_hi, open_, _ = state
        mid = 0.5 * lo + 0.5 * hi
        c = count_ge(mid)
        live = jnp.logical_and(open_ > 0.0, jnp.logical_and(mid > lo, mid < hi))
        ge = c >= want
        go_lo = jnp.logical_and(live, ge)
        go_hi = jnp.logical_and(live, jnp.logical_not(ge))
        lo = jnp.where(go_lo, mid, lo)
        c_lo = jnp.where(go_lo, c, c_lo)
        hi = jnp.where(go_hi, mid, hi)
        c_hi = jnp.where(go_hi, c, c_hi)
        wide = jnp.logical_and(c_lo != want, c_lo - c_hi > MAX_BRACKET)
        open_ = jnp.logical_and(live, wide).astype(F32)
        return it + 1, lo, hi, c_lo, c_hi, open_, jnp.max(open_)

    _, lo_f, hi_f, c_lo_f, c_hi_f, _, _ = lax.while_loop(
        search_cond, search_body,
        (jnp.int32(0), smin, hi0, n_causal, jnp.zeros((1, TQ), F32), open0, jnp.max(open0)))

    stepping = c_lo_f != want

    def step_cond(state):
        return jnp.logical_and(state[0] < MAX_BRACKET, state[-1] > 0.0)

    def step_body(state):
        it, cur, c, _ = state
        act = jnp.logical_and(stepping, c < want)
        cur = jnp.where(act, max_below(cur), cur)
        c = jnp.where(act, c + 1.0, c)
        return it + 1, cur, c, jnp.max(jnp.logical_and(stepping, c < want).astype(F32))

    _, cur_f, _, _ = lax.while_loop(
        step_cond, step_body,
        (jnp.int32(0), hi_f, c_hi_f, jnp.max(stepping.astype(F32))))
    thr0 = jnp.where(stepping, cur_f, lo_f)
    thr_ref[...] = jnp.broadcast_to(thr0, (SUBLANES, TQ))
    keep_ref[...] = jnp.full((SUBLANES, TQ), float(SEQ), F32)
    unresolved0 = (count_ge(thr0) != want).astype(F32)

    @pl.when(jnp.max(unresolved0) > 0.0)
    def _():
        def next_value(lo, below):
            def body(kb, u):
                s = sc_ref[kb]
                cand = jnp.logical_and(s >= lo, s > below)
                return jnp.minimum(u, jnp.min(jnp.where(cand, s, jnp.inf), axis=0, keepdims=True))
            return lax.fori_loop(0, nkb, body, jnp.full((1, TQ), jnp.inf, F32))

        def count_gt(x):
            def body(kb, cnt):
                gt = jnp.where(sc_ref[kb] > x, 1.0, 0.0)
                return cnt + jnp.sum(gt, axis=0, keepdims=True)
            return lax.fori_loop(0, nkb, body, jnp.zeros((1, TQ), F32))

        def peel_cond(state):
            return state[-1] > 0.0

        def peel_body(state):
            below, unres, thr, keep, _ = state
            u = next_value(lo_f, below)
            c_gt = count_gt(u)
            hit = jnp.logical_and(unres > 0.0, c_gt < want)
            thr = jnp.where(hit, u, thr)
            keep = jnp.where(hit, want - c_gt, keep)
            unres = jnp.logical_and(unres > 0.0, jnp.logical_not(hit)).astype(F32)
            return u, unres, thr, keep, jnp.max(unres)

        _, _, thr_t, keep_t, _ = lax.while_loop(
            peel_cond, peel_body,
            (jnp.full((1, TQ), -jnp.inf, F32), unresolved0, thr0,
             jnp.full((1, TQ), float(SEQ), F32), jnp.float32(1.0)))
        thr_ref[...] = jnp.broadcast_to(thr_t, (SUBLANES, TQ))
        keep_ref[...] = jnp.broadcast_to(keep_t, (SUBLANES, TQ))

        r_i = lax.broadcasted_iota(jnp.int32, (KB, KB), 0)
        c_i = lax.broadcasted_iota(jnp.int32, (KB, KB), 1)
        before = (c_i < r_i).astype(BF16)

        def drop_body(kb, seen):
            s = sc_ref[kb]
            eq = jnp.logical_and(s == thr_t, unresolved0 > 0.0)
            eq_f = eq.astype(F32)
            rank = seen + _dot(before, eq_f.astype(BF16))
            sc_ref[kb] = jnp.where(jnp.logical_and(eq, rank >= keep_t), -jnp.inf, s)
            return seen + jnp.sum(eq_f, axis=0, keepdims=True)

        lax.fori_loop(0, nkb, drop_body, jnp.zeros((1, TQ), F32))

    m_ref[...] = jnp.full(m_ref.shape, MASK_VALUE, F32)
    l_ref[...] = jnp.zeros(l_ref.shape, F32)
    acc_ref[...] = jnp.zeros(acc_ref.shape, F32)

    def logits_stage(j, slot):
        scores = sc_ref[pl.ds(2 * j, 2)].reshape(2 * KB, TQ)
        bias = jnp.where(scores >= thr_ref[0:1, :], 0.0, MASK_VALUE)
        c_n = ckv_ref[pl.ds(pl.multiple_of(j * (2 * KB), 2 * KB), 2 * KB), :]
        for h in range(N_HEADS):
            lg = _dot(c_n, qabs_ref[h]) + bias
            lg_ref[slot, h] = lg
            bmax_ref[slot, h] = jnp.broadcast_to(jnp.max(lg, axis=0, keepdims=True), (SUBLANES, TQ))

    def softmax_stage(j, slot):
        c_t = ckvT_ref[j]
        for h in range(N_HEADS):
            m_old = m_ref[h, 0:1, :]
            m_new = jnp.maximum(m_old, bmax_ref[slot, h, 0:1, :])
            alpha = jnp.exp2(m_old - m_new)
            p = jnp.exp2(lg_ref[slot, h] - m_new)
            l_new = alpha * l_ref[h, 0:1, :] + jnp.sum(p, axis=0, keepdims=True)
            acc_ref[h] = acc_ref[h] * alpha + _dot(c_t, p.astype(BF16))
            m_ref[h] = jnp.broadcast_to(m_new, (SUBLANES, TQ))
            l_ref[h] = jnp.broadcast_to(l_new, (SUBLANES, TQ))

    logits_stage(0, 0)

    def att_body(j, carry):
        for slot in range(2):
            @pl.when(j % 2 == slot)
            def _():
                logits_stage(j + 1, 1 - slot)
                softmax_stage(j, slot)
        return carry

    lax.fori_loop(0, n_pairs - 1, att_body, 0)
    for slot in range(2):
        @pl.when((n_pairs - 1) % 2 == slot)
        def _():
            softmax_stage(n_pairs - 1, slot)

    for h in range(N_HEADS):
        o_t = (acc_ref[h] / l_ref[h, 0:1, :]).astype(BF16)
        a_t = _dot(wuvT_ref[h], o_t)
        a_ref[:, h * HEAD_DIM:(h + 1) * HEAD_DIM] = a_t.T.astype(BF16)


def _attn(qT, qiT, wiT, kk2, ckv, ckvT, wukT, wuvT):
    col = lambda h: pl.BlockSpec((h, TQ), lambda i: (0, i))
    return pl.pallas_call(
        _attn_kernel,
        grid=(SEQ // TQ,),
        in_specs=[col(ATTN_WIDTH), col(IDX_HEADS * IDX_DIM), col(IDX_HEADS), _resident(kk2.shape),
                  _resident(ckv.shape), _resident(ckvT.shape), _resident(wukT.shape),
                  _resident(wuvT.shape)],
        out_specs=pl.BlockSpec((TQ, ATTN_WIDTH), lambda i: (i, 0)),
        out_shape=jax.ShapeDtypeStruct((SEQ, ATTN_WIDTH), BF16),
        scratch_shapes=[
            pltpu.VMEM((NKB, KB, TQ), F32),
            pltpu.VMEM((N_HEADS, KV_LATENT, TQ), BF16),
            pltpu.VMEM((SUBLANES, TQ), F32),
            pltpu.VMEM((SUBLANES, TQ), F32),
            pltpu.VMEM((N_HEADS, SUBLANES, TQ), F32),
            pltpu.VMEM((N_HEADS, SUBLANES, TQ), F32),
            pltpu.VMEM((N_HEADS, KV_LATENT, TQ), F32),
            pltpu.VMEM((2, N_HEADS, 2 * KB, TQ), F32),
            pltpu.VMEM((2, N_HEADS, SUBLANES, TQ), F32),
        ],
        compiler_params=_params(),
        name="attn",
    )(qT, qiT, wiT, kk2, ckv, ckvT, wukT, wuvT)


def _merge_kernel(n1_ref, a_ref, m_ref, wga_ref, wgb_ref, wba_ref, wbg_ref, o_ref):
    n1 = n1_ref[...]
    br_a = jax.nn.sigmoid(_dot(n1, wga_ref[...])) * _dot(a_ref[...], wba_ref[...])
    br_b = jax.nn.sigmoid(_dot(n1, wgb_ref[...])) * _dot(m_ref[...], wbg_ref[...])
    o_ref[...] = (br_a + br_b).astype(BF16)


def _merge(n1, a, m, wga, wgb, wba, wbg):
    tm = ROW_TILE
    row = lambda w: pl.BlockSpec((tm, w), lambda i: (i, 0))
    return pl.pallas_call(
        _merge_kernel,
        grid=(SEQ // tm,),
        in_specs=[row(D_MODEL), row(ATTN_WIDTH), row(GMLP_WIDTH), _resident(wga.shape),
                  _resident(wgb.shape), _resident(wba.shape), _resident(wbg.shape)],
        out_specs=row(D_MODEL),
        out_shape=jax.ShapeDtypeStruct((SEQ, D_MODEL), BF16),
        compiler_params=_params(),
        name="merge",
    )(n1, a, m, wga, wgb, wba, wbg)


ROUTE_ROWS = 8


def _outproj_kernel(mg_ref, x_ref, wo_ref, g2_ref, wrc_ref, br_ref,
                    h_ref, n2_ref, route_ref, counts_ref, carry_ref):
    i = pl.program_id(0)
    tm = ROW_TILE

    @pl.when(i == 0)
    def _():
        carry_ref[...] = jnp.zeros(carry_ref.shape, F32)

    h = x_ref[...] + _dot(mg_ref[...], wo_ref[...])
    h_ref[...] = h
    n2 = h * lax.rsqrt(jnp.mean(h * h, axis=-1, keepdims=True) + EPS) * g2_ref[...]
    n2_ref[...] = n2
    n2_hi = n2.astype(BF16)
    n2_lo = (n2 - n2_hi.astype(F32)).astype(BF16)
    hh_hl = _dot(n2_hi, wrc_ref[...])
    lh = _dot(n2_lo, wrc_ref[:, 0:LANES])
    logits = (hh_hl[:, 0:LANES] + hh_hl[:, LANES:2 * LANES] + lh).T + br_ref[...]
    row = lax.broadcasted_iota(jnp.int32, (LANES, tm), 0).astype(F32)
    is_group = row < N_GROUPS
    gl = jnp.where(is_group, logits, -jnp.inf)
    gmax = jnp.max(gl, axis=0, keepdims=True)
    gsum = jnp.sum(jnp.where(is_group, jnp.exp(logits - gmax), 0.0), axis=0, keepdims=True)
    g_val = 1.0 / gsum
    g_idx = jnp.min(jnp.where(gl == gmax, row, float(LANES)), axis=0, keepdims=True)
    e_id = row - N_GROUPS
    in_group = jnp.logical_and(
        jnp.logical_and(e_id >= 0, e_id < N_EXPERTS),
        jnp.floor(e_id * (1.0 / EXPERTS_PER_GROUP)) == g_idx)
    sel = jnp.where(in_group, logits, -jnp.inf)
    v1 = jnp.max(sel, axis=0, keepdims=True)
    i1 = jnp.min(jnp.where(sel == v1, row, float(LANES)), axis=0, keepdims=True)
    sel2 = jnp.where(row == i1, -jnp.inf, sel)
    v2 = jnp.max(sel2, axis=0, keepdims=True)
    i2 = jnp.min(jnp.where(sel2 == v2, row, float(LANES)), axis=0, keepdims=True)
    x2 = jnp.exp(v2 - v1)
    den = 1.0 + x2
    w1 = g_val * (1.0 / den)
    w2 = g_val * (x2 / den)
    e1 = i1 - N_GROUPS
    e2 = i2 - N_GROUPS

    e_row = lax.broadcasted_iota(jnp.int32, (N_EXPERTS, tm), 0).astype(F32)
    hit1 = (e_row == e1).astype(F32)
    hit2 = (e_row == e2).astype(F32)
    hits = hit1 + hit2
    t_from = lax.broadcasted_iota(jnp.int32, (tm, tm), 0)
    t_to = lax.broadcasted_iota(jnp.int32, (tm, tm), 1)
    earlier = (t_from < t_to).astype(BF16)
    before = carry_ref[:, 0:1] + _dot(hits.astype(BF16), earlier)
    rank1 = jnp.sum(hit1 * before, axis=0, keepdims=True)
    rank2 = jnp.sum(hit2 * before, axis=0, keepdims=True)
    carry = carry_ref[...] + jnp.sum(hits, axis=1, keepdims=True)
    carry_ref[...] = carry
    counts_ref[...] = carry

    r = lax.broadcasted_iota(jnp.int32, (ROUTE_ROWS, tm), 0)
    route = jnp.where(r == 0, e1, 0.0)
    for k, v in enumerate((e2, w1, w2, rank1, rank2), start=1):
        route = jnp.where(r == k, v, route)
    route_ref[...] = route


def _outproj(mg, x2, wo, g2, wr_cat, br):
    tm = ROW_TILE
    row = lambda w: pl.BlockSpec((tm, w), lambda i: (i, 0))
    return pl.pallas_call(
        _outproj_kernel,
        grid=(SEQ // tm,),
        in_specs=[row(D_MODEL), row(D_MODEL), _resident(wo.shape), _resident(g2.shape),
                  _resident(wr_cat.shape), _resident(br.shape)],
        out_specs=[row(D_MODEL), row(D_MODEL),
                   pl.BlockSpec((ROUTE_ROWS, tm), lambda i: (0, i)),
                   pl.BlockSpec((N_EXPERTS, LANES), lambda i: (0, 0))],
        out_shape=[jax.ShapeDtypeStruct((SEQ, D_MODEL), F32),
                   jax.ShapeDtypeStruct((SEQ, D_MODEL), F32),
                   jax.ShapeDtypeStruct((ROUTE_ROWS, SEQ), F32),
                   jax.ShapeDtypeStruct((N_EXPERTS, LANES), F32)],
        scratch_shapes=[pltpu.VMEM((N_EXPERTS, LANES), F32)],
        compiler_params=_params(),
        name="outproj",
    )(mg, x2, wo, g2, wr_cat, br)


def _start_row_gather(src_hbm, row_of, dst, slot, sem, rows):
    for r in rows:
        pltpu.make_async_copy(src_hbm.at[row_of(r)], dst.at[slot, r], sem.at[slot]).start()


def _wait_row_gather(src_hbm, dst, slot, sem, n_rows):
    pltpu.make_async_copy(src_hbm.at[pl.ds(0, n_rows)], dst.at[slot], sem.at[slot]).wait()


def _tiles_to_matrix(rows_ref, first, n):
    return jnp.concatenate(
        [rows_ref[pl.ds(first + a, n, stride=TOKEN_ROWS), :] for a in range(TOKEN_ROWS)], axis=1)


def _scatter_kernel(padstart_ref, padlen_ref, pos_ref, x_ref, xs_hbm, xf, xbuf, zbuf, sem, zsem):
    i = pl.program_id(0)
    n = pl.num_programs(0)
    tm = ROW_TILE
    slot = i % 2

    def row_copies_done(s):
        for _ in range(2):
            pltpu.make_async_copy(xbuf.at[s], xs_hbm.at[pl.ds(0, tm)], sem.at[s]).wait()

    def pad_copy(e, bit):
        rows = 1 << bit
        start = padstart_ref[e] + (padlen_ref[e] & (rows - 1))
        return pltpu.make_async_copy(zbuf.at[pl.ds(0, rows)], xs_hbm.at[pl.ds(start, rows)], zsem.at[0])

    def for_each_pad_piece(fn):
        def body(e, carry):
            for bit in range(PAD_BITS):
                @pl.when((lax.shift_right_logical(padlen_ref[e], bit) & 1) == 1)
                def _():
                    fn(pad_copy(e, bit))
            return carry
        lax.fori_loop(0, N_EXPERTS, body, 0)

    @pl.when(i == 0)
    def _():
        zbuf[...] = jnp.zeros(zbuf.shape, BF16)
        for_each_pad_piece(lambda c: c.start())

    @pl.when(i >= 2)
    def _():
        row_copies_done(slot)

    for a in range(TOKEN_ROWS):
        xf[pl.ds(a, tm, stride=TOKEN_ROWS), :] = x_ref[:, a * LANES:(a + 1) * LANES]
    for s in range(2):
        @pl.when(slot == s)
        def _():
            xbuf[s] = xf[...].reshape(tm, TOKEN_ROWS, LANES).astype(BF16)
            for r in range(2 * tm):
                pltpu.make_async_copy(xbuf.at[s, r % tm], xs_hbm.at[pos_ref[0, 0, r]],
                                      sem.at[s]).start()

    @pl.when(i == n - 1)
    def _():
        row_copies_done(1 - slot)
        row_copies_done(slot)
        for_each_pad_piece(lambda c: c.wait())


def _scatter_rows(padstart, padlen, pos_tiles, n2):
    tm = ROW_TILE
    grid_spec = pltpu.PrefetchScalarGridSpec(
        num_scalar_prefetch=2,
        grid=(SEQ // tm,),
        in_specs=[pl.BlockSpec((1, 1, 2 * tm), lambda i, *_: (i, 0, 0), memory_space=pltpu.SMEM),
                  pl.BlockSpec((tm, D_MODEL), lambda i, *_: (i, 0))],
        out_specs=pl.BlockSpec(memory_space=pl.ANY),
        scratch_shapes=[pltpu.VMEM((tm * TOKEN_ROWS, LANES), F32),
                        pltpu.VMEM((2, tm, TOKEN_ROWS, LANES), BF16),
                        pltpu.VMEM((1 << (PAD_BITS - 1), TOKEN_ROWS, LANES), BF16),
                        pltpu.SemaphoreType.DMA((2,)),
                        pltpu.SemaphoreType.DMA((1,))],
    )
    return pl.pallas_call(
        _scatter_kernel,
        grid_spec=grid_spec,
        out_shape=jax.ShapeDtypeStruct((N_EXPERT_TILES * EXPERT_TILE, TOKEN_ROWS, LANES), BF16),
        compiler_params=_params(),
        name="scatter_rows",
    )(padstart, padlen, pos_tiles, n2)


def _expert_kernel(ntl_ref, tst_ref, ntot_ref,
                   xs_hbm, wg_hbm, wu_hbm, wd_hbm,
                   y_hbm,
                   xbuf, ybuf, xsem, ysem, wg_st, wu_st, wd_st, wsem, wgb, wub, wdb, rows_f, rows_y):
    e = pl.program_id(0)
    ntot = ntot_ref[0]
    n_rows = EXPERT_TILE * TOKEN_ROWS

    def weight_copies(ex, slot):
        return [pltpu.make_async_copy(src.at[ex], dst.at[slot], wsem.at[slot, j])
                for j, (src, dst) in enumerate(((wg_hbm, wg_st), (wu_hbm, wu_st), (wd_hbm, wd_st)))]

    def tile_rows(t):
        return pl.ds(pl.multiple_of(t * EXPERT_TILE, EXPERT_TILE), EXPERT_TILE)

    def x_copy(t, slot):
        return pltpu.make_async_copy(xs_hbm.at[tile_rows(t)], xbuf.at[slot], xsem.at[slot])

    def y_copy(t, slot):
        return pltpu.make_async_copy(ybuf.at[slot], y_hbm.at[tile_rows(t)], ysem.at[slot])

    @pl.when(e == 0)
    def _():
        for c in weight_copies(0, 0):
            c.start()
        x_copy(0, 0).start()

    @pl.when(e + 1 < pl.num_programs(0))
    def _():
        for c in weight_copies(e + 1, (e + 1) % 2):
            c.start()

    n_here = ntl_ref[e]
    wslot = e % 2
    for c in weight_copies(e, wslot):
        c.wait()

    @pl.when(n_here > 0)
    def _():
        wgb[...] = wg_st[wslot].astype(BF16)
        wub[...] = wu_st[wslot].astype(BF16)
        wdb[...] = wd_st[wslot].astype(BF16)

        def tile_body(k, carry):
            t = tst_ref[e] + k
            slot = t % 2
            x_copy(t, slot).wait()

            @pl.when(t + 1 < ntot)
            def _():
                x_copy(t + 1, 1 - slot).start()

            rows_f[...] = xbuf[slot].astype(F32).reshape(n_rows, LANES)
            xt = _tiles_to_matrix(rows_f, 0, EXPERT_TILE).astype(BF16)
            hid = jax.nn.silu(_dot(xt, wgb[...])) * _dot(xt, wub[...])
            y = _dot(hid.astype(BF16), wdb[...])
            for a in range(TOKEN_ROWS):
                rows_y[pl.ds(a, EXPERT_TILE, stride=TOKEN_ROWS), :] = y[:, a * LANES:(a + 1) * LANES]

            @pl.when(t >= 2)
            def _():
                y_copy(t - 2, slot).wait()

            ybuf[slot] = rows_y[...].reshape(EXPERT_TILE, TOKEN_ROWS, LANES).astype(BF16)
            y_copy(t, slot).start()
            return carry
        lax.fori_loop(0, n_here, tile_body, 0)

    @pl.when(e == pl.num_programs(0) - 1)
    def _():
        @pl.when(ntot >= 2)
        def _():
            y_copy(ntot - 2, (ntot - 2) % 2).wait()
        y_copy(ntot - 1, (ntot - 1) % 2).wait()


def _experts(ntl, tst, ntot, x_sorted, w_gate, w_up, w_down):
    hbm = pl.BlockSpec(memory_space=pl.ANY)
    grid_spec = pltpu.PrefetchScalarGridSpec(
        num_scalar_prefetch=3,
        grid=(N_EXPERTS,),
        in_specs=[hbm, hbm, hbm, hbm],
        out_specs=hbm,
        scratch_shapes=[
            pltpu.VMEM((2, EXPERT_TILE, TOKEN_ROWS, LANES), BF16),
            pltpu.VMEM((2, EXPERT_TILE, TOKEN_ROWS, LANES), BF16),
            pltpu.SemaphoreType.DMA((2,)),
            pltpu.SemaphoreType.DMA((2,)),
            pltpu.VMEM((2, D_MODEL, EXPERT_FF), F32),
            pltpu.VMEM((2, D_MODEL, EXPERT_FF), F32),
            pltpu.VMEM((2, EXPERT_FF, D_MODEL), F32),
            pltpu.SemaphoreType.DMA((2, 3)),
            pltpu.VMEM((D_MODEL, EXPERT_FF), BF16),
            pltpu.VMEM((D_MODEL, EXPERT_FF), BF16),
            pltpu.VMEM((EXPERT_FF, D_MODEL), BF16),
            pltpu.VMEM((EXPERT_TILE * TOKEN_ROWS, LANES), F32),
            pltpu.VMEM((EXPERT_TILE * TOKEN_ROWS, LANES), F32),
        ],
    )
    return pl.pallas_call(
        _expert_kernel,
        grid_spec=grid_spec,
        out_shape=jax.ShapeDtypeStruct((N_EXPERT_TILES * EXPERT_TILE, TOKEN_ROWS, LANES), BF16),
        compiler_params=_params(),
        name="experts",
    )(ntl, tst, ntot, x_sorted, w_gate, w_up, w_down)


def _final_kernel(pos_ref, posn_ref, h_ref, route_ref, gf_ref, y_hbm, o_ref, ybuf, sem, rows_f):
    i = pl.program_id(0)
    n = pl.num_programs(0)
    tm = ROW_TILE

    @pl.when(i == 0)
    def _():
        _start_row_gather(y_hbm, lambda r: pos_ref[0, 0, r], ybuf, 0, sem, range(2 * tm))

    for s in range(2):
        @pl.when(jnp.logical_and(i + 1 < n, (i + 1) % 2 == s))
        def _():
            _start_row_gather(y_hbm, lambda r: posn_ref[0, 0, r], ybuf, s, sem, range(2 * tm))

    slot = i % 2
    _wait_row_gather(y_hbm, ybuf, slot, sem, 2 * tm)
    rows_f[...] = ybuf[slot].astype(F32).reshape(2 * tm * TOKEN_ROWS, LANES)
    w1 = route_ref[:, 0:1]
    w2 = route_ref[:, 1:2]
    h2 = h_ref[...] + (w1 * _tiles_to_matrix(rows_f, 0, tm)
                       + w2 * _tiles_to_matrix(rows_f, tm * TOKEN_ROWS, tm))
    o_ref[...] = h2 * lax.rsqrt(jnp.mean(h2 * h2, axis=-1, keepdims=True) + EPS) * gf_ref[...]


def _final(pos, h, route, gf, y_sorted):
    tm = ROW_TILE
    n = SEQ // tm
    row = lambda w: pl.BlockSpec((tm, w), lambda i: (i, 0))
    smem_blk = lambda f: pl.BlockSpec((1, 1, 2 * tm), f, memory_space=pltpu.SMEM)
    return pl.pallas_call(
        _final_kernel,
        grid=(n,),
        in_specs=[smem_blk(lambda i: (i, 0, 0)),
                  smem_blk(lambda i: (jnp.minimum(i + 1, n - 1), 0, 0)),
                  row(D_MODEL), row(2), _resident(gf.shape),
                  pl.BlockSpec(memory_space=pl.ANY)],
        out_specs=row(D_MODEL),
        out_shape=jax.ShapeDtypeStruct((SEQ, D_MODEL), F32),
        scratch_shapes=[pltpu.VMEM((2, 2 * tm, TOKEN_ROWS, LANES), BF16),
                        pltpu.SemaphoreType.DMA((2,)),
                        pltpu.VMEM((2 * tm * TOKEN_ROWS, LANES), F32)],
        compiler_params=_params(),
        name="final",
    )(pos, pos, h, route, gf, y_sorted)


def _dispatch(route, counts):
    i32 = jnp.int32
    e_pair = route[0:2].T.astype(i32).reshape(-1)
    rank = route[4:6].T.astype(i32).reshape(-1)
    cnt = counts[:, 0].astype(i32)
    ntl = (cnt + EXPERT_TILE - 1) // EXPERT_TILE
    tend = jnp.cumsum(ntl)
    tst = tend - ntl
    experts = jnp.arange(N_EXPERTS, dtype=i32)
    of_pair = e_pair[:, None] == experts[None, :]
    pos = jnp.sum(jnp.where(of_pair, (tst * EXPERT_TILE)[None, :], 0), axis=1) + rank
    padstart = tst * EXPERT_TILE + cnt
    padlen = ntl * EXPERT_TILE - cnt
    return ntl, tst, tend[-1:], padstart, padlen, pos


def kernel(x, norm1_g, w_in, kv_norm_g, w_uk, w_uv, gmlp_ws, gmlp_bs, ln_v_g, ln_v_b, w_br_attn,
           w_br_gmlp, w_out, norm2_g, w_group, b_group, w_router, b_router, w_e_gate, w_e_up,
           w_e_down, norm_f_g):
    assert x.shape == (1, SEQ, D_MODEL)
    x2 = x.reshape(SEQ, D_MODEL)
    row_vec = lambda v: v.reshape(1, -1).astype(F32)

    assert w_in.shape == (D_MODEL, IN_COLS)
    wqT, wc, wqiT, wkk, wwT, wuv_in, wga, wgb = _split_w_in(w_in.T)

    n1, qT, ckv, ckvT, qiT, kk2, wiT = _proj(x2, row_vec(norm1_g), wqT, wc, wqiT, wkk, wwT,
                                             row_vec(kv_norm_g))
    m = _gmlp(n1, wuv_in, gmlp_ws, jnp.pad(gmlp_bs.T, ((0, 0), (0, LANES - GMLP_GROUPS))),
              row_vec(ln_v_g), row_vec(ln_v_b))
    a = _attn(qT, qiT, wiT, kk2, ckv, ckvT,
              jnp.swapaxes(w_uk, 1, 2).astype(BF16), jnp.swapaxes(w_uv, 1, 2).astype(BF16))
    mg = _merge(n1, a, m, wga, wgb, w_br_attn.astype(BF16), w_br_gmlp.astype(BF16))

    w_route = jnp.pad(jnp.concatenate([w_group, w_router], axis=1),
                      ((0, 0), (0, LANES - N_GROUPS - N_EXPERTS)))
    b_route = jnp.pad(jnp.concatenate([b_group, b_router]), (0, LANES - N_GROUPS - N_EXPERTS))
    wr_hi = w_route.astype(BF16)
    wr_lo = (w_route - wr_hi.astype(F32)).astype(BF16)
    h, n2, route, counts = _outproj(mg, x2, w_out.astype(BF16), row_vec(norm2_g),
                                    jnp.concatenate([wr_hi, wr_lo], axis=1),
                                    b_route.reshape(LANES, 1))

    ntl, tst, ntot, padstart, padlen, pos = _dispatch(route, counts)
    pos_tiles = pos.reshape(SEQ // ROW_TILE, ROW_TILE, 2).transpose(0, 2, 1).reshape(
        SEQ // ROW_TILE, 1, 2 * ROW_TILE)
    x_sorted = _scatter_rows(padstart, padlen, pos_tiles, n2)
    y_sorted = _experts(ntl, tst, ntot, x_sorted, w_e_gate, w_e_up, w_e_down)
    out = _final(pos_tiles, h, route[2:4].T, row_vec(norm_f_g), y_sorted)
    return out.reshape(1, SEQ, D_MODEL)
```

```python
import jax
import jax.numpy as jnp
import numpy as np
from jax import lax
from jax.experimental import pallas as pl
from jax.experimental.pallas import tpu as pltpu

F32 = jnp.float32
BF16 = jnp.bfloat16

D_MODEL = 2048
SEQ = 8192
N_HEADS = 8
HEAD_DIM = 128
KV_LATENT = 256
IDX_HEADS = 16
IDX_DIM = 64
TOPK = 256
ATTN_WIDTH = N_HEADS * HEAD_DIM
GMLP_GROUPS = 8
GMLP_WIDTH = 1024
CHUNK = 128
N_GROUPS = 8
EXPERTS_PER_GROUP = 8
N_EXPERTS = 64
EXPERT_FF = 512
EPS = 1e-6

LANES = 128
SUBLANES = 8
VMEM_LIMIT = 60 * 1024 * 1024
MASK_VALUE = -0.7 * float(np.finfo(np.float32).max)
LOG2_E = float(np.log2(np.e))

ROW_TILE = 256
GMLP_TILE = 512
TQ = 256
KB = 256
NKB = SEQ // KB
COUNT_ROWS = 32
MAX_SEARCH_ITERS = 64
MAX_BRACKET = 4
TINY = 1e-30
HI_MARGIN = 1e-6
EXPERT_TILE = 128
N_EXPERT_TILES = (2 * SEQ) // EXPERT_TILE + N_EXPERTS
PAD_BITS = 7
TOKEN_ROWS = D_MODEL // LANES


def _dot(a, b):
    return jnp.dot(a, b, preferred_element_type=F32)


def _dot_nt(a, b):
    return lax.dot_general(a, b, (((1,), (1,)), ((), ())), preferred_element_type=F32)


def _resident(shape):
    zeros = (0,) * len(shape)
    return pl.BlockSpec(shape, lambda *_: zeros, pipeline_mode=pl.Buffered(1))


def _params(n_axes=1):
    return pltpu.CompilerParams(
        dimension_semantics=("arbitrary",) * n_axes, vmem_limit_bytes=VMEM_LIMIT)


C_Q = ATTN_WIDTH
C_KV = C_Q + KV_LATENT
C_QI = C_KV + IDX_HEADS * IDX_DIM
C_K = C_QI + IDX_DIM
C_W = C_K + IDX_HEADS
C_UV = C_W + 2 * GMLP_WIDTH
C_GA = C_UV + D_MODEL
IN_COLS = C_GA + D_MODEL


def _split_w_in_kernel(wt_ref, wqT_ref, wc_ref, wqiT_ref, wkk_ref, wwT_ref, wuv_ref, wga_ref, wgb_ref):
    wqT_ref[...] = wt_ref[0:C_Q, :].astype(BF16)
    wc_ref[...] = wt_ref[C_Q:C_KV, :].T.astype(BF16)
    wqiT_ref[...] = wt_ref[C_KV:C_QI, :].astype(BF16)
    k_t = wt_ref[C_QI:C_K, :].T.astype(BF16)
    zero = jnp.zeros_like(k_t)
    wkk_ref[...] = jnp.concatenate([k_t, zero, zero, k_t], axis=1)
    wwT_ref[...] = wt_ref[C_K:C_W, :].astype(BF16)
    wuv_ref[...] = wt_ref[C_W:C_UV, :].T.astype(BF16)
    wga_ref[...] = wt_ref[C_UV:C_GA, :].T.astype(BF16)
    wgb_ref[...] = wt_ref[C_GA:IN_COLS, :].T.astype(BF16)


def _split_w_in(w_in_t):
    tr = ROW_TILE
    rows = lambda w: pl.BlockSpec((tr, w), lambda i: (i, 0))
    cols = lambda h: pl.BlockSpec((h, tr), lambda i: (0, i))
    sds = jax.ShapeDtypeStruct
    return pl.pallas_call(
        _split_w_in_kernel,
        grid=(D_MODEL // tr,),
        in_specs=[cols(IN_COLS)],
        out_specs=[cols(C_Q), rows(KV_LATENT), cols(C_QI - C_KV), rows(2 * LANES), cols(IDX_HEADS),
                   rows(2 * GMLP_WIDTH), rows(D_MODEL), rows(D_MODEL)],
        out_shape=[sds((C_Q, D_MODEL), BF16), sds((D_MODEL, KV_LATENT), BF16),
                   sds((C_QI - C_KV, D_MODEL), BF16), sds((D_MODEL, 2 * LANES), BF16),
                   sds((IDX_HEADS, D_MODEL), BF16), sds((D_MODEL, 2 * GMLP_WIDTH), BF16),
                   sds((D_MODEL, D_MODEL), BF16), sds((D_MODEL, D_MODEL), BF16)],
        compiler_params=_params(),
        name="split_w_in",
    )(w_in_t)


def _proj_kernel(x_ref, g1_ref, wqT_ref, wc_ref, wqiT_ref, wkk_ref, wwT_ref, kvg_ref,
                 n1_ref, qT_ref, ckv_ref, ckvT_ref, qiT_ref, kk2_ref, wiT_ref):
    x = x_ref[...]
    ms = jnp.mean(x * x, axis=-1, keepdims=True)
    n1 = (x * lax.rsqrt(ms + EPS) * g1_ref[...]).astype(BF16)
    n1_ref[...] = n1
    qT_ref[...] = _dot_nt(wqT_ref[...], n1).astype(BF16)
    qiT_ref[...] = (_dot_nt(wqiT_ref[...], n1) * (IDX_DIM ** -0.5)).astype(BF16)
    wiT_ref[...] = _dot_nt(wwT_ref[...], n1) * (IDX_HEADS ** -0.5)
    c = _dot(n1, wc_ref[...])
    c = c * lax.rsqrt(jnp.mean(c * c, axis=-1, keepdims=True) + EPS) * kvg_ref[...]
    ckv_ref[...] = c.astype(BF16)
    ckvT_ref[0] = c.T.astype(BF16)
    kk = _dot(n1, wkk_ref[...]).astype(BF16)
    for b in range(2):
        rows = slice(b * KB, (b + 1) * KB)
        kk2_ref[b, 0:KB, :] = kk[rows, 0:LANES]
        kk2_ref[b, KB:2 * KB, :] = kk[rows, LANES:2 * LANES]


def _proj(x2, g1, wqT, wc, wqiT, wkk, wwT, kvg):
    tm = 2 * KB
    row = lambda w: pl.BlockSpec((tm, w), lambda i: (i, 0))
    col = lambda h: pl.BlockSpec((h, tm), lambda i: (0, i))
    return pl.pallas_call(
        _proj_kernel,
        grid=(SEQ // tm,),
        in_specs=[row(D_MODEL), _resident(g1.shape), _resident(wqT.shape), _resident(wc.shape),
                  _resident(wqiT.shape), _resident(wkk.shape), _resident(wwT.shape),
                  _resident(kvg.shape)],
        out_specs=[row(D_MODEL), col(ATTN_WIDTH), row(KV_LATENT),
                   pl.BlockSpec((1, KV_LATENT, tm), lambda i: (i, 0, 0)),
                   col(IDX_HEADS * IDX_DIM),
                   pl.BlockSpec((2, 2 * KB, LANES), lambda i: (i, 0, 0)),
                   col(IDX_HEADS)],
        out_shape=[
            jax.ShapeDtypeStruct((SEQ, D_MODEL), BF16),
            jax.ShapeDtypeStruct((ATTN_WIDTH, SEQ), BF16),
            jax.ShapeDtypeStruct((SEQ, KV_LATENT), BF16),
            jax.ShapeDtypeStruct((NKB // 2, KV_LATENT, 2 * KB), BF16),
            jax.ShapeDtypeStruct((IDX_HEADS * IDX_DIM, SEQ), BF16),
            jax.ShapeDtypeStruct((NKB, 2 * KB, LANES), BF16),
            jax.ShapeDtypeStruct((IDX_HEADS, SEQ), F32),
        ],
        compiler_params=_params(),
        name="proj",
    )(x2, g1, wqT, wc, wqiT, wkk, wwT, kvg)


def _gmlp_kernel(n1_ref, wuv_ref, ws_ref, bsT_ref, lng_ref, lnb_ref, m_ref):
    uv = _dot(n1_ref[...], wuv_ref[...])
    z = jax.nn.gelu(uv)
    u = z[:, :GMLP_WIDTH]
    v = z[:, GMLP_WIDTH:]
    mu = jnp.mean(v, axis=-1, keepdims=True)
    var = jnp.mean(jnp.square(v - mu), axis=-1, keepdims=True)
    vn = ((v - mu) * lax.rsqrt(var + EPS) * lng_ref[...] + lnb_ref[...]).astype(BF16)
    t_pos = lax.broadcasted_iota(jnp.int32, (CHUNK, CHUNK), 0)
    s_pos = lax.broadcasted_iota(jnp.int32, (CHUNK, CHUNK), 1)
    causal = s_pos <= t_pos
    for g in range(GMLP_GROUPS):
        wm = jnp.where(causal, ws_ref[g], 0.0).astype(BF16)
        bias = bsT_ref[:, g:g + 1]
        cols = slice(g * LANES, (g + 1) * LANES)
        for c in range(GMLP_TILE // CHUNK):
            rows = slice(c * CHUNK, (c + 1) * CHUNK)
            y = _dot(wm, vn[rows, cols]) + bias
            m_ref[rows, cols] = (u[rows, cols] * y).astype(BF16)


def _gmlp(n1, wuv, ws, bsT, lng, lnb):
    tm = GMLP_TILE
    return pl.pallas_call(
        _gmlp_kernel,
        grid=(SEQ // tm,),
        in_specs=[pl.BlockSpec((tm, D_MODEL), lambda i: (i, 0)), _resident(wuv.shape),
                  _resident(ws.shape), _resident(bsT.shape), _resident(lng.shape),
                  _resident(lnb.shape)],
        out_specs=pl.BlockSpec((tm, GMLP_WIDTH), lambda i: (i, 0)),
        out_shape=jax.ShapeDtypeStruct((SEQ, GMLP_WIDTH), BF16),
        compiler_params=_params(),
        name="gmlp",
    )(n1, wuv, ws, bsT, lng, lnb)


def _attn_kernel(qT_ref, qiT_ref, wiT_ref, kk2_ref, ckv_ref, ckvT_ref, wukT_ref, wuvT_ref,
                 a_ref,
                 sc_ref, qabs_ref, thr_ref, keep_ref, m_ref, l_ref, acc_ref, lg_ref, bmax_ref):
    i = pl.program_id(0)
    nkb = i + 1

    for h in range(N_HEADS):
        qa = _dot(wukT_ref[h], qT_ref[h * HEAD_DIM:(h + 1) * HEAD_DIM, :]) * (
            HEAD_DIM ** -0.5 * LOG2_E)
        qabs_ref[h] = qa.astype(BF16)

    q_pos = i * TQ + lax.broadcasted_iota(jnp.int32, (KB, TQ), 1)
    k_off = lax.broadcasted_iota(jnp.int32, (KB, TQ), 0)

    def score_body(kb, carry):
        smax, smin = carry
        keys = kk2_ref[kb]
        acc = jnp.zeros((KB, TQ), F32)
        for j in range(IDX_HEADS // 2):
            d = _dot(keys, qiT_ref[j * LANES:(j + 1) * LANES, :])
            acc = acc + jnp.maximum(d[0:KB], 0.0) * wiT_ref[2 * j:2 * j + 1, :]
            acc = acc + jnp.maximum(d[KB:2 * KB], 0.0) * wiT_ref[2 * j + 1:2 * j + 2, :]
        causal = (kb * KB + k_off) <= q_pos
        sc_ref[kb] = jnp.where(causal, acc, -jnp.inf)
        smax = jnp.maximum(smax, jnp.max(jnp.where(causal, acc, -jnp.inf), axis=0, keepdims=True))
        smin = jnp.minimum(smin, jnp.min(jnp.where(causal, acc, jnp.inf), axis=0, keepdims=True))
        return smax, smin

    n_pairs = (nkb + 1) // 2
    smax, smin = lax.fori_loop(
        0, n_pairs, lambda j, c: score_body(2 * j + 1, score_body(2 * j, c)),
        (jnp.full((1, TQ), -jnp.inf, F32), jnp.full((1, TQ), jnp.inf, F32)))

    def count_ge(x):
        def body(j, cnt):
            for kb in (2 * j, 2 * j + 1):
                ge = jnp.where(sc_ref[kb] >= x, 1.0, 0.0)
                cnt = cnt + jnp.sum(ge.reshape(KB // COUNT_ROWS, COUNT_ROWS, TQ), axis=0)
            return cnt
        cnt = lax.fori_loop(0, n_pairs, body, jnp.zeros((COUNT_ROWS, TQ), F32))
        return jnp.sum(cnt, axis=0, keepdims=True)

    def max_below(x):
        def body(j, best):
            for kb in (2 * j, 2 * j + 1):
                s = sc_ref[kb]
                v = jnp.where(s < x, s, -jnp.inf)
                best = jnp.maximum(best, jnp.max(v.reshape(KB // COUNT_ROWS, COUNT_ROWS, TQ), axis=0))
            return best
        best = lax.fori_loop(0, n_pairs, body, jnp.full((COUNT_ROWS, TQ), -jnp.inf, F32))
        return jnp.max(best, axis=0, keepdims=True)

    n_causal = (i * TQ + 1 + lax.broadcasted_iota(jnp.int32, (1, TQ), 1)).astype(F32)
    want = jnp.minimum(n_causal, float(TOPK))
    hi0 = smax + jnp.maximum(jnp.abs(smax), TINY) * HI_MARGIN
    open0 = (n_causal != want).astype(F32)

    def search_cond(state):
        return jnp.logical_and(state[0] < MAX_SEARCH_ITERS, state[-1] > 0.0)

    def search_body(state):
        it, lo, hi, c_lo, c_hi, open_, _ = state
        mid = 0.5 * lo + 0.5 * hi
        c = count_ge(mid)
        live = jnp.logical_and(open_ > 0.0, jnp.logical_and(mid > lo, mid < hi))
        ge = c >= want
        go_lo = jnp.logical_and(live, ge)
        go_hi = jnp.logical_and(live, jnp.logical_not(ge))
        lo = jnp.where(go_lo, mid, lo)
        c_lo = jnp.where(go_lo, c, c_lo)
        hi = jnp.where(go_hi, mid, hi)
        c_hi = jnp.where(go_hi, c, c_hi)
        wide = jnp.logical_and(c_lo != want, c_lo - c_hi > MAX_BRACKET)
        open_ = jnp.logical_and(live, wide).astype(F32)
        return it + 1, lo, hi, c_lo, c_hi, open_, jnp.max(open_)

    _, lo_f, hi_f, c_lo_f, c_hi_f, _, _ = lax.while_loop(
        search_cond, search_body,
        (jnp.int32(0), smin, hi0, n_causal, jnp.zeros((1, TQ), F32), open0, jnp.max(open0)))

    stepping = c_lo_f != want

    def step_cond(state):
        return jnp.logical_and(state[0] < MAX_BRACKET, state[-1] > 0.0)

    def step_body(state):
        it, cur, c, _ = state
        act = jnp.logical_and(stepping, c < want)
        cur = jnp.where(act, max_below(cur), cur)
        c = jnp.where(act, c + 1.0, c)
        return it + 1, cur, c, jnp.max(jnp.logical_and(stepping, c < want).astype(F32))

    _, cur_f, _, _ = lax.while_loop(
        step_cond, step_body,
        (jnp.int32(0), hi_f, c_hi_f, jnp.max(stepping.astype(F32))))
    thr0 = jnp.where(stepping, cur_f, lo_f)
    thr_ref[...] = jnp.broadcast_to(thr0, (SUBLANES, TQ))
    keep_ref[...] = jnp.full((SUBLANES, TQ), float(SEQ), F32)
    unresolved0 = (count_ge(thr0) != want).astype(F32)

    @pl.when(jnp.max(unresolved0) > 0.0)
    def _():
        def next_value(lo, below):
            def body(kb, u):
                s = sc_ref[kb]
                cand = jnp.logical_and(s >= lo, s > below)
                return jnp.minimum(u, jnp.min(jnp.where(cand, s, jnp.inf), axis=0, keepdims=True))
            return lax.fori_loop(0, nkb, body, jnp.full((1, TQ), jnp.inf, F32))

        def count_gt(x):
            def body(kb, cnt):
                gt = jnp.where(sc_ref[kb] > x, 1.0, 0.0)
                return cnt + jnp.sum(gt, axis=0, keepdims=True)
            return lax.fori_loop(0, nkb, body, jnp.zeros((1, TQ), F32))

        def peel_cond(state):
            return state[-1] > 0.0

        def peel_body(state):
            below, unres, thr, keep, _ = state
            u = next_value(lo_f, below)
            c_gt = count_gt(u)
            hit = jnp.logical_and(unres > 0.0, c_gt < want)
            thr = jnp.where(hit, u, thr)
            keep = jnp.where(hit, want - c_gt, keep)
            unres = jnp.logical_and(unres > 0.0, jnp.logical_not(hit)).astype(F32)
            return u, unres, thr, keep, jnp.max(unres)

        _, _, thr_t, keep_t, _ = lax.while_loop(
            peel_cond, peel_body,
            (jnp.full((1, TQ), -jnp.inf, F32), unresolved0, thr0,
             jnp.full((1, TQ), float(SEQ), F32), jnp.float32(1.0)))
        thr_ref[...] = jnp.broadcast_to(thr_t, (SUBLANES, TQ))
        keep_ref[...] = jnp.broadcast_to(keep_t, (SUBLANES, TQ))

        r_i = lax.broadcasted_iota(jnp.int32, (KB, KB), 0)
        c_i = lax.broadcasted_iota(jnp.int32, (KB, KB), 1)
        before = (c_i < r_i).astype(BF16)

        def drop_body(kb, seen):
            s = sc_ref[kb]
            eq = jnp.logical_and(s == thr_t, unresolved0 > 0.0)
            eq_f = eq.astype(F32)
            rank = seen + _dot(before, eq_f.astype(BF16))
            sc_ref[kb] = jnp.where(jnp.logical_and(eq, rank >= keep_t), -jnp.inf, s)
            return seen + jnp.sum(eq_f, axis=0, keepdims=True)

        lax.fori_loop(0, nkb, drop_body, jnp.zeros((1, TQ), F32))

    m_ref[...] = jnp.full(m_ref.shape, MASK_VALUE, F32)
    l_ref[...] = jnp.zeros(l_ref.shape, F32)
    acc_ref[...] = jnp.zeros(acc_ref.shape, F32)

    def logits_stage(j, slot):
        scores = sc_ref[pl.ds(2 * j, 2)].reshape(2 * KB, TQ)
        bias = jnp.where(scores >= thr_ref[0:1, :], 0.0, MASK_VALUE)
        c_n = ckv_ref[pl.ds(pl.multiple_of(j * (2 * KB), 2 * KB), 2 * KB), :]
        for h in range(N_HEADS):
            lg = _dot(c_n, qabs_ref[h]) + bias
            lg_ref[slot, h] = lg
            bmax_ref[slot, h] = jnp.broadcast_to(jnp.max(lg, axis=0, keepdims=True), (SUBLANES, TQ))

    def softmax_stage(j, slot):
        c_t = ckvT_ref[j]
        for h in range(N_HEADS):
            m_old = m_ref[h, 0:1, :]
            m_new = jnp.maximum(m_old, bmax_ref[slot, h, 0:1, :])
            alpha = jnp.exp2(m_old - m_new)
            p = jnp.exp2(lg_ref[slot, h] - m_new)
            l_new = alpha * l_ref[h, 0:1, :] + jnp.sum(p, axis=0, keepdims=True)
            acc_ref[h] = acc_ref[h] * alpha + _dot(c_t, p.astype(BF16))
            m_ref[h] = jnp.broadcast_to(m_new, (SUBLANES, TQ))
            l_ref[h] = jnp.broadcast_to(l_new, (SUBLANES, TQ))

    logits_stage(0, 0)

    def att_body(j, carry):
        for slot in range(2):
            @pl.when(j % 2 == slot)
            def _():
                logits_stage(j + 1, 1 - slot)
                softmax_stage(j, slot)
        return carry

    lax.fori_loop(0, n_pairs - 1, att_body, 0)
    for slot in range(2):
        @pl.when((n_pairs - 1) % 2 == slot)
        def _():
            softmax_stage(n_pairs - 1, slot)

    for h in range(N_HEADS):
        o_t = (acc_ref[h] / l_ref[h, 0:1, :]).astype(BF16)
        a_t = _dot(wuvT_ref[h], o_t)
        a_ref[:, h * HEAD_DIM:(h + 1) * HEAD_DIM] = a_t.T.astype(BF16)


def _attn(qT, qiT, wiT, kk2, ckv, ckvT, wukT, wuvT):
    col = lambda h: pl.BlockSpec((h, TQ), lambda i: (0, i))
    return pl.pallas_call(
        _attn_kernel,
        grid=(SEQ // TQ,),
        in_specs=[col(ATTN_WIDTH), col(IDX_HEADS * IDX_DIM), col(IDX_HEADS), _resident(kk2.shape),
                  _resident(ckv.shape), _resident(ckvT.shape), _resident(wukT.shape),
                  _resident(wuvT.shape)],
        out_specs=pl.BlockSpec((TQ, ATTN_WIDTH), lambda i: (i, 0)),
        out_shape=jax.ShapeDtypeStruct((SEQ, ATTN_WIDTH), BF16),
        scratch_shapes=[
            pltpu.VMEM((NKB, KB, TQ), F32),
            pltpu.VMEM((N_HEADS, KV_LATENT, TQ), BF16),
            pltpu.VMEM((SUBLANES, TQ), F32),
            pltpu.VMEM((SUBLANES, TQ), F32),
            pltpu.VMEM((N_HEADS, SUBLANES, TQ), F32),
            pltpu.VMEM((N_HEADS, SUBLANES, TQ), F32),
            pltpu.VMEM((N_HEADS, KV_LATENT, TQ), F32),
            pltpu.VMEM((2, N_HEADS, 2 * KB, TQ), F32),
            pltpu.VMEM((2, N_HEADS, SUBLANES, TQ), F32),
        ],
        compiler_params=_params(),
        name="attn",
    )(qT, qiT, wiT, kk2, ckv, ckvT, wukT, wuvT)


def _merge_kernel(n1_ref, a_ref, m_ref, wga_ref, wgb_ref, wba_ref, wbg_ref, o_ref):
    n1 = n1_ref[...]
    br_a = jax.nn.sigmoid(_dot(n1, wga_ref[...])) * _dot(a_ref[...], wba_ref[...])
    br_b = jax.nn.sigmoid(_dot(n1, wgb_ref[...])) * _dot(m_ref[...], wbg_ref[...])
    o_ref[...] = (br_a + br_b).astype(BF16)


def _merge(n1, a, m, wga, wgb, wba, wbg):
    tm = ROW_TILE
    row = lambda w: pl.BlockSpec((tm, w), lambda i: (i, 0))
    return pl.pallas_call(
        _merge_kernel,
        grid=(SEQ // tm,),
        in_specs=[row(D_MODEL), row(ATTN_WIDTH), row(GMLP_WIDTH), _resident(wga.shape),
                  _resident(wgb.shape), _resident(wba.shape), _resident(wbg.shape)],
        out_specs=row(D_MODEL),
        out_shape=jax.ShapeDtypeStruct((SEQ, D_MODEL), BF16),
        compiler_params=_params(),
        name="merge",
    )(n1, a, m, wga, wgb, wba, wbg)


ROUTE_ROWS = 8


def _outproj_kernel(mg_ref, x_ref, wo_ref, g2_ref, wrc_ref, br_ref,
                    h_ref, n2_ref, route_ref, counts_ref, carry_ref):
    i = pl.program_id(0)
    tm = ROW_TILE

    @pl.when(i == 0)
    def _():
        carry_ref[...] = jnp.zeros(carry_ref.shape, F32)

    h = x_ref[...] + _dot(mg_ref[...], wo_ref[...])
    h_ref[...] = h
    n2 = h * lax.rsqrt(jnp.mean(h * h, axis=-1, keepdims=True) + EPS) * g2_ref[...]
    n2_ref[...] = n2
    n2_hi = n2.astype(BF16)
    n2_lo = (n2 - n2_hi.astype(F32)).astype(BF16)
    hh_hl = _dot(n2_hi, wrc_ref[...])
    lh = _dot(n2_lo, wrc_ref[:, 0:LANES])
    logits = (hh_hl[:, 0:LANES] + hh_hl[:, LANES:2 * LANES] + lh).T + br_ref[...]
    row = lax.broadcasted_iota(jnp.int32, (LANES, tm), 0).astype(F32)
    is_group = row < N_GROUPS
    gl = jnp.where(is_group, logits, -jnp.inf)
    gmax = jnp.max(gl, axis=0, keepdims=True)
    gsum = jnp.sum(jnp.where(is_group, jnp.exp(logits - gmax), 0.0), axis=0, keepdims=True)
    g_val = 1.0 / gsum
    g_idx = jnp.min(jnp.where(gl == gmax, row, float(LANES)), axis=0, keepdims=True)
    e_id = row - N_GROUPS
    in_group = jnp.logical_and(
        jnp.logical_and(e_id >= 0, e_id < N_EXPERTS),
        jnp.floor(e_id * (1.0 / EXPERTS_PER_GROUP)) == g_idx)
    sel = jnp.where(in_group, logits, -jnp.inf)
    v1 = jnp.max(sel, axis=0, keepdims=True)
    i1 = jnp.min(jnp.where(sel == v1, row, float(LANES)), axis=0, keepdims=True)
    sel2 = jnp.where(row == i1, -jnp.inf, sel)
    v2 = jnp.max(sel2, axis=0, keepdims=True)
    i2 = jnp.min(jnp.where(sel2 == v2, row, float(LANES)), axis=0, keepdims=True)
    x2 = jnp.exp(v2 - v1)
    den = 1.0 + x2
    w1 = g_val * (1.0 / den)
    w2 = g_val * (x2 / den)
    e1 = i1 - N_GROUPS
    e2 = i2 - N_GROUPS

    e_row = lax.broadcasted_iota(jnp.int32, (N_EXPERTS, tm), 0).astype(F32)
    hit1 = (e_row == e1).astype(F32)
    hit2 = (e_row == e2).astype(F32)
    hits = hit1 + hit2
    t_from = lax.broadcasted_iota(jnp.int32, (tm, tm), 0)
    t_to = lax.broadcasted_iota(jnp.int32, (tm, tm), 1)
    earlier = (t_from < t_to).astype(BF16)
    before = carry_ref[:, 0:1] + _dot(hits.astype(BF16), earlier)
    rank1 = jnp.sum(hit1 * before, axis=0, keepdims=True)
    rank2 = jnp.sum(hit2 * before, axis=0, keepdims=True)
    carry = carry_ref[...] + jnp.sum(hits, axis=1, keepdims=True)
    carry_ref[...] = carry
    counts_ref[...] = carry

    r = lax.broadcasted_iota(jnp.int32, (ROUTE_ROWS, tm), 0)
    route = jnp.where(r == 0, e1, 0.0)
    for k, v in enumerate((e2, w1, w2, rank1, rank2), start=1):
        route = jnp.where(r == k, v, route)
    route_ref[...] = route


def _outproj(mg, x2, wo, g2, wr_cat, br):
    tm = ROW_TILE
    row = lambda w: pl.BlockSpec((tm, w), lambda i: (i, 0))
    return pl.pallas_call(
        _outproj_kernel,
        grid=(SEQ // tm,),
        in_specs=[row(D_MODEL), row(D_MODEL), _resident(wo.shape), _resident(g2.shape),
                  _resident(wr_cat.shape), _resident(br.shape)],
        out_specs=[row(D_MODEL), row(D_MODEL),
                   pl.BlockSpec((ROUTE_ROWS, tm), lambda i: (0, i)),
                   pl.BlockSpec((N_EXPERTS, LANES), lambda i: (0, 0))],
        out_shape=[jax.ShapeDtypeStruct((SEQ, D_MODEL), F32),
                   jax.ShapeDtypeStruct((SEQ, D_MODEL), F32),
                   jax.ShapeDtypeStruct((ROUTE_ROWS, SEQ), F32),
                   jax.ShapeDtypeStruct((N_EXPERTS, LANES), F32)],
        scratch_shapes=[pltpu.VMEM((N_EXPERTS, LANES), F32)],
        compiler_params=_params(),
        name="outproj",
    )(mg, x2, wo, g2, wr_cat, br)


def _start_row_gather(src_hbm, row_of, dst, slot, sem, rows):
    for r in rows:
        pltpu.make_async_copy(src_hbm.at[row_of(r)], dst.at[slot, r], sem.at[slot]).start()


def _wait_row_gather(src_hbm, dst, slot, sem, n_rows):
    pltpu.make_async_copy(src_hbm.at[pl.ds(0, n_rows)], dst.at[slot], sem.at[slot]).wait()


def _tiles_to_matrix(rows_ref, first, n):
    return jnp.concatenate(
        [rows_ref[pl.ds(first + a, n, stride=TOKEN_ROWS), :] for a in range(TOKEN_ROWS)], axis=1)


def _scatter_kernel(padstart_ref, padlen_ref, pos_ref, x_ref, xs_hbm, xf, xbuf, zbuf, sem, zsem):
    i = pl.program_id(0)
    n = pl.num_programs(0)
    tm = ROW_TILE
    slot = i % 2

    def row_copies_done(s):
        for _ in range(2):
            pltpu.make_async_copy(xbuf.at[s], xs_hbm.at[pl.ds(0, tm)], sem.at[s]).wait()

    def pad_copy(e, bit):
        rows = 1 << bit
        start = padstart_ref[e] + (padlen_ref[e] & (rows - 1))
        return pltpu.make_async_copy(zbuf.at[pl.ds(0, rows)], xs_hbm.at[pl.ds(start, rows)], zsem.at[0])

    def for_each_pad_piece(fn):
        def body(e, carry):
            for bit in range(PAD_BITS):
                @pl.when((lax.shift_right_logical(padlen_ref[e], bit) & 1) == 1)
                def _():
                    fn(pad_copy(e, bit))
            return carry
        lax.fori_loop(0, N_EXPERTS, body, 0)

    @pl.when(i == 0)
    def _():
        zbuf[...] = jnp.zeros(zbuf.shape, BF16)
        for_each_pad_piece(lambda c: c.start())

    @pl.when(i >= 2)
    def _():
        row_copies_done(slot)

    for a in range(TOKEN_ROWS):
        xf[pl.ds(a, tm, stride=TOKEN_ROWS), :] = x_ref[:, a * LANES:(a + 1) * LANES]
    for s in range(2):
        @pl.when(slot == s)
        def _():
            xbuf[s] = xf[...].reshape(tm, TOKEN_ROWS, LANES).astype(BF16)
            for r in range(2 * tm):
                pltpu.make_async_copy(xbuf.at[s, r % tm], xs_hbm.at[pos_ref[0, 0, r]],
                                      sem.at[s]).start()

    @pl.when(i == n - 1)
    def _():
        row_copies_done(1 - slot)
        row_copies_done(slot)
        for_each_pad_piece(lambda c: c.wait())


def _scatter_rows(padstart, padlen, pos_tiles, n2):
    tm = ROW_TILE
    grid_spec = pltpu.PrefetchScalarGridSpec(
        num_scalar_prefetch=2,
        grid=(SEQ // tm,),
        in_specs=[pl.BlockSpec((1, 1, 2 * tm), lambda i, *_: (i, 0, 0), memory_space=pltpu.SMEM),
                  pl.BlockSpec((tm, D_MODEL), lambda i, *_: (i, 0))],
        out_specs=pl.BlockSpec(memory_space=pl.ANY),
        scratch_shapes=[pltpu.VMEM((tm * TOKEN_ROWS, LANES), F32),
                        pltpu.VMEM((2, tm, TOKEN_ROWS, LANES), BF16),
                        pltpu.VMEM((1 << (PAD_BITS - 1), TOKEN_ROWS, LANES), BF16),
                        pltpu.SemaphoreType.DMA((2,)),
                        pltpu.SemaphoreType.DMA((1,))],
    )
    return pl.pallas_call(
        _scatter_kernel,
        grid_spec=grid_spec,
        out_shape=jax.ShapeDtypeStruct((N_EXPERT_TILES * EXPERT_TILE, TOKEN_ROWS, LANES), BF16),
        compiler_params=_params(),
        name="scatter_rows",
    )(padstart, padlen, pos_tiles, n2)


def _expert_kernel(ntl_ref, tst_ref, ntot_ref,
                   xs_hbm, wg_hbm, wu_hbm, wd_hbm,
                   y_hbm,
                   xbuf, ybuf, xsem, ysem, wg_st, wu_st, wd_st, wsem, wgb, wub, wdb, rows_f, rows_y):
    e = pl.program_id(0)
    ntot = ntot_ref[0]
    n_rows = EXPERT_TILE * TOKEN_ROWS

    def weight_copies(ex, slot):
        return [pltpu.make_async_copy(src.at[ex], dst.at[slot], wsem.at[slot, j])
                for j, (src, dst) in enumerate(((wg_hbm, wg_st), (wu_hbm, wu_st), (wd_hbm, wd_st)))]

    def tile_rows(t):
        return pl.ds(pl.multiple_of(t * EXPERT_TILE, EXPERT_TILE), EXPERT_TILE)

    def x_copy(t, slot):
        return pltpu.make_async_copy(xs_hbm.at[tile_rows(t)], xbuf.at[slot], xsem.at[slot])

    def y_copy(t, slot):
        return pltpu.make_async_copy(ybuf.at[slot], y_hbm.at[tile_rows(t)], ysem.at[slot])

    @pl.when(e == 0)
    def _():
        for c in weight_copies(0, 0):
            c.start()
        x_copy(0, 0).start()

    @pl.when(e + 1 < pl.num_programs(0))
    def _():
        for c in weight_copies(e + 1, (e + 1) % 2):
            c.start()

    n_here = ntl_ref[e]
    wslot = e % 2
    for c in weight_copies(e, wslot):
        c.wait()

    @pl.when(n_here > 0)
    def _():
        wgb[...] = wg_st[wslot].astype(BF16)
        wub[...] = wu_st[wslot].astype(BF16)
        wdb[...] = wd_st[wslot].astype(BF16)

        def tile_body(k, carry):
            t = tst_ref[e] + k
            slot = t % 2
            x_copy(t, slot).wait()

            @pl.when(t + 1 < ntot)
            def _():
                x_copy(t + 1, 1 - slot).start()

            rows_f[...] = xbuf[slot].astype(F32).reshape(n_rows, LANES)
            xt = _tiles_to_matrix(rows_f, 0, EXPERT_TILE).astype(BF16)
            hid = jax.nn.silu(_dot(xt, wgb[...])) * _dot(xt, wub[...])
            y = _dot(hid.astype(BF16), wdb[...])
            for a in range(TOKEN_ROWS):
                rows_y[pl.ds(a, EXPERT_TILE, stride=TOKEN_ROWS), :] = y[:, a * LANES:(a + 1) * LANES]

            @pl.when(t >= 2)
            def _():
                y_copy(t - 2, slot).wait()

            ybuf[slot] = rows_y[...].reshape(EXPERT_TILE, TOKEN_ROWS, LANES).astype(BF16)
            y_copy(t, slot).start()
            return carry
        lax.fori_loop(0, n_here, tile_body, 0)

    @pl.when(e == pl.num_programs(0) - 1)
    def _():
        @pl.when(ntot >= 2)
        def _():
            y_copy(ntot - 2, (ntot - 2) % 2).wait()
        y_copy(ntot - 1, (ntot - 1) % 2).wait()


def _experts(ntl, tst, ntot, x_sorted, w_gate, w_up, w_down):
    hbm = pl.BlockSpec(memory_space=pl.ANY)
    grid_spec = pltpu.PrefetchScalarGridSpec(
        num_scalar_prefetch=3,
        grid=(N_EXPERTS,),
        in_specs=[hbm, hbm, hbm, hbm],
        out_specs=hbm,
        scratch_shapes=[
            pltpu.VMEM((2, EXPERT_TILE, TOKEN_ROWS, LANES), BF16),
            pltpu.VMEM((2, EXPERT_TILE, TOKEN_ROWS, LANES), BF16),
            pltpu.SemaphoreType.DMA((2,)),
            pltpu.SemaphoreType.DMA((2,)),
            pltpu.VMEM((2, D_MODEL, EXPERT_FF), F32),
            pltpu.VMEM((2, D_MODEL, EXPERT_FF), F32),
            pltpu.VMEM((2, EXPERT_FF, D_MODEL), F32),
            pltpu.SemaphoreType.DMA((2, 3)),
            pltpu.VMEM((D_MODEL, EXPERT_FF), BF16),
            pltpu.VMEM((D_MODEL, EXPERT_FF), BF16),
            pltpu.VMEM((EXPERT_FF, D_MODEL), BF16),
            pltpu.VMEM((EXPERT_TILE * TOKEN_ROWS, LANES), F32),
            pltpu.VMEM((EXPERT_TILE * TOKEN_ROWS, LANES), F32),
        ],
    )
    return pl.pallas_call(
        _expert_kernel,
        grid_spec=grid_spec,
        out_shape=jax.ShapeDtypeStruct((N_EXPERT_TILES * EXPERT_TILE, TOKEN_ROWS, LANES), BF16),
        compiler_params=_params(),
        name="experts",
    )(ntl, tst, ntot, x_sorted, w_gate, w_up, w_down)


def _final_kernel(pos_ref, posn_ref, h_ref, route_ref, gf_ref, y_hbm, o_ref, ybuf, sem, rows_f):
    i = pl.program_id(0)
    n = pl.num_programs(0)
    tm = ROW_TILE

    @pl.when(i == 0)
    def _():
        _start_row_gather(y_hbm, lambda r: pos_ref[0, 0, r], ybuf, 0, sem, range(2 * tm))

    for s in range(2):
        @pl.when(jnp.logical_and(i + 1 < n, (i + 1) % 2 == s))
        def _():
            _start_row_gather(y_hbm, lambda r: posn_ref[0, 0, r], ybuf, s, sem, range(2 * tm))

    slot = i % 2
    _wait_row_gather(y_hbm, ybuf, slot, sem, 2 * tm)
    rows_f[...] = ybuf[slot].astype(F32).reshape(2 * tm * TOKEN_ROWS, LANES)
    w1 = route_ref[:, 0:1]
    w2 = route_ref[:, 1:2]
    h2 = h_ref[...] + (w1 * _tiles_to_matrix(rows_f, 0, tm)
                       + w2 * _tiles_to_matrix(rows_f, tm * TOKEN_ROWS, tm))
    o_ref[...] = h2 * lax.rsqrt(jnp.mean(h2 * h2, axis=-1, keepdims=True) + EPS) * gf_ref[...]


def _final(pos, h, route, gf, y_sorted):
    tm = ROW_TILE
    n = SEQ // tm
    row = lambda w: pl.BlockSpec((tm, w), lambda i: (i, 0))
    smem_blk = lambda f: pl.BlockSpec((1, 1, 2 * tm), f, memory_space=pltpu.SMEM)
    return pl.pallas_call(
        _final_kernel,
        grid=(n,),
        in_specs=[smem_blk(lambda i: (i, 0, 0)),
                  smem_blk(lambda i: (jnp.minimum(i + 1, n - 1), 0, 0)),
                  row(D_MODEL), row(2), _resident(gf.shape),
                  pl.BlockSpec(memory_space=pl.ANY)],
        out_specs=row(D_MODEL),
        out_shape=jax.ShapeDtypeStruct((SEQ, D_MODEL), F32),
        scratch_shapes=[pltpu.VMEM((2, 2 * tm, TOKEN_ROWS, LANES), BF16),
                        pltpu.SemaphoreType.DMA((2,)),
                        pltpu.VMEM((2 * tm * TOKEN_ROWS, LANES), F32)],
        compiler_params=_params(),
        name="final",
    )(pos, pos, h, route, gf, y_sorted)


def _dispatch(route, counts):
    i32 = jnp.int32
    e_pair = route[0:2].T.astype(i32).reshape(-1)
    rank = route[4:6].T.astype(i32).reshape(-1)
    cnt = counts[:, 0].astype(i32)
    ntl = (cnt + EXPERT_TILE - 1) // EXPERT_TILE
    tend = jnp.cumsum(ntl)
    tst = tend - ntl
    experts = jnp.arange(N_EXPERTS, dtype=i32)
    of_pair = e_pair[:, None] == experts[None, :]
    pos = jnp.sum(jnp.where(of_pair, (tst * EXPERT_TILE)[None, :], 0), axis=1) + rank
    padstart = tst * EXPERT_TILE + cnt
    padlen = ntl * EXPERT_TILE - cnt
    return ntl, tst, tend[-1:], padstart, padlen, pos


def kernel(x, norm1_g, w_in, kv_norm_g, w_uk, w_uv, gmlp_ws, gmlp_bs, ln_v_g, ln_v_b, w_br_attn,
           w_br_gmlp, w_out, norm2_g, w_group, b_group, w_router, b_router, w_e_gate, w_e_up,
           w_e_down, norm_f_g):
    assert x.shape == (1, SEQ, D_MODEL)
    x2 = x.reshape(SEQ, D_MODEL)
    row_vec = lambda v: v.reshape(1, -1).astype(F32)

    assert w_in.shape == (D_MODEL, IN_COLS)
    wqT, wc, wqiT, wkk, wwT, wuv_in, wga, wgb = _split_w_in(w_in.T)

    n1, qT, ckv, ckvT, qiT, kk2, wiT = _proj(x2, row_vec(norm1_g), wqT, wc, wqiT, wkk, wwT,
                                             row_vec(kv_norm_g))
    m = _gmlp(n1, wuv_in, gmlp_ws, jnp.pad(gmlp_bs.T, ((0, 0), (0, LANES - GMLP_GROUPS))),
              row_vec(ln_v_g), row_vec(ln_v_b))
    a = _attn(qT, qiT, wiT, kk2, ckv, ckvT,
              jnp.swapaxes(w_uk, 1, 2).astype(BF16), jnp.swapaxes(w_uv, 1, 2).astype(BF16))
    mg = _merge(n1, a, m, wga, wgb, w_br_attn.astype(BF16), w_br_gmlp.astype(BF16))

    w_route = jnp.pad(jnp.concatenate([w_group, w_router], axis=1),
                      ((0, 0), (0, LANES - N_GROUPS - N_EXPERTS)))
    b_route = jnp.pad(jnp.concatenate([b_group, b_router]), (0, LANES - N_GROUPS - N_EXPERTS))
    wr_hi = w_route.astype(BF16)
    wr_lo = (w_route - wr_hi.astype(F32)).astype(BF16)
    h, n2, route, counts = _outproj(mg, x2, w_out.astype(BF16), row_vec(norm2_g),
                                    jnp.concatenate([wr_hi, wr_lo], axis=1),
                                    b_route.reshape(LANES, 1))

    ntl, tst, ntot, padstart, padlen, pos = _dispatch(route, counts)
    pos_tiles = pos.reshape(SEQ // ROW_TILE, ROW_TILE, 2).transpose(0, 2, 1).reshape(
        SEQ // ROW_TILE, 1, 2 * ROW_TILE)
    x_sorted = _scatter_rows(padstart, padlen, pos_tiles, n2)
    y_sorted = _experts(ntl, tst, ntot, x_sorted, w_e_gate, w_e_up, w_e_down)
    out = _final(pos_tiles, h, route[2:4].T, row_vec(norm_f_g), y_sorted)
    return out.reshape(1, SEQ, D_MODEL)
```
